```python
import math
import jax, jax.numpy as jnp
from jax import lax
import numpy as np

D_MODEL = 2048
BATCH = 8
SEQ = 4096
DEPTH = 4

GRID_W = 64
HEAD_DIM = 128
N_Q_HEADS = 16
N_KV_HEADS = 4
GROUP = N_Q_HEADS // N_KV_HEADS
ATTN_WIDTH = N_Q_HEADS * HEAD_DIM
KV_WIDTH = N_KV_HEADS * HEAD_DIM
CONV_WIDTH = D_MODEL
CONV_K = 3
D_FF = 4 * D_MODEL
Q_BLOCK = 128
ROPE_THETA = 10000.0
RMS_EPS = 1e-6
AXIS_DIM = HEAD_DIM // 2
N_FREQ = AXIS_DIM // 2
IN_SPLITS = (CONV_WIDTH, CONV_WIDTH, CONV_WIDTH, ATTN_WIDTH, KV_WIDTH, KV_WIDTH, D_MODEL, D_MODEL)
IN_WIDTH = sum(IN_SPLITS)
IN_OFFSETS = tuple(int(o) for o in np.cumsum(IN_SPLITS)[:-1])

kernel_name = 'hybrid_shortconv_gqa_axial_encoder'


def rmsnorm(x, g):
    xf = x.astype(jnp.float32)
    y = xf * lax.rsqrt(jnp.mean(xf * xf, axis=-1, keepdims=True) + RMS_EPS)
    return (y * g.astype(jnp.float32)).astype(x.dtype)


def axial_rope_tables(seq_len):
    rows = seq_len // GRID_W
    row_idx = jnp.repeat(jnp.arange(rows, dtype=jnp.int32), GRID_W)
    col_idx = jnp.tile(jnp.arange(GRID_W, dtype=jnp.int32), rows)
    inv_freq = ROPE_THETA ** (-jnp.arange(0, AXIS_DIM, 2, dtype=jnp.float32) / AXIS_DIM)
    ang = jnp.stack([row_idx.astype(jnp.float32)[:, None] * inv_freq,
                     col_idx.astype(jnp.float32)[:, None] * inv_freq], axis=1)
    return jnp.cos(ang), jnp.sin(ang)


def apply_axial_rope(x, cos, sin):
    b, s, h, _ = x.shape
    xr = x.astype(jnp.float32).reshape(b, s, h, 2, 2, N_FREQ)
    x1, x2 = xr[..., 0, :], xr[..., 1, :]
    c, sn = cos[None, :, None], sin[None, :, None]
    out = jnp.stack([x1 * c - x2 * sn, x2 * c + x1 * sn], axis=-2)
    return out.reshape(b, s, h, HEAD_DIM).astype(x.dtype)


def short_conv_mixer(conv_b, conv_c, h_in, w_conv, w_out):
    u = conv_c * h_in
    up = jnp.pad(u, ((0, 0), (1, 1), (0, 0)))
    conv = w_conv[0] * up[:, :-2] + w_conv[1] * up[:, 1:-1] + w_conv[2] * up[:, 2:]
    return (conv_b * conv) @ w_out


def block_gqa(q, k, v):
    b, s, _, _ = q.shape
    nb = s // Q_BLOCK
    scale = 1.0 / math.sqrt(HEAD_DIM)
    qb = (q * scale).reshape(b, nb, Q_BLOCK, N_KV_HEADS, GROUP, HEAD_DIM).transpose(1, 0, 2, 3, 4, 5)

    def one_block(q_blk):
        scores = jnp.einsum('bqkgd,bskd->bkgqs', q_blk, k).astype(jnp.float32)
        p = jax.nn.softmax(scores, axis=-1).astype(v.dtype)
        return jnp.einsum('bkgqs,bskd->bqkgd', p, v)

    o = lax.map(one_block, qb)
    return o.transpose(1, 0, 2, 3, 4, 5).reshape(b, s, ATTN_WIDTH)


def _fwd_setup_inputs(seed: int = 0) -> dict:
    key = jax.random.key(seed)
    ks = jax.random.split(key, 16)
    f32 = jnp.float32
    nrm = lambda k, shape, scale: jax.random.normal(k, shape, f32) * scale
    gain = lambda k, shape: 1.0 + 0.02 * jax.random.normal(k, shape, f32)
    return {
        'x': jax.random.normal(ks[0], (BATCH, SEQ, D_MODEL), f32),
        'norm_mix_pre': gain(ks[1], (DEPTH, D_MODEL)),
        'w_in': nrm(ks[2], (DEPTH, D_MODEL, IN_WIDTH), D_MODEL ** -0.5),
        'gate_bias': nrm(ks[3], (DEPTH, 2 * D_MODEL), 0.01),
        'conv_w': nrm(ks[4], (DEPTH, CONV_K, CONV_WIDTH), CONV_K ** -0.5),
        'q_norm': gain(ks[5], (DEPTH, HEAD_DIM)),
        'k_norm': gain(ks[6], (DEPTH, HEAD_DIM)),
        'w_out_conv': nrm(ks[7], (DEPTH, CONV_WIDTH, D_MODEL), CONV_WIDTH ** -0.5),
        'w_out_attn': nrm(ks[8], (DEPTH, ATTN_WIDTH, D_MODEL), ATTN_WIDTH ** -0.5),
        'w_merge': nrm(ks[9], (DEPTH, D_MODEL, D_MODEL), D_MODEL ** -0.5),
        'norm_mix_post': gain(ks[10], (DEPTH, D_MODEL)),
        'norm_mlp_pre': gain(ks[11], (DEPTH, D_MODEL)),
        'w_up': nrm(ks[12], (DEPTH, D_MODEL, D_FF), D_MODEL ** -0.5),
        'w_down': nrm(ks[13], (DEPTH, D_FF, D_MODEL), D_FF ** -0.5),
        'norm_mlp_post': gain(ks[14], (DEPTH, D_MODEL)),
    }


def _fwd_reference(x, norm_mix_pre, w_in, gate_bias, conv_w, q_norm, k_norm, w_out_conv, w_out_attn,
              w_merge, norm_mix_post, norm_mlp_pre, w_up, w_down, norm_mlp_post):
    b, s, _ = x.shape
    cos, sin = axial_rope_tables(s)
    for l in range(DEPTH):
        h = rmsnorm(x, norm_mix_pre[l])
        z = h @ w_in[l]
        conv_b, conv_c, conv_in, q, k, v, g_a, g_b = jnp.split(z, IN_OFFSETS, axis=-1)

        y_a = short_conv_mixer(conv_b, conv_c, conv_in, conv_w[l], w_out_conv[l])

        q = apply_axial_rope(rmsnorm(q.reshape(b, s, N_Q_HEADS, HEAD_DIM), q_norm[l]), cos, sin)
        k = apply_axial_rope(rmsnorm(k.reshape(b, s, N_KV_HEADS, HEAD_DIM), k_norm[l]), cos, sin)
        v = v.reshape(b, s, N_KV_HEADS, HEAD_DIM)
        y_b = block_gqa(q, k, v) @ w_out_attn[l]

        gates = jax.nn.sigmoid(jnp.concatenate([g_a, g_b], axis=-1) + gate_bias[l])
        gate_a, gate_b = jnp.split(gates, 2, axis=-1)
        mixed = (gate_a * y_a + gate_b * y_b) @ w_merge[l]
        x = x + rmsnorm(mixed, norm_mix_post[l])

        h = rmsnorm(x, norm_mlp_pre[l])
        f = jnp.square(jax.nn.relu(h @ w_up[l])) @ w_down[l]
        x = x + rmsnorm(f, norm_mlp_post[l])
    return x


import jax as _jax
import jax.numpy as _jnp

TWIN_FORMAT = 'train_step'
FWD_PARAMS = ['x', 'norm_mix_pre', 'w_in', 'gate_bias', 'conv_w', 'q_norm', 'k_norm', 'w_out_conv', 'w_out_attn', 'w_merge', 'norm_mix_post', 'norm_mlp_pre', 'w_up', 'w_down', 'norm_mlp_post']
TWIN_WEIGHTS = ['norm_mix_pre', 'w_in', 'gate_bias', 'conv_w', 'q_norm', 'k_norm', 'w_out_conv', 'w_out_attn', 'w_merge', 'norm_mix_post', 'norm_mlp_pre', 'w_up', 'w_down', 'norm_mlp_post']
TWIN_DIFF_INPUT = 'x'
TWIN_INPUTS = ['x', 'norm_mix_pre', 'w_in', 'gate_bias', 'conv_w', 'q_norm', 'k_norm', 'w_out_conv', 'w_out_attn', 'w_merge', 'norm_mix_post', 'norm_mlp_pre', 'w_up', 'w_down', 'norm_mlp_post', 'loss_target', 'm_norm_mix_pre', 'm_w_in', 'm_gate_bias', 'm_conv_w', 'm_q_norm', 'm_k_norm', 'm_w_out_conv', 'm_w_out_attn', 'm_w_merge', 'm_norm_mix_post', 'm_norm_mlp_pre', 'm_w_up', 'm_w_down', 'm_norm_mlp_post', 'v_norm_mix_pre', 'v_w_in', 'v_gate_bias', 'v_conv_w', 'v_q_norm', 'v_k_norm', 'v_w_out_conv', 'v_w_out_attn', 'v_w_merge', 'v_norm_mix_post', 'v_norm_mlp_pre', 'v_w_up', 'v_w_down', 'v_norm_mlp_post']
TWIN_OUTPUTS = ['loss', 'grad_x', 'grad_norm_mix_pre', 'grad_w_in', 'grad_gate_bias', 'grad_conv_w', 'grad_q_norm', 'grad_k_norm', 'grad_w_out_conv', 'grad_w_out_attn', 'grad_w_merge', 'grad_norm_mix_post', 'grad_norm_mlp_pre', 'grad_w_up', 'grad_w_down', 'grad_norm_mlp_post', 'delta_norm_mix_pre', 'delta_w_in', 'delta_gate_bias', 'delta_conv_w', 'delta_q_norm', 'delta_k_norm', 'delta_w_out_conv', 'delta_w_out_attn', 'delta_w_merge', 'delta_norm_mix_post', 'delta_norm_mlp_pre', 'delta_w_up', 'delta_w_down', 'delta_norm_mlp_post', 'new_m_norm_mix_pre', 'new_m_w_in', 'new_m_gate_bias', 'new_m_conv_w', 'new_m_q_norm', 'new_m_k_norm', 'new_m_w_out_conv', 'new_m_w_out_attn', 'new_m_w_merge', 'new_m_norm_mix_post', 'new_m_norm_mlp_pre', 'new_m_w_up', 'new_m_w_down', 'new_m_norm_mlp_post', 'new_v_norm_mix_pre', 'new_v_w_in', 'new_v_gate_bias', 'new_v_conv_w', 'new_v_q_norm', 'new_v_k_norm', 'new_v_w_out_conv', 'new_v_w_out_attn', 'new_v_w_merge', 'new_v_norm_mix_post', 'new_v_norm_mlp_pre', 'new_v_w_up', 'new_v_w_down', 'new_v_norm_mlp_post']
TWIN_LEAF_KINDS = {'loss': 'loss', 'grad_x': 'grad_x', 'grad_norm_mix_pre': 'grad_w', 'grad_w_in': 'grad_w', 'grad_gate_bias': 'grad_w', 'grad_conv_w': 'grad_w', 'grad_q_norm': 'grad_w', 'grad_k_norm': 'grad_w', 'grad_w_out_conv': 'grad_w', 'grad_w_out_attn': 'grad_w', 'grad_w_merge': 'grad_w', 'grad_norm_mix_post': 'grad_w', 'grad_norm_mlp_pre': 'grad_w', 'grad_w_up': 'grad_w', 'grad_w_down': 'grad_w', 'grad_norm_mlp_post': 'grad_w', 'delta_norm_mix_pre': 'delta_w', 'delta_w_in': 'delta_w', 'delta_gate_bias': 'delta_w', 'delta_conv_w': 'delta_w', 'delta_q_norm': 'delta_w', 'delta_k_norm': 'delta_w', 'delta_w_out_conv': 'delta_w', 'delta_w_out_attn': 'delta_w', 'delta_w_merge': 'delta_w', 'delta_norm_mix_post': 'delta_w', 'delta_norm_mlp_pre': 'delta_w', 'delta_w_up': 'delta_w', 'delta_w_down': 'delta_w', 'delta_norm_mlp_post': 'delta_w', 'new_m_norm_mix_pre': 'new_m', 'new_m_w_in': 'new_m', 'new_m_gate_bias': 'new_m', 'new_m_conv_w': 'new_m', 'new_m_q_norm': 'new_m', 'new_m_k_norm': 'new_m', 'new_m_w_out_conv': 'new_m', 'new_m_w_out_attn': 'new_m', 'new_m_w_merge': 'new_m', 'new_m_norm_mix_post': 'new_m', 'new_m_norm_mlp_pre': 'new_m', 'new_m_w_up': 'new_m', 'new_m_w_down': 'new_m', 'new_m_norm_mlp_post': 'new_m', 'new_v_norm_mix_pre': 'new_v', 'new_v_w_in': 'new_v', 'new_v_gate_bias': 'new_v', 'new_v_conv_w': 'new_v', 'new_v_q_norm': 'new_v', 'new_v_k_norm': 'new_v', 'new_v_w_out_conv': 'new_v', 'new_v_w_out_attn': 'new_v', 'new_v_w_merge': 'new_v', 'new_v_norm_mix_post': 'new_v', 'new_v_norm_mlp_pre': 'new_v', 'new_v_w_up': 'new_v', 'new_v_w_down': 'new_v', 'new_v_norm_mlp_post': 'new_v'}


def _forward(args):
    return _fwd_reference(*[args[k] for k in FWD_PARAMS])


def _output_shape():
    def fwd():
        inp = _fwd_setup_inputs(0)
        return _fwd_reference(*[inp[k] for k in FWD_PARAMS])
    out = _jax.eval_shape(fwd)
    return out.shape, out.dtype

N_MICROBATCH = 1
ADAM_LR = 0.001
ADAM_B1 = 0.9
ADAM_B2 = 0.999
ADAM_EPS = 1e-08
ADAM_WD = 0.01
ADAM_STEP = 10
PER_EXAMPLE_BATCH_AXIS = {'x': 0, 'loss_target': 0}
SHARED_INPUTS = []
_WEIGHT_DTYPES = {'norm_mix_pre': _jnp.float32, 'w_in': _jnp.float32, 'gate_bias': _jnp.float32, 'conv_w': _jnp.float32, 'q_norm': _jnp.float32, 'k_norm': _jnp.float32, 'w_out_conv': _jnp.float32, 'w_out_attn': _jnp.float32, 'w_merge': _jnp.float32, 'norm_mix_post': _jnp.float32, 'norm_mlp_pre': _jnp.float32, 'w_up': _jnp.float32, 'w_down': _jnp.float32, 'norm_mlp_post': _jnp.float32}
MOMENT_SCALE = {'norm_mix_pre': 3.261886e+00, 'w_in': 1.276387e+00, 'gate_bias': 1.045838e+00, 'conv_w': 6.773232e-01, 'q_norm': 2.842276e-01, 'k_norm': 2.737540e-01, 'w_out_conv': 6.994905e-01, 'w_out_attn': 3.715397e+00, 'w_merge': 3.457574e+00, 'norm_mix_post': 1.640467e+01, 'norm_mlp_pre': 2.375648e+00, 'w_up': 1.180869e+00, 'w_down': 7.383603e+00, 'norm_mlp_post': 1.806315e+01}


def _to_microbatches(a, axis):
    t = _jnp.moveaxis(a, axis, 0)
    t = t.reshape((N_MICROBATCH, t.shape[0] // N_MICROBATCH) + t.shape[1:])
    return _jnp.moveaxis(t, 1, axis + 1)


def setup_inputs(seed: int = 0) -> dict:
    inp = _fwd_setup_inputs(seed)
    key = _jax.random.fold_in(_jax.random.key(seed), 7919)
    shape, _ = _output_shape()
    out = dict(inp)
    out["loss_target"] = _jax.random.normal(_jax.random.fold_in(key, 0), shape, _jnp.float32)
    for i, name in enumerate(TWIN_WEIGHTS):
        w = inp[name].astype(_jnp.float32)
        if MOMENT_SCALE is None:
            s = _jnp.sqrt(_jnp.mean(_jnp.square(w)) + 1e-30)
        else:
            s = MOMENT_SCALE[name]
        km, kv = _jax.random.split(_jax.random.fold_in(key, i + 1))
        out[name] = w
        out["m_" + name] = s * _jax.random.normal(km, w.shape, _jnp.float32)
        out["v_" + name] = (s * s) * _jax.random.uniform(kv, w.shape, _jnp.float32, 0.5, 1.5)
    if N_MICROBATCH > 1:
        for name, axis in PER_EXAMPLE_BATCH_AXIS.items():
            out[name] = _to_microbatches(out[name], axis)
    return {'x': out['x'], 'norm_mix_pre': out['norm_mix_pre'], 'w_in': out['w_in'], 'gate_bias': out['gate_bias'], 'conv_w': out['conv_w'], 'q_norm': out['q_norm'], 'k_norm': out['k_norm'], 'w_out_conv': out['w_out_conv'], 'w_out_attn': out['w_out_attn'], 'w_merge': out['w_merge'], 'norm_mix_post': out['norm_mix_post'], 'norm_mlp_pre': out['norm_mlp_pre'], 'w_up': out['w_up'], 'w_down': out['w_down'], 'norm_mlp_post': out['norm_mlp_post'], 'loss_target': out['loss_target'], 'm_norm_mix_pre': out['m_norm_mix_pre'], 'm_w_in': out['m_w_in'], 'm_gate_bias': out['m_gate_bias'], 'm_conv_w': out['m_conv_w'], 'm_q_norm': out['m_q_norm'], 'm_k_norm': out['m_k_norm'], 'm_w_out_conv': out['m_w_out_conv'], 'm_w_out_attn': out['m_w_out_attn'], 'm_w_merge': out['m_w_merge'], 'm_norm_mix_post': out['m_norm_mix_post'], 'm_norm_mlp_pre': out['m_norm_mlp_pre'], 'm_w_up': out['m_w_up'], 'm_w_down': out['m_w_down'], 'm_norm_mlp_post': out['m_norm_mlp_post'], 'v_norm_mix_pre': out['v_norm_mix_pre'], 'v_w_in': out['v_w_in'], 'v_gate_bias': out['v_gate_bias'], 'v_conv_w': out['v_conv_w'], 'v_q_norm': out['v_q_norm'], 'v_k_norm': out['v_k_norm'], 'v_w_out_conv': out['v_w_out_conv'], 'v_w_out_attn': out['v_w_out_attn'], 'v_w_merge': out['v_w_merge'], 'v_norm_mix_post': out['v_norm_mix_post'], 'v_norm_mlp_pre': out['v_norm_mlp_pre'], 'v_w_up': out['v_w_up'], 'v_w_down': out['v_w_down'], 'v_norm_mlp_post': out['v_norm_mlp_post']}


def _loss(weights, diff, rest, loss_target):
    with _jax.named_scope("forward"):
        args = {**rest, TWIN_DIFF_INPUT: diff, **{k: w.astype(_WEIGHT_DTYPES[k]) for k, w in weights.items()}}
        y = _forward(args)
    with _jax.named_scope("loss_head"):
        err = _jnp.square(y.astype(_jnp.float32) - loss_target)
        return 0.5 * _jnp.sum(_jnp.mean(err, axis=-1)) if err.ndim else 0.5 * err


def _adamw(w, g, m, v):
    m = ADAM_B1 * m + (1.0 - ADAM_B1) * g
    v = ADAM_B2 * v + (1.0 - ADAM_B2) * _jnp.square(g)
    m_hat = m / (1.0 - ADAM_B1 ** ADAM_STEP)
    v_hat = v / (1.0 - ADAM_B2 ** ADAM_STEP)
    delta = -ADAM_LR * (m_hat / (_jnp.sqrt(v_hat) + ADAM_EPS) + ADAM_WD * w)
    return delta, m, v


def reference(x, norm_mix_pre, w_in, gate_bias, conv_w, q_norm, k_norm, w_out_conv, w_out_attn, w_merge, norm_mix_post, norm_mlp_pre, w_up, w_down, norm_mlp_post, loss_target, m_norm_mix_pre, m_w_in, m_gate_bias, m_conv_w, m_q_norm, m_k_norm, m_w_out_conv, m_w_out_attn, m_w_merge, m_norm_mix_post, m_norm_mlp_pre, m_w_up, m_w_down, m_norm_mlp_post, v_norm_mix_pre, v_w_in, v_gate_bias, v_conv_w, v_q_norm, v_k_norm, v_w_out_conv, v_w_out_attn, v_w_merge, v_norm_mix_post, v_norm_mlp_pre, v_w_up, v_w_down, v_norm_mlp_post):
    given = dict(x=x, norm_mix_pre=norm_mix_pre, w_in=w_in, gate_bias=gate_bias, conv_w=conv_w, q_norm=q_norm, k_norm=k_norm, w_out_conv=w_out_conv, w_out_attn=w_out_attn, w_merge=w_merge, norm_mix_post=norm_mix_post, norm_mlp_pre=norm_mlp_pre, w_up=w_up, w_down=w_down, norm_mlp_post=norm_mlp_post, loss_target=loss_target, m_norm_mix_pre=m_norm_mix_pre, m_w_in=m_w_in, m_gate_bias=m_gate_bias, m_conv_w=m_conv_w, m_q_norm=m_q_norm, m_k_norm=m_k_norm, m_w_out_conv=m_w_out_conv, m_w_out_attn=m_w_out_attn, m_w_merge=m_w_merge, m_norm_mix_post=m_norm_mix_post, m_norm_mlp_pre=m_norm_mlp_pre, m_w_up=m_w_up, m_w_down=m_w_down, m_norm_mlp_post=m_norm_mlp_post, v_norm_mix_pre=v_norm_mix_pre, v_w_in=v_w_in, v_gate_bias=v_gate_bias, v_conv_w=v_conv_w, v_q_norm=v_q_norm, v_k_norm=v_k_norm, v_w_out_conv=v_w_out_conv, v_w_out_attn=v_w_out_attn, v_w_merge=v_w_merge, v_norm_mix_post=v_norm_mix_post, v_norm_mlp_pre=v_norm_mlp_pre, v_w_up=v_w_up, v_w_down=v_w_down, v_norm_mlp_post=v_norm_mlp_post)
    weights = {n: given[n] for n in TWIN_WEIGHTS}
    shared = {n: given[n] for n in SHARED_INPUTS}
    per_example = {n: given[n] for n in ['x']}
    grad_fn = _jax.value_and_grad(_loss, argnums=(0, 1))

    def one_microbatch(ex, loss_target):
        ex = dict(ex)
        diff = ex.pop(TWIN_DIFF_INPUT)
        return grad_fn(weights, diff, {**shared, **ex}, loss_target)

    if N_MICROBATCH == 1:
        loss, (grad_w, grad_x) = one_microbatch(per_example, given["loss_target"])
    else:
        def body(carry, xs):
            loss_sum, grad_sum = carry
            l_k, (gw_k, gx_k) = one_microbatch(xs[0], xs[1])
            with _jax.named_scope("update"):
                return (loss_sum + l_k, _jax.tree.map(_jnp.add, grad_sum, gw_k)), gx_k

        init = (_jnp.zeros((), _jnp.float32), _jax.tree.map(_jnp.zeros_like, weights))
        (loss, grad_w), grad_x = _jax.lax.scan(body, init, (per_example, given["loss_target"]))
    with _jax.named_scope("update"):
        delta_w, new_m, new_v = {}, {}, {}
        for n in TWIN_WEIGHTS:
            delta_w[n], new_m[n], new_v[n] = _adamw(weights[n], grad_w[n], given["m_" + n], given["v_" + n])
    return (loss, grad_x, *[grad_w[n] for n in TWIN_WEIGHTS], *[delta_w[n] for n in TWIN_WEIGHTS],
            *[new_m[n] for n in TWIN_WEIGHTS], *[new_v[n] for n in TWIN_WEIGHTS])
```

```python
import math

import jax
import jax.numpy as jnp
from jax import lax
from jax.experimental import pallas as pl
from jax.experimental.pallas import tpu as pltpu

F32 = jnp.float32
MXU_DTYPE = jnp.bfloat16
COMM_DTYPE = jnp.bfloat16

D_MODEL = 2048
HEAD_DIM = 128
N_Q_HEADS = 16
N_KV_HEADS = 4
GROUP = N_Q_HEADS // N_KV_HEADS
ATTN_WIDTH = N_Q_HEADS * HEAD_DIM
KV_WIDTH = N_KV_HEADS * HEAD_DIM
CONV_WIDTH = D_MODEL
D_FF = 4 * D_MODEL
GRID_W = 64
ROPE_THETA = 10000.0
RMS_EPS = 1e-6
IN_WIDTH = 3 * CONV_WIDTH + ATTN_WIDTH + 2 * KV_WIDTH + 2 * D_MODEL
OFF_CB, OFF_CC, OFF_CI = 0, CONV_WIDTH, 2 * CONV_WIDTH
OFF_Q = 3 * CONV_WIDTH
OFF_K = OFF_Q + ATTN_WIDTH
OFF_V = OFF_K + KV_WIDTH
OFF_GA = OFF_V + KV_WIDTH
OFF_GB = OFF_GA + D_MODEL
ATTN_SCALE = 1.0 / math.sqrt(HEAD_DIM)

ADAM_LR, ADAM_B1, ADAM_B2, ADAM_EPS, ADAM_WD, ADAM_STEP = 0.001, 0.9, 0.999, 1e-08, 0.01, 10

N_DEV = 8
N_CHIP = 4
LANE = 128
SUBLANE = 8
VMEM_LIMIT = 48 * 1024 * 1024
MESH = pl.DeviceIdType.MESH
ANY = pl.BlockSpec(memory_space=pl.ANY)


def _sds(shape, dtype):
    return jax.ShapeDtypeStruct(tuple(shape), dtype)


def _params(sem, vmem=VMEM_LIMIT):
    return pltpu.CompilerParams(dimension_semantics=sem, vmem_limit_bytes=vmem)


def _rows(ts, w, col=0):
    return pl.BlockSpec((ts, w), lambda i: (i, col))


def _fixed(shape):
    return pl.BlockSpec(shape, lambda *_: (0,) * len(shape))


def _fold8(v):
    ts, w = v.shape
    return v.reshape(ts // SUBLANE, SUBLANE, w).sum(axis=0)


def matmul(a, b, *, mode, m, n, k, name, out_dtypes, tm=1024, tn=1024, tk=2048, b_layer=None, epilogue=None,
           extra=(), out_into=None):
    tm, tn, tk = min(tm, m), min(tn, n), min(tk, k)
    nm, nn, nk = m // tm, n // tn, k // tk
    assert nm * tm == m and nn * tn == n and nk * tk == k, (name, m, n, k, tm, tn, tk)
    if mode == "tn":
        a_spec = pl.BlockSpec((tk, tm), lambda i, j, kk: (kk, i))
        dims = (((0,), (0,)), ((), ()))
    else:
        a_spec = pl.BlockSpec((tm, tk), lambda i, j, kk: (i, kk))
        dims = (((1,), (1 if mode == "nt" else 0,)), ((), ()))
    b_blk, b_idx = ((tn, tk), lambda i, j, kk: (j, kk)) if mode == "nt" else ((tk, tn), lambda i, j, kk: (kk, j))
    if b_layer is None:
        b_spec = pl.BlockSpec(b_blk, b_idx)
    else:
        b_spec = pl.BlockSpec((None,) + b_blk, lambda i, j, kk: (b_layer,) + b_idx(i, j, kk))
    tile = pl.BlockSpec((tm, tn), lambda i, j, kk: (i, j))
    n_out, n_extra = len(out_dtypes), len(extra)
    if epilogue is None:
        epilogue = lambda acc: (acc,)

    def body(a_ref, b_ref, *rest):
        extra_refs = rest[:n_extra]
        outs = rest[n_extra + (out_into is not None):][:n_out]
        part = lax.dot_general(a_ref[...].astype(MXU_DTYPE), b_ref[...].astype(MXU_DTYPE), dims,
                               preferred_element_type=F32)

        def finish(acc):
            for o_ref, val in zip(outs, epilogue(acc, *[r[...] for r in extra_refs])):
                o_ref[...] = val.astype(o_ref.dtype)

        if nk == 1:
            finish(part)
        else:
            acc_ref = rest[-1]
            kk = pl.program_id(2)

            @pl.when(kk == 0)
            def _():
                acc_ref[...] = part

            @pl.when(kk > 0)
            def _():
                acc_ref[...] += part

            @pl.when(kk == nk - 1)
            def _():
                finish(acc_ref[...])

    in_specs = [a_spec, b_spec] + [tile] * n_extra
    operands = [a, b, *extra]
    if out_into is None:
        out_specs = [tile] * n_out
        out_shape = [_sds((m, n), dt) for dt in out_dtypes]
        aliases = {}
    else:
        buf, layer = out_into
        in_specs.append(ANY)
        operands.append(buf)
        out_specs = [pl.BlockSpec((None, tm, tn), lambda i, j, kk: (layer, i, j))]
        out_shape = [_sds(buf.shape, buf.dtype)]
        aliases = {len(operands) - 1: 0}
    res = pl.pallas_call(
        body, name=name, grid=(nm, nn, nk), in_specs=in_specs, out_specs=out_specs, out_shape=out_shape,
        scratch_shapes=[pltpu.VMEM((tm, tn), F32)] if nk > 1 else [], input_output_aliases=aliases,
        compiler_params=_params(("parallel", "parallel", "arbitrary")),
    )(*operands)
    return res[0] if n_out == 1 else res


def _rstd(x):
    return lax.rsqrt(jnp.mean(x * x, axis=-1, keepdims=True) + RMS_EPS)


def _rms_bwd(x, g, dy):
    rstd = _rstd(x)
    xh = x * rstd
    gy = dy * g
    dx = rstd * (gy - xh * jnp.mean(gy * xh, axis=-1, keepdims=True))
    return dx, dy * xh


def rms_fwd(x, g, *, name):
    s = x.shape[0]
    ts = min(s, 512)

    def body(x_ref, g_ref, h_ref):
        xv = x_ref[...]
        h_ref[...] = (xv * _rstd(xv) * g_ref[...]).astype(h_ref.dtype)

    return pl.pallas_call(
        body, name=name, grid=(s // ts,), in_specs=[_rows(ts, D_MODEL), _fixed((1, D_MODEL))],
        out_specs=_rows(ts, D_MODEL), out_shape=_sds((s, D_MODEL), MXU_DTYPE), compiler_params=_params(("parallel",)),
    )(x, g)


def rms_residual_fwd(x, y, g_post, g_next, *, name):
    s = x.shape[0]
    ts = min(s, 512)
    with_next = g_next is not None

    def body(x_ref, y_ref, gp_ref, *rest):
        yv = y_ref[...]
        xn = x_ref[...] + yv * _rstd(yv) * gp_ref[...]
        if with_next:
            gn_ref, xo_ref, h_ref = rest
            h_ref[...] = (xn * _rstd(xn) * gn_ref[...]).astype(h_ref.dtype)
        else:
            (xo_ref,) = rest
        xo_ref[...] = xn

    gspec = _fixed((1, D_MODEL))
    res = pl.pallas_call(
        body, name=name, grid=(s // ts,),
        in_specs=[_rows(ts, D_MODEL), _rows(ts, D_MODEL), gspec] + [gspec] * with_next,
        out_specs=[_rows(ts, D_MODEL)] * (1 + with_next),
        out_shape=[_sds((s, D_MODEL), F32)] + [_sds((s, D_MODEL), MXU_DTYPE)] * with_next,
        compiler_params=_params(("parallel",)),
    )(x, y, g_post, *([g_next] if with_next else []))
    return (res[0], res[1]) if with_next else (res[0], None)


def rms_bwd(x, g, dy, residual, *, out_dtype, name):
    s = x.shape[0]
    ts = min(s, 256)
    nt = s // ts
    with_res = residual is not None

    def body(x_ref, g_ref, dy_ref, *rest):
        if with_res:
            r_ref, dx_ref, dg_ref, acc_ref = rest
        else:
            dx_ref, dg_ref, acc_ref = rest
        dx, dg_rows = _rms_bwd(x_ref[...], g_ref[...], dy_ref[...].astype(F32))
        if with_res:
            dx = dx + r_ref[...]
        dx_ref[...] = dx.astype(dx_ref.dtype)
        i = pl.program_id(0)

        @pl.when(i == 0)
        def _():
            acc_ref[...] = jnp.zeros_like(acc_ref)

        acc_ref[...] += _fold8(dg_rows)

        @pl.when(i == nt - 1)
        def _():
            dg_ref[...] = acc_ref[...].sum(axis=0, keepdims=True)

    return pl.pallas_call(
        body, name=name, grid=(nt,),
        in_specs=[_rows(ts, D_MODEL), _fixed((1, D_MODEL)), _rows(ts, D_MODEL)] + [_rows(ts, D_MODEL)] * with_res,
        out_specs=[_rows(ts, D_MODEL), _fixed((1, D_MODEL))],
        out_shape=[_sds((s, D_MODEL), out_dtype), _sds((1, D_MODEL), F32)],
        scratch_shapes=[pltpu.VMEM((SUBLANE, D_MODEL), F32)], compiler_params=_params(("arbitrary",)),
    )(x, g, dy, *([residual] if with_res else []))


def loss_and_grad(y, target, *, name):
    s = y.shape[0]
    ts = min(s, 512)
    nt = s // ts

    def body(y_ref, t_ref, dy_ref, part_ref):
        e = y_ref[...] - t_ref[...]
        dy_ref[...] = e * (1.0 / D_MODEL)
        sq = _fold8(e * e)
        lanes = sq[:, 0:LANE]
        for j in range(1, D_MODEL // LANE):
            lanes = lanes + sq[:, j * LANE:(j + 1) * LANE]
        i = pl.program_id(0)

        @pl.when(i == 0)
        def _():
            part_ref[...] = jnp.zeros_like(part_ref)

        part_ref[...] += lanes * (0.5 / D_MODEL)

    return pl.pallas_call(
        body, name=name, grid=(nt,), in_specs=[_rows(ts, D_MODEL), _rows(ts, D_MODEL)],
        out_specs=[_rows(ts, D_MODEL), _fixed((SUBLANE, LANE))],
        out_shape=[_sds((s, D_MODEL), F32), _sds((SUBLANE, LANE), F32)], compiler_params=_params(("arbitrary",)),
    )(y, target)


CONV_TC = LANE


def _conv_taps(u, s):
    row = lax.broadcasted_iota(jnp.int32, u.shape, 0)
    prev = jnp.where(row == 0, 0.0, pltpu.roll(u, 1, 0))
    nxt = jnp.where(row == s - 1, 0.0, pltpu.roll(u, s - 1, 0))
    return prev, nxt


def _zcol(s, off):
    return pl.BlockSpec((s, CONV_TC), lambda j: (0, off // CONV_TC + j))


def conv_fwd(z, w3, *, name):
    s = z.shape[0]

    def body(cb_ref, cc_ref, ci_ref, w_ref, t_ref):
        u = cc_ref[...] * ci_ref[...]
        prev, nxt = _conv_taps(u, s)
        w = w_ref[...]
        conv = w[0:1] * prev + w[1:2] * u + w[2:3] * nxt
        t_ref[...] = (cb_ref[...] * conv).astype(t_ref.dtype)

    return pl.pallas_call(
        body, name=name, grid=(CONV_WIDTH // CONV_TC,),
        in_specs=[_zcol(s, OFF_CB), _zcol(s, OFF_CC), _zcol(s, OFF_CI), pl.BlockSpec((3, CONV_TC), lambda j: (0, j))],
        out_specs=pl.BlockSpec((s, CONV_TC), lambda j: (0, j)), out_shape=_sds((s, CONV_WIDTH), MXU_DTYPE),
        compiler_params=_params(("parallel",)),
    )(z, z, z, w3)


def conv_bwd(dt, z, w3, *, name):
    s = z.shape[0]

    def body(dt_ref, cb_ref, cc_ref, ci_ref, w_ref, dcb_ref, dcc_ref, dci_ref, dw_ref):
        cc, ci = cc_ref[...], ci_ref[...]
        u = cc * ci
        prev, nxt = _conv_taps(u, s)
        w = w_ref[...]
        dtv = dt_ref[...]
        dcb_ref[...] = (dtv * (w[0:1] * prev + w[1:2] * u + w[2:3] * nxt)).astype(dcb_ref.dtype)
        dconv = dtv * cb_ref[...]
        dprev, dnxt = _conv_taps(dconv, s)
        du = w[0:1] * dnxt + w[1:2] * dconv + w[2:3] * dprev
        dcc_ref[...] = (du * ci).astype(dcc_ref.dtype)
        dci_ref[...] = (du * cc).astype(dci_ref.dtype)
        dw_ref[0:1, :] = jnp.sum(dconv * prev, axis=0, keepdims=True)
        dw_ref[1:2, :] = jnp.sum(dconv * u, axis=0, keepdims=True)
        dw_ref[2:3, :] = jnp.sum(dconv * nxt, axis=0, keepdims=True)

    col = pl.BlockSpec((s, CONV_TC), lambda j: (0, j))
    wspec = pl.BlockSpec((3, CONV_TC), lambda j: (0, j))
    return pl.pallas_call(
        body, name=name, grid=(CONV_WIDTH // CONV_TC,),
        in_specs=[col, _zcol(s, OFF_CB), _zcol(s, OFF_CC), _zcol(s, OFF_CI), wspec],
        out_specs=[col, col, col, wspec],
        out_shape=[_sds((s, CONV_WIDTH), MXU_DTYPE)] * 3 + [_sds((3, CONV_WIDTH), F32)],
        compiler_params=_params(("parallel",)),
    )(dt, z, z, z, w3)


def rope_tables(s):
    n_freq = HEAD_DIM // 4
    t = jnp.arange(s, dtype=jnp.int32)
    inv_freq = ROPE_THETA ** (-jnp.arange(0, HEAD_DIM // 2, 2, dtype=F32) / (HEAD_DIM // 2))
    ang_r = (t // GRID_W).astype(F32)[:, None] * inv_freq
    ang_c = (t % GRID_W).astype(F32)[:, None] * inv_freq
    cos_t = jnp.concatenate([jnp.cos(ang_r)] * 2 + [jnp.cos(ang_c)] * 2, axis=1)
    sin_t = jnp.concatenate([-jnp.sin(ang_r), jnp.sin(ang_r), -jnp.sin(ang_c), jnp.sin(ang_c)], axis=1)
    assert cos_t.shape == (s, 4 * n_freq)
    return cos_t, sin_t


def _swap_halves(v):
    lane = lax.broadcasted_iota(jnp.int32, v.shape, 1)
    return jnp.where(lane % 64 < 32, pltpu.roll(v, HEAD_DIM - 32, 1), pltpu.roll(v, 32, 1))


def qk_prep_fwd(z, qn, kn, cos_t, sin_t, *, name):
    s = z.shape[0]
    ts = min(s, 256)

    def body(q_ref, k_ref, v_ref, qn_ref, kn_ref, c_ref, s_ref, qo_ref, ko_ref, vo_ref):
        cs, sn = c_ref[...], s_ref[...]

        def head(x, g, scale):
            n = x * _rstd(x) * g
            return (n * cs + _swap_halves(n) * sn) * scale

        for h in range(N_Q_HEADS):
            sl = slice(h * HEAD_DIM, (h + 1) * HEAD_DIM)
            qo_ref[:, sl] = head(q_ref[:, sl], qn_ref[...], ATTN_SCALE).astype(qo_ref.dtype)
        for h in range(N_KV_HEADS):
            sl = slice(h * HEAD_DIM, (h + 1) * HEAD_DIM)
            ko_ref[:, sl] = head(k_ref[:, sl], kn_ref[...], 1.0).astype(ko_ref.dtype)
        vo_ref[...] = v_ref[...].astype(vo_ref.dtype)

    tab = _rows(ts, HEAD_DIM)
    gsp = _fixed((1, HEAD_DIM))
    return pl.pallas_call(
        body, name=name, grid=(s // ts,),
        in_specs=[_rows(ts, ATTN_WIDTH, OFF_Q // ATTN_WIDTH), _rows(ts, KV_WIDTH, OFF_K // KV_WIDTH),
                  _rows(ts, KV_WIDTH, OFF_V // KV_WIDTH), gsp, gsp, tab, tab],
        out_specs=[_rows(ts, ATTN_WIDTH), _rows(ts, KV_WIDTH), _rows(ts, KV_WIDTH)],
        out_shape=[_sds((s, ATTN_WIDTH), MXU_DTYPE), _sds((s, KV_WIDTH), MXU_DTYPE), _sds((s, KV_WIDTH), MXU_DTYPE)],
        compiler_params=_params(("parallel",)),
    )(z, z, z, qn, kn, cos_t, sin_t)


def qk_prep_bwd(dqp, dkp, z, qn, kn, cos_t, sin_t, *, name):
    s = z.shape[0]
    ts = min(s, 256)
    nt = s // ts

    def body(dq_ref, dk_ref, q_ref, k_ref, qn_ref, kn_ref, c_ref, s_ref, dqo_ref, dko_ref, dqn_ref, dkn_ref,
             qacc_ref, kacc_ref):
        cs, sn = c_ref[...], s_ref[...]
        i = pl.program_id(0)

        @pl.when(i == 0)
        def _():
            qacc_ref[...] = jnp.zeros_like(qacc_ref)
            kacc_ref[...] = jnp.zeros_like(kacc_ref)

        def head(x, g, dout, scale):
            d = dout.astype(F32) * scale
            dn = d * cs + _swap_halves(d * sn)
            return _rms_bwd(x, g, dn)

        qacc = jnp.zeros((SUBLANE, HEAD_DIM), F32)
        for h in range(N_Q_HEADS):
            sl = slice(h * HEAD_DIM, (h + 1) * HEAD_DIM)
            dx, dg_rows = head(q_ref[:, sl], qn_ref[...], dq_ref[:, sl], ATTN_SCALE)
            dqo_ref[:, sl] = dx.astype(dqo_ref.dtype)
            qacc = qacc + _fold8(dg_rows)
        kacc = jnp.zeros((SUBLANE, HEAD_DIM), F32)
        for h in range(N_KV_HEADS):
            sl = slice(h * HEAD_DIM, (h + 1) * HEAD_DIM)
            dx, dg_rows = head(k_ref[:, sl], kn_ref[...], dk_ref[:, sl], 1.0)
            dko_ref[:, sl] = dx.astype(dko_ref.dtype)
            kacc = kacc + _fold8(dg_rows)
        qacc_ref[...] += qacc
        kacc_ref[...] += kacc

        @pl.when(i == nt - 1)
        def _():
            dqn_ref[...] = qacc_ref[...].sum(axis=0, keepdims=True)
            dkn_ref[...] = kacc_ref[...].sum(axis=0, keepdims=True)

    tab = _rows(ts, HEAD_DIM)
    gsp = _fixed((1, HEAD_DIM))
    return pl.pallas_call(
        body, name=name, grid=(nt,),
        in_specs=[_rows(ts, ATTN_WIDTH), _rows(ts, KV_WIDTH), _rows(ts, ATTN_WIDTH, OFF_Q // ATTN_WIDTH),
                  _rows(ts, KV_WIDTH, OFF_K // KV_WIDTH), gsp, gsp, tab, tab],
        out_specs=[_rows(ts, ATTN_WIDTH), _rows(ts, KV_WIDTH), gsp, gsp],
        out_shape=[_sds((s, ATTN_WIDTH), MXU_DTYPE), _sds((s, KV_WIDTH), MXU_DTYPE), _sds((1, HEAD_DIM), F32),
                   _sds((1, HEAD_DIM), F32)],
        scratch_shapes=[pltpu.VMEM((SUBLANE, HEAD_DIM), F32)] * 2, compiler_params=_params(("arbitrary",)),
    )(dqp, dkp, z, z, qn, kn, cos_t, sin_t)


_NT = (((1,), (1,)), ((), ()))
_GW = GROUP * HEAD_DIM


def _dot(a, b, dims=(((1,), (0,)), ((), ()))):
    return lax.dot_general(a, b, dims, preferred_element_type=F32)


def attn_fwd(qp, kp, vb, *, name):
    s = qp.shape[0]
    tq = min(s, 256)

    def body(q_ref, k_ref, v_ref, o_ref, lse_ref):
        k, v = k_ref[...], v_ref[...]
        for g in range(GROUP):
            sl = slice(g * HEAD_DIM, (g + 1) * HEAD_DIM)
            sc = _dot(q_ref[:, sl], k, _NT)
            mx = jnp.max(sc, axis=-1, keepdims=True)
            p = jnp.exp(sc - mx)
            den = jnp.sum(p, axis=-1, keepdims=True)
            o = _dot(p.astype(v.dtype), v)
            o_ref[:, sl] = (o / den).astype(o_ref.dtype)
            lse_ref[:, g:g + 1] = mx + jnp.log(den)

    return pl.pallas_call(
        body, name=name, grid=(N_KV_HEADS, s // tq),
        in_specs=[pl.BlockSpec((tq, _GW), lambda j, i: (i, j)), pl.BlockSpec((s, HEAD_DIM), lambda j, i: (0, j)),
                  pl.BlockSpec((s, HEAD_DIM), lambda j, i: (0, j))],
        out_specs=[pl.BlockSpec((tq, _GW), lambda j, i: (i, j)), pl.BlockSpec((None, tq, GROUP), lambda j, i: (j, i, 0))],
        out_shape=[_sds((s, ATTN_WIDTH), MXU_DTYPE), _sds((N_KV_HEADS, s, GROUP), F32)],
        compiler_params=_params(("parallel", "parallel")),
    )(qp, kp, vb)


def attn_bwd_q(qp, kp, vb, do, o, lse, *, name):
    s = qp.shape[0]
    tq = min(s, 256)

    def body(q_ref, k_ref, v_ref, do_ref, o_ref, lse_ref, dq_ref, dd_ref):
        k, v = k_ref[...], v_ref[...]
        for g in range(GROUP):
            sl = slice(g * HEAD_DIM, (g + 1) * HEAD_DIM)
            dog = do_ref[:, sl]
            dd = jnp.sum(dog.astype(F32) * o_ref[:, sl].astype(F32), axis=-1, keepdims=True)
            p = jnp.exp(_dot(q_ref[:, sl], k, _NT) - lse_ref[:, g:g + 1])
            ds = p * (_dot(dog, v, _NT) - dd)
            dq_ref[:, sl] = _dot(ds.astype(k.dtype), k)
            dd_ref[:, g:g + 1] = dd

    qspec = pl.BlockSpec((tq, _GW), lambda j, i: (i, j))
    kspec = pl.BlockSpec((s, HEAD_DIM), lambda j, i: (0, j))
    lspec = pl.BlockSpec((None, tq, GROUP), lambda j, i: (j, i, 0))
    return pl.pallas_call(
        body, name=name, grid=(N_KV_HEADS, s // tq), in_specs=[qspec, kspec, kspec, qspec, qspec, lspec],
        out_specs=[qspec, lspec], out_shape=[_sds((s, ATTN_WIDTH), F32), _sds((N_KV_HEADS, s, GROUP), F32)],
        compiler_params=_params(("parallel", "parallel")),
    )(qp, kp, vb, do, o, lse)


def attn_bwd_kv(qp, kp, vb, do, lse_t, dd_t, *, name):
    s = qp.shape[0]
    tk = min(s, 512)
    tqc = min(s, 512)

    def body(k_ref, v_ref, q_ref, do_ref, lse_ref, dd_ref, dk_ref, dv_ref):
        k, v = k_ref[...], v_ref[...]

        def chunk(c, carry):
            dk, dv = carry
            r0 = pl.multiple_of(c * tqc, tqc)
            for g in range(GROUP):
                sl = slice(g * HEAD_DIM, (g + 1) * HEAD_DIM)
                qg = q_ref[pl.ds(r0, tqc), sl]
                dog = do_ref[pl.ds(r0, tqc), sl]
                p_t = jnp.exp(_dot(k, qg, _NT) - lse_ref[g:g + 1, pl.ds(r0, tqc)])
                dv = dv + _dot(p_t.astype(dog.dtype), dog)
                ds_t = p_t * (_dot(v, dog, _NT) - dd_ref[g:g + 1, pl.ds(r0, tqc)])
                dk = dk + _dot(ds_t.astype(qg.dtype), qg)
            return dk, dv

        zero = jnp.zeros((tk, HEAD_DIM), F32)
        dk, dv = lax.fori_loop(0, s // tqc, chunk, (zero, zero))
        dk_ref[...] = dk
        dv_ref[...] = dv.astype(dv_ref.dtype)

    kspec = pl.BlockSpec((tk, HEAD_DIM), lambda j, t: (t, j))
    qspec = pl.BlockSpec((s, _GW), lambda j, t: (0, j))
    lspec = pl.BlockSpec((None, GROUP, s), lambda j, t: (j, 0, 0))
    return pl.pallas_call(
        body, name=name, grid=(N_KV_HEADS, s // tk), in_specs=[kspec, kspec, qspec, qspec, lspec, lspec],
        out_specs=[kspec, kspec], out_shape=[_sds((s, KV_WIDTH), F32), _sds((s, KV_WIDTH), MXU_DTYPE)],
        compiler_params=_params(("parallel", "parallel")),
    )(kp, vb, qp, do, lse_t, dd_t)


GATE_TW = 1024


def _gate_specs(ts):
    tile = pl.BlockSpec((ts, GATE_TW), lambda i, j: (i, j))
    ga = pl.BlockSpec((ts, GATE_TW), lambda i, j: (i, OFF_GA // GATE_TW + j))
    gb = pl.BlockSpec((ts, GATE_TW), lambda i, j: (i, OFF_GB // GATE_TW + j))
    ba = pl.BlockSpec((1, GATE_TW), lambda i, j: (0, j))
    bb = pl.BlockSpec((1, GATE_TW), lambda i, j: (0, D_MODEL // GATE_TW + j))
    return tile, ga, gb, ba, bb


def gate_fwd(ya, yb, z, bias, *, name):
    s = z.shape[0]
    ts = min(s, 512)

    def body(ya_ref, yb_ref, ga_ref, gb_ref, ba_ref, bb_ref, o_ref):
        sa = jax.nn.sigmoid(ga_ref[...] + ba_ref[...])
        sb = jax.nn.sigmoid(gb_ref[...] + bb_ref[...])
        o_ref[...] = (sa * ya_ref[...] + sb * yb_ref[...]).astype(o_ref.dtype)

    tile, ga, gb, ba, bb = _gate_specs(ts)
    return pl.pallas_call(
        body, name=name, grid=(s // ts, D_MODEL // GATE_TW), in_specs=[tile, tile, ga, gb, ba, bb], out_specs=tile,
        out_shape=_sds((s, D_MODEL), MXU_DTYPE), compiler_params=_params(("parallel", "parallel")),
    )(ya, yb, z, z, bias, bias)


def gate_bwd(dmi, ya, yb, z, bias, *, name):
    s = z.shape[0]
    ts = min(s, 512)
    nt = s // ts

    def body(d_ref, ya_ref, yb_ref, ga_ref, gb_ref, ba_ref, bb_ref, dya_ref, dyb_ref, dga_ref, dgb_ref, dba_ref,
             dbb_ref, acc_a, acc_b):
        d = d_ref[...]
        sa = jax.nn.sigmoid(ga_ref[...] + ba_ref[...])
        sb = jax.nn.sigmoid(gb_ref[...] + bb_ref[...])
        dya_ref[...] = (d * sa).astype(dya_ref.dtype)
        dyb_ref[...] = (d * sb).astype(dyb_ref.dtype)
        dga = d * ya_ref[...] * sa * (1.0 - sa)
        dgb = d * yb_ref[...] * sb * (1.0 - sb)
        dga_ref[...] = dga.astype(dga_ref.dtype)
        dgb_ref[...] = dgb.astype(dgb_ref.dtype)
        i = pl.program_id(1)

        @pl.when(i == 0)
        def _():
            acc_a[...] = jnp.zeros_like(acc_a)
            acc_b[...] = jnp.zeros_like(acc_b)

        acc_a[...] += _fold8(dga)
        acc_b[...] += _fold8(dgb)

        @pl.when(i == nt - 1)
        def _():
            dba_ref[...] = acc_a[...].sum(axis=0, keepdims=True)
            dbb_ref[...] = acc_b[...].sum(axis=0, keepdims=True)

    tile = pl.BlockSpec((ts, GATE_TW), lambda j, i: (i, j))
    ga = pl.BlockSpec((ts, GATE_TW), lambda j, i: (i, OFF_GA // GATE_TW + j))
    gb = pl.BlockSpec((ts, GATE_TW), lambda j, i: (i, OFF_GB // GATE_TW + j))
    ba = pl.BlockSpec((1, GATE_TW), lambda j, i: (0, j))
    bb = pl.BlockSpec((1, GATE_TW), lambda j, i: (0, D_MODEL // GATE_TW + j))
    acc = pltpu.VMEM((SUBLANE, GATE_TW), F32)
    return pl.pallas_call(
        body, name=name, grid=(D_MODEL // GATE_TW, nt), in_specs=[tile, tile, tile, ga, gb, ba, bb],
        out_specs=[tile, tile, tile, tile, ba, ba],
        out_shape=[_sds((s, D_MODEL), MXU_DTYPE)] * 4 + [_sds((1, D_MODEL), F32)] * 2,
        scratch_shapes=[acc, acc], compiler_params=_params(("parallel", "arbitrary")),
    )(dmi, ya, yb, z, z, bias, bias)


def _place():
    x, y, c = lax.axis_index("x"), lax.axis_index("y"), lax.axis_index("c")
    return x, y, c


def _window(ref, shard_shape, axis, d):
    _, r, c = shard_shape
    if axis == 1:
        return ref.at[:, pl.ds(pl.multiple_of(d * r, SUBLANE), r), :]
    return ref.at[:, :, pl.ds(pl.multiple_of(d * c, LANE), c)]


def all_gather(shards, axes, *, name):
    n = len(shards)
    shapes = [a.shape for a in shards]

    def body(*refs):
        ins, outs = refs[:n], refs[n:2 * n]
        send_sems, recv_sems, local_sems = refs[2 * n:]
        x, y, c = _place()
        sibling = (x, y, 1 - c)
        chips = [(1 - x, y), (x, 1 - y), (1 - x, 1 - y)]

        def win(a, px, py, pc):
            return _window(outs[a], shapes[a], axes[a], 4 * px + 2 * py + pc)

        def copy(a, k, block, to, src=None):
            return pltpu.make_async_remote_copy(
                src_ref=win(a, *block) if src is None else src, dst_ref=win(a, *block),
                send_sem=send_sems.at[a, k], recv_sem=recv_sems.at[a, k], device_id=to, device_id_type=MESH)

        mine = [pltpu.make_async_copy(ins[a], win(a, x, y, c), local_sems.at[a]) for a in range(n)]
        for cp in mine:
            cp.start()
        first = []
        for a in range(n):
            first.append(copy(a, 0, (x, y, c), sibling, src=ins[a]))
            first += [copy(a, 1 + j, (x, y, c), (*chip, c), src=ins[a]) for j, chip in enumerate(chips)]
        for cp in first:
            cp.start()
        passed = []
        for j, chip in enumerate(chips):
            for a in range(n):
                copy(a, 1 + j, (*chip, c), (x, y, c)).wait_recv()
                fwd = copy(a, 4 + j, (*chip, c), sibling)
                fwd.start()
                passed.append(fwd)
        for a in range(n):
            copy(a, 0, (x, y, 1 - c), (x, y, c)).wait_recv()
            for j, chip in enumerate(chips):
                copy(a, 4 + j, (*chip, 1 - c), (x, y, c)).wait_recv()
        for cp in first + passed:
            cp.wait_send()
        for cp in mine:
            cp.wait()

    out_shape = [
        _sds((s[0], s[1] * N_DEV, s[2]) if ax == 1 else (s[0], s[1], s[2] * N_DEV), a.dtype)
        for a, s, ax in zip(shards, shapes, axes)
    ]
    return pl.pallas_call(
        body, name=name, in_specs=[ANY] * n, out_specs=[ANY] * n, out_shape=out_shape,
        scratch_shapes=[pltpu.SemaphoreType.DMA((n, 7)), pltpu.SemaphoreType.DMA((n, 7)), pltpu.SemaphoreType.DMA((n,))],
    )(*shards)


def pair_exchange(grads, shard_shapes, axes, *, name):
    n = len(grads)

    def body(*refs):
        ins, outs = refs[:n], refs[n:2 * n]
        send_sems, recv_sems = refs[2 * n:]
        x, y, c = _place()
        sibling = (x, y, 1 - c)
        copies = []
        for a in range(n):
            for q in range(N_CHIP):
                copies.append(pltpu.make_async_remote_copy(
                    src_ref=_window(ins[a], shard_shapes[a], axes[a], 2 * q + (1 - c)), dst_ref=outs[a].at[q],
                    send_sem=send_sems.at[a, q], recv_sem=recv_sems.at[a, q], device_id=sibling, device_id_type=MESH))
        for cp in copies:
            cp.start()
        for cp in copies:
            cp.wait_recv()
        for cp in copies:
            cp.wait_send()

    return pl.pallas_call(
        body, name=name, in_specs=[ANY] * n, out_specs=[ANY] * n,
        out_shape=[_sds((N_CHIP, *s), g.dtype) for g, s in zip(grads, shard_shapes)],
        scratch_shapes=[pltpu.SemaphoreType.DMA((n, N_CHIP)), pltpu.SemaphoreType.DMA((n, N_CHIP))],
    )(*grads)


def pair_sum(grad, recv, shard_shape, axis, core, *, name):
    lx, r, c = shard_shape
    tr = min(r, 256)
    nr = r // tr
    if axis == 1:
        gspec = pl.BlockSpec((None, tr, c), lambda q, l, i, cref: (l, (2 * q + cref[0]) * nr + i, 0))
    else:
        gspec = pl.BlockSpec((None, tr, c), lambda q, l, i, cref: (l, i, 2 * q + cref[0]))
    rspec = pl.BlockSpec((None, None, tr, c), lambda q, l, i, cref: (q, l, i, 0))

    def body(c_ref, g_ref, r_ref, o_ref):
        o_ref[...] = (g_ref[...].astype(F32) + r_ref[...].astype(F32)).astype(o_ref.dtype)

    return pl.pallas_call(
        body, name=name,
        grid_spec=pltpu.PrefetchScalarGridSpec(num_scalar_prefetch=1, grid=(N_CHIP, lx, nr), in_specs=[gspec, rspec],
                                               out_specs=rspec),
        out_shape=_sds((N_CHIP, lx, r, c), recv.dtype), compiler_params=_params(("parallel",) * 3),
    )(core, grad, recv)


def chip_exchange(sums, *, name):
    n = len(sums)

    def body(*refs):
        ins, outs = refs[:n], refs[n:2 * n]
        send_sems, recv_sems, local_sems = refs[2 * n:]
        x, y, c = _place()
        chips = [(1 - x, y), (x, 1 - y), (1 - x, 1 - y)]
        my_chip = 2 * x + y
        mine = [pltpu.make_async_copy(ins[a].at[my_chip], outs[a].at[my_chip], local_sems.at[a]) for a in range(n)]
        for cp in mine:
            cp.start()
        sends, recvs = [], []
        for a in range(n):
            for j, (px, py) in enumerate(chips):
                their_chip = 2 * px + py
                sends.append(pltpu.make_async_remote_copy(
                    src_ref=ins[a].at[their_chip], dst_ref=outs[a].at[my_chip], send_sem=send_sems.at[a, j],
                    recv_sem=recv_sems.at[a, j], device_id=(px, py, c), device_id_type=MESH))
                recvs.append(pltpu.make_async_remote_copy(
                    src_ref=ins[a].at[my_chip], dst_ref=outs[a].at[their_chip], send_sem=send_sems.at[a, j],
                    recv_sem=recv_sems.at[a, j], device_id=(px, py, c), device_id_type=MESH))
        for cp in sends:
            cp.start()
        for cp in recvs:
            cp.wait_recv()
        for cp in sends:
            cp.wait_send()
        for cp in mine:
            cp.wait()

    return pl.pallas_call(
        body, name=name, in_specs=[ANY] * n, out_specs=[ANY] * n, out_shape=[_sds(a.shape, a.dtype) for a in sums],
        scratch_shapes=[pltpu.SemaphoreType.DMA((n, 3)), pltpu.SemaphoreType.DMA((n, 3)), pltpu.SemaphoreType.DMA((n,))],
    )(*sums)


def all_reduce_small(part, *, name):
    r, c = part.shape

    def body(p_ref, o_ref, gath_ref, send_sems, recv_sems):
        x, y, cc = _place()
        me = 4 * x + 2 * y + cc
        gath_ref[me] = p_ref[...]
        peers = [(1 - x if k & 4 else x, 1 - y if k & 2 else y, 1 - cc if k & 1 else cc) for k in range(1, N_DEV)]
        copies = [
            pltpu.make_async_remote_copy(
                src_ref=p_ref, dst_ref=gath_ref.at[me], send_sem=send_sems.at[i], recv_sem=recv_sems.at[i],
                device_id=peer, device_id_type=MESH)
            for i, peer in enumerate(peers)]
        for cp in copies:
            cp.start()
        for i, (px, py, pc) in enumerate(peers):
            pltpu.make_async_remote_copy(
                src_ref=p_ref, dst_ref=gath_ref.at[4 * px + 2 * py + pc], send_sem=send_sems.at[i],
                recv_sem=recv_sems.at[i], device_id=(x, y, cc), device_id_type=MESH).wait_recv()
        for cp in copies:
            cp.wait_send()
        acc = gath_ref[0]
        for d in range(1, N_DEV):
            acc = acc + gath_ref[d]
        o_ref[...] = acc

    vm = pl.BlockSpec(memory_space=pltpu.VMEM)
    return pl.pallas_call(
        body, name=name, in_specs=[vm], out_specs=vm, out_shape=_sds((r, c), F32),
        scratch_shapes=[pltpu.VMEM((N_DEV, r, c), F32), pltpu.SemaphoreType.DMA((N_DEV - 1,)),
                        pltpu.SemaphoreType.DMA((N_DEV - 1,))],
    )(part)


def _adamw(w, g, m, v):
    m = ADAM_B1 * m + (1.0 - ADAM_B1) * g
    v = ADAM_B2 * v + (1.0 - ADAM_B2) * (g * g)
    m_hat = m / (1.0 - ADAM_B1 ** ADAM_STEP)
    v_hat = v / (1.0 - ADAM_B2 ** ADAM_STEP)
    delta = -ADAM_LR * (m_hat / (jnp.sqrt(v_hat) + ADAM_EPS) + ADAM_WD * w)
    return delta, m, v


def reduce_adam(parts, w, m, v, *, name):
    lx, r, c = w.shape
    tr = min(r, 128)

    def body(p_ref, w_ref, m_ref, v_ref, g_out, d_out, m_out, v_out):
        g = p_ref[0].astype(F32)
        for q in range(1, N_CHIP):
            g = g + p_ref[q].astype(F32)
        d, mn, vn = _adamw(w_ref[...], g, m_ref[...], v_ref[...])
        g_out[...] = g
        d_out[...] = d
        m_out[...] = mn
        v_out[...] = vn

    spec = pl.BlockSpec((None, tr, c), lambda l, i: (l, i, 0))
    pspec = pl.BlockSpec((N_CHIP, None, tr, c), lambda l, i: (0, l, i, 0))
    return pl.pallas_call(
        body, name=name, grid=(lx, r // tr), in_specs=[pspec, spec, spec, spec], out_specs=[spec] * 4,
        out_shape=[_sds(w.shape, F32)] * 4, compiler_params=_params(("parallel", "parallel")),
    )(parts, w, m, v)


def adam_small(g, w, m, v, *, name):
    def body(g_ref, w_ref, m_ref, v_ref, d_out, m_out, v_out):
        d, mn, vn = _adamw(w_ref[...], g_ref[...], m_ref[...], v_ref[...])
        d_out[...] = d
        m_out[...] = mn
        v_out[...] = vn

    return pl.pallas_call(body, name=name, out_shape=[_sds(w.shape, F32)] * 3)(g, w, m, v)


def cast_shard(w, *, name):
    lx, r, c = w.shape
    tr = min(r, 256)

    def body(w_ref, o_ref):
        o_ref[...] = w_ref[...].astype(o_ref.dtype)

    spec = pl.BlockSpec((None, tr, c), lambda l, i: (l, i, 0))
    return pl.pallas_call(
        body, name=name, grid=(lx, r // tr), in_specs=[spec], out_specs=spec, out_shape=_sds(w.shape, MXU_DTYPE),
        compiler_params=_params(("parallel", "parallel")),
    )(w)


BIG = ("w_in", "w_out_conv", "w_out_attn", "w_merge", "w_up", "w_down")
BIG_AXIS = {"w_in": 2, "w_out_conv": 1, "w_out_attn": 1, "w_merge": 1, "w_up": 2, "w_down": 1}
SMALL = (("norm_mix_pre", D_MODEL), ("gate_bias", 2 * D_MODEL), ("norm_mix_post", D_MODEL), ("norm_mlp_pre", D_MODEL),
         ("norm_mlp_post", D_MODEL), ("q_norm", HEAD_DIM), ("k_norm", HEAD_DIM))
SMALL_WIDTH = sum(w for _, w in SMALL)
WEIGHTS = ("norm_mix_pre", "w_in", "gate_bias", "conv_w", "q_norm", "k_norm", "w_out_conv", "w_out_attn", "w_merge",
           "norm_mix_post", "norm_mlp_pre", "w_up", "w_down", "norm_mlp_post")


def kernel(x, norm_mix_pre, w_in, gate_bias, conv_w, q_norm, k_norm, w_out_conv, w_out_attn, w_merge, norm_mix_post, norm_mlp_pre, w_up, w_down, norm_mlp_post, loss_target, m_norm_mix_pre, m_w_in, m_gate_bias, m_conv_w, m_q_norm, m_k_norm, m_w_out_conv, m_w_out_attn, m_w_merge, m_norm_mix_post, m_norm_mlp_pre, m_w_up, m_w_down, m_norm_mlp_post, v_norm_mix_pre, v_w_in, v_gate_bias, v_conv_w, v_q_norm, v_k_norm, v_w_out_conv, v_w_out_attn, v_w_merge, v_norm_mix_post, v_norm_mlp_pre, v_w_up, v_w_down, v_norm_mlp_post):
    w = dict(norm_mix_pre=norm_mix_pre, w_in=w_in, gate_bias=gate_bias, conv_w=conv_w, q_norm=q_norm, k_norm=k_norm,
             w_out_conv=w_out_conv, w_out_attn=w_out_attn, w_merge=w_merge, norm_mix_post=norm_mix_post,
             norm_mlp_pre=norm_mlp_pre, w_up=w_up, w_down=w_down, norm_mlp_post=norm_mlp_post)
    mom = dict(norm_mix_pre=m_norm_mix_pre, w_in=m_w_in, gate_bias=m_gate_bias, conv_w=m_conv_w, q_norm=m_q_norm,
               k_norm=m_k_norm, w_out_conv=m_w_out_conv, w_out_attn=m_w_out_attn, w_merge=m_w_merge,
               norm_mix_post=m_norm_mix_post, norm_mlp_pre=m_norm_mlp_pre, w_up=m_w_up, w_down=m_w_down,
               norm_mlp_post=m_norm_mlp_post)
    var = dict(norm_mix_pre=v_norm_mix_pre, w_in=v_w_in, gate_bias=v_gate_bias, conv_w=v_conv_w, q_norm=v_q_norm,
               k_norm=v_k_norm, w_out_conv=v_w_out_conv, w_out_attn=v_w_out_attn, w_merge=v_w_merge,
               norm_mix_post=v_norm_mix_post, norm_mlp_pre=v_norm_mlp_pre, w_up=v_w_up, w_down=v_w_down,
               norm_mlp_post=v_norm_mlp_post)
    depth = w_in.shape[0]
    s = x.shape[1]
    xs = x.reshape(s, D_MODEL)
    target = loss_target.reshape(s, D_MODEL)
    x_idx, y_idx, c_idx = _place()
    core = jnp.reshape(c_idx, (1,)).astype(jnp.int32)
    cos_t, sin_t = rope_tables(s)

    shards = [cast_shard(w[k], name="cast_" + k) for k in BIG] + [conv_w]
    axes = [BIG_AXIS[k] for k in BIG] + [2]
    full = dict(zip(BIG + ("conv_w",), all_gather(shards, axes, name="gather_weights")))

    def vec(name, l):
        return w[name][l].reshape(1, -1)

    saved = []
    h = rms_fwd(xs, vec("norm_mix_pre", 0), name="rms_first")
    for l in range(depth):
        z = matmul(h, full["w_in"], mode="nn", m=s, n=IN_WIDTH, k=D_MODEL, b_layer=l, out_dtypes=[F32], name="mm_in")
        t = conv_fwd(z, full["conv_w"][l], name="conv_fwd")
        qp, kp, vb = qk_prep_fwd(z, vec("q_norm", l), vec("k_norm", l), cos_t, sin_t, name="qk_fwd")
        o, lse = attn_fwd(qp, kp, vb, name="attn_fwd")
        ya = matmul(t, full["w_out_conv"], mode="nn", m=s, n=D_MODEL, k=CONV_WIDTH, b_layer=l, out_dtypes=[F32],
                    name="mm_out_conv")
        yb = matmul(o, full["w_out_attn"], mode="nn", m=s, n=D_MODEL, k=ATTN_WIDTH, b_layer=l, out_dtypes=[F32],
                    name="mm_out_attn")
        mi = gate_fwd(ya, yb, z, vec("gate_bias", l), name="gate_fwd")
        mixed = matmul(mi, full["w_merge"], mode="nn", m=s, n=D_MODEL, k=D_MODEL, b_layer=l, out_dtypes=[F32],
                       name="mm_merge")
        x_mid, h2 = rms_residual_fwd(xs, mixed, vec("norm_mix_post", l), vec("norm_mlp_pre", l), name="res_mix")
        act, r = matmul(h2, full["w_up"], mode="nn", m=s, n=D_FF, k=D_MODEL, b_layer=l,
                        out_dtypes=[MXU_DTYPE, MXU_DTYPE], name="mm_up",
                        epilogue=lambda acc: (acc, jnp.square(jnp.maximum(acc, 0.0))))
        f = matmul(r, full["w_down"], mode="nn", m=s, n=D_MODEL, k=D_FF, b_layer=l, out_dtypes=[F32], name="mm_down")
        g_next = vec("norm_mix_pre", l + 1) if l + 1 < depth else None
        x_out, h_next = rms_residual_fwd(x_mid, f, vec("norm_mlp_post", l), g_next,
                                         name="res_mlp" if g_next is not None else "res_last")
        saved.append(dict(x_in=xs, h=h, z=z, t=t, qp=qp, kp=kp, vb=vb, o=o, lse=lse, ya=ya, yb=yb, mi=mi, mixed=mixed,
                          x_mid=x_mid, h2=h2, act=act, r=r, f=f))
        xs, h = x_out, h_next

    dx, loss_part = loss_and_grad(xs, target, name="loss")
    loss = lax.psum(jnp.sum(loss_part), ("x", "y", "c"))

    grads = {k: lax.empty((depth,) + ((D_MODEL, IN_WIDTH) if k == "w_in" else (D_MODEL, D_FF) if k == "w_up" else
                                     (D_FF, D_MODEL) if k == "w_down" else (D_MODEL, D_MODEL)), COMM_DTYPE) for k in BIG}
    small_rows = [None] * depth
    conv_rows = [None] * depth

    def wgrad(key, lhs, rhs, m, n, l):
        grads[key] = matmul(lhs, rhs, mode="tn", m=m, n=n, k=s, out_dtypes=[COMM_DTYPE], name="wg_" + key,
                            out_into=(grads[key], l))

    for l in reversed(range(depth)):
        sv = saved[l]
        df, dg_mlp_post = rms_bwd(sv["f"], vec("norm_mlp_post", l), dx, None, out_dtype=MXU_DTYPE, name="rmsb_mlp_post")
        da = matmul(df, full["w_down"], mode="nt", m=s, n=D_FF, k=D_MODEL, b_layer=l, out_dtypes=[MXU_DTYPE],
                    name="mm_d_down", extra=(sv["act"],),
                    epilogue=lambda acc, a: (acc * (2.0 * jnp.maximum(a.astype(F32), 0.0)),))
        wgrad("w_down", sv["r"], df, D_FF, D_MODEL, l)
        dh2 = matmul(da, full["w_up"], mode="nt", m=s, n=D_MODEL, k=D_FF, b_layer=l, out_dtypes=[F32], name="mm_d_up")
        wgrad("w_up", sv["h2"], da, D_MODEL, D_FF, l)
        dx_mid, dg_mlp_pre = rms_bwd(sv["x_mid"], vec("norm_mlp_pre", l), dh2, dx, out_dtype=F32, name="rmsb_mlp_pre")
        dmixed, dg_mix_post = rms_bwd(sv["mixed"], vec("norm_mix_post", l), dx_mid, None, out_dtype=MXU_DTYPE,
                                      name="rmsb_mix_post")
        dmi = matmul(dmixed, full["w_merge"], mode="nt", m=s, n=D_MODEL, k=D_MODEL, b_layer=l, out_dtypes=[F32],
                     name="mm_d_merge")
        wgrad("w_merge", sv["mi"], dmixed, D_MODEL, D_MODEL, l)
        dya, dyb, dga, dgb, dba, dbb = gate_bwd(dmi, sv["ya"], sv["yb"], sv["z"], vec("gate_bias", l), name="gate_bwd")
        dt = matmul(dya, full["w_out_conv"], mode="nt", m=s, n=CONV_WIDTH, k=D_MODEL, b_layer=l, out_dtypes=[F32],
                    name="mm_d_out_conv")
        wgrad("w_out_conv", sv["t"], dya, CONV_WIDTH, D_MODEL, l)
        do = matmul(dyb, full["w_out_attn"], mode="nt", m=s, n=ATTN_WIDTH, k=D_MODEL, b_layer=l, out_dtypes=[MXU_DTYPE],
                    name="mm_d_out_attn")
        wgrad("w_out_attn", sv["o"], dyb, ATTN_WIDTH, D_MODEL, l)
        dcb, dcc, dci, dconv_w = conv_bwd(dt, sv["z"], full["conv_w"][l], name="conv_bwd")
        dqp, dd = attn_bwd_q(sv["qp"], sv["kp"], sv["vb"], do, sv["o"], sv["lse"], name="attn_bwd_q")
        dkp, dv = attn_bwd_kv(sv["qp"], sv["kp"], sv["vb"], do, jnp.swapaxes(sv["lse"], 1, 2), jnp.swapaxes(dd, 1, 2),
                              name="attn_bwd_kv")
        dq, dk, dqn, dkn = qk_prep_bwd(dqp, dkp, sv["z"], vec("q_norm", l), vec("k_norm", l), cos_t, sin_t, name="qk_bwd")
        dz = jnp.concatenate([dcb, dcc, dci, dq, dk, dv, dga, dgb], axis=1)
        dh = matmul(dz, full["w_in"], mode="nt", m=s, n=D_MODEL, k=IN_WIDTH, tk=1024, b_layer=l, out_dtypes=[F32],
                    name="mm_d_in")
        wgrad("w_in", sv["h"], dz, D_MODEL, IN_WIDTH, l)
        dx, dg_mix_pre = rms_bwd(sv["x_in"], vec("norm_mix_pre", l), dh, dx_mid, out_dtype=F32, name="rmsb_mix_pre")
        small_rows[l] = jnp.concatenate([dg_mix_pre, dba, dbb, dg_mix_post, dg_mlp_pre, dg_mlp_post, dqn, dkn], axis=1)
        conv_rows[l] = dconv_w.reshape(1, 3 * CONV_WIDTH)

    grad_x = dx.reshape(1, s, D_MODEL)

    small_part = jnp.concatenate([jnp.concatenate(small_rows, axis=0), jnp.concatenate(conv_rows, axis=0)], axis=1)
    small_sum = all_reduce_small(small_part, name="allreduce_small")
    out_g, out_d, out_m, out_v = {}, {}, {}, {}
    pack = lambda src: jnp.concatenate([src[k].reshape(depth, wd) for k, wd in SMALL], axis=1)
    g_small = small_sum[:, :SMALL_WIDTH]
    d_small, m_small, v_small = adam_small(g_small, pack(w), pack(mom), pack(var), name="adam_small")
    off = 0
    for k, wd in SMALL:
        for dst, src in ((out_g, g_small), (out_d, d_small), (out_m, m_small), (out_v, v_small)):
            dst[k] = src[:, off:off + wd]
        off += wd
    me = 4 * x_idx + 2 * y_idx + c_idx
    cshard = CONV_WIDTH // N_DEV
    g_conv = lax.dynamic_slice_in_dim(small_sum[:, SMALL_WIDTH:].reshape(depth, 3, CONV_WIDTH), me * cshard, cshard, axis=2)
    g_conv2 = g_conv.reshape(depth, 3 * cshard)
    flat = lambda a: a.reshape(depth, 3 * cshard)
    d_conv, m_conv, v_conv = adam_small(g_conv2, flat(conv_w), flat(m_conv_w), flat(v_conv_w), name="adam_conv")
    for dst, src in ((out_g, g_conv2), (out_d, d_conv), (out_m, m_conv), (out_v, v_conv)):
        dst["conv_w"] = src.reshape(depth, 3, cshard)

    shard_shapes = [w[k].shape for k in BIG]
    big_axes = [BIG_AXIS[k] for k in BIG]
    from_sibling = pair_exchange([grads[k] for k in BIG], shard_shapes, big_axes, name="rs_pair")
    pair_sums = [pair_sum(grads[k], rcv, shp, ax, core, name="pair_sum_" + k)
                 for k, rcv, shp, ax in zip(BIG, from_sibling, shard_shapes, big_axes)]
    chip_parts = chip_exchange(pair_sums, name="rs_chips")
    for k, parts in zip(BIG, chip_parts):
        out_g[k], out_d[k], out_m[k], out_v[k] = reduce_adam(parts, w[k], mom[k], var[k], name="adam_" + k)

    return (loss, grad_x, *[out_g[k] for k in WEIGHTS], *[out_d[k] for k in WEIGHTS], *[out_m[k] for k in WEIGHTS],
            *[out_v[k] for k in WEIGHTS])
```

```python
import math

import jax
import jax.numpy as jnp
from jax import lax
from jax.experimental import pallas as pl
from jax.experimental.pallas import tpu as pltpu

F32 = jnp.float32
MXU_DTYPE = jnp.bfloat16
COMM_DTYPE = jnp.bfloat16

D_MODEL = 2048
HEAD_DIM = 128
N_Q_HEADS = 16
N_KV_HEADS = 4
GROUP = N_Q_HEADS // N_KV_HEADS
ATTN_WIDTH = N_Q_HEADS * HEAD_DIM
KV_WIDTH = N_KV_HEADS * HEAD_DIM
CONV_WIDTH = D_MODEL
D_FF = 4 * D_MODEL
GRID_W = 64
ROPE_THETA = 10000.0
RMS_EPS = 1e-6
IN_WIDTH = 3 * CONV_WIDTH + ATTN_WIDTH + 2 * KV_WIDTH + 2 * D_MODEL
OFF_CB, OFF_CC, OFF_CI = 0, CONV_WIDTH, 2 * CONV_WIDTH
OFF_Q = 3 * CONV_WIDTH
OFF_K = OFF_Q + ATTN_WIDTH
OFF_V = OFF_K + KV_WIDTH
OFF_GA = OFF_V + KV_WIDTH
OFF_GB = OFF_GA + D_MODEL
ATTN_SCALE = 1.0 / math.sqrt(HEAD_DIM)

ADAM_LR, ADAM_B1, ADAM_B2, ADAM_EPS, ADAM_WD, ADAM_STEP = 0.001, 0.9, 0.999, 1e-08, 0.01, 10

N_DEV = 8
N_CHIP = 4
LANE = 128
SUBLANE = 8
VMEM_LIMIT = 48 * 1024 * 1024
MESH = pl.DeviceIdType.MESH
ANY = pl.BlockSpec(memory_space=pl.ANY)


def _sds(shape, dtype):
    return jax.ShapeDtypeStruct(tuple(shape), dtype)


def _params(sem, vmem=VMEM_LIMIT):
    return pltpu.CompilerParams(dimension_semantics=sem, vmem_limit_bytes=vmem)


def _rows(ts, w, col=0):
    return pl.BlockSpec((ts, w), lambda i: (i, col))


def _fixed(shape):
    return pl.BlockSpec(shape, lambda *_: (0,) * len(shape))


def _fold8(v):
    ts, w = v.shape
    return v.reshape(ts // SUBLANE, SUBLANE, w).sum(axis=0)


def matmul(a, b, *, mode, m, n, k, name, out_dtypes, tm=1024, tn=1024, tk=2048, epilogue=None, extra=(), after=()):
    tm, tn, tk = min(tm, m), min(tn, n), min(tk, k)
    nm, nn, nk = m // tm, n // tn, k // tk
    assert nm * tm == m and nn * tn == n and nk * tk == k, (name, m, n, k, tm, tn, tk)
    if mode == "tn":
        a_spec = pl.BlockSpec((tk, tm), lambda i, j, kk: (kk, i))
        dims = (((0,), (0,)), ((), ()))
    else:
        a_spec = pl.BlockSpec((tm, tk), lambda i, j, kk: (i, kk))
        dims = (((1,), (1 if mode == "nt" else 0,)), ((), ()))
    if mode == "nt":
        b_spec = pl.BlockSpec((tn, tk), lambda i, j, kk: (j, kk))
    else:
        b_spec = pl.BlockSpec((tk, tn), lambda i, j, kk: (kk, j))
    tile = pl.BlockSpec((tm, tn), lambda i, j, kk: (i, j))
    n_out, n_extra, n_after = len(out_dtypes), len(extra), len(after)
    if epilogue is None:
        epilogue = lambda acc: (acc,)

    def body(a_ref, b_ref, *rest):
        extra_refs = rest[:n_extra]
        outs = rest[n_extra + n_after:][:n_out]
        part = lax.dot_general(a_ref[...].astype(MXU_DTYPE), b_ref[...].astype(MXU_DTYPE), dims,
                               preferred_element_type=F32)

        def finish(acc):
            for o_ref, val in zip(outs, epilogue(acc, *[r[...] for r in extra_refs])):
                o_ref[...] = val.astype(o_ref.dtype)

        if nk == 1:
            finish(part)
        else:
            acc_ref = rest[-1]
            kk = pl.program_id(2)

            @pl.when(kk == 0)
            def _():
                acc_ref[...] = part

            @pl.when(kk > 0)
            def _():
                acc_ref[...] += part

            @pl.when(kk == nk - 1)
            def _():
                finish(acc_ref[...])

    res = pl.pallas_call(
        body, name=name, grid=(nm, nn, nk), in_specs=[a_spec, b_spec] + [tile] * n_extra + [ANY] * n_after,
        out_specs=[tile] * n_out, out_shape=[_sds((m, n), dt) for dt in out_dtypes],
        scratch_shapes=[pltpu.VMEM((tm, tn), F32)] if nk > 1 else [],
        compiler_params=_params(("parallel", "parallel", "arbitrary")),
    )(a, b, *extra, *after)
    return res[0] if n_out == 1 else res


def _rstd(x):
    return lax.rsqrt(jnp.mean(x * x, axis=-1, keepdims=True) + RMS_EPS)


def _rms_bwd(x, g, dy):
    rstd = _rstd(x)
    xh = x * rstd
    gy = dy * g
    dx = rstd * (gy - xh * jnp.mean(gy * xh, axis=-1, keepdims=True))
    return dx, dy * xh


def rms_fwd(x, g, *, name):
    s = x.shape[0]
    ts = min(s, 512)

    def body(x_ref, g_ref, h_ref):
        xv = x_ref[...]
        h_ref[...] = (xv * _rstd(xv) * g_ref[...]).astype(h_ref.dtype)

    return pl.pallas_call(
        body, name=name, grid=(s // ts,), in_specs=[_rows(ts, D_MODEL), _fixed((1, D_MODEL))],
        out_specs=_rows(ts, D_MODEL), out_shape=_sds((s, D_MODEL), MXU_DTYPE), compiler_params=_params(("parallel",)),
    )(x, g)


def rms_residual_fwd(x, y, g_post, g_next, *, name):
    s = x.shape[0]
    ts = min(s, 512)
    with_next = g_next is not None

    def body(x_ref, y_ref, gp_ref, *rest):
        yv = y_ref[...]
        xn = x_ref[...] + yv * _rstd(yv) * gp_ref[...]
        if with_next:
            gn_ref, xo_ref, h_ref = rest
            h_ref[...] = (xn * _rstd(xn) * gn_ref[...]).astype(h_ref.dtype)
        else:
            (xo_ref,) = rest
        xo_ref[...] = xn

    gspec = _fixed((1, D_MODEL))
    res = pl.pallas_call(
        body, name=name, grid=(s // ts,),
        in_specs=[_rows(ts, D_MODEL), _rows(ts, D_MODEL), gspec] + [gspec] * with_next,
        out_specs=[_rows(ts, D_MODEL)] * (1 + with_next),
        out_shape=[_sds((s, D_MODEL), F32)] + [_sds((s, D_MODEL), MXU_DTYPE)] * with_next,
        compiler_params=_params(("parallel",)),
    )(x, y, g_post, *([g_next] if with_next else []))
    return (res[0], res[1]) if with_next else (res[0], None)


def rms_bwd(x, g, dy, residual, *, out_dtype, name, after=()):
    s = x.shape[0]
    ts = min(s, 256)
    nt = s // ts
    with_res = residual is not None

    def body(x_ref, g_ref, dy_ref, *rest):
        dx_ref, dg_ref, acc_ref = rest[-3:]
        dx, dg_rows = _rms_bwd(x_ref[...], g_ref[...], dy_ref[...].astype(F32))
        if with_res:
            dx = dx + rest[0][...]
        dx_ref[...] = dx.astype(dx_ref.dtype)
        i = pl.program_id(0)

        @pl.when(i == 0)
        def _():
            acc_ref[...] = jnp.zeros_like(acc_ref)

        acc_ref[...] += _fold8(dg_rows)

        @pl.when(i == nt - 1)
        def _():
            dg_ref[...] = acc_ref[...].sum(axis=0, keepdims=True)

    return pl.pallas_call(
        body, name=name, grid=(nt,),
        in_specs=[_rows(ts, D_MODEL), _fixed((1, D_MODEL)), _rows(ts, D_MODEL)] + [_rows(ts, D_MODEL)] * with_res
        + [ANY] * len(after),
        out_specs=[_rows(ts, D_MODEL), _fixed((1, D_MODEL))],
        out_shape=[_sds((s, D_MODEL), out_dtype), _sds((1, D_MODEL), F32)],
        scratch_shapes=[pltpu.VMEM((SUBLANE, D_MODEL), F32)], compiler_params=_params(("arbitrary",)),
    )(x, g, dy, *([residual] if with_res else []), *after)


def loss_and_grad(y, target, *, name):
    s = y.shape[0]
    ts = min(s, 512)
    nt = s // ts

    def body(y_ref, t_ref, dy_ref, part_ref):
        e = y_ref[...] - t_ref[...]
        dy_ref[...] = e * (1.0 / D_MODEL)
        sq = _fold8(e * e)
        lanes = sq[:, 0:LANE]
        for j in range(1, D_MODEL // LANE):
            lanes = lanes + sq[:, j * LANE:(j + 1) * LANE]
        i = pl.program_id(0)

        @pl.when(i == 0)
        def _():
            part_ref[...] = jnp.zeros_like(part_ref)

        part_ref[...] += lanes * (0.5 / D_MODEL)

    return pl.pallas_call(
        body, name=name, grid=(nt,), in_specs=[_rows(ts, D_MODEL), _rows(ts, D_MODEL)],
        out_specs=[_rows(ts, D_MODEL), _fixed((SUBLANE, LANE))],
        out_shape=[_sds((s, D_MODEL), F32), _sds((SUBLANE, LANE), F32)], compiler_params=_params(("arbitrary",)),
    )(y, target)


CONV_TC = LANE


def _conv_taps(u, s):
    row = lax.broadcasted_iota(jnp.int32, u.shape, 0)
    prev = jnp.where(row == 0, 0.0, pltpu.roll(u, 1, 0))
    nxt = jnp.where(row == s - 1, 0.0, pltpu.roll(u, s - 1, 0))
    return prev, nxt


def _zcol(s, off):
    return pl.BlockSpec((s, CONV_TC), lambda j: (0, off // CONV_TC + j))


def conv_fwd(z, w3, *, name):
    s = z.shape[0]

    def body(cb_ref, cc_ref, ci_ref, w_ref, t_ref):
        u = cc_ref[...] * ci_ref[...]
        prev, nxt = _conv_taps(u, s)
        w = w_ref[...]
        conv = w[0:1] * prev + w[1:2] * u + w[2:3] * nxt
        t_ref[...] = (cb_ref[...] * conv).astype(t_ref.dtype)

    return pl.pallas_call(
        body, name=name, grid=(CONV_WIDTH // CONV_TC,),
        in_specs=[_zcol(s, OFF_CB), _zcol(s, OFF_CC), _zcol(s, OFF_CI), pl.BlockSpec((3, CONV_TC), lambda j: (0, j))],
        out_specs=pl.BlockSpec((s, CONV_TC), lambda j: (0, j)), out_shape=_sds((s, CONV_WIDTH), MXU_DTYPE),
        compiler_params=_params(("parallel",)),
    )(z, z, z, w3)


def conv_bwd(dt, z, w3, *, name):
    s = z.shape[0]

    def body(dt_ref, cb_ref, cc_ref, ci_ref, w_ref, dcb_ref, dcc_ref, dci_ref, dw_ref):
        cc, ci = cc_ref[...], ci_ref[...]
        u = cc * ci
        prev, nxt = _conv_taps(u, s)
        w = w_ref[...]
        dtv = dt_ref[...]
        dcb_ref[...] = (dtv * (w[0:1] * prev + w[1:2] * u + w[2:3] * nxt)).astype(dcb_ref.dtype)
        dconv = dtv * cb_ref[...]
        dprev, dnxt = _conv_taps(dconv, s)
        du = w[0:1] * dnxt + w[1:2] * dconv + w[2:3] * dprev
        dcc_ref[...] = (du * ci).astype(dcc_ref.dtype)
        dci_ref[...] = (du * cc).astype(dci_ref.dtype)
        dw_ref[0:1, :] = jnp.sum(dconv * prev, axis=0, keepdims=True)
        dw_ref[1:2, :] = jnp.sum(dconv * u, axis=0, keepdims=True)
        dw_ref[2:3, :] = jnp.sum(dconv * nxt, axis=0, keepdims=True)

    col = pl.BlockSpec((s, CONV_TC), lambda j: (0, j))
    wspec = pl.BlockSpec((3, CONV_TC), lambda j: (0, j))
    return pl.pallas_call(
        body, name=name, grid=(CONV_WIDTH // CONV_TC,),
        in_specs=[col, _zcol(s, OFF_CB), _zcol(s, OFF_CC), _zcol(s, OFF_CI), wspec],
        out_specs=[col, col, col, wspec],
        out_shape=[_sds((s, CONV_WIDTH), MXU_DTYPE)] * 3 + [_sds((3, CONV_WIDTH), F32)],
        compiler_params=_params(("parallel",)),
    )(dt, z, z, z, w3)


def rope_tables(s):
    n_freq = HEAD_DIM // 4
    t = jnp.arange(s, dtype=jnp.int32)
    inv_freq = ROPE_THETA ** (-jnp.arange(0, HEAD_DIM // 2, 2, dtype=F32) / (HEAD_DIM // 2))
    ang_r = (t // GRID_W).astype(F32)[:, None] * inv_freq
    ang_c = (t % GRID_W).astype(F32)[:, None] * inv_freq
    cos_t = jnp.concatenate([jnp.cos(ang_r)] * 2 + [jnp.cos(ang_c)] * 2, axis=1)
    sin_t = jnp.concatenate([-jnp.sin(ang_r), jnp.sin(ang_r), -jnp.sin(ang_c), jnp.sin(ang_c)], axis=1)
    assert cos_t.shape == (s, 4 * n_freq)
    return cos_t, sin_t


def _swap_halves(v):
    lane = lax.broadcasted_iota(jnp.int32, v.shape, 1)
    return jnp.where(lane % 64 < 32, pltpu.roll(v, HEAD_DIM - 32, 1), pltpu.roll(v, 32, 1))


def qk_prep_fwd(z, qn, kn, cos_t, sin_t, *, name):
    s = z.shape[0]
    ts = min(s, 256)

    def body(q_ref, k_ref, v_ref, qn_ref, kn_ref, c_ref, s_ref, qo_ref, ko_ref, vo_ref):
        cs, sn = c_ref[...], s_ref[...]

        def head(x, g, scale):
            n = x * _rstd(x) * g
            return (n * cs + _swap_halves(n) * sn) * scale

        for h in range(N_Q_HEADS):
            sl = slice(h * HEAD_DIM, (h + 1) * HEAD_DIM)
            qo_ref[:, sl] = head(q_ref[:, sl], qn_ref[...], ATTN_SCALE).astype(qo_ref.dtype)
        for h in range(N_KV_HEADS):
            sl = slice(h * HEAD_DIM, (h + 1) * HEAD_DIM)
            ko_ref[:, sl] = head(k_ref[:, sl], kn_ref[...], 1.0).astype(ko_ref.dtype)
        vo_ref[...] = v_ref[...].astype(vo_ref.dtype)

    tab = _rows(ts, HEAD_DIM)
    gsp = _fixed((1, HEAD_DIM))
    return pl.pallas_call(
        body, name=name, grid=(s // ts,),
        in_specs=[_rows(ts, ATTN_WIDTH, OFF_Q // ATTN_WIDTH), _rows(ts, KV_WIDTH, OFF_K // KV_WIDTH),
                  _rows(ts, KV_WIDTH, OFF_V // KV_WIDTH), gsp, gsp, tab, tab],
        out_specs=[_rows(ts, ATTN_WIDTH), _rows(ts, KV_WIDTH), _rows(ts, KV_WIDTH)],
        out_shape=[_sds((s, ATTN_WIDTH), MXU_DTYPE), _sds((s, KV_WIDTH), MXU_DTYPE), _sds((s, KV_WIDTH), MXU_DTYPE)],
        compiler_params=_params(("parallel",)),
    )(z, z, z, qn, kn, cos_t, sin_t)


def qk_prep_bwd(dqp, dkp, z, qn, kn, cos_t, sin_t, *, name):
    s = z.shape[0]
    ts = min(s, 256)
    nt = s // ts

    def body(dq_ref, dk_ref, q_ref, k_ref, qn_ref, kn_ref, c_ref, s_ref, dqo_ref, dko_ref, dqn_ref, dkn_ref,
             qacc_ref, kacc_ref):
        cs, sn = c_ref[...], s_ref[...]
        i = pl.program_id(0)

        @pl.when(i == 0)
        def _():
            qacc_ref[...] = jnp.zeros_like(qacc_ref)
            kacc_ref[...] = jnp.zeros_like(kacc_ref)

        def head(x, g, dout, scale):
            d = dout.astype(F32) * scale
            dn = d * cs + _swap_halves(d * sn)
            return _rms_bwd(x, g, dn)

        qacc = jnp.zeros((SUBLANE, HEAD_DIM), F32)
        for h in range(N_Q_HEADS):
            sl = slice(h * HEAD_DIM, (h + 1) * HEAD_DIM)
            dx, dg_rows = head(q_ref[:, sl], qn_ref[...], dq_ref[:, sl], ATTN_SCALE)
            dqo_ref[:, sl] = dx.astype(dqo_ref.dtype)
            qacc = qacc + _fold8(dg_rows)
        kacc = jnp.zeros((SUBLANE, HEAD_DIM), F32)
        for h in range(N_KV_HEADS):
            sl = slice(h * HEAD_DIM, (h + 1) * HEAD_DIM)
            dx, dg_rows = head(k_ref[:, sl], kn_ref[...], dk_ref[:, sl], 1.0)
            dko_ref[:, sl] = dx.astype(dko_ref.dtype)
            kacc = kacc + _fold8(dg_rows)
        qacc_ref[...] += qacc
        kacc_ref[...] += kacc

        @pl.when(i == nt - 1)
        def _():
            dqn_ref[...] = qacc_ref[...].sum(axis=0, keepdims=True)
            dkn_ref[...] = kacc_ref[...].sum(axis=0, keepdims=True)

    tab = _rows(ts, HEAD_DIM)
    gsp = _fixed((1, HEAD_DIM))
    return pl.pallas_call(
        body, name=name, grid=(nt,),
        in_specs=[_rows(ts, ATTN_WIDTH), _rows(ts, KV_WIDTH), _rows(ts, ATTN_WIDTH, OFF_Q // ATTN_WIDTH),
                  _rows(ts, KV_WIDTH, OFF_K // KV_WIDTH), gsp, gsp, tab, tab],
        out_specs=[_rows(ts, ATTN_WIDTH), _rows(ts, KV_WIDTH), gsp, gsp],
        out_shape=[_sds((s, ATTN_WIDTH), MXU_DTYPE), _sds((s, KV_WIDTH), MXU_DTYPE), _sds((1, HEAD_DIM), F32),
                   _sds((1, HEAD_DIM), F32)],
        scratch_shapes=[pltpu.VMEM((SUBLANE, HEAD_DIM), F32)] * 2, compiler_params=_params(("arbitrary",)),
    )(dqp, dkp, z, z, qn, kn, cos_t, sin_t)


_NT = (((1,), (1,)), ((), ()))
_GW = GROUP * HEAD_DIM


def _dot(a, b, dims=(((1,), (0,)), ((), ()))):
    return lax.dot_general(a, b, dims, preferred_element_type=F32)


def attn_fwd(qp, kp, vb, *, name):
    s = qp.shape[0]
    tq = min(s, 256)

    def body(q_ref, k_ref, v_ref, o_ref, lse_ref):
        k, v = k_ref[...], v_ref[...]
        for g in range(GROUP):
            sl = slice(g * HEAD_DIM, (g + 1) * HEAD_DIM)
            sc = _dot(q_ref[:, sl], k, _NT)
            mx = jnp.max(sc, axis=-1, keepdims=True)
            p = jnp.exp(sc - mx)
            den = jnp.sum(p, axis=-1, keepdims=True)
            o = _dot(p.astype(v.dtype), v)
            o_ref[:, sl] = (o / den).astype(o_ref.dtype)
            lse_ref[:, g:g + 1] = mx + jnp.log(den)

    return pl.pallas_call(
        body, name=name, grid=(N_KV_HEADS, s // tq),
        in_specs=[pl.BlockSpec((tq, _GW), lambda j, i: (i, j)), pl.BlockSpec((s, HEAD_DIM), lambda j, i: (0, j)),
                  pl.BlockSpec((s, HEAD_DIM), lambda j, i: (0, j))],
        out_specs=[pl.BlockSpec((tq, _GW), lambda j, i: (i, j)), pl.BlockSpec((None, tq, GROUP), lambda j, i: (j, i, 0))],
        out_shape=[_sds((s, ATTN_WIDTH), MXU_DTYPE), _sds((N_KV_HEADS, s, GROUP), F32)],
        compiler_params=_params(("parallel", "parallel")),
    )(qp, kp, vb)


def attn_bwd_q(qp, kp, vb, do, o, lse, *, name):
    s = qp.shape[0]
    tq = min(s, 256)

    def body(q_ref, k_ref, v_ref, do_ref, o_ref, lse_ref, dq_ref, dd_ref):
        k, v = k_ref[...], v_ref[...]
        for g in range(GROUP):
            sl = slice(g * HEAD_DIM, (g + 1) * HEAD_DIM)
            dog = do_ref[:, sl]
            dd = jnp.sum(dog.astype(F32) * o_ref[:, sl].astype(F32), axis=-1, keepdims=True)
            p = jnp.exp(_dot(q_ref[:, sl], k, _NT) - lse_ref[:, g:g + 1])
            ds = p * (_dot(dog, v, _NT) - dd)
            dq_ref[:, sl] = _dot(ds.astype(k.dtype), k)
            dd_ref[:, g:g + 1] = dd

    qspec = pl.BlockSpec((tq, _GW), lambda j, i: (i, j))
    kspec = pl.BlockSpec((s, HEAD_DIM), lambda j, i: (0, j))
    lspec = pl.BlockSpec((None, tq, GROUP), lambda j, i: (j, i, 0))
    return pl.pallas_call(
        body, name=name, grid=(N_KV_HEADS, s // tq), in_specs=[qspec, kspec, kspec, qspec, qspec, lspec],
        out_specs=[qspec, lspec], out_shape=[_sds((s, ATTN_WIDTH), F32), _sds((N_KV_HEADS, s, GROUP), F32)],
        compiler_params=_params(("parallel", "parallel")),
    )(qp, kp, vb, do, o, lse)


def attn_bwd_kv(qp, kp, vb, do, lse_t, dd_t, *, name):
    s = qp.shape[0]
    tk = min(s, 512)
    tqc = min(s, 512)

    def body(k_ref, v_ref, q_ref, do_ref, lse_ref, dd_ref, dk_ref, dv_ref):
        k, v = k_ref[...], v_ref[...]

        def chunk(c, carry):
            dk, dv = carry
            r0 = pl.multiple_of(c * tqc, tqc)
            for g in range(GROUP):
                sl = slice(g * HEAD_DIM, (g + 1) * HEAD_DIM)
                qg = q_ref[pl.ds(r0, tqc), sl]
                dog = do_ref[pl.ds(r0, tqc), sl]
                p_t = jnp.exp(_dot(k, qg, _NT) - lse_ref[g:g + 1, pl.ds(r0, tqc)])
                dv = dv + _dot(p_t.astype(dog.dtype), dog)
                ds_t = p_t * (_dot(v, dog, _NT) - dd_ref[g:g + 1, pl.ds(r0, tqc)])
                dk = dk + _dot(ds_t.astype(qg.dtype), qg)
            return dk, dv

        zero = jnp.zeros((tk, HEAD_DIM), F32)
        dk, dv = lax.fori_loop(0, s // tqc, chunk, (zero, zero))
        dk_ref[...] = dk
        dv_ref[...] = dv.astype(dv_ref.dtype)

    kspec = pl.BlockSpec((tk, HEAD_DIM), lambda j, t: (t, j))
    qspec = pl.BlockSpec((s, _GW), lambda j, t: (0, j))
    lspec = pl.BlockSpec((None, GROUP, s), lambda j, t: (j, 0, 0))
    return pl.pallas_call(
        body, name=name, grid=(N_KV_HEADS, s // tk), in_specs=[kspec, kspec, qspec, qspec, lspec, lspec],
        out_specs=[kspec, kspec], out_shape=[_sds((s, KV_WIDTH), F32), _sds((s, KV_WIDTH), MXU_DTYPE)],
        compiler_params=_params(("parallel", "parallel")),
    )(kp, vb, qp, do, lse_t, dd_t)


GATE_TW = 1024


def gate_fwd(ya, yb, z, bias, *, name):
    s = z.shape[0]
    ts = min(s, 512)

    def body(ya_ref, yb_ref, ga_ref, gb_ref, ba_ref, bb_ref, o_ref):
        sa = jax.nn.sigmoid(ga_ref[...] + ba_ref[...])
        sb = jax.nn.sigmoid(gb_ref[...] + bb_ref[...])
        o_ref[...] = (sa * ya_ref[...] + sb * yb_ref[...]).astype(o_ref.dtype)

    tile = pl.BlockSpec((ts, GATE_TW), lambda i, j: (i, j))
    ga = pl.BlockSpec((ts, GATE_TW), lambda i, j: (i, OFF_GA // GATE_TW + j))
    gb = pl.BlockSpec((ts, GATE_TW), lambda i, j: (i, OFF_GB // GATE_TW + j))
    ba = pl.BlockSpec((1, GATE_TW), lambda i, j: (0, j))
    bb = pl.BlockSpec((1, GATE_TW), lambda i, j: (0, D_MODEL // GATE_TW + j))
    return pl.pallas_call(
        body, name=name, grid=(s // ts, D_MODEL // GATE_TW), in_specs=[tile, tile, ga, gb, ba, bb], out_specs=tile,
        out_shape=_sds((s, D_MODEL), MXU_DTYPE), compiler_params=_params(("parallel", "parallel")),
    )(ya, yb, z, z, bias, bias)


def gate_bwd(dmi, ya, yb, z, bias, *, name):
    s = z.shape[0]
    ts = min(s, 512)
    nt = s // ts

    def body(d_ref, ya_ref, yb_ref, ga_ref, gb_ref, ba_ref, bb_ref, dya_ref, dyb_ref, dga_ref, dgb_ref, dba_ref,
             dbb_ref, acc_a, acc_b):
        d = d_ref[...]
        sa = jax.nn.sigmoid(ga_ref[...] + ba_ref[...])
        sb = jax.nn.sigmoid(gb_ref[...] + bb_ref[...])
        dya_ref[...] = (d * sa).astype(dya_ref.dtype)
        dyb_ref[...] = (d * sb).astype(dyb_ref.dtype)
        dga = d * ya_ref[...] * sa * (1.0 - sa)
        dgb = d * yb_ref[...] * sb * (1.0 - sb)
        dga_ref[...] = dga.astype(dga_ref.dtype)
        dgb_ref[...] = dgb.astype(dgb_ref.dtype)
        i = pl.program_id(1)

        @pl.when(i == 0)
        def _():
            acc_a[...] = jnp.zeros_like(acc_a)
            acc_b[...] = jnp.zeros_like(acc_b)

        acc_a[...] += _fold8(dga)
        acc_b[...] += _fold8(dgb)

        @pl.when(i == nt - 1)
        def _():
            dba_ref[...] = acc_a[...].sum(axis=0, keepdims=True)
            dbb_ref[...] = acc_b[...].sum(axis=0, keepdims=True)

    tile = pl.BlockSpec((ts, GATE_TW), lambda j, i: (i, j))
    ga = pl.BlockSpec((ts, GATE_TW), lambda j, i: (i, OFF_GA // GATE_TW + j))
    gb = pl.BlockSpec((ts, GATE_TW), lambda j, i: (i, OFF_GB // GATE_TW + j))
    ba = pl.BlockSpec((1, GATE_TW), lambda j, i: (0, j))
    bb = pl.BlockSpec((1, GATE_TW), lambda j, i: (0, D_MODEL // GATE_TW + j))
    acc = pltpu.VMEM((SUBLANE, GATE_TW), F32)
    return pl.pallas_call(
        body, name=name, grid=(D_MODEL // GATE_TW, nt), in_specs=[tile, tile, tile, ga, gb, ba, bb],
        out_specs=[tile, tile, tile, tile, ba, ba],
        out_shape=[_sds((s, D_MODEL), MXU_DTYPE)] * 4 + [_sds((1, D_MODEL), F32)] * 2,
        scratch_shapes=[acc, acc], compiler_params=_params(("parallel", "arbitrary")),
    )(dmi, ya, yb, z, z, bias, bias)


HBM = pl.BlockSpec(memory_space=pltpu.HBM)
SEM = pl.BlockSpec(memory_space=pltpu.SEMAPHORE)
EFFECT = pltpu.SideEffectType.DATAFLOW_SIDE_EFFECTING
TOKEN = (SUBLANE, LANE)


def _place():
    return lax.axis_index("x"), lax.axis_index("y"), lax.axis_index("c")


def _window(ref, shard_shape, axis, d):
    r, c = shard_shape
    lead = (slice(None),) * (len(ref.shape) - 2)
    if axis == 0:
        return ref.at[lead + (pl.ds(pl.multiple_of(d * r, SUBLANE), r), slice(None))]
    return ref.at[lead + (slice(None), pl.ds(pl.multiple_of(d * c, LANE), c))]


def _hbm(a):
    return pltpu.with_memory_space_constraint(a, pltpu.HBM)


def _remote(src, dst, send_sems, recv_sems, i, to):
    return pltpu.make_async_remote_copy(src_ref=src, dst_ref=dst, send_sem=send_sems.at[i], recv_sem=recv_sems.at[i],
                                        device_id=to, device_id_type=MESH)


def cast_into_full(w, layer, axis, me, *, name):
    _, r, c = w.shape
    tr = min(r, 256)
    nr = r // tr
    in_spec = pl.BlockSpec((None, tr, c), lambda i, me_ref: (layer, i, 0))
    if axis == 0:
        out_spec = pl.BlockSpec((tr, c), lambda i, me_ref: (me_ref[0] * nr + i, 0))
        shape = (N_DEV * r, c)
    else:
        out_spec = pl.BlockSpec((tr, c), lambda i, me_ref: (i, me_ref[0]))
        shape = (r, N_DEV * c)

    def body(me_ref, w_ref, o_ref):
        o_ref[...] = w_ref[...].astype(o_ref.dtype)

    return pl.pallas_call(
        body, name=name,
        grid_spec=pltpu.PrefetchScalarGridSpec(num_scalar_prefetch=1, grid=(nr,), in_specs=[in_spec], out_specs=out_spec),
        out_shape=_sds(shape, MXU_DTYPE), compiler_params=_params(("parallel",)),
    )(me, w)


class _GatherPlan:
    def __init__(self, fulls, shard_shapes, axes):
        x, y, c = _place()
        self.n = len(fulls)
        self.me, self.sibling = (x, y, c), (x, y, 1 - c)
        self.chips = [(1 - x, y), (x, 1 - y), (1 - x, 1 - y)]
        self.win = lambda a, p: _window(fulls[a], shard_shapes[a], axes[a], 4 * p[0] + 2 * p[1] + p[2])

    def first(self, send_sems, recv_sems):
        out = []
        for a in range(self.n):
            mine = self.win(a, self.me)
            out.append(_remote(mine, mine, send_sems, recv_sems, 4 * a, self.sibling))
            out += [_remote(mine, mine, send_sems, recv_sems, 4 * a + 1 + j, (*chip, self.me[2]))
                    for j, chip in enumerate(self.chips)]
        return out

    def first_arrivals(self, send_sems, recv_sems):
        c = self.me[2]
        out = []
        for a in range(self.n):
            blocks = [self.sibling] + [(*chip, c) for chip in self.chips]
            out += [_remote(self.win(a, b), self.win(a, b), send_sems, recv_sems, 4 * a + k, self.me)
                    for k, b in enumerate(blocks)]
        return out

    def passed(self, send_sems, recv_sems):
        c = self.me[2]
        return [_remote(self.win(a, (*chip, c)), self.win(a, (*chip, c)), send_sems, recv_sems, 3 * a + j, self.sibling)
                for a in range(self.n) for j, chip in enumerate(self.chips)]

    def passed_arrivals(self, send_sems, recv_sems):
        c = self.me[2]
        return [_remote(self.win(a, (*chip, 1 - c)), self.win(a, (*chip, 1 - c)), send_sems, recv_sems, 3 * a + j, self.me)
                for a in range(self.n) for j, chip in enumerate(self.chips)]


def gather_start(fulls, after, shard_shapes, axes, *, name):
    n = len(fulls)

    def body(*refs):
        ins, send_sems, recv_sems, token = refs[:n], refs[n + 1], refs[n + 2], refs[-1]
        for cp in _GatherPlan(ins, shard_shapes, axes).first(send_sems, recv_sems):
            cp.start()
        token[...] = jnp.zeros_like(token)

    res = pl.pallas_call(
        body, name=name, in_specs=[HBM] * n + [ANY],
        out_shape=(pltpu.SemaphoreType.DMA((4 * n,)), pltpu.SemaphoreType.DMA((4 * n,)),
                   *[pltpu.HBM(f.shape, f.dtype) for f in fulls], _sds(TOKEN, F32)),
        out_specs=(SEM, SEM, *[HBM] * n, pl.BlockSpec(memory_space=pltpu.VMEM)),
        input_output_aliases={a: 2 + a for a in range(n)}, compiler_params=pltpu.CompilerParams(has_side_effects=EFFECT),
    )(*[_hbm(f) for f in fulls], after)
    return res[0], res[1], list(res[2:2 + n]), res[-1]


def gather_pass(send_sems, recv_sems, fulls, after, shard_shapes, axes, *, name):
    n = len(fulls)

    def body(*refs):
        ins, s1, r1 = refs[:n], refs[n], refs[n + 1]
        s2, r2, token = refs[n + 3], refs[n + 4], refs[-1]
        plan = _GatherPlan(ins, shard_shapes, axes)
        for cp in plan.first_arrivals(s1, r1):
            cp.wait_recv()
        for cp in plan.first(s1, r1):
            cp.wait_send()
        for cp in plan.passed(s2, r2):
            cp.start()
        token[...] = jnp.zeros_like(token)

    res = pl.pallas_call(
        body, name=name, in_specs=[HBM] * n + [SEM, SEM, ANY],
        out_shape=(pltpu.SemaphoreType.DMA((3 * n,)), pltpu.SemaphoreType.DMA((3 * n,)),
                   *[pltpu.HBM(f.shape, f.dtype) for f in fulls], _sds(TOKEN, F32)),
        out_specs=(SEM, SEM, *[HBM] * n, pl.BlockSpec(memory_space=pltpu.VMEM)),
        input_output_aliases={a: 2 + a for a in range(n)}, compiler_params=pltpu.CompilerParams(has_side_effects=EFFECT),
    )(*fulls, send_sems, recv_sems, after)
    return res[0], res[1], list(res[2:2 + n]), res[-1]


def gather_wait(send_sems, recv_sems, fulls, after, shard_shapes, axes, *, name):
    n = len(fulls)

    def body(*refs):
        ins, s2, r2 = refs[:n], refs[n], refs[n + 1]
        plan = _GatherPlan(ins, shard_shapes, axes)
        for cp in plan.passed_arrivals(s2, r2):
            cp.wait_recv()
        for cp in plan.passed(s2, r2):
            cp.wait_send()

    return list(pl.pallas_call(
        body, name=name, in_specs=[HBM] * n + [SEM, SEM, ANY], out_shape=tuple(pltpu.HBM(f.shape, f.dtype) for f in fulls),
        out_specs=tuple([HBM] * n), input_output_aliases={a: a for a in range(n)},
        compiler_params=pltpu.CompilerParams(has_side_effects=EFFECT),
    )(*fulls, send_sems, recv_sems, after))


def _pair_copies(grads, lands, shard_shapes, axes, send_sems, recv_sems):
    x, y, c = _place()
    return [_remote(_window(grads[a], shard_shapes[a], axes[a], 2 * q + (1 - c)), lands[a].at[q], send_sems, recv_sems,
                    N_CHIP * a + q, (x, y, 1 - c))
            for a in range(len(grads)) for q in range(N_CHIP)]


def _chip_copies(sums, lands, send_sems, recv_sems):
    x, y, c = _place()
    my_chip = 2 * x + y
    sends, arrivals = [], []
    for a in range(len(sums)):
        for j, (px, py) in enumerate([(1 - x, y), (x, 1 - y), (1 - x, 1 - y)]):
            sends.append(_remote(sums[a].at[2 * px + py], lands[a].at[my_chip], send_sems, recv_sems, 3 * a + j, (px, py, c)))
            arrivals.append(_remote(sums[a].at[my_chip], lands[a].at[2 * px + py], send_sems, recv_sems, 3 * a + j, (x, y, c)))
    return sends, arrivals


def exchange_start(srcs, after, land_shapes, make_sends, per_array, *, name):
    n = len(srcs)
    lands = [lax.empty(s, a.dtype) for s, a in zip(land_shapes, srcs)]

    def body(*refs):
        ins, zones = refs[:n], refs[n:2 * n]
        send_sems, recv_sems, token = refs[2 * n + 1], refs[2 * n + 2], refs[-1]
        for cp in make_sends(ins, zones, send_sems, recv_sems):
            cp.start()
        token[...] = jnp.zeros_like(token)

    res = pl.pallas_call(
        body, name=name, in_specs=[HBM] * (2 * n) + [ANY],
        out_shape=(pltpu.SemaphoreType.DMA((per_array * n,)), pltpu.SemaphoreType.DMA((per_array * n,)),
                   *[pltpu.HBM(a.shape, a.dtype) for a in srcs], *[pltpu.HBM(a.shape, a.dtype) for a in lands],
                   _sds(TOKEN, F32)),
        out_specs=(SEM, SEM, *[HBM] * (2 * n), pl.BlockSpec(memory_space=pltpu.VMEM)),
        input_output_aliases={a: 2 + a for a in range(2 * n)}, compiler_params=pltpu.CompilerParams(has_side_effects=EFFECT),
    )(*[_hbm(a) for a in srcs], *[_hbm(a) for a in lands], after)
    return res[0], res[1], list(res[2:2 + n]), list(res[2 + n:2 + 2 * n]), res[-1]


def exchange_wait(send_sems, recv_sems, srcs, lands, after, make_waits, *, name):
    n = len(srcs)

    def body(*refs):
        ins, zones, s, r = refs[:n], refs[n:2 * n], refs[2 * n], refs[2 * n + 1]
        sends, arrivals = make_waits(ins, zones, s, r)
        for cp in arrivals:
            cp.wait_recv()
        for cp in sends:
            cp.wait_send()

    res = pl.pallas_call(
        body, name=name, in_specs=[HBM] * (2 * n) + [SEM, SEM, ANY],
        out_shape=tuple(pltpu.HBM(a.shape, a.dtype) for a in (*srcs, *lands)), out_specs=tuple([HBM] * (2 * n)),
        input_output_aliases={a: a for a in range(2 * n)}, compiler_params=pltpu.CompilerParams(has_side_effects=EFFECT),
    )(*srcs, *lands, send_sems, recv_sems, after)
    return list(res[:n]), list(res[n:])


def gather_conv_taps(shard, *, name):
    lx, r, c = shard.shape

    def body(in_ref, out_ref, send_sems, recv_sems, local_sem):
        plan = _GatherPlan([out_ref], [(r, c)], [1])
        mine = pltpu.make_async_copy(in_ref, plan.win(0, plan.me), local_sem)
        mine.start()
        mine.wait()
        first = plan.first(send_sems.at[0], recv_sems.at[0])
        for cp in first:
            cp.start()
        arrivals = plan.first_arrivals(send_sems.at[0], recv_sems.at[0])
        for cp in arrivals[1:]:
            cp.wait_recv()
        passed = plan.passed(send_sems.at[1], recv_sems.at[1])
        for cp in passed:
            cp.start()
        arrivals[0].wait_recv()
        for cp in plan.passed_arrivals(send_sems.at[1], recv_sems.at[1]):
            cp.wait_recv()
        for cp in first + passed:
            cp.wait_send()

    return pl.pallas_call(
        body, name=name, in_specs=[ANY], out_specs=ANY, out_shape=_sds((lx, r, N_DEV * c), shard.dtype),
        scratch_shapes=[pltpu.SemaphoreType.DMA((2, 4)), pltpu.SemaphoreType.DMA((2, 4)), pltpu.SemaphoreType.DMA(())],
    )(shard)


def pair_sum(grad, recv, shard_shape, axis, core, *, name):
    r, c = shard_shape
    tr = min(r, 256)
    nr = r // tr
    if axis == 0:
        gspec = pl.BlockSpec((tr, c), lambda q, i, cref: ((2 * q + cref[0]) * nr + i, 0))
    else:
        gspec = pl.BlockSpec((tr, c), lambda q, i, cref: (i, 2 * q + cref[0]))
    rspec = pl.BlockSpec((None, tr, c), lambda q, i, cref: (q, i, 0))

    def body(c_ref, g_ref, r_ref, o_ref):
        o_ref[...] = (g_ref[...].astype(F32) + r_ref[...].astype(F32)).astype(o_ref.dtype)

    return pl.pallas_call(
        body, name=name,
        grid_spec=pltpu.PrefetchScalarGridSpec(num_scalar_prefetch=1, grid=(N_CHIP, nr), in_specs=[gspec, rspec],
                                               out_specs=rspec),
        out_shape=_sds((N_CHIP, r, c), recv.dtype), compiler_params=_params(("parallel", "parallel")),
    )(core, grad, recv)


def all_reduce_small(part, *, name):
    r, c = part.shape

    def body(p_ref, o_ref, gath_ref, send_sems, recv_sems):
        x, y, cc = _place()
        me = 4 * x + 2 * y + cc
        gath_ref[me] = p_ref[...]
        peers = [(1 - x if k & 4 else x, 1 - y if k & 2 else y, 1 - cc if k & 1 else cc) for k in range(1, N_DEV)]
        copies = [_remote(p_ref, gath_ref.at[me], send_sems, recv_sems, i, peer) for i, peer in enumerate(peers)]
        for cp in copies:
            cp.start()
        for i, (px, py, pc) in enumerate(peers):
            _remote(p_ref, gath_ref.at[4 * px + 2 * py + pc], send_sems, recv_sems, i, (x, y, cc)).wait_recv()
        for cp in copies:
            cp.wait_send()
        acc = gath_ref[0]
        for d in range(1, N_DEV):
            acc = acc + gath_ref[d]
        o_ref[...] = acc

    vm = pl.BlockSpec(memory_space=pltpu.VMEM)
    return pl.pallas_call(
        body, name=name, in_specs=[vm], out_specs=vm, out_shape=_sds((r, c), F32),
        scratch_shapes=[pltpu.VMEM((N_DEV, r, c), F32), pltpu.SemaphoreType.DMA((N_DEV - 1,)),
                        pltpu.SemaphoreType.DMA((N_DEV - 1,))],
    )(part)


def _adamw(w, g, m, v):
    m = ADAM_B1 * m + (1.0 - ADAM_B1) * g
    v = ADAM_B2 * v + (1.0 - ADAM_B2) * (g * g)
    m_hat = m / (1.0 - ADAM_B1 ** ADAM_STEP)
    v_hat = v / (1.0 - ADAM_B2 ** ADAM_STEP)
    delta = -ADAM_LR * (m_hat / (jnp.sqrt(v_hat) + ADAM_EPS) + ADAM_WD * w)
    return delta, m, v


def reduce_adam(own, landed, chip, w, m, v, layer, outs, *, name):
    _, r, c = w.shape
    tr = min(r, 128)
    first = outs is None

    def body(chip_ref, own_ref, land_ref, w_ref, m_ref, v_ref, *rest):
        g_out, d_out, m_out, v_out = rest[-4:]
        g = None
        for q in range(N_CHIP):
            term = jnp.where(chip_ref[0] == q, own_ref[q], land_ref[q]).astype(F32)
            g = term if g is None else g + term
        d, mn, vn = _adamw(w_ref[...], g, m_ref[...], v_ref[...])
        g_out[...] = g
        d_out[...] = d
        m_out[...] = mn
        v_out[...] = vn

    spec = pl.BlockSpec((None, tr, c), lambda i, chip_ref: (layer, i, 0))
    pspec = pl.BlockSpec((N_CHIP, tr, c), lambda i, chip_ref: (0, i, 0))
    n_in = 6
    return pl.pallas_call(
        body, name=name,
        grid_spec=pltpu.PrefetchScalarGridSpec(num_scalar_prefetch=1, grid=(r // tr,),
                                               in_specs=[pspec, pspec, spec, spec, spec] + ([] if first else [ANY] * 4),
                                               out_specs=[spec] * 4),
        out_shape=[_sds(w.shape, F32)] * 4, input_output_aliases={} if first else {n_in + i: i for i in range(4)},
        compiler_params=_params(("parallel",)),
    )(chip, own, landed, w, m, v, *([] if first else outs))


def adam_small(g, w, m, v, *, name):
    def body(g_ref, w_ref, m_ref, v_ref, d_out, m_out, v_out):
        d, mn, vn = _adamw(w_ref[...], g_ref[...], m_ref[...], v_ref[...])
        d_out[...] = d
        m_out[...] = mn
        v_out[...] = vn

    return pl.pallas_call(body, name=name, out_shape=[_sds(w.shape, F32)] * 3)(g, w, m, v)


BIG = ("w_in", "w_out_conv", "w_out_attn", "w_merge", "w_up", "w_down")
BIG_AXIS = {"w_in": 1, "w_out_conv": 0, "w_out_attn": 0, "w_merge": 0, "w_up": 1, "w_down": 0}
SMALL = (("norm_mix_pre", D_MODEL), ("gate_bias", 2 * D_MODEL), ("norm_mix_post", D_MODEL), ("norm_mlp_pre", D_MODEL),
         ("norm_mlp_post", D_MODEL), ("q_norm", HEAD_DIM), ("k_norm", HEAD_DIM))
SMALL_WIDTH = sum(w for _, w in SMALL)
WEIGHTS = ("norm_mix_pre", "w_in", "gate_bias", "conv_w", "q_norm", "k_norm", "w_out_conv", "w_out_attn", "w_merge",
           "norm_mix_post", "norm_mlp_pre", "w_up", "w_down", "norm_mlp_post")


def kernel(x, norm_mix_pre, w_in, gate_bias, conv_w, q_norm, k_norm, w_out_conv, w_out_attn, w_merge, norm_mix_post, norm_mlp_pre, w_up, w_down, norm_mlp_post, loss_target, m_norm_mix_pre, m_w_in, m_gate_bias, m_conv_w, m_q_norm, m_k_norm, m_w_out_conv, m_w_out_attn, m_w_merge, m_norm_mix_post, m_norm_mlp_pre, m_w_up, m_w_down, m_norm_mlp_post, v_norm_mix_pre, v_w_in, v_gate_bias, v_conv_w, v_q_norm, v_k_norm, v_w_out_conv, v_w_out_attn, v_w_merge, v_norm_mix_post, v_norm_mlp_pre, v_w_up, v_w_down, v_norm_mlp_post):
    w = dict(norm_mix_pre=norm_mix_pre, w_in=w_in, gate_bias=gate_bias, conv_w=conv_w, q_norm=q_norm, k_norm=k_norm,
             w_out_conv=w_out_conv, w_out_attn=w_out_attn, w_merge=w_merge, norm_mix_post=norm_mix_post,
             norm_mlp_pre=norm_mlp_pre, w_up=w_up, w_down=w_down, norm_mlp_post=norm_mlp_post)
    mom = dict(norm_mix_pre=m_norm_mix_pre, w_in=m_w_in, gate_bias=m_gate_bias, conv_w=m_conv_w, q_norm=m_q_norm,
               k_norm=m_k_norm, w_out_conv=m_w_out_conv, w_out_attn=m_w_out_attn, w_merge=m_w_merge,
               norm_mix_post=m_norm_mix_post, norm_mlp_pre=m_norm_mlp_pre, w_up=m_w_up, w_down=m_w_down,
               norm_mlp_post=m_norm_mlp_post)
    var = dict(norm_mix_pre=v_norm_mix_pre, w_in=v_w_in, gate_bias=v_gate_bias, conv_w=v_conv_w, q_norm=v_q_norm,
               k_norm=v_k_norm, w_out_conv=v_w_out_conv, w_out_attn=v_w_out_attn, w_merge=v_w_merge,
               norm_mix_post=v_norm_mix_post, norm_mlp_pre=v_norm_mlp_pre, w_up=v_w_up, w_down=v_w_down,
               norm_mlp_post=v_norm_mlp_post)
    depth = w_in.shape[0]
    s = x.shape[1]
    xs = x.reshape(s, D_MODEL)
    target = loss_target.reshape(s, D_MODEL)
    x_idx, y_idx, c_idx = _place()
    as_operand = lambda i: jnp.reshape(i, (1,)).astype(jnp.int32)
    core, chip, me = as_operand(c_idx), as_operand(2 * x_idx + y_idx), as_operand(4 * x_idx + 2 * y_idx + c_idx)
    cos_t, sin_t = rope_tables(s)
    shard_shapes = [w[k].shape[1:] for k in BIG]
    axes = [BIG_AXIS[k] for k in BIG]

    def vec(name, l):
        return w[name][l].reshape(1, -1)

    conv_taps = gather_conv_taps(conv_w, name="gather_conv_taps")

    def start_gather(l, after):
        own = [cast_into_full(w[k], l, ax, me, name="cast_" + k) for k, ax in zip(BIG, axes)]
        return gather_start(own, after, shard_shapes, axes, name=f"gather_start_{l}")

    saved = []
    h = rms_fwd(xs, vec("norm_mix_pre", 0), name="rms_first")
    send1, recv1, fulls, tok = start_gather(0, h)
    send2, recv2, fulls, tok = gather_pass(send1, recv1, fulls, tok, shard_shapes, axes, name="gather_pass_0")
    full = dict(zip(BIG, gather_wait(send2, recv2, fulls, tok, shard_shapes, axes, name="gather_wait_0")))
    layers = []
    for l in range(depth):
        layers.append(full)
        nxt = l + 1 < depth
        if nxt:
            send1, recv1, nfulls, tok = start_gather(l + 1, full["w_in"])
        z = matmul(h, full["w_in"], mode="nn", m=s, n=IN_WIDTH, k=D_MODEL, out_dtypes=[F32], name="mm_in",
                   after=[tok] if nxt else [])
        t = conv_fwd(z, conv_taps[l], name="conv_fwd")
        qp, kp, vb = qk_prep_fwd(z, vec("q_norm", l), vec("k_norm", l), cos_t, sin_t, name="qk_fwd")
        o, lse = attn_fwd(qp, kp, vb, name="attn_fwd")
        ya = matmul(t, full["w_out_conv"], mode="nn", m=s, n=D_MODEL, k=CONV_WIDTH, out_dtypes=[F32], name="mm_out_conv")
        yb = matmul(o, full["w_out_attn"], mode="nn", m=s, n=D_MODEL, k=ATTN_WIDTH, out_dtypes=[F32], name="mm_out_attn")
        if nxt:
            send2, recv2, nfulls, tok = gather_pass(send1, recv1, nfulls, yb, shard_shapes, axes, name=f"gather_pass_{l + 1}")
        mi = gate_fwd(ya, yb, z, vec("gate_bias", l), name="gate_fwd")
        mixed = matmul(mi, full["w_merge"], mode="nn", m=s, n=D_MODEL, k=D_MODEL, out_dtypes=[F32], name="mm_merge",
                       after=[tok] if nxt else [])
        x_mid, h2 = rms_residual_fwd(xs, mixed, vec("norm_mix_post", l), vec("norm_mlp_pre", l), name="res_mix")
        act, r = matmul(h2, full["w_up"], mode="nn", m=s, n=D_FF, k=D_MODEL, out_dtypes=[MXU_DTYPE, MXU_DTYPE],
                        name="mm_up", epilogue=lambda acc: (acc, jnp.square(jnp.maximum(acc, 0.0))))
        f = matmul(r, full["w_down"], mode="nn", m=s, n=D_MODEL, k=D_FF, out_dtypes=[F32], name="mm_down")
        g_next = vec("norm_mix_pre", l + 1) if nxt else None
        x_out, h_next = rms_residual_fwd(x_mid, f, vec("norm_mlp_post", l), g_next, name="res_mlp" if nxt else "res_last")
        saved.append(dict(x_in=xs, h=h, z=z, t=t, qp=qp, kp=kp, vb=vb, o=o, lse=lse, ya=ya, yb=yb, mi=mi, mixed=mixed,
                          x_mid=x_mid, h2=h2, act=act, r=r, f=f))
        if nxt:
            full = dict(zip(BIG, gather_wait(send2, recv2, nfulls, x_out, shard_shapes, axes, name=f"gather_wait_{l + 1}")))
        xs, h = x_out, h_next

    dx, loss_part = loss_and_grad(xs, target, name="loss")
    loss = lax.psum(jnp.sum(loss_part), ("x", "y", "c"))

    small_rows = [None] * depth
    conv_rows = [None] * depth
    out_g, out_d, out_m, out_v = {}, {}, {}, {}
    big_outs = {k: None for k in BIG}
    pair_shapes = [(N_CHIP, *shp) for shp in shard_shapes]
    pair_copies = lambda i, z, ss, rs: _pair_copies(i, z, shard_shapes, axes, ss, rs)
    pair_waits = lambda i, z, ss, rs: (pair_copies(i, z, ss, rs),) * 2
    chip_sends = lambda i, z, ss, rs: _chip_copies(i, z, ss, rs)[0]
    pending = None

    def finish_layer(l, own, landed):
        for k, a, b in zip(BIG, own, landed):
            big_outs[k] = reduce_adam(a, b, chip, w[k], mom[k], var[k], l, big_outs[k], name="adam_" + k)

    def level2(l, grads, from_sibling, after):
        sums = [pair_sum(g, rcv, shp, ax, core, name="pair_sum_" + k)
                for k, g, rcv, shp, ax in zip(BIG, grads, from_sibling, shard_shapes, axes)]
        return exchange_start(sums, after, pair_shapes, chip_sends, 3, name=f"rs_chips_start_{l}")

    for l in reversed(range(depth)):
        sv, full = saved[l], layers[l]
        grads = {}

        def wgrad(key, lhs, rhs, m, n):
            grads[key] = matmul(lhs, rhs, mode="tn", m=m, n=n, k=s, out_dtypes=[COMM_DTYPE], name="wg_" + key)

        df, dg_mlp_post = rms_bwd(sv["f"], vec("norm_mlp_post", l), dx, None, out_dtype=MXU_DTYPE, name="rmsb_mlp_post",
                                  after=[pending[-1]] if pending else [])
        da = matmul(df, full["w_down"], mode="nt", m=s, n=D_FF, k=D_MODEL, out_dtypes=[MXU_DTYPE], name="mm_d_down",
                    extra=(sv["act"],), epilogue=lambda acc, a: (acc * (2.0 * jnp.maximum(a.astype(F32), 0.0)),))
        if pending:
            above, s1, r1, gsrc, glands, _ = pending
            gsrc, from_sibling = exchange_wait(s1, r1, gsrc, glands, da, pair_waits, name=f"rs_pair_wait_{above}")
            s2, r2, sums, lands2, tok2 = level2(above, gsrc, from_sibling, da)
        wgrad("w_down", sv["r"], df, D_FF, D_MODEL)
        dh2 = matmul(da, full["w_up"], mode="nt", m=s, n=D_MODEL, k=D_FF, out_dtypes=[F32], name="mm_d_up",
                     after=[tok2] if pending else [])
        wgrad("w_up", sv["h2"], da, D_MODEL, D_FF)
        dx_mid, dg_mlp_pre = rms_bwd(sv["x_mid"], vec("norm_mlp_pre", l), dh2, dx, out_dtype=F32, name="rmsb_mlp_pre")
        dmixed, dg_mix_post = rms_bwd(sv["mixed"], vec("norm_mix_post", l), dx_mid, None, out_dtype=MXU_DTYPE,
                                      name="rmsb_mix_post")
        dmi = matmul(dmixed, full["w_merge"], mode="nt", m=s, n=D_MODEL, k=D_MODEL, out_dtypes=[F32], name="mm_d_merge")
        wgrad("w_merge", sv["mi"], dmixed, D_MODEL, D_MODEL)
        dya, dyb, dga, dgb, dba, dbb = gate_bwd(dmi, sv["ya"], sv["yb"], sv["z"], vec("gate_bias", l), name="gate_bwd")
        dt = matmul(dya, full["w_out_conv"], mode="nt", m=s, n=CONV_WIDTH, k=D_MODEL, out_dtypes=[F32],
                    name="mm_d_out_conv")
        wgrad("w_out_conv", sv["t"], dya, CONV_WIDTH, D_MODEL)
        do = matmul(dyb, full["w_out_attn"], mode="nt", m=s, n=ATTN_WIDTH, k=D_MODEL, out_dtypes=[MXU_DTYPE],
                    name="mm_d_out_attn")
        wgrad("w_out_attn", sv["o"], dyb, ATTN_WIDTH, D_MODEL)
        dcb, dcc, dci, dconv_w = conv_bwd(dt, sv["z"], conv_taps[l], name="conv_bwd")
        dqp, dd = attn_bwd_q(sv["qp"], sv["kp"], sv["vb"], do, sv["o"], sv["lse"], name="attn_bwd_q")
        dkp, dv = attn_bwd_kv(sv["qp"], sv["kp"], sv["vb"], do, jnp.swapaxes(sv["lse"], 1, 2), jnp.swapaxes(dd, 1, 2),
                              name="attn_bwd_kv")
        dq, dk, dqn, dkn = qk_prep_bwd(dqp, dkp, sv["z"], vec("q_norm", l), vec("k_norm", l), cos_t, sin_t, name="qk_bwd")
        dz = jnp.concatenate([dcb, dcc, dci, dq, dk, dv, dga, dgb], axis=1)
        dh = matmul(dz, full["w_in"], mode="nt", m=s, n=D_MODEL, k=IN_WIDTH, tk=1024, out_dtypes=[F32], name="mm_d_in")
        wgrad("w_in", sv["h"], dz, D_MODEL, IN_WIDTH)
        dx, dg_mix_pre = rms_bwd(sv["x_in"], vec("norm_mix_pre", l), dh, dx_mid, out_dtype=F32, name="rmsb_mix_pre")
        small_rows[l] = jnp.concatenate([dg_mix_pre, dba, dbb, dg_mix_post, dg_mlp_pre, dg_mlp_post, dqn, dkn], axis=1)
        conv_rows[l] = dconv_w.reshape(1, 3 * CONV_WIDTH)
        if pending:
            sums, landed = exchange_wait(s2, r2, sums, lands2, dx, _chip_copies, name=f"rs_chips_wait_{above}")
            finish_layer(above, sums, landed)
        s1, r1, gsrc, glands, tok1 = exchange_start([grads[k] for k in BIG], dx, pair_shapes, pair_copies, N_CHIP,
                                                    name=f"rs_pair_start_{l}")
        pending = (l, s1, r1, gsrc, glands, tok1)

    grad_x = dx.reshape(1, s, D_MODEL)

    small_part = jnp.concatenate([jnp.concatenate(small_rows, axis=0), jnp.concatenate(conv_rows, axis=0)], axis=1)
    small_sum = all_reduce_small(small_part, name="allreduce_small")
    pack = lambda src: jnp.concatenate([src[k].reshape(depth, wd) for k, wd in SMALL], axis=1)
    g_small = small_sum[:, :SMALL_WIDTH]
    d_small, m_small, v_small = adam_small(g_small, pack(w), pack(mom), pack(var), name="adam_small")
    off = 0
    for k, wd in SMALL:
        for dst, src in ((out_g, g_small), (out_d, d_small), (out_m, m_small), (out_v, v_small)):
            dst[k] = src[:, off:off + wd]
        off += wd
    cshard = CONV_WIDTH // N_DEV
    g_conv = lax.dynamic_slice_in_dim(small_sum[:, SMALL_WIDTH:].reshape(depth, 3, CONV_WIDTH), me[0] * cshard, cshard,
                                      axis=2).reshape(depth, 3 * cshard)
    flat = lambda a: a.reshape(depth, 3 * cshard)
    d_conv, m_conv, v_conv = adam_small(g_conv, flat(conv_w), flat(m_conv_w), flat(v_conv_w), name="adam_conv")
    for dst, src in ((out_g, g_conv), (out_d, d_conv), (out_m, m_conv), (out_v, v_conv)):
        dst["conv_w"] = src.reshape(depth, 3, cshard)

    above, s1, r1, gsrc, glands, _ = pending
    gsrc, from_sibling = exchange_wait(s1, r1, gsrc, glands, d_small, pair_waits, name=f"rs_pair_wait_{above}")
    s2, r2, sums, lands2, tok2 = level2(above, gsrc, from_sibling, d_small)
    sums, landed = exchange_wait(s2, r2, sums, lands2, tok2, _chip_copies, name=f"rs_chips_wait_{above}")
    finish_layer(above, sums, landed)
    for k in BIG:
        out_g[k], out_d[k], out_m[k], out_v[k] = big_outs[k]

    return (loss, grad_x, *[out_g[k] for k in WEIGHTS], *[out_d[k] for k in WEIGHTS], *[out_m[k] for k in WEIGHTS],
            *[out_v[k] for k in WEIGHTS])
```

```python
import math

import jax
import jax.numpy as jnp
from jax import lax
from jax.experimental import pallas as pl
from jax.experimental.pallas import tpu as pltpu

F32 = jnp.float32
MXU_DTYPE = jnp.bfloat16
COMM_DTYPE = jnp.bfloat16

D_MODEL = 2048
HEAD_DIM = 128
N_Q_HEADS = 16
N_KV_HEADS = 4
GROUP = N_Q_HEADS // N_KV_HEADS
ATTN_WIDTH = N_Q_HEADS * HEAD_DIM
KV_WIDTH = N_KV_HEADS * HEAD_DIM
CONV_WIDTH = D_MODEL
D_FF = 4 * D_MODEL
GRID_W = 64
ROPE_THETA = 10000.0
RMS_EPS = 1e-6
IN_WIDTH = 3 * CONV_WIDTH + ATTN_WIDTH + 2 * KV_WIDTH + 2 * D_MODEL
OFF_CB, OFF_CC, OFF_CI = 0, CONV_WIDTH, 2 * CONV_WIDTH
OFF_Q = 3 * CONV_WIDTH
OFF_K = OFF_Q + ATTN_WIDTH
OFF_V = OFF_K + KV_WIDTH
OFF_GA = OFF_V + KV_WIDTH
OFF_GB = OFF_GA + D_MODEL
ATTN_SCALE = 1.0 / math.sqrt(HEAD_DIM)

ADAM_LR, ADAM_B1, ADAM_B2, ADAM_EPS, ADAM_WD, ADAM_STEP = 0.001, 0.9, 0.999, 1e-08, 0.01, 10

N_DEV = 8
N_CHIP = 4
LANE = 128
SUBLANE = 8
VMEM_LIMIT = 48 * 1024 * 1024
MESH = pl.DeviceIdType.MESH
ANY = pl.BlockSpec(memory_space=pl.ANY)


def _sds(shape, dtype):
    return jax.ShapeDtypeStruct(tuple(shape), dtype)


def _params(sem, vmem=VMEM_LIMIT):
    return pltpu.CompilerParams(dimension_semantics=sem, vmem_limit_bytes=vmem)


def _rows(ts, w, col=0):
    return pl.BlockSpec((ts, w), lambda i: (i, col))


def _fixed(shape):
    return pl.BlockSpec(shape, lambda *_: (0,) * len(shape))


def _fold8(v):
    ts, w = v.shape
    return v.reshape(ts // SUBLANE, SUBLANE, w).sum(axis=0)


def matmul(a, b, *, mode, m, n, k, name, out_dtypes, tm=1024, tn=1024, tk=2048, epilogue=None, extra=(), after=()):
    tm, tn, tk = min(tm, m), min(tn, n), min(tk, k)
    nm, nn, nk = m // tm, n // tn, k // tk
    assert nm * tm == m and nn * tn == n and nk * tk == k, (name, m, n, k, tm, tn, tk)
    if mode == "tn":
        a_spec = pl.BlockSpec((tk, tm), lambda i, j, kk: (kk, i))
        dims = (((0,), (0,)), ((), ()))
    else:
        a_spec = pl.BlockSpec((tm, tk), lambda i, j, kk: (i, kk))
        dims = (((1,), (1 if mode == "nt" else 0,)), ((), ()))
    if mode == "nt":
        b_spec = pl.BlockSpec((tn, tk), lambda i, j, kk: (j, kk))
    else:
        b_spec = pl.BlockSpec((tk, tn), lambda i, j, kk: (kk, j))
    tile = pl.BlockSpec((tm, tn), lambda i, j, kk: (i, j))
    n_out, n_extra, n_after = len(out_dtypes), len(extra), len(after)
    if epilogue is None:
        epilogue = lambda acc: (acc,)

    def body(a_ref, b_ref, *rest):
        extra_refs = rest[:n_extra]
        outs = rest[n_extra + n_after:][:n_out]
        part = lax.dot_general(a_ref[...].astype(MXU_DTYPE), b_ref[...].astype(MXU_DTYPE), dims,
                               preferred_element_type=F32)

        def finish(acc):
            for o_ref, val in zip(outs, epilogue(acc, *[r[...] for r in extra_refs])):
                o_ref[...] = val.astype(o_ref.dtype)

        if nk == 1:
            finish(part)
        else:
            acc_ref = rest[-1]
            kk = pl.program_id(2)

            @pl.when(kk == 0)
            def _():
                acc_ref[...] = part

            @pl.when(kk > 0)
            def _():
                acc_ref[...] += part

            @pl.when(kk == nk - 1)
            def _():
                finish(acc_ref[...])

    res = pl.pallas_call(
        body, name=name, grid=(nm, nn, nk), in_specs=[a_spec, b_spec] + [tile] * n_extra + [ANY] * n_after,
        out_specs=[tile] * n_out, out_shape=[_sds((m, n), dt) for dt in out_dtypes],
        scratch_shapes=[pltpu.VMEM((tm, tn), F32)] if nk > 1 else [],
        compiler_params=_params(("parallel", "parallel", "arbitrary")),
    )(a, b, *extra, *after)
    return res[0] if n_out == 1 else res


def _rstd(x):
    return lax.rsqrt(jnp.mean(x * x, axis=-1, keepdims=True) + RMS_EPS)


def _rms_bwd(x, g, dy):
    rstd = _rstd(x)
    xh = x * rstd
    gy = dy * g
    dx = rstd * (gy - xh * jnp.mean(gy * xh, axis=-1, keepdims=True))
    return dx, dy * xh


def rms_fwd(x, g, *, name):
    s = x.shape[0]
    ts = min(s, 512)

    def body(x_ref, g_ref, h_ref):
        xv = x_ref[...]
        h_ref[...] = (xv * _rstd(xv) * g_ref[...]).astype(h_ref.dtype)

    return pl.pallas_call(
        body, name=name, grid=(s // ts,), in_specs=[_rows(ts, D_MODEL), _fixed((1, D_MODEL))],
        out_specs=_rows(ts, D_MODEL), out_shape=_sds((s, D_MODEL), MXU_DTYPE), compiler_params=_params(("parallel",)),
    )(x, g)


def rms_residual_fwd(x, y, g_post, g_next, *, name):
    s = x.shape[0]
    ts = min(s, 512)
    with_next = g_next is not None

    def body(x_ref, y_ref, gp_ref, *rest):
        yv = y_ref[...]
        xn = x_ref[...] + yv * _rstd(yv) * gp_ref[...]
        if with_next:
            gn_ref, xo_ref, h_ref = rest
            h_ref[...] = (xn * _rstd(xn) * gn_ref[...]).astype(h_ref.dtype)
        else:
            (xo_ref,) = rest
        xo_ref[...] = xn

    gspec = _fixed((1, D_MODEL))
    res = pl.pallas_call(
        body, name=name, grid=(s // ts,),
        in_specs=[_rows(ts, D_MODEL), _rows(ts, D_MODEL), gspec] + [gspec] * with_next,
        out_specs=[_rows(ts, D_MODEL)] * (1 + with_next),
        out_shape=[_sds((s, D_MODEL), F32)] + [_sds((s, D_MODEL), MXU_DTYPE)] * with_next,
        compiler_params=_params(("parallel",)),
    )(x, y, g_post, *([g_next] if with_next else []))
    return (res[0], res[1]) if with_next else (res[0], None)


def rms_bwd(x, g, dy, residual, *, out_dtype, name, after=()):
    s = x.shape[0]
    ts = min(s, 256)
    nt = s // ts
    with_res = residual is not None

    def body(x_ref, g_ref, dy_ref, *rest):
        dx_ref, dg_ref, acc_ref = rest[-3:]
        dx, dg_rows = _rms_bwd(x_ref[...], g_ref[...], dy_ref[...].astype(F32))
        if with_res:
            dx = dx + rest[0][...]
        dx_ref[...] = dx.astype(dx_ref.dtype)
        i = pl.program_id(0)

        @pl.when(i == 0)
        def _():
            acc_ref[...] = jnp.zeros_like(acc_ref)

        acc_ref[...] += _fold8(dg_rows)

        @pl.when(i == nt - 1)
        def _():
            dg_ref[...] = acc_ref[...].sum(axis=0, keepdims=True)

    return pl.pallas_call(
        body, name=name, grid=(nt,),
        in_specs=[_rows(ts, D_MODEL), _fixed((1, D_MODEL)), _rows(ts, D_MODEL)] + [_rows(ts, D_MODEL)] * with_res
        + [ANY] * len(after),
        out_specs=[_rows(ts, D_MODEL), _fixed((1, D_MODEL))],
        out_shape=[_sds((s, D_MODEL), out_dtype), _sds((1, D_MODEL), F32)],
        scratch_shapes=[pltpu.VMEM((SUBLANE, D_MODEL), F32)], compiler_params=_params(("arbitrary",)),
    )(x, g, dy, *([residual] if with_res else []), *after)


def loss_and_grad(y, target, *, name):
    s = y.shape[0]
    ts = min(s, 512)
    nt = s // ts

    def body(y_ref, t_ref, dy_ref, part_ref):
        e = y_ref[...] - t_ref[...]
        dy_ref[...] = e * (1.0 / D_MODEL)
        sq = _fold8(e * e)
        lanes = sq[:, 0:LANE]
        for j in range(1, D_MODEL // LANE):
            lanes = lanes + sq[:, j * LANE:(j + 1) * LANE]
        i = pl.program_id(0)

        @pl.when(i == 0)
        def _():
            part_ref[...] = jnp.zeros_like(part_ref)

        part_ref[...] += lanes * (0.5 / D_MODEL)

    return pl.pallas_call(
        body, name=name, grid=(nt,), in_specs=[_rows(ts, D_MODEL), _rows(ts, D_MODEL)],
        out_specs=[_rows(ts, D_MODEL), _fixed((SUBLANE, LANE))],
        out_shape=[_sds((s, D_MODEL), F32), _sds((SUBLANE, LANE), F32)], compiler_params=_params(("arbitrary",)),
    )(y, target)


CONV_TC = LANE


def _conv_taps(u, s):
    row = lax.broadcasted_iota(jnp.int32, u.shape, 0)
    prev = jnp.where(row == 0, 0.0, pltpu.roll(u, 1, 0))
    nxt = jnp.where(row == s - 1, 0.0, pltpu.roll(u, s - 1, 0))
    return prev, nxt


def _zcol(s, off):
    return pl.BlockSpec((s, CONV_TC), lambda j: (0, off // CONV_TC + j))


def conv_fwd(z, w3, *, name):
    s = z.shape[0]

    def body(cb_ref, cc_ref, ci_ref, w_ref, t_ref):
        u = cc_ref[...] * ci_ref[...]
        prev, nxt = _conv_taps(u, s)
        w = w_ref[...]
        conv = w[0:1] * prev + w[1:2] * u + w[2:3] * nxt
        t_ref[...] = (cb_ref[...] * conv).astype(t_ref.dtype)

    return pl.pallas_call(
        body, name=name, grid=(CONV_WIDTH // CONV_TC,),
        in_specs=[_zcol(s, OFF_CB), _zcol(s, OFF_CC), _zcol(s, OFF_CI), pl.BlockSpec((3, CONV_TC), lambda j: (0, j))],
        out_specs=pl.BlockSpec((s, CONV_TC), lambda j: (0, j)), out_shape=_sds((s, CONV_WIDTH), MXU_DTYPE),
        compiler_params=_params(("parallel",)),
    )(z, z, z, w3)


def conv_bwd(dt, z, w3, *, name, after=()):
    s = z.shape[0]

    def body(dt_ref, cb_ref, cc_ref, ci_ref, w_ref, *rest):
        dcb_ref, dcc_ref, dci_ref, dw_ref = rest[-4:]
        cc, ci = cc_ref[...], ci_ref[...]
        u = cc * ci
        prev, nxt = _conv_taps(u, s)
        w = w_ref[...]
        dtv = dt_ref[...]
        dcb_ref[...] = (dtv * (w[0:1] * prev + w[1:2] * u + w[2:3] * nxt)).astype(dcb_ref.dtype)
        dconv = dtv * cb_ref[...]
        dprev, dnxt = _conv_taps(dconv, s)
        du = w[0:1] * dnxt + w[1:2] * dconv + w[2:3] * dprev
        dcc_ref[...] = (du * ci).astype(dcc_ref.dtype)
        dci_ref[...] = (du * cc).astype(dci_ref.dtype)
        dw_ref[0:1, :] = jnp.sum(dconv * prev, axis=0, keepdims=True)
        dw_ref[1:2, :] = jnp.sum(dconv * u, axis=0, keepdims=True)
        dw_ref[2:3, :] = jnp.sum(dconv * nxt, axis=0, keepdims=True)

    col = pl.BlockSpec((s, CONV_TC), lambda j: (0, j))
    wspec = pl.BlockSpec((3, CONV_TC), lambda j: (0, j))
    return pl.pallas_call(
        body, name=name, grid=(CONV_WIDTH // CONV_TC,),
        in_specs=[col, _zcol(s, OFF_CB), _zcol(s, OFF_CC), _zcol(s, OFF_CI), wspec] + [ANY] * len(after),
        out_specs=[col, col, col, wspec],
        out_shape=[_sds((s, CONV_WIDTH), MXU_DTYPE)] * 3 + [_sds((3, CONV_WIDTH), F32)],
        compiler_params=_params(("parallel",)),
    )(dt, z, z, z, w3, *after)


def rope_tables(s):
    n_freq = HEAD_DIM // 4
    t = jnp.arange(s, dtype=jnp.int32)
    inv_freq = ROPE_THETA ** (-jnp.arange(0, HEAD_DIM // 2, 2, dtype=F32) / (HEAD_DIM // 2))
    ang_r = (t // GRID_W).astype(F32)[:, None] * inv_freq
    ang_c = (t % GRID_W).astype(F32)[:, None] * inv_freq
    cos_t = jnp.concatenate([jnp.cos(ang_r)] * 2 + [jnp.cos(ang_c)] * 2, axis=1)
    sin_t = jnp.concatenate([-jnp.sin(ang_r), jnp.sin(ang_r), -jnp.sin(ang_c), jnp.sin(ang_c)], axis=1)
    assert cos_t.shape == (s, 4 * n_freq)
    return cos_t, sin_t


def _swap_halves(v):
    lane = lax.broadcasted_iota(jnp.int32, v.shape, 1)
    return jnp.where(lane % 64 < 32, pltpu.roll(v, HEAD_DIM - 32, 1), pltpu.roll(v, 32, 1))


def qk_prep_fwd(z, qn, kn, cos_t, sin_t, *, name):
    s = z.shape[0]
    ts = min(s, 256)

    def body(q_ref, k_ref, v_ref, qn_ref, kn_ref, c_ref, s_ref, qo_ref, ko_ref, vo_ref):
        cs, sn = c_ref[...], s_ref[...]

        def head(x, g, scale):
            n = x * _rstd(x) * g
            return (n * cs + _swap_halves(n) * sn) * scale

        for h in range(N_Q_HEADS):
            sl = slice(h * HEAD_DIM, (h + 1) * HEAD_DIM)
            qo_ref[:, sl] = head(q_ref[:, sl], qn_ref[...], ATTN_SCALE).astype(qo_ref.dtype)
        for h in range(N_KV_HEADS):
            sl = slice(h * HEAD_DIM, (h + 1) * HEAD_DIM)
            ko_ref[:, sl] = head(k_ref[:, sl], kn_ref[...], 1.0).astype(ko_ref.dtype)
        vo_ref[...] = v_ref[...].astype(vo_ref.dtype)

    tab = _rows(ts, HEAD_DIM)
    gsp = _fixed((1, HEAD_DIM))
    return pl.pallas_call(
        body, name=name, grid=(s // ts,),
        in_specs=[_rows(ts, ATTN_WIDTH, OFF_Q // ATTN_WIDTH), _rows(ts, KV_WIDTH, OFF_K // KV_WIDTH),
                  _rows(ts, KV_WIDTH, OFF_V // KV_WIDTH), gsp, gsp, tab, tab],
        out_specs=[_rows(ts, ATTN_WIDTH), _rows(ts, KV_WIDTH), _rows(ts, KV_WIDTH)],
        out_shape=[_sds((s, ATTN_WIDTH), MXU_DTYPE), _sds((s, KV_WIDTH), MXU_DTYPE), _sds((s, KV_WIDTH), MXU_DTYPE)],
        compiler_params=_params(("parallel",)),
    )(z, z, z, qn, kn, cos_t, sin_t)


def qk_prep_bwd(dqp, dkp, z, qn, kn, cos_t, sin_t, *, name):
    s = z.shape[0]
    ts = min(s, 256)
    nt = s // ts

    def body(dq_ref, dk_ref, q_ref, k_ref, qn_ref, kn_ref, c_ref, s_ref, dqo_ref, dko_ref, dqn_ref, dkn_ref,
             qacc_ref, kacc_ref):
        cs, sn = c_ref[...], s_ref[...]
        i = pl.program_id(0)

        @pl.when(i == 0)
        def _():
            qacc_ref[...] = jnp.zeros_like(qacc_ref)
            kacc_ref[...] = jnp.zeros_like(kacc_ref)

        def head(x, g, dout, scale):
            d = dout.astype(F32) * scale
            dn = d * cs + _swap_halves(d * sn)
            return _rms_bwd(x, g, dn)

        qacc = jnp.zeros((SUBLANE, HEAD_DIM), F32)
        for h in range(N_Q_HEADS):
            sl = slice(h * HEAD_DIM, (h + 1) * HEAD_DIM)
            dx, dg_rows = head(q_ref[:, sl], qn_ref[...], dq_ref[:, sl], ATTN_SCALE)
            dqo_ref[:, sl] = dx.astype(dqo_ref.dtype)
            qacc = qacc + _fold8(dg_rows)
        kacc = jnp.zeros((SUBLANE, HEAD_DIM), F32)
        for h in range(N_KV_HEADS):
            sl = slice(h * HEAD_DIM, (h + 1) * HEAD_DIM)
            dx, dg_rows = head(k_ref[:, sl], kn_ref[...], dk_ref[:, sl], 1.0)
            dko_ref[:, sl] = dx.astype(dko_ref.dtype)
            kacc = kacc + _fold8(dg_rows)
        qacc_ref[...] += qacc
        kacc_ref[...] += kacc

        @pl.when(i == nt - 1)
        def _():
            dqn_ref[...] = qacc_ref[...].sum(axis=0, keepdims=True)
            dkn_ref[...] = kacc_ref[...].sum(axis=0, keepdims=True)

    tab = _rows(ts, HEAD_DIM)
    gsp = _fixed((1, HEAD_DIM))
    return pl.pallas_call(
        body, name=name, grid=(nt,),
        in_specs=[_rows(ts, ATTN_WIDTH), _rows(ts, KV_WIDTH), _rows(ts, ATTN_WIDTH, OFF_Q // ATTN_WIDTH),
                  _rows(ts, KV_WIDTH, OFF_K // KV_WIDTH), gsp, gsp, tab, tab],
        out_specs=[_rows(ts, ATTN_WIDTH), _rows(ts, KV_WIDTH), gsp, gsp],
        out_shape=[_sds((s, ATTN_WIDTH), MXU_DTYPE), _sds((s, KV_WIDTH), MXU_DTYPE), _sds((1, HEAD_DIM), F32),
                   _sds((1, HEAD_DIM), F32)],
        scratch_shapes=[pltpu.VMEM((SUBLANE, HEAD_DIM), F32)] * 2, compiler_params=_params(("arbitrary",)),
    )(dqp, dkp, z, z, qn, kn, cos_t, sin_t)


_NT = (((1,), (1,)), ((), ()))
_GW = GROUP * HEAD_DIM


def _dot(a, b, dims=(((1,), (0,)), ((), ()))):
    return lax.dot_general(a, b, dims, preferred_element_type=F32)


def attn_fwd(qp, kp, vb, *, name):
    s = qp.shape[0]
    tq = min(s, 256)

    def body(q_ref, k_ref, v_ref, o_ref, lse_ref):
        k, v = k_ref[...], v_ref[...]
        for g in range(GROUP):
            sl = slice(g * HEAD_DIM, (g + 1) * HEAD_DIM)
            sc = _dot(q_ref[:, sl], k, _NT)
            mx = jnp.max(sc, axis=-1, keepdims=True)
            p = jnp.exp(sc - mx)
            den = jnp.sum(p, axis=-1, keepdims=True)
            o = _dot(p.astype(v.dtype), v)
            o_ref[:, sl] = (o / den).astype(o_ref.dtype)
            lse_ref[:, g:g + 1] = mx + jnp.log(den)

    return pl.pallas_call(
        body, name=name, grid=(N_KV_HEADS, s // tq),
        in_specs=[pl.BlockSpec((tq, _GW), lambda j, i: (i, j)), pl.BlockSpec((s, HEAD_DIM), lambda j, i: (0, j)),
                  pl.BlockSpec((s, HEAD_DIM), lambda j, i: (0, j))],
        out_specs=[pl.BlockSpec((tq, _GW), lambda j, i: (i, j)), pl.BlockSpec((None, tq, GROUP), lambda j, i: (j, i, 0))],
        out_shape=[_sds((s, ATTN_WIDTH), MXU_DTYPE), _sds((N_KV_HEADS, s, GROUP), F32)],
        compiler_params=_params(("parallel", "parallel")),
    )(qp, kp, vb)


def attn_bwd_q(qp, kp, vb, do, o, lse, *, name):
    s = qp.shape[0]
    tq = min(s, 256)

    def body(q_ref, k_ref, v_ref, do_ref, o_ref, lse_ref, dq_ref, dd_ref):
        k, v = k_ref[...], v_ref[...]
        for g in range(GROUP):
            sl = slice(g * HEAD_DIM, (g + 1) * HEAD_DIM)
            dog = do_ref[:, sl]
            dd = jnp.sum(dog.astype(F32) * o_ref[:, sl].astype(F32), axis=-1, keepdims=True)
            p = jnp.exp(_dot(q_ref[:, sl], k, _NT) - lse_ref[:, g:g + 1])
            ds = p * (_dot(dog, v, _NT) - dd)
            dq_ref[:, sl] = _dot(ds.astype(k.dtype), k)
            dd_ref[:, g:g + 1] = dd

    qspec = pl.BlockSpec((tq, _GW), lambda j, i: (i, j))
    kspec = pl.BlockSpec((s, HEAD_DIM), lambda j, i: (0, j))
    lspec = pl.BlockSpec((None, tq, GROUP), lambda j, i: (j, i, 0))
    return pl.pallas_call(
        body, name=name, grid=(N_KV_HEADS, s // tq), in_specs=[qspec, kspec, kspec, qspec, qspec, lspec],
        out_specs=[qspec, lspec], out_shape=[_sds((s, ATTN_WIDTH), F32), _sds((N_KV_HEADS, s, GROUP), F32)],
        compiler_params=_params(("parallel", "parallel")),
    )(qp, kp, vb, do, o, lse)


def attn_bwd_kv(qp, kp, vb, do, lse_t, dd_t, *, name):
    s = qp.shape[0]
    tk = min(s, 512)
    tqc = min(s, 512)

    def body(k_ref, v_ref, q_ref, do_ref, lse_ref, dd_ref, dk_ref, dv_ref):
        k, v = k_ref[...], v_ref[...]

        def chunk(c, carry):
            dk, dv = carry
            r0 = pl.multiple_of(c * tqc, tqc)
            for g in range(GROUP):
                sl = slice(g * HEAD_DIM, (g + 1) * HEAD_DIM)
                qg = q_ref[pl.ds(r0, tqc), sl]
                dog = do_ref[pl.ds(r0, tqc), sl]
                p_t = jnp.exp(_dot(k, qg, _NT) - lse_ref[g:g + 1, pl.ds(r0, tqc)])
                dv = dv + _dot(p_t.astype(dog.dtype), dog)
                ds_t = p_t * (_dot(v, dog, _NT) - dd_ref[g:g + 1, pl.ds(r0, tqc)])
                dk = dk + _dot(ds_t.astype(qg.dtype), qg)
            return dk, dv

        zero = jnp.zeros((tk, HEAD_DIM), F32)
        dk, dv = lax.fori_loop(0, s // tqc, chunk, (zero, zero))
        dk_ref[...] = dk
        dv_ref[...] = dv.astype(dv_ref.dtype)

    kspec = pl.BlockSpec((tk, HEAD_DIM), lambda j, t: (t, j))
    qspec = pl.BlockSpec((s, _GW), lambda j, t: (0, j))
    lspec = pl.BlockSpec((None, GROUP, s), lambda j, t: (j, 0, 0))
    return pl.pallas_call(
        body, name=name, grid=(N_KV_HEADS, s // tk), in_specs=[kspec, kspec, qspec, qspec, lspec, lspec],
        out_specs=[kspec, kspec], out_shape=[_sds((s, KV_WIDTH), F32), _sds((s, KV_WIDTH), MXU_DTYPE)],
        compiler_params=_params(("parallel", "parallel")),
    )(kp, vb, qp, do, lse_t, dd_t)


GATE_TW = 1024


def gate_fwd(ya, yb, z, bias, *, name):
    s = z.shape[0]
    ts = min(s, 512)

    def body(ya_ref, yb_ref, ga_ref, gb_ref, ba_ref, bb_ref, o_ref):
        sa = jax.nn.sigmoid(ga_ref[...] + ba_ref[...])
        sb = jax.nn.sigmoid(gb_ref[...] + bb_ref[...])
        o_ref[...] = (sa * ya_ref[...] + sb * yb_ref[...]).astype(o_ref.dtype)

    tile = pl.BlockSpec((ts, GATE_TW), lambda i, j: (i, j))
    ga = pl.BlockSpec((ts, GATE_TW), lambda i, j: (i, OFF_GA // GATE_TW + j))
    gb = pl.BlockSpec((ts, GATE_TW), lambda i, j: (i, OFF_GB // GATE_TW + j))
    ba = pl.BlockSpec((1, GATE_TW), lambda i, j: (0, j))
    bb = pl.BlockSpec((1, GATE_TW), lambda i, j: (0, D_MODEL // GATE_TW + j))
    return pl.pallas_call(
        body, name=name, grid=(s // ts, D_MODEL // GATE_TW), in_specs=[tile, tile, ga, gb, ba, bb], out_specs=tile,
        out_shape=_sds((s, D_MODEL), MXU_DTYPE), compiler_params=_params(("parallel", "parallel")),
    )(ya, yb, z, z, bias, bias)


def gate_bwd(dmi, ya, yb, z, bias, *, name):
    s = z.shape[0]
    ts = min(s, 512)
    nt = s // ts

    def body(d_ref, ya_ref, yb_ref, ga_ref, gb_ref, ba_ref, bb_ref, dya_ref, dyb_ref, dga_ref, dgb_ref, dba_ref,
             dbb_ref, acc_a, acc_b):
        d = d_ref[...]
        sa = jax.nn.sigmoid(ga_ref[...] + ba_ref[...])
        sb = jax.nn.sigmoid(gb_ref[...] + bb_ref[...])
        dya_ref[...] = (d * sa).astype(dya_ref.dtype)
        dyb_ref[...] = (d * sb).astype(dyb_ref.dtype)
        dga = d * ya_ref[...] * sa * (1.0 - sa)
        dgb = d * yb_ref[...] * sb * (1.0 - sb)
        dga_ref[...] = dga.astype(dga_ref.dtype)
        dgb_ref[...] = dgb.astype(dgb_ref.dtype)
        i = pl.program_id(1)

        @pl.when(i == 0)
        def _():
            acc_a[...] = jnp.zeros_like(acc_a)
            acc_b[...] = jnp.zeros_like(acc_b)

        acc_a[...] += _fold8(dga)
        acc_b[...] += _fold8(dgb)

        @pl.when(i == nt - 1)
        def _():
            dba_ref[...] = acc_a[...].sum(axis=0, keepdims=True)
            dbb_ref[...] = acc_b[...].sum(axis=0, keepdims=True)

    tile = pl.BlockSpec((ts, GATE_TW), lambda j, i: (i, j))
    ga = pl.BlockSpec((ts, GATE_TW), lambda j, i: (i, OFF_GA // GATE_TW + j))
    gb = pl.BlockSpec((ts, GATE_TW), lambda j, i: (i, OFF_GB // GATE_TW + j))
    ba = pl.BlockSpec((1, GATE_TW), lambda j, i: (0, j))
    bb = pl.BlockSpec((1, GATE_TW), lambda j, i: (0, D_MODEL // GATE_TW + j))
    acc = pltpu.VMEM((SUBLANE, GATE_TW), F32)
    return pl.pallas_call(
        body, name=name, grid=(D_MODEL // GATE_TW, nt), in_specs=[tile, tile, tile, ga, gb, ba, bb],
        out_specs=[tile, tile, tile, tile, ba, ba],
        out_shape=[_sds((s, D_MODEL), MXU_DTYPE)] * 4 + [_sds((1, D_MODEL), F32)] * 2,
        scratch_shapes=[acc, acc], compiler_params=_params(("parallel", "arbitrary")),
    )(dmi, ya, yb, z, z, bias, bias)


HBM = pl.BlockSpec(memory_space=pltpu.HBM)
SEM = pl.BlockSpec(memory_space=pltpu.SEMAPHORE)
EFFECT = pltpu.SideEffectType.DATAFLOW_SIDE_EFFECTING
TOKEN = (SUBLANE, LANE)


def _place():
    return lax.axis_index("x"), lax.axis_index("y"), lax.axis_index("c")


def _window(ref, shard_shape, axis, d):
    r, c = shard_shape
    lead = (slice(None),) * (len(ref.shape) - 2)
    if axis == 0:
        return ref.at[lead + (pl.ds(pl.multiple_of(d * r, SUBLANE), r), slice(None))]
    return ref.at[lead + (slice(None), pl.ds(pl.multiple_of(d * c, LANE), c))]


def _hbm(a):
    return pltpu.with_memory_space_constraint(a, pltpu.HBM)


def _remote(src, dst, send_sems, recv_sems, i, to):
    return pltpu.make_async_remote_copy(src_ref=src, dst_ref=dst, send_sem=send_sems.at[i], recv_sem=recv_sems.at[i],
                                        device_id=to, device_id_type=MESH)


def cast_into_full(w, layer, axis, me, *, name):
    _, r, c = w.shape
    tr = min(r, 256)
    nr = r // tr
    in_spec = pl.BlockSpec((None, tr, c), lambda i, me_ref: (layer, i, 0))
    if axis == 0:
        out_spec = pl.BlockSpec((tr, c), lambda i, me_ref: (me_ref[0] * nr + i, 0))
        shape = (N_DEV * r, c)
    else:
        out_spec = pl.BlockSpec((tr, c), lambda i, me_ref: (i, me_ref[0]))
        shape = (r, N_DEV * c)

    def body(me_ref, w_ref, o_ref):
        o_ref[...] = w_ref[...].astype(o_ref.dtype)

    return pl.pallas_call(
        body, name=name,
        grid_spec=pltpu.PrefetchScalarGridSpec(num_scalar_prefetch=1, grid=(nr,), in_specs=[in_spec], out_specs=out_spec),
        out_shape=_sds(shape, MXU_DTYPE), compiler_params=_params(("parallel",)),
    )(me, w)


class _GatherPlan:
    def __init__(self, fulls, shard_shapes, axes):
        x, y, c = _place()
        self.n = len(fulls)
        self.me, self.sibling = (x, y, c), (x, y, 1 - c)
        self.chips = [(1 - x, y), (x, 1 - y), (1 - x, 1 - y)]
        self.win = lambda a, p: _window(fulls[a], shard_shapes[a], axes[a], 4 * p[0] + 2 * p[1] + p[2])

    def first(self, send_sems, recv_sems):
        out = []
        for a in range(self.n):
            mine = self.win(a, self.me)
            out.append(_remote(mine, mine, send_sems, recv_sems, 4 * a, self.sibling))
            out += [_remote(mine, mine, send_sems, recv_sems, 4 * a + 1 + j, (*chip, self.me[2]))
                    for j, chip in enumerate(self.chips)]
        return out

    def first_arrivals(self, send_sems, recv_sems):
        c = self.me[2]
        out = []
        for a in range(self.n):
            blocks = [self.sibling] + [(*chip, c) for chip in self.chips]
            out += [_remote(self.win(a, b), self.win(a, b), send_sems, recv_sems, 4 * a + k, self.me)
                    for k, b in enumerate(blocks)]
        return out

    def passed(self, send_sems, recv_sems):
        c = self.me[2]
        return [_remote(self.win(a, (*chip, c)), self.win(a, (*chip, c)), send_sems, recv_sems, 3 * a + j, self.sibling)
                for a in range(self.n) for j, chip in enumerate(self.chips)]

    def passed_arrivals(self, send_sems, recv_sems):
        c = self.me[2]
        return [_remote(self.win(a, (*chip, 1 - c)), self.win(a, (*chip, 1 - c)), send_sems, recv_sems, 3 * a + j, self.me)
                for a in range(self.n) for j, chip in enumerate(self.chips)]


def gather_start(fulls, after, shard_shapes, axes, *, name):
    n = len(fulls)

    def body(*refs):
        ins, send_sems, recv_sems, token = refs[:n], refs[n + 1], refs[n + 2], refs[-1]
        for cp in _GatherPlan(ins, shard_shapes, axes).first(send_sems, recv_sems):
            cp.start()
        token[...] = jnp.zeros_like(token)

    res = pl.pallas_call(
        body, name=name, in_specs=[HBM] * n + [ANY],
        out_shape=(pltpu.SemaphoreType.DMA((4 * n,)), pltpu.SemaphoreType.DMA((4 * n,)),
                   *[pltpu.HBM(f.shape, f.dtype) for f in fulls], _sds(TOKEN, F32)),
        out_specs=(SEM, SEM, *[HBM] * n, pl.BlockSpec(memory_space=pltpu.VMEM)),
        input_output_aliases={a: 2 + a for a in range(n)}, compiler_params=pltpu.CompilerParams(has_side_effects=EFFECT),
    )(*[_hbm(f) for f in fulls], after)
    return res[0], res[1], list(res[2:2 + n]), res[-1]


def gather_pass(send_sems, recv_sems, fulls, after, shard_shapes, axes, *, name):
    n = len(fulls)

    def body(*refs):
        ins, s1, r1 = refs[:n], refs[n], refs[n + 1]
        s2, r2, token = refs[n + 3], refs[n + 4], refs[-1]
        plan = _GatherPlan(ins, shard_shapes, axes)
        for cp in plan.first_arrivals(s1, r1):
            cp.wait_recv()
        for cp in plan.first(s1, r1):
            cp.wait_send()
        for cp in plan.passed(s2, r2):
            cp.start()
        token[...] = jnp.zeros_like(token)

    res = pl.pallas_call(
        body, name=name, in_specs=[HBM] * n + [SEM, SEM, ANY],
        out_shape=(pltpu.SemaphoreType.DMA((3 * n,)), pltpu.SemaphoreType.DMA((3 * n,)),
                   *[pltpu.HBM(f.shape, f.dtype) for f in fulls], _sds(TOKEN, F32)),
        out_specs=(SEM, SEM, *[HBM] * n, pl.BlockSpec(memory_space=pltpu.VMEM)),
        input_output_aliases={a: 2 + a for a in range(n)}, compiler_params=pltpu.CompilerParams(has_side_effects=EFFECT),
    )(*fulls, send_sems, recv_sems, after)
    return res[0], res[1], list(res[2:2 + n]), res[-1]


def gather_wait(send_sems, recv_sems, fulls, after, shard_shapes, axes, *, name):
    n = len(fulls)

    def body(*refs):
        ins, s2, r2 = refs[:n], refs[n], refs[n + 1]
        plan = _GatherPlan(ins, shard_shapes, axes)
        for cp in plan.passed_arrivals(s2, r2):
            cp.wait_recv()
        for cp in plan.passed(s2, r2):
            cp.wait_send()

    return list(pl.pallas_call(
        body, name=name, in_specs=[HBM] * n + [SEM, SEM, ANY], out_shape=tuple(pltpu.HBM(f.shape, f.dtype) for f in fulls),
        out_specs=tuple([HBM] * n), input_output_aliases={a: a for a in range(n)},
        compiler_params=pltpu.CompilerParams(has_side_effects=EFFECT),
    )(*fulls, send_sems, recv_sems, after))


def _pair_copies(grads, lands, shard_shapes, axes, send_sems, recv_sems):
    x, y, c = _place()
    return [_remote(_window(grads[a], shard_shapes[a], axes[a], 2 * q + (1 - c)), lands[a].at[q], send_sems, recv_sems,
                    N_CHIP * a + q, (x, y, 1 - c))
            for a in range(len(grads)) for q in range(N_CHIP)]


def _chip_sends(sums, lands, send_sems, recv_sems):
    x, y, c = _place()
    return [_remote(sums[a].at[2 * px + py], lands[a].at[2 * x + y], send_sems, recv_sems, 3 * a + j, (px, py, c))
            for a in range(len(sums)) for j, (px, py) in enumerate([(1 - x, y), (x, 1 - y), (1 - x, 1 - y)])]


def _chip_arrivals(sums, lands, send_sems, recv_sems):
    x, y, c = _place()
    return [_remote(sums[a].at[2 * x + y], lands[a].at[2 * px + py], send_sems, recv_sems, 3 * a + j, (x, y, c))
            for a in range(len(sums)) for j, (px, py) in enumerate([(1 - x, y), (x, 1 - y), (1 - x, 1 - y)])]


def exchange_start(srcs, after, land_shapes, make_sends, per_array, *, name):
    n = len(srcs)
    lands = [lax.empty(s, a.dtype) for s, a in zip(land_shapes, srcs)]

    def body(*refs):
        ins, zones = refs[:n], refs[n:2 * n]
        send_sems, recv_sems, token = refs[2 * n + 1], refs[2 * n + 2], refs[-1]
        for cp in make_sends(ins, zones, send_sems, recv_sems):
            cp.start()
        token[...] = jnp.zeros_like(token)

    res = pl.pallas_call(
        body, name=name, in_specs=[HBM] * (2 * n) + [ANY],
        out_shape=(pltpu.SemaphoreType.DMA((per_array * n,)), pltpu.SemaphoreType.DMA((per_array * n,)),
                   *[pltpu.HBM(a.shape, a.dtype) for a in srcs], *[pltpu.HBM(a.shape, a.dtype) for a in lands],
                   _sds(TOKEN, F32)),
        out_specs=(SEM, SEM, *[HBM] * (2 * n), pl.BlockSpec(memory_space=pltpu.VMEM)),
        input_output_aliases={a: 2 + a for a in range(2 * n)}, compiler_params=pltpu.CompilerParams(has_side_effects=EFFECT),
    )(*[_hbm(a) for a in srcs], *[_hbm(a) for a in lands], after)
    return res[0], res[1], list(res[2:2 + n]), list(res[2 + n:2 + 2 * n]), res[-1]


def exchange_wait(send_sems, recv_sems, srcs, lands, after, make_waits, *, name):
    n = len(srcs)

    def body(*refs):
        ins, zones, s, r = refs[:n], refs[n:2 * n], refs[2 * n], refs[2 * n + 1]
        sends, arrivals = make_waits(ins, zones, s, r)
        for cp in arrivals:
            cp.wait_recv()
        for cp in sends:
            cp.wait_send()

    res = pl.pallas_call(
        body, name=name, in_specs=[HBM] * (2 * n) + [SEM, SEM, ANY],
        out_shape=tuple(pltpu.HBM(a.shape, a.dtype) for a in (*srcs, *lands)), out_specs=tuple([HBM] * (2 * n)),
        input_output_aliases={a: a for a in range(2 * n)}, compiler_params=pltpu.CompilerParams(has_side_effects=EFFECT),
    )(*srcs, *lands, send_sems, recv_sems, after)
    return list(res[:n]), list(res[n:])


def gather_conv_taps(shard, *, name):
    lx, r, c = shard.shape

    def body(in_ref, out_ref, send_sems, recv_sems, local_sem):
        plan = _GatherPlan([out_ref], [(r, c)], [1])
        mine = pltpu.make_async_copy(in_ref, plan.win(0, plan.me), local_sem)
        mine.start()
        mine.wait()
        first = plan.first(send_sems.at[0], recv_sems.at[0])
        for cp in first:
            cp.start()
        arrivals = plan.first_arrivals(send_sems.at[0], recv_sems.at[0])
        for cp in arrivals[1:]:
            cp.wait_recv()
        passed = plan.passed(send_sems.at[1], recv_sems.at[1])
        for cp in passed:
            cp.start()
        arrivals[0].wait_recv()
        for cp in plan.passed_arrivals(send_sems.at[1], recv_sems.at[1]):
            cp.wait_recv()
        for cp in first + passed:
            cp.wait_send()

    return pl.pallas_call(
        body, name=name, in_specs=[ANY], out_specs=ANY, out_shape=_sds((lx, r, N_DEV * c), shard.dtype),
        scratch_shapes=[pltpu.SemaphoreType.DMA((2, 4)), pltpu.SemaphoreType.DMA((2, 4)), pltpu.SemaphoreType.DMA(())],
    )(shard)


def pair_sum(grad, recv, shard_shape, axis, core, *, name):
    r, c = shard_shape
    tr = min(r, 256)
    nr = r // tr
    if axis == 0:
        gspec = pl.BlockSpec((tr, c), lambda q, i, cref: ((2 * q + cref[0]) * nr + i, 0))
    else:
        gspec = pl.BlockSpec((tr, c), lambda q, i, cref: (i, 2 * q + cref[0]))
    rspec = pl.BlockSpec((None, tr, c), lambda q, i, cref: (q, i, 0))

    def body(c_ref, g_ref, r_ref, o_ref):
        o_ref[...] = (g_ref[...].astype(F32) + r_ref[...].astype(F32)).astype(o_ref.dtype)

    return pl.pallas_call(
        body, name=name,
        grid_spec=pltpu.PrefetchScalarGridSpec(num_scalar_prefetch=1, grid=(N_CHIP, nr), in_specs=[gspec, rspec],
                                               out_specs=rspec),
        out_shape=_sds((N_CHIP, r, c), recv.dtype), compiler_params=_params(("parallel", "parallel")),
    )(core, grad, recv)


def all_reduce_small(part, *, name):
    r, c = part.shape

    def body(p_ref, o_ref, gath_ref, send_sems, recv_sems):
        x, y, cc = _place()
        me = 4 * x + 2 * y + cc
        gath_ref[me] = p_ref[...]
        peers = [(1 - x if k & 4 else x, 1 - y if k & 2 else y, 1 - cc if k & 1 else cc) for k in range(1, N_DEV)]
        copies = [_remote(p_ref, gath_ref.at[me], send_sems, recv_sems, i, peer) for i, peer in enumerate(peers)]
        for cp in copies:
            cp.start()
        for i, (px, py, pc) in enumerate(peers):
            _remote(p_ref, gath_ref.at[4 * px + 2 * py + pc], send_sems, recv_sems, i, (x, y, cc)).wait_recv()
        for cp in copies:
            cp.wait_send()
        acc = gath_ref[0]
        for d in range(1, N_DEV):
            acc = acc + gath_ref[d]
        o_ref[...] = acc

    vm = pl.BlockSpec(memory_space=pltpu.VMEM)
    return pl.pallas_call(
        body, name=name, in_specs=[vm], out_specs=vm, out_shape=_sds((r, c), F32),
        scratch_shapes=[pltpu.VMEM((N_DEV, r, c), F32), pltpu.SemaphoreType.DMA((N_DEV - 1,)),
                        pltpu.SemaphoreType.DMA((N_DEV - 1,))],
    )(part)


def _adamw(w, g, m, v):
    m = ADAM_B1 * m + (1.0 - ADAM_B1) * g
    v = ADAM_B2 * v + (1.0 - ADAM_B2) * (g * g)
    m_hat = m / (1.0 - ADAM_B1 ** ADAM_STEP)
    v_hat = v / (1.0 - ADAM_B2 ** ADAM_STEP)
    delta = -ADAM_LR * (m_hat / (jnp.sqrt(v_hat) + ADAM_EPS) + ADAM_WD * w)
    return delta, m, v


def reduce_adam(own, landed, chip, w, m, v, layer, outs, *, name):
    _, r, c = w.shape
    tr = min(r, 128)
    first = outs is None

    def body(chip_ref, own_ref, land_ref, w_ref, m_ref, v_ref, *rest):
        g_out, d_out, m_out, v_out = rest[-4:]
        g = None
        for q in range(N_CHIP):
            term = jnp.where(chip_ref[0] == q, own_ref[q], land_ref[q]).astype(F32)
            g = term if g is None else g + term
        d, mn, vn = _adamw(w_ref[...], g, m_ref[...], v_ref[...])
        g_out[...] = g
        d_out[...] = d
        m_out[...] = mn
        v_out[...] = vn

    spec = pl.BlockSpec((None, tr, c), lambda i, chip_ref: (layer, i, 0))
    pspec = pl.BlockSpec((N_CHIP, tr, c), lambda i, chip_ref: (0, i, 0))
    n_in = 6
    return pl.pallas_call(
        body, name=name,
        grid_spec=pltpu.PrefetchScalarGridSpec(num_scalar_prefetch=1, grid=(r // tr,),
                                               in_specs=[pspec, pspec, spec, spec, spec] + ([] if first else [ANY] * 4),
                                               out_specs=[spec] * 4),
        out_shape=[_sds(w.shape, F32)] * 4, input_output_aliases={} if first else {n_in + i: i for i in range(4)},
        compiler_params=_params(("parallel",)),
    )(chip, own, landed, w, m, v, *([] if first else outs))


def adam_small(g, w, m, v, *, name):
    def body(g_ref, w_ref, m_ref, v_ref, d_out, m_out, v_out):
        d, mn, vn = _adamw(w_ref[...], g_ref[...], m_ref[...], v_ref[...])
        d_out[...] = d
        m_out[...] = mn
        v_out[...] = vn

    return pl.pallas_call(body, name=name, out_shape=[_sds(w.shape, F32)] * 3)(g, w, m, v)


BIG = ("w_in", "w_out_conv", "w_out_attn", "w_merge", "w_up", "w_down")
BIG_AXIS = {"w_in": 1, "w_out_conv": 0, "w_out_attn": 0, "w_merge": 0, "w_up": 1, "w_down": 0}
FIRST_GATHER_GROUPS = (("w_in",), ("w_out_conv", "w_out_attn", "w_merge"), ("w_up", "w_down"))
LAST_REDUCE_GROUPS = (("w_down", "w_up"), ("w_merge", "w_out_conv", "w_out_attn"), ("w_in",))


class _Gather:
    def __init__(self, weights, keys, layer, me, after, tag):
        self.keys, self.tag = keys, tag
        self.shapes = [weights[k].shape[1:] for k in keys]
        self.axes = [BIG_AXIS[k] for k in keys]
        own = [cast_into_full(weights[k], layer, ax, me, name="cast_" + k) for k, ax in zip(keys, self.axes)]
        self.send, self.recv, self.fulls, self.token = gather_start(own, after, self.shapes, self.axes,
                                                                    name="gather_start_" + tag)

    def pass_on(self, after):
        self.send, self.recv, self.fulls, self.token = gather_pass(self.send, self.recv, self.fulls, after, self.shapes,
                                                                   self.axes, name="gather_pass_" + self.tag)
        return self.token

    def wait(self, after):
        fulls = gather_wait(self.send, self.recv, self.fulls, after, self.shapes, self.axes, name="gather_wait_" + self.tag)
        return dict(zip(self.keys, fulls))


class _Reduce:
    def __init__(self, weights, keys, tag):
        self.keys, self.tag = keys, tag
        self.shapes = [weights[k].shape[1:] for k in keys]
        self.axes = [BIG_AXIS[k] for k in keys]
        self.pair_shapes = [(N_CHIP, *shp) for shp in self.shapes]

    def _pair(self, i, z, ss, rs):
        return _pair_copies(i, z, self.shapes, self.axes, ss, rs)

    def begin(self, grads, after):
        self.send, self.recv, self.src, self.land, self.token = exchange_start(
            [grads[k] for k in self.keys], after, self.pair_shapes, self._pair, N_CHIP, name="rs_pair_start_" + self.tag)
        return self.token

    def middle(self, after, core):
        both = lambda i, z, ss, rs: (self._pair(i, z, ss, rs),) * 2
        grads, from_sibling = exchange_wait(self.send, self.recv, self.src, self.land, after, both,
                                            name="rs_pair_wait_" + self.tag)
        sums = [pair_sum(g, rcv, shp, ax, core, name="pair_sum_" + k)
                for k, g, rcv, shp, ax in zip(self.keys, grads, from_sibling, self.shapes, self.axes)]
        self.send, self.recv, self.src, self.land, self.token = exchange_start(
            sums, after, self.pair_shapes, _chip_sends, 3, name="rs_chips_start_" + self.tag)
        return self.token

    def end(self, after):
        both = lambda i, z, ss, rs: (_chip_sends(i, z, ss, rs), _chip_arrivals(i, z, ss, rs))
        sums, landed = exchange_wait(self.send, self.recv, self.src, self.land, after, both,
                                     name="rs_chips_wait_" + self.tag)
        return list(zip(self.keys, sums, landed))
SMALL = (("norm_mix_pre", D_MODEL), ("gate_bias", 2 * D_MODEL), ("norm_mix_post", D_MODEL), ("norm_mlp_pre", D_MODEL),
         ("norm_mlp_post", D_MODEL), ("q_norm", HEAD_DIM), ("k_norm", HEAD_DIM))
SMALL_WIDTH = sum(w for _, w in SMALL)
WEIGHTS = ("norm_mix_pre", "w_in", "gate_bias", "conv_w", "q_norm", "k_norm", "w_out_conv", "w_out_attn", "w_merge",
           "norm_mix_post", "norm_mlp_pre", "w_up", "w_down", "norm_mlp_post")


def kernel(x, norm_mix_pre, w_in, gate_bias, conv_w, q_norm, k_norm, w_out_conv, w_out_attn, w_merge, norm_mix_post, norm_mlp_pre, w_up, w_down, norm_mlp_post, loss_target, m_norm_mix_pre, m_w_in, m_gate_bias, m_conv_w, m_q_norm, m_k_norm, m_w_out_conv, m_w_out_attn, m_w_merge, m_norm_mix_post, m_norm_mlp_pre, m_w_up, m_w_down, m_norm_mlp_post, v_norm_mix_pre, v_w_in, v_gate_bias, v_conv_w, v_q_norm, v_k_norm, v_w_out_conv, v_w_out_attn, v_w_merge, v_norm_mix_post, v_norm_mlp_pre, v_w_up, v_w_down, v_norm_mlp_post):
    w = dict(norm_mix_pre=norm_mix_pre, w_in=w_in, gate_bias=gate_bias, conv_w=conv_w, q_norm=q_norm, k_norm=k_norm,
             w_out_conv=w_out_conv, w_out_attn=w_out_attn, w_merge=w_merge, norm_mix_post=norm_mix_post,
             norm_mlp_pre=norm_mlp_pre, w_up=w_up, w_down=w_down, norm_mlp_post=norm_mlp_post)
    mom = dict(norm_mix_pre=m_norm_mix_pre, w_in=m_w_in, gate_bias=m_gate_bias, conv_w=m_conv_w, q_norm=m_q_norm,
               k_norm=m_k_norm, w_out_conv=m_w_out_conv, w_out_attn=m_w_out_attn, w_merge=m_w_merge,
               norm_mix_post=m_norm_mix_post, norm_mlp_pre=m_norm_mlp_pre, w_up=m_w_up, w_down=m_w_down,
               norm_mlp_post=m_norm_mlp_post)
    var = dict(norm_mix_pre=v_norm_mix_pre, w_in=v_w_in, gate_bias=v_gate_bias, conv_w=v_conv_w, q_norm=v_q_norm,
               k_norm=v_k_norm, w_out_conv=v_w_out_conv, w_out_attn=v_w_out_attn, w_merge=v_w_merge,
               norm_mix_post=v_norm_mix_post, norm_mlp_pre=v_norm_mlp_pre, w_up=v_w_up, w_down=v_w_down,
               norm_mlp_post=v_norm_mlp_post)
    depth = w_in.shape[0]
    s = x.shape[1]
    xs = x.reshape(s, D_MODEL)
    target = loss_target.reshape(s, D_MODEL)
    x_idx, y_idx, c_idx = _place()
    as_operand = lambda i: jnp.reshape(i, (1,)).astype(jnp.int32)
    core, chip, me = as_operand(c_idx), as_operand(2 * x_idx + y_idx), as_operand(4 * x_idx + 2 * y_idx + c_idx)
    cos_t, sin_t = rope_tables(s)

    def vec(name, l):
        return w[name][l].reshape(1, -1)

    conv_taps = gather_conv_taps(conv_w, name="gather_conv_taps")

    saved = []
    h = rms_fwd(xs, vec("norm_mix_pre", 0), name="rms_first")
    first, after = [], h
    for i, keys in enumerate(FIRST_GATHER_GROUPS):
        first.append(_Gather(w, keys, 0, me, after, f"0{'abc'[i]}"))
        after = first[-1].token
    full = first[0].wait(first[0].pass_on(after))
    layers = []
    for l in range(depth):
        layers.append(full)
        nxt = l + 1 < depth
        if nxt:
            coming = _Gather(w, BIG, l + 1, me, full["w_in"], str(l + 1))
        z = matmul(h, full["w_in"], mode="nn", m=s, n=IN_WIDTH, k=D_MODEL, out_dtypes=[F32], name="mm_in",
                   after=[coming.token] if nxt else [])
        t = conv_fwd(z, conv_taps[l], name="conv_fwd")
        qp, kp, vb = qk_prep_fwd(z, vec("q_norm", l), vec("k_norm", l), cos_t, sin_t, name="qk_fwd")
        if l == 0:
            first[1].pass_on(qp)
        o, lse = attn_fwd(qp, kp, vb, name="attn_fwd")
        if l == 0:
            full.update(first[1].wait(o))
            first[2].pass_on(o)
        ya = matmul(t, full["w_out_conv"], mode="nn", m=s, n=D_MODEL, k=CONV_WIDTH, out_dtypes=[F32], name="mm_out_conv")
        yb = matmul(o, full["w_out_attn"], mode="nn", m=s, n=D_MODEL, k=ATTN_WIDTH, out_dtypes=[F32], name="mm_out_attn")
        tokens = [coming.pass_on(yb)] if nxt else []
        mi = gate_fwd(ya, yb, z, vec("gate_bias", l), name="gate_fwd")
        mixed = matmul(mi, full["w_merge"], mode="nn", m=s, n=D_MODEL, k=D_MODEL, out_dtypes=[F32], name="mm_merge",
                       after=tokens)
        if l == 0:
            full.update(first[2].wait(mixed))
        x_mid, h2 = rms_residual_fwd(xs, mixed, vec("norm_mix_post", l), vec("norm_mlp_pre", l), name="res_mix")
        act, r = matmul(h2, full["w_up"], mode="nn", m=s, n=D_FF, k=D_MODEL, out_dtypes=[MXU_DTYPE, MXU_DTYPE],
                        name="mm_up", epilogue=lambda acc: (acc, jnp.square(jnp.maximum(acc, 0.0))))
        f = matmul(r, full["w_down"], mode="nn", m=s, n=D_MODEL, k=D_FF, out_dtypes=[F32], name="mm_down")
        g_next = vec("norm_mix_pre", l + 1) if nxt else None
        x_out, h_next = rms_residual_fwd(x_mid, f, vec("norm_mlp_post", l), g_next, name="res_mlp" if nxt else "res_last")
        saved.append(dict(x_in=xs, h=h, z=z, t=t, qp=qp, kp=kp, vb=vb, o=o, lse=lse, ya=ya, yb=yb, mi=mi, mixed=mixed,
                          x_mid=x_mid, h2=h2, act=act, r=r, f=f))
        if nxt:
            full = coming.wait(x_out)
        xs, h = x_out, h_next

    dx, loss_part = loss_and_grad(xs, target, name="loss")
    loss = lax.psum(jnp.sum(loss_part), ("x", "y", "c"))

    small_rows = [None] * depth
    conv_rows = [None] * depth
    out_g, out_d, out_m, out_v = {}, {}, {}, {}
    big_outs = {k: None for k in BIG}
    pending = None
    last = []

    def finish(reduction, layer, after):
        for k, own, landed in reduction.end(after):
            big_outs[k] = reduce_adam(own, landed, chip, w[k], mom[k], var[k], layer, big_outs[k], name="adam_" + k)

    for l in reversed(range(depth)):
        sv, full = saved[l], layers[l]
        grads = {}
        groups = l == 0

        def wgrad(key, lhs, rhs, m, n, after=()):
            grads[key] = matmul(lhs, rhs, mode="tn", m=m, n=n, k=s, out_dtypes=[COMM_DTYPE], name="wg_" + key,
                                after=after)

        def begin_group(i, after):
            last.append(_Reduce(w, LAST_REDUCE_GROUPS[i], f"{l}{'abc'[i]}"))
            return [last[i].begin(grads, after)]

        df, dg_mlp_post = rms_bwd(sv["f"], vec("norm_mlp_post", l), dx, None, out_dtype=MXU_DTYPE, name="rmsb_mlp_post",
                                  after=[pending.token] if pending else [])
        da = matmul(df, full["w_down"], mode="nt", m=s, n=D_FF, k=D_MODEL, out_dtypes=[MXU_DTYPE], name="mm_d_down",
                    extra=(sv["act"],), epilogue=lambda acc, a: (acc * (2.0 * jnp.maximum(a.astype(F32), 0.0)),))
        tokens = [pending.middle(da, core)] if pending else []
        wgrad("w_down", sv["r"], df, D_FF, D_MODEL)
        dh2 = matmul(da, full["w_up"], mode="nt", m=s, n=D_MODEL, k=D_FF, out_dtypes=[F32], name="mm_d_up", after=tokens)
        wgrad("w_up", sv["h2"], da, D_MODEL, D_FF)
        tokens = begin_group(0, dh2) if groups else []
        dx_mid, dg_mlp_pre = rms_bwd(sv["x_mid"], vec("norm_mlp_pre", l), dh2, dx, out_dtype=F32, name="rmsb_mlp_pre",
                                     after=tokens)
        dmixed, dg_mix_post = rms_bwd(sv["mixed"], vec("norm_mix_post", l), dx_mid, None, out_dtype=MXU_DTYPE,
                                      name="rmsb_mix_post")
        dmi = matmul(dmixed, full["w_merge"], mode="nt", m=s, n=D_MODEL, k=D_MODEL, out_dtypes=[F32], name="mm_d_merge")
        tokens = [last[0].middle(dmi, core)] if groups else []
        wgrad("w_merge", sv["mi"], dmixed, D_MODEL, D_MODEL, after=tokens)
        dya, dyb, dga, dgb, dba, dbb = gate_bwd(dmi, sv["ya"], sv["yb"], sv["z"], vec("gate_bias", l), name="gate_bwd")
        wgrad("w_out_conv", sv["t"], dya, CONV_WIDTH, D_MODEL)
        wgrad("w_out_attn", sv["o"], dyb, ATTN_WIDTH, D_MODEL)
        tokens = begin_group(1, dyb) if groups else []
        dt = matmul(dya, full["w_out_conv"], mode="nt", m=s, n=CONV_WIDTH, k=D_MODEL, out_dtypes=[F32],
                    name="mm_d_out_conv", after=tokens)
        do = matmul(dyb, full["w_out_attn"], mode="nt", m=s, n=ATTN_WIDTH, k=D_MODEL, out_dtypes=[MXU_DTYPE],
                    name="mm_d_out_attn")
        tokens = [last[1].middle(do, core)] if groups else []
        dcb, dcc, dci, dconv_w = conv_bwd(dt, sv["z"], conv_taps[l], name="conv_bwd", after=tokens)
        dqp, dd = attn_bwd_q(sv["qp"], sv["kp"], sv["vb"], do, sv["o"], sv["lse"], name="attn_bwd_q")
        dkp, dv = attn_bwd_kv(sv["qp"], sv["kp"], sv["vb"], do, jnp.swapaxes(sv["lse"], 1, 2), jnp.swapaxes(dd, 1, 2),
                              name="attn_bwd_kv")
        dq, dk, dqn, dkn = qk_prep_bwd(dqp, dkp, sv["z"], vec("q_norm", l), vec("k_norm", l), cos_t, sin_t, name="qk_bwd")
        dz = jnp.concatenate([dcb, dcc, dci, dq, dk, dv, dga, dgb], axis=1)
        wgrad("w_in", sv["h"], dz, D_MODEL, IN_WIDTH)
        tokens = begin_group(2, dz) if groups else []
        dh = matmul(dz, full["w_in"], mode="nt", m=s, n=D_MODEL, k=IN_WIDTH, tk=IN_WIDTH // 4, out_dtypes=[F32],
                    name="mm_d_in", after=tokens)
        dx, dg_mix_pre = rms_bwd(sv["x_in"], vec("norm_mix_pre", l), dh, dx_mid, out_dtype=F32, name="rmsb_mix_pre")
        small_rows[l] = jnp.concatenate([dg_mix_pre, dba, dbb, dg_mix_post, dg_mlp_pre, dg_mlp_post, dqn, dkn], axis=1)
        conv_rows[l] = dconv_w.reshape(1, 3 * CONV_WIDTH)
        if pending:
            finish(pending, l + 1, dx)
        pending = None
        if not groups:
            pending = _Reduce(w, BIG, str(l))
            pending.begin(grads, dx)

    grad_x = dx.reshape(1, s, D_MODEL)

    small_part = jnp.concatenate([jnp.concatenate(small_rows, axis=0), jnp.concatenate(conv_rows, axis=0)], axis=1)
    small_sum = all_reduce_small(small_part, name="allreduce_small")
    pack = lambda src: jnp.concatenate([src[k].reshape(depth, wd) for k, wd in SMALL], axis=1)
    g_small = small_sum[:, :SMALL_WIDTH]
    d_small, m_small, v_small = adam_small(g_small, pack(w), pack(mom), pack(var), name="adam_small")
    off = 0
    for k, wd in SMALL:
        for dst, src in ((out_g, g_small), (out_d, d_small), (out_m, m_small), (out_v, v_small)):
            dst[k] = src[:, off:off + wd]
        off += wd
    cshard = CONV_WIDTH // N_DEV
    g_conv = lax.dynamic_slice_in_dim(small_sum[:, SMALL_WIDTH:].reshape(depth, 3, CONV_WIDTH), me[0] * cshard, cshard,
                                      axis=2).reshape(depth, 3 * cshard)
    flat = lambda a: a.reshape(depth, 3 * cshard)
    d_conv, m_conv, v_conv = adam_small(g_conv, flat(conv_w), flat(m_conv_w), flat(v_conv_w), name="adam_conv")
    for dst, src in ((out_g, g_conv), (out_d, d_conv), (out_m, m_conv), (out_v, v_conv)):
        dst["conv_w"] = src.reshape(depth, 3, cshard)

    token = last[2].middle(d_small, core)
    for reduction in last:
        finish(reduction, 0, token)
    for k in BIG:
        out_g[k], out_d[k], out_m[k], out_v[k] = big_outs[k]

    return (loss, grad_x, *[out_g[k] for k in WEIGHTS], *[out_d[k] for k in WEIGHTS], *[out_m[k] for k in WEIGHTS],
            *[out_v[k] for k in WEIGHTS])
```

```python
import math

import jax
import jax.numpy as jnp
from jax import lax
from jax.experimental import pallas as pl
from jax.experimental.pallas import tpu as pltpu

F32 = jnp.float32
MXU_DTYPE = jnp.bfloat16
COMM_DTYPE = jnp.bfloat16

D_MODEL = 2048
HEAD_DIM = 128
N_Q_HEADS = 16
N_KV_HEADS = 4
GROUP = N_Q_HEADS // N_KV_HEADS
ATTN_WIDTH = N_Q_HEADS * HEAD_DIM
KV_WIDTH = N_KV_HEADS * HEAD_DIM
CONV_WIDTH = D_MODEL
D_FF = 4 * D_MODEL
GRID_W = 64
ROPE_THETA = 10000.0
RMS_EPS = 1e-6
IN_WIDTH = 3 * CONV_WIDTH + ATTN_WIDTH + 2 * KV_WIDTH + 2 * D_MODEL
OFF_CB, OFF_CC, OFF_CI = 0, CONV_WIDTH, 2 * CONV_WIDTH
OFF_Q = 3 * CONV_WIDTH
OFF_K = OFF_Q + ATTN_WIDTH
OFF_V = OFF_K + KV_WIDTH
OFF_GA = OFF_V + KV_WIDTH
OFF_GB = OFF_GA + D_MODEL
ATTN_SCALE = 1.0 / math.sqrt(HEAD_DIM)

ADAM_LR, ADAM_B1, ADAM_B2, ADAM_EPS, ADAM_WD, ADAM_STEP = 0.001, 0.9, 0.999, 1e-08, 0.01, 10

N_DEV = 8
N_CHIP = 4
LANE = 128
SUBLANE = 8
VMEM_LIMIT = 48 * 1024 * 1024
MESH = pl.DeviceIdType.MESH
ANY = pl.BlockSpec(memory_space=pl.ANY)


def _sds(shape, dtype):
    return jax.ShapeDtypeStruct(tuple(shape), dtype)


def _params(sem, vmem=VMEM_LIMIT):
    return pltpu.CompilerParams(dimension_semantics=sem, vmem_limit_bytes=vmem)


def _rows(ts, w, col=0):
    return pl.BlockSpec((ts, w), lambda i: (i, col))


def _fixed(shape):
    return pl.BlockSpec(shape, lambda *_: (0,) * len(shape))


def _fold8(v):
    ts, w = v.shape
    return v.reshape(ts // SUBLANE, SUBLANE, w).sum(axis=0)


def matmul(a, b, *, mode, m, n, k, name, out_dtypes, tm=1024, tn=1024, tk=2048, epilogue=None, extra=(), after=()):
    tm, tn, tk = min(tm, m), min(tn, n), min(tk, k)
    nm, nn, nk = m // tm, n // tn, k // tk
    assert nm * tm == m and nn * tn == n and nk * tk == k, (name, m, n, k, tm, tn, tk)
    if mode == "tn":
        a_spec = pl.BlockSpec((tk, tm), lambda i, j, kk: (kk, i))
        dims = (((0,), (0,)), ((), ()))
    else:
        a_spec = pl.BlockSpec((tm, tk), lambda i, j, kk: (i, kk))
        dims = (((1,), (1 if mode == "nt" else 0,)), ((), ()))
    if mode == "nt":
        b_spec = pl.BlockSpec((tn, tk), lambda i, j, kk: (j, kk))
    else:
        b_spec = pl.BlockSpec((tk, tn), lambda i, j, kk: (kk, j))
    tile = pl.BlockSpec((tm, tn), lambda i, j, kk: (i, j))
    n_out, n_extra, n_after = len(out_dtypes), len(extra), len(after)
    if epilogue is None:
        epilogue = lambda acc: (acc,)

    def body(a_ref, b_ref, *rest):
        extra_refs = rest[:n_extra]
        outs = rest[n_extra + n_after:][:n_out]
        part = lax.dot_general(a_ref[...].astype(MXU_DTYPE), b_ref[...].astype(MXU_DTYPE), dims,
                               preferred_element_type=F32)

        def finish(acc):
            for o_ref, val in zip(outs, epilogue(acc, *[r[...] for r in extra_refs])):
                o_ref[...] = val.astype(o_ref.dtype)

        if nk == 1:
            finish(part)
        else:
            acc_ref = rest[-1]
            kk = pl.program_id(2)

            @pl.when(kk == 0)
            def _():
                acc_ref[...] = part

            @pl.when(kk > 0)
            def _():
                acc_ref[...] += part

            @pl.when(kk == nk - 1)
            def _():
                finish(acc_ref[...])

    res = pl.pallas_call(
        body, name=name, grid=(nm, nn, nk), in_specs=[a_spec, b_spec] + [tile] * n_extra + [ANY] * n_after,
        out_specs=[tile] * n_out, out_shape=[_sds((m, n), dt) for dt in out_dtypes],
        scratch_shapes=[pltpu.VMEM((tm, tn), F32)] if nk > 1 else [],
        compiler_params=_params(("parallel", "parallel", "arbitrary")),
    )(a, b, *extra, *after)
    return res[0] if n_out == 1 else res


def _rstd(x):
    return lax.rsqrt(jnp.mean(x * x, axis=-1, keepdims=True) + RMS_EPS)


def _rms_bwd(x, g, dy):
    rstd = _rstd(x)
    xh = x * rstd
    gy = dy * g
    dx = rstd * (gy - xh * jnp.mean(gy * xh, axis=-1, keepdims=True))
    return dx, dy * xh


def rms_fwd(x, g, *, name):
    s = x.shape[0]
    ts = min(s, 512)

    def body(x_ref, g_ref, h_ref):
        xv = x_ref[...]
        h_ref[...] = (xv * _rstd(xv) * g_ref[...]).astype(h_ref.dtype)

    return pl.pallas_call(
        body, name=name, grid=(s // ts,), in_specs=[_rows(ts, D_MODEL), _fixed((1, D_MODEL))],
        out_specs=_rows(ts, D_MODEL), out_shape=_sds((s, D_MODEL), MXU_DTYPE), compiler_params=_params(("parallel",)),
    )(x, g)


def rms_residual_fwd(x, y, g_post, g_next, *, name):
    s = x.shape[0]
    ts = min(s, 512)
    with_next = g_next is not None

    def body(x_ref, y_ref, gp_ref, *rest):
        yv = y_ref[...]
        xn = x_ref[...] + yv * _rstd(yv) * gp_ref[...]
        if with_next:
            gn_ref, xo_ref, h_ref = rest
            h_ref[...] = (xn * _rstd(xn) * gn_ref[...]).astype(h_ref.dtype)
        else:
            (xo_ref,) = rest
        xo_ref[...] = xn

    gspec = _fixed((1, D_MODEL))
    res = pl.pallas_call(
        body, name=name, grid=(s // ts,),
        in_specs=[_rows(ts, D_MODEL), _rows(ts, D_MODEL), gspec] + [gspec] * with_next,
        out_specs=[_rows(ts, D_MODEL)] * (1 + with_next),
        out_shape=[_sds((s, D_MODEL), F32)] + [_sds((s, D_MODEL), MXU_DTYPE)] * with_next,
        compiler_params=_params(("parallel",)),
    )(x, y, g_post, *([g_next] if with_next else []))
    return (res[0], res[1]) if with_next else (res[0], None)


def rms_bwd(x, g, dy, residual, *, out_dtype, name, after=()):
    s = x.shape[0]
    ts = min(s, 256)
    nt = s // ts
    with_res = residual is not None

    def body(x_ref, g_ref, dy_ref, *rest):
        dx_ref, dg_ref, acc_ref = rest[-3:]
        dx, dg_rows = _rms_bwd(x_ref[...], g_ref[...], dy_ref[...].astype(F32))
        if with_res:
            dx = dx + rest[0][...]
        dx_ref[...] = dx.astype(dx_ref.dtype)
        i = pl.program_id(0)

        @pl.when(i == 0)
        def _():
            acc_ref[...] = jnp.zeros_like(acc_ref)

        acc_ref[...] += _fold8(dg_rows)

        @pl.when(i == nt - 1)
        def _():
            dg_ref[...] = acc_ref[...].sum(axis=0, keepdims=True)

    return pl.pallas_call(
        body, name=name, grid=(nt,),
        in_specs=[_rows(ts, D_MODEL), _fixed((1, D_MODEL)), _rows(ts, D_MODEL)] + [_rows(ts, D_MODEL)] * with_res
        + [ANY] * len(after),
        out_specs=[_rows(ts, D_MODEL), _fixed((1, D_MODEL))],
        out_shape=[_sds((s, D_MODEL), out_dtype), _sds((1, D_MODEL), F32)],
        scratch_shapes=[pltpu.VMEM((SUBLANE, D_MODEL), F32)], compiler_params=_params(("arbitrary",)),
    )(x, g, dy, *([residual] if with_res else []), *after)


def loss_and_grad(y, target, *, name):
    s = y.shape[0]
    ts = min(s, 512)
    nt = s // ts

    def body(y_ref, t_ref, dy_ref, part_ref):
        e = y_ref[...] - t_ref[...]
        dy_ref[...] = e * (1.0 / D_MODEL)
        sq = _fold8(e * e)
        lanes = sq[:, 0:LANE]
        for j in range(1, D_MODEL // LANE):
            lanes = lanes + sq[:, j * LANE:(j + 1) * LANE]
        i = pl.program_id(0)

        @pl.when(i == 0)
        def _():
            part_ref[...] = jnp.zeros_like(part_ref)

        part_ref[...] += lanes * (0.5 / D_MODEL)

    return pl.pallas_call(
        body, name=name, grid=(nt,), in_specs=[_rows(ts, D_MODEL), _rows(ts, D_MODEL)],
        out_specs=[_rows(ts, D_MODEL), _fixed((SUBLANE, LANE))],
        out_shape=[_sds((s, D_MODEL), F32), _sds((SUBLANE, LANE), F32)], compiler_params=_params(("arbitrary",)),
    )(y, target)


CONV_TC = LANE


def _conv_taps(u, s):
    row = lax.broadcasted_iota(jnp.int32, u.shape, 0)
    prev = jnp.where(row == 0, 0.0, pltpu.roll(u, 1, 0))
    nxt = jnp.where(row == s - 1, 0.0, pltpu.roll(u, s - 1, 0))
    return prev, nxt


def _zcol(s, off):
    return pl.BlockSpec((s, CONV_TC), lambda j: (0, off // CONV_TC + j))


def conv_fwd(z, w3, *, name):
    s = z.shape[0]

    def body(cb_ref, cc_ref, ci_ref, w_ref, t_ref):
        u = cc_ref[...] * ci_ref[...]
        prev, nxt = _conv_taps(u, s)
        w = w_ref[...]
        conv = w[0:1] * prev + w[1:2] * u + w[2:3] * nxt
        t_ref[...] = (cb_ref[...] * conv).astype(t_ref.dtype)

    return pl.pallas_call(
        body, name=name, grid=(CONV_WIDTH // CONV_TC,),
        in_specs=[_zcol(s, OFF_CB), _zcol(s, OFF_CC), _zcol(s, OFF_CI), pl.BlockSpec((3, CONV_TC), lambda j: (0, j))],
        out_specs=pl.BlockSpec((s, CONV_TC), lambda j: (0, j)), out_shape=_sds((s, CONV_WIDTH), MXU_DTYPE),
        compiler_params=_params(("parallel",)),
    )(z, z, z, w3)


def conv_bwd(dt, z, w3, *, name, after=()):
    s = z.shape[0]

    def body(dt_ref, cb_ref, cc_ref, ci_ref, w_ref, *rest):
        dcb_ref, dcc_ref, dci_ref, dw_ref = rest[-4:]
        cc, ci = cc_ref[...], ci_ref[...]
        u = cc * ci
        prev, nxt = _conv_taps(u, s)
        w = w_ref[...]
        dtv = dt_ref[...]
        dcb_ref[...] = (dtv * (w[0:1] * prev + w[1:2] * u + w[2:3] * nxt)).astype(dcb_ref.dtype)
        dconv = dtv * cb_ref[...]
        dprev, dnxt = _conv_taps(dconv, s)
        du = w[0:1] * dnxt + w[1:2] * dconv + w[2:3] * dprev
        dcc_ref[...] = (du * ci).astype(dcc_ref.dtype)
        dci_ref[...] = (du * cc).astype(dci_ref.dtype)
        dw_ref[0:1, :] = jnp.sum(dconv * prev, axis=0, keepdims=True)
        dw_ref[1:2, :] = jnp.sum(dconv * u, axis=0, keepdims=True)
        dw_ref[2:3, :] = jnp.sum(dconv * nxt, axis=0, keepdims=True)

    col = pl.BlockSpec((s, CONV_TC), lambda j: (0, j))
    wspec = pl.BlockSpec((3, CONV_TC), lambda j: (0, j))
    return pl.pallas_call(
        body, name=name, grid=(CONV_WIDTH // CONV_TC,),
        in_specs=[col, _zcol(s, OFF_CB), _zcol(s, OFF_CC), _zcol(s, OFF_CI), wspec] + [ANY] * len(after),
        out_specs=[col, col, col, wspec],
        out_shape=[_sds((s, CONV_WIDTH), MXU_DTYPE)] * 3 + [_sds((3, CONV_WIDTH), F32)],
        compiler_params=_params(("parallel",)),
    )(dt, z, z, z, w3, *after)


def rope_tables(s):
    n_freq = HEAD_DIM // 4
    t = jnp.arange(s, dtype=jnp.int32)
    inv_freq = ROPE_THETA ** (-jnp.arange(0, HEAD_DIM // 2, 2, dtype=F32) / (HEAD_DIM // 2))
    ang_r = (t // GRID_W).astype(F32)[:, None] * inv_freq
    ang_c = (t % GRID_W).astype(F32)[:, None] * inv_freq
    cos_t = jnp.concatenate([jnp.cos(ang_r)] * 2 + [jnp.cos(ang_c)] * 2, axis=1)
    sin_t = jnp.concatenate([-jnp.sin(ang_r), jnp.sin(ang_r), -jnp.sin(ang_c), jnp.sin(ang_c)], axis=1)
    assert cos_t.shape == (s, 4 * n_freq)
    return cos_t, sin_t


def _swap_halves(v):
    lane = lax.broadcasted_iota(jnp.int32, v.shape, 1)
    return jnp.where(lane % 64 < 32, pltpu.roll(v, HEAD_DIM - 32, 1), pltpu.roll(v, 32, 1))


def qk_prep_fwd(z, qn, kn, cos_t, sin_t, *, name):
    s = z.shape[0]
    ts = min(s, 256)

    def body(q_ref, k_ref, v_ref, qn_ref, kn_ref, c_ref, s_ref, qo_ref, ko_ref, vo_ref):
        cs, sn = c_ref[...], s_ref[...]

        def head(x, g, scale):
            n = x * _rstd(x) * g
            return (n * cs + _swap_halves(n) * sn) * scale

        for h in range(N_Q_HEADS):
            sl = slice(h * HEAD_DIM, (h + 1) * HEAD_DIM)
            qo_ref[:, sl] = head(q_ref[:, sl], qn_ref[...], ATTN_SCALE).astype(qo_ref.dtype)
        for h in range(N_KV_HEADS):
            sl = slice(h * HEAD_DIM, (h + 1) * HEAD_DIM)
            ko_ref[:, sl] = head(k_ref[:, sl], kn_ref[...], 1.0).astype(ko_ref.dtype)
        vo_ref[...] = v_ref[...].astype(vo_ref.dtype)

    tab = _rows(ts, HEAD_DIM)
    gsp = _fixed((1, HEAD_DIM))
    return pl.pallas_call(
        body, name=name, grid=(s // ts,),
        in_specs=[_rows(ts, ATTN_WIDTH, OFF_Q // ATTN_WIDTH), _rows(ts, KV_WIDTH, OFF_K // KV_WIDTH),
                  _rows(ts, KV_WIDTH, OFF_V // KV_WIDTH), gsp, gsp, tab, tab],
        out_specs=[_rows(ts, ATTN_WIDTH), _rows(ts, KV_WIDTH), _rows(ts, KV_WIDTH)],
        out_shape=[_sds((s, ATTN_WIDTH), MXU_DTYPE), _sds((s, KV_WIDTH), MXU_DTYPE), _sds((s, KV_WIDTH), MXU_DTYPE)],
        compiler_params=_params(("parallel",)),
    )(z, z, z, qn, kn, cos_t, sin_t)


def qk_prep_bwd(dqp, dkp, z, qn, kn, cos_t, sin_t, *, name):
    s = z.shape[0]
    ts = min(s, 256)
    nt = s // ts

    def body(dq_ref, dk_ref, q_ref, k_ref, qn_ref, kn_ref, c_ref, s_ref, dqo_ref, dko_ref, dqn_ref, dkn_ref,
             qacc_ref, kacc_ref):
        cs, sn = c_ref[...], s_ref[...]
        i = pl.program_id(0)

        @pl.when(i == 0)
        def _():
            qacc_ref[...] = jnp.zeros_like(qacc_ref)
            kacc_ref[...] = jnp.zeros_like(kacc_ref)

        def head(x, g, dout, scale):
            d = dout.astype(F32) * scale
            dn = d * cs + _swap_halves(d * sn)
            return _rms_bwd(x, g, dn)

        qacc = jnp.zeros((SUBLANE, HEAD_DIM), F32)
        for h in range(N_Q_HEADS):
            sl = slice(h * HEAD_DIM, (h + 1) * HEAD_DIM)
            dx, dg_rows = head(q_ref[:, sl], qn_ref[...], dq_ref[:, sl], ATTN_SCALE)
            dqo_ref[:, sl] = dx.astype(dqo_ref.dtype)
            qacc = qacc + _fold8(dg_rows)
        kacc = jnp.zeros((SUBLANE, HEAD_DIM), F32)
        for h in range(N_KV_HEADS):
            sl = slice(h * HEAD_DIM, (h + 1) * HEAD_DIM)
            dx, dg_rows = head(k_ref[:, sl], kn_ref[...], dk_ref[:, sl], 1.0)
            dko_ref[:, sl] = dx.astype(dko_ref.dtype)
            kacc = kacc + _fold8(dg_rows)
        qacc_ref[...] += qacc
        kacc_ref[...] += kacc

        @pl.when(i == nt - 1)
        def _():
            dqn_ref[...] = qacc_ref[...].sum(axis=0, keepdims=True)
            dkn_ref[...] = kacc_ref[...].sum(axis=0, keepdims=True)

    tab = _rows(ts, HEAD_DIM)
    gsp = _fixed((1, HEAD_DIM))
    return pl.pallas_call(
        body, name=name, grid=(nt,),
        in_specs=[_rows(ts, ATTN_WIDTH), _rows(ts, KV_WIDTH), _rows(ts, ATTN_WIDTH, OFF_Q // ATTN_WIDTH),
                  _rows(ts, KV_WIDTH, OFF_K // KV_WIDTH), gsp, gsp, tab, tab],
        out_specs=[_rows(ts, ATTN_WIDTH), _rows(ts, KV_WIDTH), gsp, gsp],
        out_shape=[_sds((s, ATTN_WIDTH), MXU_DTYPE), _sds((s, KV_WIDTH), MXU_DTYPE), _sds((1, HEAD_DIM), F32),
                   _sds((1, HEAD_DIM), F32)],
        scratch_shapes=[pltpu.VMEM((SUBLANE, HEAD_DIM), F32)] * 2, compiler_params=_params(("arbitrary",)),
    )(dqp, dkp, z, z, qn, kn, cos_t, sin_t)


_NT = (((1,), (1,)), ((), ()))
_GW = GROUP * HEAD_DIM


def _dot(a, b, dims=(((1,), (0,)), ((), ()))):
    return lax.dot_general(a, b, dims, preferred_element_type=F32)


def attn_fwd(qp, kp, vb, *, name, after=()):
    s = qp.shape[0]
    tq = min(s, 256)

    def body(q_ref, k_ref, v_ref, *rest):
        o_ref, lse_ref = rest[-2:]
        k, v = k_ref[...], v_ref[...]
        for g in range(GROUP):
            sl = slice(g * HEAD_DIM, (g + 1) * HEAD_DIM)
            sc = _dot(q_ref[:, sl], k, _NT)
            mx = jnp.max(sc, axis=-1, keepdims=True)
            p = jnp.exp(sc - mx)
            den = jnp.sum(p, axis=-1, keepdims=True)
            o = _dot(p.astype(v.dtype), v)
            o_ref[:, sl] = (o / den).astype(o_ref.dtype)
            lse_ref[:, g:g + 1] = mx + jnp.log(den)

    return pl.pallas_call(
        body, name=name, grid=(N_KV_HEADS, s // tq),
        in_specs=[pl.BlockSpec((tq, _GW), lambda j, i: (i, j)), pl.BlockSpec((s, HEAD_DIM), lambda j, i: (0, j)),
                  pl.BlockSpec((s, HEAD_DIM), lambda j, i: (0, j))] + [ANY] * len(after),
        out_specs=[pl.BlockSpec((tq, _GW), lambda j, i: (i, j)), pl.BlockSpec((None, tq, GROUP), lambda j, i: (j, i, 0))],
        out_shape=[_sds((s, ATTN_WIDTH), MXU_DTYPE), _sds((N_KV_HEADS, s, GROUP), F32)],
        compiler_params=_params(("parallel", "parallel")),
    )(qp, kp, vb, *after)


ATTN_BWD_TQ = 256


def attn_bwd(qp, kp, vb, do, o, lse, *, name):
    s = qp.shape[0]
    tq = min(s, ATTN_BWD_TQ)
    nq = s // tq
    over_rows = (((0,), (0,)), ((), ()))

    def body(q_ref, k_ref, v_ref, do_ref, o_ref, lse_ref, dq_ref, dk_ref, dv_ref, p_all, ds_all, q_all, do_all, dk_acc,
             dv_acc):
        i = pl.program_id(1)

        @pl.when(i == 0)
        def _():
            dk_acc[...] = jnp.zeros_like(dk_acc)
            dv_acc[...] = jnp.zeros_like(dv_acc)

        k, v = k_ref[...], v_ref[...]
        for g in range(GROUP):
            sl = slice(g * HEAD_DIM, (g + 1) * HEAD_DIM)
            rows = slice(g * tq, (g + 1) * tq)
            qg, dog = q_ref[:, sl], do_ref[:, sl]
            dd = jnp.sum(dog.astype(F32) * o_ref[:, sl].astype(F32), axis=-1, keepdims=True)
            p = jnp.exp(_dot(qg, k, _NT) - lse_ref[:, g:g + 1])
            ds = (p * (_dot(dog, v, _NT) - dd)).astype(k.dtype)
            dq_ref[:, sl] = _dot(ds, k)
            p_all[rows, :] = p.astype(p_all.dtype)
            ds_all[rows, :] = ds
            q_all[rows, :] = qg
            do_all[rows, :] = dog
        dv_acc[...] += _dot(p_all[...], do_all[...], over_rows)
        dk_acc[...] += _dot(ds_all[...], q_all[...], over_rows)

        @pl.when(i == nq - 1)
        def _():
            dk_ref[...] = dk_acc[...]
            dv_ref[...] = dv_acc[...].astype(dv_ref.dtype)

    qspec = pl.BlockSpec((tq, _GW), lambda j, i: (i, j))
    kspec = pl.BlockSpec((s, HEAD_DIM), lambda j, i: (0, j))
    lspec = pl.BlockSpec((None, tq, GROUP), lambda j, i: (j, i, 0))
    return pl.pallas_call(
        body, name=name, grid=(N_KV_HEADS, nq), in_specs=[qspec, kspec, kspec, qspec, qspec, lspec],
        out_specs=[qspec, kspec, kspec],
        out_shape=[_sds((s, ATTN_WIDTH), F32), _sds((s, KV_WIDTH), F32), _sds((s, KV_WIDTH), MXU_DTYPE)],
        scratch_shapes=[pltpu.VMEM((GROUP * tq, s), MXU_DTYPE), pltpu.VMEM((GROUP * tq, s), MXU_DTYPE),
                        pltpu.VMEM((GROUP * tq, HEAD_DIM), MXU_DTYPE), pltpu.VMEM((GROUP * tq, HEAD_DIM), MXU_DTYPE),
                        pltpu.VMEM((s, HEAD_DIM), F32), pltpu.VMEM((s, HEAD_DIM), F32)],
        compiler_params=_params(("parallel", "arbitrary")),
    )(qp, kp, vb, do, o, lse)


GATE_TW = 1024


def gate_fwd(ya, yb, z, bias, *, name):
    s = z.shape[0]
    ts = min(s, 512)

    def body(ya_ref, yb_ref, ga_ref, gb_ref, ba_ref, bb_ref, o_ref):
        sa = jax.nn.sigmoid(ga_ref[...] + ba_ref[...])
        sb = jax.nn.sigmoid(gb_ref[...] + bb_ref[...])
        o_ref[...] = (sa * ya_ref[...] + sb * yb_ref[...]).astype(o_ref.dtype)

    tile = pl.BlockSpec((ts, GATE_TW), lambda i, j: (i, j))
    ga = pl.BlockSpec((ts, GATE_TW), lambda i, j: (i, OFF_GA // GATE_TW + j))
    gb = pl.BlockSpec((ts, GATE_TW), lambda i, j: (i, OFF_GB // GATE_TW + j))
    ba = pl.BlockSpec((1, GATE_TW), lambda i, j: (0, j))
    bb = pl.BlockSpec((1, GATE_TW), lambda i, j: (0, D_MODEL // GATE_TW + j))
    return pl.pallas_call(
        body, name=name, grid=(s // ts, D_MODEL // GATE_TW), in_specs=[tile, tile, ga, gb, ba, bb], out_specs=tile,
        out_shape=_sds((s, D_MODEL), MXU_DTYPE), compiler_params=_params(("parallel", "parallel")),
    )(ya, yb, z, z, bias, bias)


def gate_bwd(dmi, ya, yb, z, bias, *, name):
    s = z.shape[0]
    ts = min(s, 512)
    nt = s // ts

    def body(d_ref, ya_ref, yb_ref, ga_ref, gb_ref, ba_ref, bb_ref, dya_ref, dyb_ref, dga_ref, dgb_ref, dba_ref,
             dbb_ref, acc_a, acc_b):
        d = d_ref[...]
        sa = jax.nn.sigmoid(ga_ref[...] + ba_ref[...])
        sb = jax.nn.sigmoid(gb_ref[...] + bb_ref[...])
        dya_ref[...] = (d * sa).astype(dya_ref.dtype)
        dyb_ref[...] = (d * sb).astype(dyb_ref.dtype)
        dga = d * ya_ref[...] * sa * (1.0 - sa)
        dgb = d * yb_ref[...] * sb * (1.0 - sb)
        dga_ref[...] = dga.astype(dga_ref.dtype)
        dgb_ref[...] = dgb.astype(dgb_ref.dtype)
        i = pl.program_id(1)

        @pl.when(i == 0)
        def _():
            acc_a[...] = jnp.zeros_like(acc_a)
            acc_b[...] = jnp.zeros_like(acc_b)

        acc_a[...] += _fold8(dga)
        acc_b[...] += _fold8(dgb)

        @pl.when(i == nt - 1)
        def _():
            dba_ref[...] = acc_a[...].sum(axis=0, keepdims=True)
            dbb_ref[...] = acc_b[...].sum(axis=0, keepdims=True)

    tile = pl.BlockSpec((ts, GATE_TW), lambda j, i: (i, j))
    ga = pl.BlockSpec((ts, GATE_TW), lambda j, i: (i, OFF_GA // GATE_TW + j))
    gb = pl.BlockSpec((ts, GATE_TW), lambda j, i: (i, OFF_GB // GATE_TW + j))
    ba = pl.BlockSpec((1, GATE_TW), lambda j, i: (0, j))
    bb = pl.BlockSpec((1, GATE_TW), lambda j, i: (0, D_MODEL // GATE_TW + j))
    acc = pltpu.VMEM((SUBLANE, GATE_TW), F32)
    return pl.pallas_call(
        body, name=name, grid=(D_MODEL // GATE_TW, nt), in_specs=[tile, tile, tile, ga, gb, ba, bb],
        out_specs=[tile, tile, tile, tile, ba, ba],
        out_shape=[_sds((s, D_MODEL), MXU_DTYPE)] * 4 + [_sds((1, D_MODEL), F32)] * 2,
        scratch_shapes=[acc, acc], compiler_params=_params(("parallel", "arbitrary")),
    )(dmi, ya, yb, z, z, bias, bias)


HBM = pl.BlockSpec(memory_space=pltpu.HBM)
SEM = pl.BlockSpec(memory_space=pltpu.SEMAPHORE)
EFFECT = pltpu.SideEffectType.DATAFLOW_SIDE_EFFECTING
TOKEN = (SUBLANE, LANE)


def _place():
    return lax.axis_index("x"), lax.axis_index("y"), lax.axis_index("c")


def _window(ref, shard_shape, axis, d):
    r, c = shard_shape
    lead = (slice(None),) * (len(ref.shape) - 2)
    if axis == 0:
        return ref.at[lead + (pl.ds(pl.multiple_of(d * r, SUBLANE), r), slice(None))]
    return ref.at[lead + (slice(None), pl.ds(pl.multiple_of(d * c, LANE), c))]


def _hbm(a):
    return pltpu.with_memory_space_constraint(a, pltpu.HBM)


def _remote(src, dst, send_sems, recv_sems, i, to):
    return pltpu.make_async_remote_copy(src_ref=src, dst_ref=dst, send_sem=send_sems.at[i], recv_sem=recv_sems.at[i],
                                        device_id=to, device_id_type=MESH)


def cast_into_full(w, layer, axis, me, dtype, *, name):
    _, r, c = w.shape
    tr = min(r, 256)
    nr = r // tr
    in_spec = pl.BlockSpec((None, tr, c), lambda i, me_ref: (layer, i, 0))
    if axis == 0:
        out_spec = pl.BlockSpec((tr, c), lambda i, me_ref: (me_ref[0] * nr + i, 0))
        shape = (N_DEV * r, c)
    else:
        out_spec = pl.BlockSpec((tr, c), lambda i, me_ref: (i, me_ref[0]))
        shape = (r, N_DEV * c)

    def body(me_ref, w_ref, o_ref):
        o_ref[...] = w_ref[...].astype(o_ref.dtype)

    return pl.pallas_call(
        body, name=name,
        grid_spec=pltpu.PrefetchScalarGridSpec(num_scalar_prefetch=1, grid=(nr,), in_specs=[in_spec], out_specs=out_spec),
        out_shape=_sds(shape, dtype), compiler_params=_params(("parallel",)),
    )(me, w)


class _GatherPlan:
    def __init__(self, fulls, shard_shapes, axes):
        x, y, c = _place()
        self.n = len(fulls)
        self.me, self.sibling = (x, y, c), (x, y, 1 - c)
        self.chips = [(1 - x, y), (x, 1 - y), (1 - x, 1 - y)]
        self.win = lambda a, p: _window(fulls[a], shard_shapes[a], axes[a], 4 * p[0] + 2 * p[1] + p[2])

    def first(self, send_sems, recv_sems):
        out = []
        for a in range(self.n):
            mine = self.win(a, self.me)
            out.append(_remote(mine, mine, send_sems, recv_sems, 4 * a, self.sibling))
            out += [_remote(mine, mine, send_sems, recv_sems, 4 * a + 1 + j, (*chip, self.me[2]))
                    for j, chip in enumerate(self.chips)]
        return out

    def first_arrivals(self, send_sems, recv_sems):
        c = self.me[2]
        out = []
        for a in range(self.n):
            blocks = [self.sibling] + [(*chip, c) for chip in self.chips]
            out += [_remote(self.win(a, b), self.win(a, b), send_sems, recv_sems, 4 * a + k, self.me)
                    for k, b in enumerate(blocks)]
        return out

    def passed(self, send_sems, recv_sems):
        c = self.me[2]
        return [_remote(self.win(a, (*chip, c)), self.win(a, (*chip, c)), send_sems, recv_sems, 3 * a + j, self.sibling)
                for a in range(self.n) for j, chip in enumerate(self.chips)]

    def passed_arrivals(self, send_sems, recv_sems):
        c = self.me[2]
        return [_remote(self.win(a, (*chip, 1 - c)), self.win(a, (*chip, 1 - c)), send_sems, recv_sems, 3 * a + j, self.me)
                for a in range(self.n) for j, chip in enumerate(self.chips)]


def gather_start(fulls, after, shard_shapes, axes, *, name):
    n = len(fulls)

    def body(*refs):
        ins, send_sems, recv_sems, token = refs[:n], refs[n + 1], refs[n + 2], refs[-1]
        for cp in _GatherPlan(ins, shard_shapes, axes).first(send_sems, recv_sems):
            cp.start()
        token[...] = jnp.zeros_like(token)

    res = pl.pallas_call(
        body, name=name, in_specs=[HBM] * n + [ANY],
        out_shape=(pltpu.SemaphoreType.DMA((4 * n,)), pltpu.SemaphoreType.DMA((4 * n,)),
                   *[pltpu.HBM(f.shape, f.dtype) for f in fulls], _sds(TOKEN, F32)),
        out_specs=(SEM, SEM, *[HBM] * n, pl.BlockSpec(memory_space=pltpu.VMEM)),
        input_output_aliases={a: 2 + a for a in range(n)}, compiler_params=pltpu.CompilerParams(has_side_effects=EFFECT),
    )(*[_hbm(f) for f in fulls], after)
    return res[0], res[1], list(res[2:2 + n]), res[-1]


def gather_pass(send_sems, recv_sems, fulls, after, shard_shapes, axes, *, name):
    n = len(fulls)

    def body(*refs):
        ins, s1, r1 = refs[:n], refs[n], refs[n + 1]
        s2, r2, token = refs[n + 3], refs[n + 4], refs[-1]
        plan = _GatherPlan(ins, shard_shapes, axes)
        for cp in plan.first_arrivals(s1, r1):
            cp.wait_recv()
        for cp in plan.first(s1, r1):
            cp.wait_send()
        for cp in plan.passed(s2, r2):
            cp.start()
        token[...] = jnp.zeros_like(token)

    res = pl.pallas_call(
        body, name=name, in_specs=[HBM] * n + [SEM, SEM, ANY],
        out_shape=(pltpu.SemaphoreType.DMA((3 * n,)), pltpu.SemaphoreType.DMA((3 * n,)),
                   *[pltpu.HBM(f.shape, f.dtype) for f in fulls], _sds(TOKEN, F32)),
        out_specs=(SEM, SEM, *[HBM] * n, pl.BlockSpec(memory_space=pltpu.VMEM)),
        input_output_aliases={a: 2 + a for a in range(n)}, compiler_params=pltpu.CompilerParams(has_side_effects=EFFECT),
    )(*fulls, send_sems, recv_sems, after)
    return res[0], res[1], list(res[2:2 + n]), res[-1]


def gather_wait(send_sems, recv_sems, fulls, after, shard_shapes, axes, *, name):
    n = len(fulls)

    def body(*refs):
        ins, s2, r2 = refs[:n], refs[n], refs[n + 1]
        plan = _GatherPlan(ins, shard_shapes, axes)
        for cp in plan.passed_arrivals(s2, r2):
            cp.wait_recv()
        for cp in plan.passed(s2, r2):
            cp.wait_send()

    return list(pl.pallas_call(
        body, name=name, in_specs=[HBM] * n + [SEM, SEM, ANY], out_shape=tuple(pltpu.HBM(f.shape, f.dtype) for f in fulls),
        out_specs=tuple([HBM] * n), input_output_aliases={a: a for a in range(n)},
        compiler_params=pltpu.CompilerParams(has_side_effects=EFFECT),
    )(*fulls, send_sems, recv_sems, after))


def _pair_copies(grads, lands, shard_shapes, axes, send_sems, recv_sems):
    x, y, c = _place()
    return [_remote(_window(grads[a], shard_shapes[a], axes[a], 2 * q + (1 - c)), lands[a].at[q], send_sems, recv_sems,
                    N_CHIP * a + q, (x, y, 1 - c))
            for a in range(len(grads)) for q in range(N_CHIP)]


def _chip_sends(sums, lands, send_sems, recv_sems):
    x, y, c = _place()
    return [_remote(sums[a].at[2 * px + py], lands[a].at[2 * x + y], send_sems, recv_sems, 3 * a + j, (px, py, c))
            for a in range(len(sums)) for j, (px, py) in enumerate([(1 - x, y), (x, 1 - y), (1 - x, 1 - y)])]


def _chip_arrivals(sums, lands, send_sems, recv_sems):
    x, y, c = _place()
    return [_remote(sums[a].at[2 * x + y], lands[a].at[2 * px + py], send_sems, recv_sems, 3 * a + j, (x, y, c))
            for a in range(len(sums)) for j, (px, py) in enumerate([(1 - x, y), (x, 1 - y), (1 - x, 1 - y)])]


def exchange_start(srcs, after, land_shapes, make_sends, per_array, *, name):
    n = len(srcs)
    lands = [lax.empty(s, a.dtype) for s, a in zip(land_shapes, srcs)]

    def body(*refs):
        ins, zones = refs[:n], refs[n:2 * n]
        send_sems, recv_sems, token = refs[2 * n + 1], refs[2 * n + 2], refs[-1]
        for cp in make_sends(ins, zones, send_sems, recv_sems):
            cp.start()
        token[...] = jnp.zeros_like(token)

    res = pl.pallas_call(
        body, name=name, in_specs=[HBM] * (2 * n) + [ANY],
        out_shape=(pltpu.SemaphoreType.DMA((per_array * n,)), pltpu.SemaphoreType.DMA((per_array * n,)),
                   *[pltpu.HBM(a.shape, a.dtype) for a in srcs], *[pltpu.HBM(a.shape, a.dtype) for a in lands],
                   _sds(TOKEN, F32)),
        out_specs=(SEM, SEM, *[HBM] * (2 * n), pl.BlockSpec(memory_space=pltpu.VMEM)),
        input_output_aliases={a: 2 + a for a in range(2 * n)}, compiler_params=pltpu.CompilerParams(has_side_effects=EFFECT),
    )(*[_hbm(a) for a in srcs], *[_hbm(a) for a in lands], after)
    return res[0], res[1], list(res[2:2 + n]), list(res[2 + n:2 + 2 * n]), res[-1]


def exchange_wait(send_sems, recv_sems, srcs, lands, after, make_waits, *, name):
    n = len(srcs)

    def body(*refs):
        ins, zones, s, r = refs[:n], refs[n:2 * n], refs[2 * n], refs[2 * n + 1]
        sends, arrivals = make_waits(ins, zones, s, r)
        for cp in arrivals:
            cp.wait_recv()
        for cp in sends:
            cp.wait_send()

    res = pl.pallas_call(
        body, name=name, in_specs=[HBM] * (2 * n) + [SEM, SEM, ANY],
        out_shape=tuple(pltpu.HBM(a.shape, a.dtype) for a in (*srcs, *lands)), out_specs=tuple([HBM] * (2 * n)),
        input_output_aliases={a: a for a in range(2 * n)}, compiler_params=pltpu.CompilerParams(has_side_effects=EFFECT),
    )(*srcs, *lands, send_sems, recv_sems, after)
    return list(res[:n]), list(res[n:])


def pair_sum(grad, recv, shard_shape, axis, core, *, name):
    r, c = shard_shape
    tr = min(r, 256)
    nr = r // tr
    if axis == 0:
        gspec = pl.BlockSpec((tr, c), lambda q, i, cref: ((2 * q + cref[0]) * nr + i, 0))
    else:
        gspec = pl.BlockSpec((tr, c), lambda q, i, cref: (i, 2 * q + cref[0]))
    rspec = pl.BlockSpec((None, tr, c), lambda q, i, cref: (q, i, 0))

    def body(c_ref, g_ref, r_ref, o_ref):
        o_ref[...] = (g_ref[...].astype(F32) + r_ref[...].astype(F32)).astype(o_ref.dtype)

    return pl.pallas_call(
        body, name=name,
        grid_spec=pltpu.PrefetchScalarGridSpec(num_scalar_prefetch=1, grid=(N_CHIP, nr), in_specs=[gspec, rspec],
                                               out_specs=rspec),
        out_shape=_sds((N_CHIP, r, c), recv.dtype), compiler_params=_params(("parallel", "parallel")),
    )(core, grad, recv)


def all_reduce_small(part, *, name):
    r, c = part.shape

    def body(p_ref, o_ref, gath_ref, send_sems, recv_sems):
        x, y, cc = _place()
        me = 4 * x + 2 * y + cc
        gath_ref[me] = p_ref[...]
        peers = [(1 - x if k & 4 else x, 1 - y if k & 2 else y, 1 - cc if k & 1 else cc) for k in range(1, N_DEV)]
        copies = [_remote(p_ref, gath_ref.at[me], send_sems, recv_sems, i, peer) for i, peer in enumerate(peers)]
        for cp in copies:
            cp.start()
        for i, (px, py, pc) in enumerate(peers):
            _remote(p_ref, gath_ref.at[4 * px + 2 * py + pc], send_sems, recv_sems, i, (x, y, cc)).wait_recv()
        for cp in copies:
            cp.wait_send()
        acc = gath_ref[0]
        for d in range(1, N_DEV):
            acc = acc + gath_ref[d]
        o_ref[...] = acc

    vm = pl.BlockSpec(memory_space=pltpu.VMEM)
    return pl.pallas_call(
        body, name=name, in_specs=[vm], out_specs=vm, out_shape=_sds((r, c), F32),
        scratch_shapes=[pltpu.VMEM((N_DEV, r, c), F32), pltpu.SemaphoreType.DMA((N_DEV - 1,)),
                        pltpu.SemaphoreType.DMA((N_DEV - 1,))],
    )(part)


def _adamw(w, g, m, v):
    m = ADAM_B1 * m + (1.0 - ADAM_B1) * g
    v = ADAM_B2 * v + (1.0 - ADAM_B2) * (g * g)
    m_hat = m / (1.0 - ADAM_B1 ** ADAM_STEP)
    v_hat = v / (1.0 - ADAM_B2 ** ADAM_STEP)
    delta = -ADAM_LR * (m_hat / (jnp.sqrt(v_hat) + ADAM_EPS) + ADAM_WD * w)
    return delta, m, v


def reduce_adam(own, landed, chip, w, m, v, layer, outs, *, name):
    _, r, c = w.shape
    tr = min(r, 128)
    first = outs is None

    def body(chip_ref, own_ref, land_ref, w_ref, m_ref, v_ref, *rest):
        g_out, d_out, m_out, v_out = rest[-4:]
        g = None
        for q in range(N_CHIP):
            term = jnp.where(chip_ref[0] == q, own_ref[q], land_ref[q]).astype(F32)
            g = term if g is None else g + term
        d, mn, vn = _adamw(w_ref[...], g, m_ref[...], v_ref[...])
        g_out[...] = g
        d_out[...] = d
        m_out[...] = mn
        v_out[...] = vn

    spec = pl.BlockSpec((None, tr, c), lambda i, chip_ref: (layer, i, 0))
    pspec = pl.BlockSpec((N_CHIP, tr, c), lambda i, chip_ref: (0, i, 0))
    n_in = 6
    return pl.pallas_call(
        body, name=name,
        grid_spec=pltpu.PrefetchScalarGridSpec(num_scalar_prefetch=1, grid=(r // tr,),
                                               in_specs=[pspec, pspec, spec, spec, spec] + ([] if first else [ANY] * 4),
                                               out_specs=[spec] * 4),
        out_shape=[_sds(w.shape, F32)] * 4, input_output_aliases={} if first else {n_in + i: i for i in range(4)},
        compiler_params=_params(("parallel",)),
    )(chip, own, landed, w, m, v, *([] if first else outs))


def adam_small(g, w, m, v, *, name):
    def body(g_ref, w_ref, m_ref, v_ref, d_out, m_out, v_out):
        d, mn, vn = _adamw(w_ref[...], g_ref[...], m_ref[...], v_ref[...])
        d_out[...] = d
        m_out[...] = mn
        v_out[...] = vn

    return pl.pallas_call(body, name=name, out_shape=[_sds(w.shape, F32)] * 3)(g, w, m, v)


BIG = ("w_in", "w_out_conv", "w_out_attn", "w_merge", "w_up", "w_down")
GATHERED = BIG + ("conv_w",)
BIG_AXIS = {"w_in": 1, "w_out_conv": 0, "w_out_attn": 0, "w_merge": 0, "w_up": 1, "w_down": 0, "conv_w": 1}
FIRST_GATHER_GROUPS = (("w_in", "conv_w"), ("w_out_conv", "w_out_attn", "w_merge"), ("w_up", "w_down"))
LAST_REDUCE_GROUPS = (("w_down", "w_up"), ("w_merge", "w_out_conv", "w_out_attn"), ("w_in",))


class _Gather:
    def __init__(self, weights, keys, layer, me, after, tag):
        self.keys, self.tag = keys, tag
        self.shapes = [weights[k].shape[1:] for k in keys]
        self.axes = [BIG_AXIS[k] for k in keys]
        own = [cast_into_full(weights[k], layer, ax, me, F32 if k == "conv_w" else MXU_DTYPE, name="cast_" + k)
               for k, ax in zip(keys, self.axes)]
        self.send, self.recv, self.fulls, self.token = gather_start(own, after, self.shapes, self.axes,
                                                                    name="gather_start_" + tag)

    def pass_on(self, after):
        self.send, self.recv, self.fulls, self.token = gather_pass(self.send, self.recv, self.fulls, after, self.shapes,
                                                                   self.axes, name="gather_pass_" + self.tag)
        return self.token

    def wait(self, after):
        fulls = gather_wait(self.send, self.recv, self.fulls, after, self.shapes, self.axes, name="gather_wait_" + self.tag)
        return dict(zip(self.keys, fulls))


class _Reduce:
    def __init__(self, weights, keys, tag):
        self.keys, self.tag = keys, tag
        self.shapes = [weights[k].shape[1:] for k in keys]
        self.axes = [BIG_AXIS[k] for k in keys]
        self.pair_shapes = [(N_CHIP, *shp) for shp in self.shapes]

    def _pair(self, i, z, ss, rs):
        return _pair_copies(i, z, self.shapes, self.axes, ss, rs)

    def begin(self, grads, after):
        self.send, self.recv, self.src, self.land, self.token = exchange_start(
            [grads[k] for k in self.keys], after, self.pair_shapes, self._pair, N_CHIP, name="rs_pair_start_" + self.tag)
        return self.token

    def middle(self, after, core):
        both = lambda i, z, ss, rs: (self._pair(i, z, ss, rs),) * 2
        grads, from_sibling = exchange_wait(self.send, self.recv, self.src, self.land, after, both,
                                            name="rs_pair_wait_" + self.tag)
        sums = [pair_sum(g, rcv, shp, ax, core, name="pair_sum_" + k)
                for k, g, rcv, shp, ax in zip(self.keys, grads, from_sibling, self.shapes, self.axes)]
        self.send, self.recv, self.src, self.land, self.token = exchange_start(
            sums, after, self.pair_shapes, _chip_sends, 3, name="rs_chips_start_" + self.tag)
        return self.token

    def end(self, after):
        both = lambda i, z, ss, rs: (_chip_sends(i, z, ss, rs), _chip_arrivals(i, z, ss, rs))
        sums, landed = exchange_wait(self.send, self.recv, self.src, self.land, after, both,
                                     name="rs_chips_wait_" + self.tag)
        return list(zip(self.keys, sums, landed))
SMALL = (("norm_mix_pre", D_MODEL), ("gate_bias", 2 * D_MODEL), ("norm_mix_post", D_MODEL), ("norm_mlp_pre", D_MODEL),
         ("norm_mlp_post", D_MODEL), ("q_norm", HEAD_DIM), ("k_norm", HEAD_DIM))
SMALL_WIDTH = sum(w for _, w in SMALL)
WEIGHTS = ("norm_mix_pre", "w_in", "gate_bias", "conv_w", "q_norm", "k_norm", "w_out_conv", "w_out_attn", "w_merge",
           "norm_mix_post", "norm_mlp_pre", "w_up", "w_down", "norm_mlp_post")


def kernel(x, norm_mix_pre, w_in, gate_bias, conv_w, q_norm, k_norm, w_out_conv, w_out_attn, w_merge, norm_mix_post, norm_mlp_pre, w_up, w_down, norm_mlp_post, loss_target, m_norm_mix_pre, m_w_in, m_gate_bias, m_conv_w, m_q_norm, m_k_norm, m_w_out_conv, m_w_out_attn, m_w_merge, m_norm_mix_post, m_norm_mlp_pre, m_w_up, m_w_down, m_norm_mlp_post, v_norm_mix_pre, v_w_in, v_gate_bias, v_conv_w, v_q_norm, v_k_norm, v_w_out_conv, v_w_out_attn, v_w_merge, v_norm_mix_post, v_norm_mlp_pre, v_w_up, v_w_down, v_norm_mlp_post):
    w = dict(norm_mix_pre=norm_mix_pre, w_in=w_in, gate_bias=gate_bias, conv_w=conv_w, q_norm=q_norm, k_norm=k_norm,
             w_out_conv=w_out_conv, w_out_attn=w_out_attn, w_merge=w_merge, norm_mix_post=norm_mix_post,
             norm_mlp_pre=norm_mlp_pre, w_up=w_up, w_down=w_down, norm_mlp_post=norm_mlp_post)
    mom = dict(norm_mix_pre=m_norm_mix_pre, w_in=m_w_in, gate_bias=m_gate_bias, conv_w=m_conv_w, q_norm=m_q_norm,
               k_norm=m_k_norm, w_out_conv=m_w_out_conv, w_out_attn=m_w_out_attn, w_merge=m_w_merge,
               norm_mix_post=m_norm_mix_post, norm_mlp_pre=m_norm_mlp_pre, w_up=m_w_up, w_down=m_w_down,
               norm_mlp_post=m_norm_mlp_post)
    var = dict(norm_mix_pre=v_norm_mix_pre, w_in=v_w_in, gate_bias=v_gate_bias, conv_w=v_conv_w, q_norm=v_q_norm,
               k_norm=v_k_norm, w_out_conv=v_w_out_conv, w_out_attn=v_w_out_attn, w_merge=v_w_merge,
               norm_mix_post=v_norm_mix_post, norm_mlp_pre=v_norm_mlp_pre, w_up=v_w_up, w_down=v_w_down,
               norm_mlp_post=v_norm_mlp_post)
    depth = w_in.shape[0]
    s = x.shape[1]
    xs = x.reshape(s, D_MODEL)
    target = loss_target.reshape(s, D_MODEL)
    x_idx, y_idx, c_idx = _place()
    as_operand = lambda i: jnp.reshape(i, (1,)).astype(jnp.int32)
    core, chip, me = as_operand(c_idx), as_operand(2 * x_idx + y_idx), as_operand(4 * x_idx + 2 * y_idx + c_idx)
    cos_t, sin_t = rope_tables(s)

    def vec(name, l):
        return w[name][l].reshape(1, -1)

    saved = []
    h = rms_fwd(xs, vec("norm_mix_pre", 0), name="rms_first")
    first, after = [], h
    for i, keys in enumerate(FIRST_GATHER_GROUPS):
        first.append(_Gather(w, keys, 0, me, after, f"0{'abc'[i]}"))
        after = first[-1].token
    full = first[0].wait(first[0].pass_on(after))
    layers = []
    for l in range(depth):
        layers.append(full)
        nxt = l + 1 < depth
        if nxt:
            coming = _Gather(w, GATHERED, l + 1, me, full["w_in"], str(l + 1))
        z = matmul(h, full["w_in"], mode="nn", m=s, n=IN_WIDTH, k=D_MODEL, out_dtypes=[F32], name="mm_in",
                   after=[coming.token] if nxt else [])
        t = conv_fwd(z, full["conv_w"], name="conv_fwd")
        qp, kp, vb = qk_prep_fwd(z, vec("q_norm", l), vec("k_norm", l), cos_t, sin_t, name="qk_fwd")
        o, lse = attn_fwd(qp, kp, vb, name="attn_fwd", after=[first[1].pass_on(qp)] if l == 0 else [])
        if l == 0:
            full.update(first[1].wait(o))
        ya = matmul(t, full["w_out_conv"], mode="nn", m=s, n=D_MODEL, k=CONV_WIDTH, out_dtypes=[F32], name="mm_out_conv",
                    after=[first[2].pass_on(o)] if l == 0 else [])
        yb = matmul(o, full["w_out_attn"], mode="nn", m=s, n=D_MODEL, k=ATTN_WIDTH, out_dtypes=[F32], name="mm_out_attn")
        mi = gate_fwd(ya, yb, z, vec("gate_bias", l), name="gate_fwd")
        mixed = matmul(mi, full["w_merge"], mode="nn", m=s, n=D_MODEL, k=D_MODEL, out_dtypes=[F32], name="mm_merge")
        if l == 0:
            full.update(first[2].wait(mixed))
        x_mid, h2 = rms_residual_fwd(xs, mixed, vec("norm_mix_post", l), vec("norm_mlp_pre", l), name="res_mix")
        act, r = matmul(h2, full["w_up"], mode="nn", m=s, n=D_FF, k=D_MODEL, out_dtypes=[MXU_DTYPE, MXU_DTYPE],
                        name="mm_up", epilogue=lambda acc: (acc, jnp.square(jnp.maximum(acc, 0.0))))
        f = matmul(r, full["w_down"], mode="nn", m=s, n=D_MODEL, k=D_FF, out_dtypes=[F32], name="mm_down",
                   after=[coming.pass_on(r)] if nxt else [])
        g_next = vec("norm_mix_pre", l + 1) if nxt else None
        x_out, h_next = rms_residual_fwd(x_mid, f, vec("norm_mlp_post", l), g_next, name="res_mlp" if nxt else "res_last")
        saved.append(dict(x_in=xs, h=h, z=z, t=t, qp=qp, kp=kp, vb=vb, o=o, lse=lse, ya=ya, yb=yb, mi=mi, mixed=mixed,
                          x_mid=x_mid, h2=h2, act=act, r=r, f=f))
        if nxt:
            full = coming.wait(x_out)
        xs, h = x_out, h_next

    dx, loss_part = loss_and_grad(xs, target, name="loss")
    loss = lax.psum(jnp.sum(loss_part), ("x", "y", "c"))

    small_rows = [None] * depth
    conv_rows = [None] * depth
    out_g, out_d, out_m, out_v = {}, {}, {}, {}
    big_outs = {k: None for k in BIG}
    pending = None
    last = []

    def finish(reduction, layer, after):
        for k, own, landed in reduction.end(after):
            big_outs[k] = reduce_adam(own, landed, chip, w[k], mom[k], var[k], layer, big_outs[k], name="adam_" + k)

    for l in reversed(range(depth)):
        sv, full = saved[l], layers[l]
        grads = {}
        groups = l == 0

        def wgrad(key, lhs, rhs, m, n, after=()):
            grads[key] = matmul(lhs, rhs, mode="tn", m=m, n=n, k=s, out_dtypes=[COMM_DTYPE], name="wg_" + key,
                                after=after)

        def begin_group(i, after):
            last.append(_Reduce(w, LAST_REDUCE_GROUPS[i], f"{l}{'abc'[i]}"))
            return [last[i].begin(grads, after)]

        df, dg_mlp_post = rms_bwd(sv["f"], vec("norm_mlp_post", l), dx, None, out_dtype=MXU_DTYPE, name="rmsb_mlp_post",
                                  after=[pending.token] if pending else [])
        da = matmul(df, full["w_down"], mode="nt", m=s, n=D_FF, k=D_MODEL, out_dtypes=[MXU_DTYPE], name="mm_d_down",
                    extra=(sv["act"],), epilogue=lambda acc, a: (acc * (2.0 * jnp.maximum(a.astype(F32), 0.0)),))
        tokens = [pending.middle(da, core)] if pending else []
        wgrad("w_down", sv["r"], df, D_FF, D_MODEL)
        dh2 = matmul(da, full["w_up"], mode="nt", m=s, n=D_MODEL, k=D_FF, out_dtypes=[F32], name="mm_d_up", after=tokens)
        wgrad("w_up", sv["h2"], da, D_MODEL, D_FF)
        tokens = begin_group(0, dh2) if groups else []
        dx_mid, dg_mlp_pre = rms_bwd(sv["x_mid"], vec("norm_mlp_pre", l), dh2, dx, out_dtype=F32, name="rmsb_mlp_pre",
                                     after=tokens)
        dmixed, dg_mix_post = rms_bwd(sv["mixed"], vec("norm_mix_post", l), dx_mid, None, out_dtype=MXU_DTYPE,
                                      name="rmsb_mix_post")
        dmi = matmul(dmixed, full["w_merge"], mode="nt", m=s, n=D_MODEL, k=D_MODEL, out_dtypes=[F32], name="mm_d_merge")
        tokens = [last[0].middle(dmi, core)] if groups else []
        wgrad("w_merge", sv["mi"], dmixed, D_MODEL, D_MODEL, after=tokens)
        dya, dyb, dga, dgb, dba, dbb = gate_bwd(dmi, sv["ya"], sv["yb"], sv["z"], vec("gate_bias", l), name="gate_bwd")
        wgrad("w_out_conv", sv["t"], dya, CONV_WIDTH, D_MODEL)
        wgrad("w_out_attn", sv["o"], dyb, ATTN_WIDTH, D_MODEL)
        tokens = begin_group(1, dyb) if groups else []
        dt = matmul(dya, full["w_out_conv"], mode="nt", m=s, n=CONV_WIDTH, k=D_MODEL, out_dtypes=[F32],
                    name="mm_d_out_conv", after=tokens)
        do = matmul(dyb, full["w_out_attn"], mode="nt", m=s, n=ATTN_WIDTH, k=D_MODEL, out_dtypes=[MXU_DTYPE],
                    name="mm_d_out_attn")
        tokens = [last[1].middle(do, core)] if groups else []
        dcb, dcc, dci, dconv_w = conv_bwd(dt, sv["z"], full["conv_w"], name="conv_bwd", after=tokens)
        dqp, dkp, dv = attn_bwd(sv["qp"], sv["kp"], sv["vb"], do, sv["o"], sv["lse"], name="attn_bwd")
        dq, dk, dqn, dkn = qk_prep_bwd(dqp, dkp, sv["z"], vec("q_norm", l), vec("k_norm", l), cos_t, sin_t, name="qk_bwd")
        dz = jnp.concatenate([dcb, dcc, dci, dq, dk, dv, dga, dgb], axis=1)
        wgrad("w_in", sv["h"], dz, D_MODEL, IN_WIDTH)
        tokens = begin_group(2, dz) if groups else []
        dh = matmul(dz, full["w_in"], mode="nt", m=s, n=D_MODEL, k=IN_WIDTH, tk=IN_WIDTH // 4, out_dtypes=[F32],
                    name="mm_d_in", after=tokens)
        dx, dg_mix_pre = rms_bwd(sv["x_in"], vec("norm_mix_pre", l), dh, dx_mid, out_dtype=F32, name="rmsb_mix_pre")
        small_rows[l] = jnp.concatenate([dg_mix_pre, dba, dbb, dg_mix_post, dg_mlp_pre, dg_mlp_post, dqn, dkn], axis=1)
        conv_rows[l] = dconv_w.reshape(1, 3 * CONV_WIDTH)
        if pending:
            finish(pending, l + 1, dx)
        pending = None
        if not groups:
            pending = _Reduce(w, BIG, str(l))
            pending.begin(grads, dx)

    grad_x = dx.reshape(1, s, D_MODEL)

    small_part = jnp.concatenate([jnp.concatenate(small_rows, axis=0), jnp.concatenate(conv_rows, axis=0)], axis=1)
    small_sum = all_reduce_small(small_part, name="allreduce_small")
    pack = lambda src: jnp.concatenate([src[k].reshape(depth, wd) for k, wd in SMALL], axis=1)
    g_small = small_sum[:, :SMALL_WIDTH]
    d_small, m_small, v_small = adam_small(g_small, pack(w), pack(mom), pack(var), name="adam_small")
    off = 0
    for k, wd in SMALL:
        for dst, src in ((out_g, g_small), (out_d, d_small), (out_m, m_small), (out_v, v_small)):
            dst[k] = src[:, off:off + wd]
        off += wd
    cshard = CONV_WIDTH // N_DEV
    g_conv = lax.dynamic_slice_in_dim(small_sum[:, SMALL_WIDTH:].reshape(depth, 3, CONV_WIDTH), me[0] * cshard, cshard,
                                      axis=2).reshape(depth, 3 * cshard)
    flat = lambda a: a.reshape(depth, 3 * cshard)
    d_conv, m_conv, v_conv = adam_small(g_conv, flat(conv_w), flat(m_conv_w), flat(v_conv_w), name="adam_conv")
    for dst, src in ((out_g, g_conv), (out_d, d_conv), (out_m, m_conv), (out_v, v_conv)):
        dst["conv_w"] = src.reshape(depth, 3, cshard)

    token = last[2].middle(d_small, core)
    for reduction in last:
        finish(reduction, 0, token)
    for k in BIG:
        out_g[k], out_d[k], out_m[k], out_v[k] = big_outs[k]

    return (loss, grad_x, *[out_g[k] for k in WEIGHTS], *[out_d[k] for k in WEIGHTS], *[out_m[k] for k in WEIGHTS],
            *[out_v[k] for k in WEIGHTS])
```

```python
import math

import jax
import jax.numpy as jnp
from jax import lax
from jax.experimental import pallas as pl
from jax.experimental.pallas import tpu as pltpu

F32 = jnp.float32
MXU_DTYPE = jnp.bfloat16
COMM_DTYPE = jnp.bfloat16

D_MODEL = 2048
HEAD_DIM = 128
N_Q_HEADS = 16
N_KV_HEADS = 4
GROUP = N_Q_HEADS // N_KV_HEADS
ATTN_WIDTH = N_Q_HEADS * HEAD_DIM
KV_WIDTH = N_KV_HEADS * HEAD_DIM
CONV_WIDTH = D_MODEL
D_FF = 4 * D_MODEL
GRID_W = 64
ROPE_THETA = 10000.0
RMS_EPS = 1e-6
IN_WIDTH = 3 * CONV_WIDTH + ATTN_WIDTH + 2 * KV_WIDTH + 2 * D_MODEL
OFF_CB, OFF_CC, OFF_CI = 0, CONV_WIDTH, 2 * CONV_WIDTH
OFF_Q = 3 * CONV_WIDTH
OFF_K = OFF_Q + ATTN_WIDTH
OFF_V = OFF_K + KV_WIDTH
OFF_GA = OFF_V + KV_WIDTH
OFF_GB = OFF_GA + D_MODEL
ATTN_SCALE = 1.0 / math.sqrt(HEAD_DIM)

ADAM_LR, ADAM_B1, ADAM_B2, ADAM_EPS, ADAM_WD, ADAM_STEP = 0.001, 0.9, 0.999, 1e-08, 0.01, 10

N_DEV = 8
N_CHIP = 4
LANE = 128
SUBLANE = 8
VMEM_LIMIT = 48 * 1024 * 1024
MESH = pl.DeviceIdType.MESH
ANY = pl.BlockSpec(memory_space=pl.ANY)


def _sds(shape, dtype):
    return jax.ShapeDtypeStruct(tuple(shape), dtype)


def _params(sem, vmem=VMEM_LIMIT):
    return pltpu.CompilerParams(dimension_semantics=sem, vmem_limit_bytes=vmem)


def _rows(ts, w, col=0):
    return pl.BlockSpec((ts, w), lambda i: (i, col))


def _fixed(shape):
    return pl.BlockSpec(shape, lambda *_: (0,) * len(shape))


def _fold8(v):
    ts, w = v.shape
    return v.reshape(ts // SUBLANE, SUBLANE, w).sum(axis=0)


def matmul(a, b, *, mode, m, n, k, name, out_dtypes, tm=1024, tn=1024, tk=2048, epilogue=None, extra=(), after=()):
    tm, tn, tk = min(tm, m), min(tn, n), min(tk, k)
    nm, nn, nk = m // tm, n // tn, k // tk
    assert nm * tm == m and nn * tn == n and nk * tk == k, (name, m, n, k, tm, tn, tk)
    if mode == "tn":
        a_spec = pl.BlockSpec((tk, tm), lambda i, j, kk: (kk, i))
        dims = (((0,), (0,)), ((), ()))
    else:
        a_spec = pl.BlockSpec((tm, tk), lambda i, j, kk: (i, kk))
        dims = (((1,), (1 if mode == "nt" else 0,)), ((), ()))
    if mode == "nt":
        b_spec = pl.BlockSpec((tn, tk), lambda i, j, kk: (j, kk))
    else:
        b_spec = pl.BlockSpec((tk, tn), lambda i, j, kk: (kk, j))
    tile = pl.BlockSpec((tm, tn), lambda i, j, kk: (i, j))
    n_out, n_extra, n_after = len(out_dtypes), len(extra), len(after)
    if epilogue is None:
        epilogue = lambda acc: (acc,)

    def body(a_ref, b_ref, *rest):
        extra_refs = rest[:n_extra]
        outs = rest[n_extra + n_after:][:n_out]
        part = lax.dot_general(a_ref[...].astype(MXU_DTYPE), b_ref[...].astype(MXU_DTYPE), dims,
                               preferred_element_type=F32)

        def finish(acc):
            for o_ref, val in zip(outs, epilogue(acc, *[r[...] for r in extra_refs])):
                o_ref[...] = val.astype(o_ref.dtype)

        if nk == 1:
            finish(part)
        else:
            acc_ref = rest[-1]
            kk = pl.program_id(2)

            @pl.when(kk == 0)
            def _():
                acc_ref[...] = part

            @pl.when(kk > 0)
            def _():
                acc_ref[...] += part

            @pl.when(kk == nk - 1)
            def _():
                finish(acc_ref[...])

    res = pl.pallas_call(
        body, name=name, grid=(nm, nn, nk), in_specs=[a_spec, b_spec] + [tile] * n_extra + [ANY] * n_after,
        out_specs=[tile] * n_out, out_shape=[_sds((m, n), dt) for dt in out_dtypes],
        scratch_shapes=[pltpu.VMEM((tm, tn), F32)] if nk > 1 else [],
        compiler_params=_params(("parallel", "parallel", "arbitrary")),
    )(a, b, *extra, *after)
    return res[0] if n_out == 1 else res


def _rstd(x):
    return lax.rsqrt(jnp.mean(x * x, axis=-1, keepdims=True) + RMS_EPS)


def _rms_bwd(x, g, dy):
    rstd = _rstd(x)
    xh = x * rstd
    gy = dy * g
    dx = rstd * (gy - xh * jnp.mean(gy * xh, axis=-1, keepdims=True))
    return dx, dy * xh


def rms_fwd(x, g, *, name):
    s = x.shape[0]
    ts = min(s, 512)

    def body(x_ref, g_ref, h_ref):
        xv = x_ref[...]
        h_ref[...] = (xv * _rstd(xv) * g_ref[...]).astype(h_ref.dtype)

    return pl.pallas_call(
        body, name=name, grid=(s // ts,), in_specs=[_rows(ts, D_MODEL), _fixed((1, D_MODEL))],
        out_specs=_rows(ts, D_MODEL), out_shape=_sds((s, D_MODEL), MXU_DTYPE), compiler_params=_params(("parallel",)),
    )(x, g)


def rms_residual_fwd(x, y, g_post, g_next, *, name):
    s = x.shape[0]
    ts = min(s, 512)
    with_next = g_next is not None

    def body(x_ref, y_ref, gp_ref, *rest):
        yv = y_ref[...]
        xn = x_ref[...] + yv * _rstd(yv) * gp_ref[...]
        if with_next:
            gn_ref, xo_ref, h_ref = rest
            h_ref[...] = (xn * _rstd(xn) * gn_ref[...]).astype(h_ref.dtype)
        else:
            (xo_ref,) = rest
        xo_ref[...] = xn

    gspec = _fixed((1, D_MODEL))
    res = pl.pallas_call(
        body, name=name, grid=(s // ts,),
        in_specs=[_rows(ts, D_MODEL), _rows(ts, D_MODEL), gspec] + [gspec] * with_next,
        out_specs=[_rows(ts, D_MODEL)] * (1 + with_next),
        out_shape=[_sds((s, D_MODEL), F32)] + [_sds((s, D_MODEL), MXU_DTYPE)] * with_next,
        compiler_params=_params(("parallel",)),
    )(x, y, g_post, *([g_next] if with_next else []))
    return (res[0], res[1]) if with_next else (res[0], None)


def rms_bwd(x, g, dy, residual, *, out_dtype, name, after=()):
    s = x.shape[0]
    ts = min(s, 256)
    nt = s // ts
    with_res = residual is not None

    def body(x_ref, g_ref, dy_ref, *rest):
        dx_ref, dg_ref, acc_ref = rest[-3:]
        dx, dg_rows = _rms_bwd(x_ref[...], g_ref[...], dy_ref[...].astype(F32))
        if with_res:
            dx = dx + rest[0][...]
        dx_ref[...] = dx.astype(dx_ref.dtype)
        i = pl.program_id(0)

        @pl.when(i == 0)
        def _():
            acc_ref[...] = jnp.zeros_like(acc_ref)

        acc_ref[...] += _fold8(dg_rows)

        @pl.when(i == nt - 1)
        def _():
            dg_ref[...] = acc_ref[...].sum(axis=0, keepdims=True)

    return pl.pallas_call(
        body, name=name, grid=(nt,),
        in_specs=[_rows(ts, D_MODEL), _fixed((1, D_MODEL)), _rows(ts, D_MODEL)] + [_rows(ts, D_MODEL)] * with_res
        + [ANY] * len(after),
        out_specs=[_rows(ts, D_MODEL), _fixed((1, D_MODEL))],
        out_shape=[_sds((s, D_MODEL), out_dtype), _sds((1, D_MODEL), F32)],
        scratch_shapes=[pltpu.VMEM((SUBLANE, D_MODEL), F32)], compiler_params=_params(("arbitrary",)),
    )(x, g, dy, *([residual] if with_res else []), *after)


def loss_and_grad(y, target, *, name):
    s = y.shape[0]
    ts = min(s, 512)
    nt = s // ts

    def body(y_ref, t_ref, dy_ref, part_ref):
        e = y_ref[...] - t_ref[...]
        dy_ref[...] = e * (1.0 / D_MODEL)
        sq = _fold8(e * e)
        lanes = sq[:, 0:LANE]
        for j in range(1, D_MODEL // LANE):
            lanes = lanes + sq[:, j * LANE:(j + 1) * LANE]
        i = pl.program_id(0)

        @pl.when(i == 0)
        def _():
            part_ref[...] = jnp.zeros_like(part_ref)

        part_ref[...] += lanes * (0.5 / D_MODEL)

    return pl.pallas_call(
        body, name=name, grid=(nt,), in_specs=[_rows(ts, D_MODEL), _rows(ts, D_MODEL)],
        out_specs=[_rows(ts, D_MODEL), _fixed((SUBLANE, LANE))],
        out_shape=[_sds((s, D_MODEL), F32), _sds((SUBLANE, LANE), F32)], compiler_params=_params(("arbitrary",)),
    )(y, target)


CONV_TC = LANE


def _conv_taps(u, s):
    row = lax.broadcasted_iota(jnp.int32, u.shape, 0)
    prev = jnp.where(row == 0, 0.0, pltpu.roll(u, 1, 0))
    nxt = jnp.where(row == s - 1, 0.0, pltpu.roll(u, s - 1, 0))
    return prev, nxt


def _zcol(s, off):
    return pl.BlockSpec((s, CONV_TC), lambda j: (0, off // CONV_TC + j))


def conv_fwd(z, w3, *, name):
    s = z.shape[0]

    def body(cb_ref, cc_ref, ci_ref, w_ref, t_ref):
        u = cc_ref[...] * ci_ref[...]
        prev, nxt = _conv_taps(u, s)
        w = w_ref[...]
        conv = w[0:1] * prev + w[1:2] * u + w[2:3] * nxt
        t_ref[...] = (cb_ref[...] * conv).astype(t_ref.dtype)

    return pl.pallas_call(
        body, name=name, grid=(CONV_WIDTH // CONV_TC,),
        in_specs=[_zcol(s, OFF_CB), _zcol(s, OFF_CC), _zcol(s, OFF_CI), pl.BlockSpec((3, CONV_TC), lambda j: (0, j))],
        out_specs=pl.BlockSpec((s, CONV_TC), lambda j: (0, j)), out_shape=_sds((s, CONV_WIDTH), MXU_DTYPE),
        compiler_params=_params(("parallel",)),
    )(z, z, z, w3)


def conv_bwd(dt, z, w3, *, name, after=()):
    s = z.shape[0]

    def body(dt_ref, cb_ref, cc_ref, ci_ref, w_ref, *rest):
        dcb_ref, dcc_ref, dci_ref, dw_ref = rest[-4:]
        cc, ci = cc_ref[...], ci_ref[...]
        u = cc * ci
        prev, nxt = _conv_taps(u, s)
        w = w_ref[...]
        dtv = dt_ref[...]
        dcb_ref[...] = (dtv * (w[0:1] * prev + w[1:2] * u + w[2:3] * nxt)).astype(dcb_ref.dtype)
        dconv = dtv * cb_ref[...]
        dprev, dnxt = _conv_taps(dconv, s)
        du = w[0:1] * dnxt + w[1:2] * dconv + w[2:3] * dprev
        dcc_ref[...] = (du * ci).astype(dcc_ref.dtype)
        dci_ref[...] = (du * cc).astype(dci_ref.dtype)
        dw_ref[0:1, :] = jnp.sum(dconv * prev, axis=0, keepdims=True)
        dw_ref[1:2, :] = jnp.sum(dconv * u, axis=0, keepdims=True)
        dw_ref[2:3, :] = jnp.sum(dconv * nxt, axis=0, keepdims=True)

    col = pl.BlockSpec((s, CONV_TC), lambda j: (0, j))
    wspec = pl.BlockSpec((3, CONV_TC), lambda j: (0, j))
    return pl.pallas_call(
        body, name=name, grid=(CONV_WIDTH // CONV_TC,),
        in_specs=[col, _zcol(s, OFF_CB), _zcol(s, OFF_CC), _zcol(s, OFF_CI), wspec] + [ANY] * len(after),
        out_specs=[col, col, col, wspec],
        out_shape=[_sds((s, CONV_WIDTH), MXU_DTYPE)] * 3 + [_sds((3, CONV_WIDTH), F32)],
        compiler_params=_params(("parallel",)),
    )(dt, z, z, z, w3, *after)


def rope_tables(s):
    n_freq = HEAD_DIM // 4
    t = jnp.arange(s, dtype=jnp.int32)
    inv_freq = ROPE_THETA ** (-jnp.arange(0, HEAD_DIM // 2, 2, dtype=F32) / (HEAD_DIM // 2))
    ang_r = (t // GRID_W).astype(F32)[:, None] * inv_freq
    ang_c = (t % GRID_W).astype(F32)[:, None] * inv_freq
    cos_t = jnp.concatenate([jnp.cos(ang_r)] * 2 + [jnp.cos(ang_c)] * 2, axis=1)
    sin_t = jnp.concatenate([-jnp.sin(ang_r), jnp.sin(ang_r), -jnp.sin(ang_c), jnp.sin(ang_c)], axis=1)
    assert cos_t.shape == (s, 4 * n_freq)
    return cos_t, sin_t


def _swap_halves(v):
    lane = lax.broadcasted_iota(jnp.int32, v.shape, 1)
    return jnp.where(lane % 64 < 32, pltpu.roll(v, HEAD_DIM - 32, 1), pltpu.roll(v, 32, 1))


def qk_prep_fwd(z, qn, kn, cos_t, sin_t, *, name):
    s = z.shape[0]
    ts = min(s, 256)

    def body(q_ref, k_ref, v_ref, qn_ref, kn_ref, c_ref, s_ref, qo_ref, ko_ref, vo_ref):
        cs, sn = c_ref[...], s_ref[...]

        def head(x, g, scale):
            n = x * _rstd(x) * g
            return (n * cs + _swap_halves(n) * sn) * scale

        for h in range(N_Q_HEADS):
            sl = slice(h * HEAD_DIM, (h + 1) * HEAD_DIM)
            qo_ref[:, sl] = head(q_ref[:, sl], qn_ref[...], ATTN_SCALE).astype(qo_ref.dtype)
        for h in range(N_KV_HEADS):
            sl = slice(h * HEAD_DIM, (h + 1) * HEAD_DIM)
            ko_ref[:, sl] = head(k_ref[:, sl], kn_ref[...], 1.0).astype(ko_ref.dtype)
        vo_ref[...] = v_ref[...].astype(vo_ref.dtype)

    tab = _rows(ts, HEAD_DIM)
    gsp = _fixed((1, HEAD_DIM))
    return pl.pallas_call(
        body, name=name, grid=(s // ts,),
        in_specs=[_rows(ts, ATTN_WIDTH, OFF_Q // ATTN_WIDTH), _rows(ts, KV_WIDTH, OFF_K // KV_WIDTH),
                  _rows(ts, KV_WIDTH, OFF_V // KV_WIDTH), gsp, gsp, tab, tab],
        out_specs=[_rows(ts, ATTN_WIDTH), _rows(ts, KV_WIDTH), _rows(ts, KV_WIDTH)],
        out_shape=[_sds((s, ATTN_WIDTH), MXU_DTYPE), _sds((s, KV_WIDTH), MXU_DTYPE), _sds((s, KV_WIDTH), MXU_DTYPE)],
        compiler_params=_params(("parallel",)),
    )(z, z, z, qn, kn, cos_t, sin_t)


def qk_prep_bwd(dqp, dkp, z, qn, kn, cos_t, sin_t, *, name):
    s = z.shape[0]
    ts = min(s, 256)
    nt = s // ts

    def body(dq_ref, dk_ref, q_ref, k_ref, qn_ref, kn_ref, c_ref, s_ref, dqo_ref, dko_ref, dqn_ref, dkn_ref,
             qacc_ref, kacc_ref):
        cs, sn = c_ref[...], s_ref[...]
        i = pl.program_id(0)

        @pl.when(i == 0)
        def _():
            qacc_ref[...] = jnp.zeros_like(qacc_ref)
            kacc_ref[...] = jnp.zeros_like(kacc_ref)

        def head(x, g, dout, scale):
            d = dout.astype(F32) * scale
            dn = d * cs + _swap_halves(d * sn)
            return _rms_bwd(x, g, dn)

        qacc = jnp.zeros((SUBLANE, HEAD_DIM), F32)
        for h in range(N_Q_HEADS):
            sl = slice(h * HEAD_DIM, (h + 1) * HEAD_DIM)
            dx, dg_rows = head(q_ref[:, sl], qn_ref[...], dq_ref[:, sl], ATTN_SCALE)
            dqo_ref[:, sl] = dx.astype(dqo_ref.dtype)
            qacc = qacc + _fold8(dg_rows)
        kacc = jnp.zeros((SUBLANE, HEAD_DIM), F32)
        for h in range(N_KV_HEADS):
            sl = slice(h * HEAD_DIM, (h + 1) * HEAD_DIM)
            dx, dg_rows = head(k_ref[:, sl], kn_ref[...], dk_ref[:, sl], 1.0)
            dko_ref[:, sl] = dx.astype(dko_ref.dtype)
            kacc = kacc + _fold8(dg_rows)
        qacc_ref[...] += qacc
        kacc_ref[...] += kacc

        @pl.when(i == nt - 1)
        def _():
            dqn_ref[...] = qacc_ref[...].sum(axis=0, keepdims=True)
            dkn_ref[...] = kacc_ref[...].sum(axis=0, keepdims=True)

    tab = _rows(ts, HEAD_DIM)
    gsp = _fixed((1, HEAD_DIM))
    return pl.pallas_call(
        body, name=name, grid=(nt,),
        in_specs=[_rows(ts, ATTN_WIDTH), _rows(ts, KV_WIDTH), _rows(ts, ATTN_WIDTH, OFF_Q // ATTN_WIDTH),
                  _rows(ts, KV_WIDTH, OFF_K // KV_WIDTH), gsp, gsp, tab, tab],
        out_specs=[_rows(ts, ATTN_WIDTH), _rows(ts, KV_WIDTH), gsp, gsp],
        out_shape=[_sds((s, ATTN_WIDTH), MXU_DTYPE), _sds((s, KV_WIDTH), MXU_DTYPE), _sds((1, HEAD_DIM), F32),
                   _sds((1, HEAD_DIM), F32)],
        scratch_shapes=[pltpu.VMEM((SUBLANE, HEAD_DIM), F32)] * 2, compiler_params=_params(("arbitrary",)),
    )(dqp, dkp, z, z, qn, kn, cos_t, sin_t)


_NT = (((1,), (1,)), ((), ()))
_GW = GROUP * HEAD_DIM


def _dot(a, b, dims=(((1,), (0,)), ((), ()))):
    return lax.dot_general(a, b, dims, preferred_element_type=F32)


def attn_fwd(qp, kp, vb, *, name, after=()):
    s = qp.shape[0]
    tq = min(s, 1024)
    rows = min(tq, 128)

    def body(q_ref, k_ref, v_ref, *rest):
        o_ref, lse_ref, v_ones = rest[-3:]

        @pl.when(pl.program_id(1) == 0)
        def _():
            v_ones[:, :HEAD_DIM] = v_ref[...]
            v_ones[:, HEAD_DIM:] = jnp.ones((s, HEAD_DIM), v_ones.dtype)

        k = k_ref[...]
        for g in range(GROUP):
            sl = slice(g * HEAD_DIM, (g + 1) * HEAD_DIM)
            for r0 in range(0, tq, rows):
                rs = slice(r0, r0 + rows)
                sc = _dot(q_ref[rs, sl], k, _NT)
                mx = jnp.max(sc, axis=-1, keepdims=True)
                p = jnp.exp(sc - mx).astype(v_ones.dtype)
                o_den = _dot(p, v_ones[...])
                den = o_den[:, HEAD_DIM:HEAD_DIM + 1]
                o_ref[rs, sl] = (o_den[:, :HEAD_DIM] / den).astype(o_ref.dtype)
                lse_ref[rs, g:g + 1] = mx + jnp.log(den)

    return pl.pallas_call(
        body, name=name, grid=(N_KV_HEADS, s // tq),
        in_specs=[pl.BlockSpec((tq, _GW), lambda j, i: (i, j)), pl.BlockSpec((s, HEAD_DIM), lambda j, i: (0, j)),
                  pl.BlockSpec((s, HEAD_DIM), lambda j, i: (0, j))] + [ANY] * len(after),
        out_specs=[pl.BlockSpec((tq, _GW), lambda j, i: (i, j)), pl.BlockSpec((None, tq, GROUP), lambda j, i: (j, i, 0))],
        out_shape=[_sds((s, ATTN_WIDTH), MXU_DTYPE), _sds((N_KV_HEADS, s, GROUP), F32)],
        scratch_shapes=[pltpu.VMEM((s, 2 * HEAD_DIM), MXU_DTYPE)], compiler_params=_params(("parallel", "arbitrary")),
    )(qp, kp, vb, *after)


ATTN_BWD_TQ = 256


def attn_bwd(qp, kp, vb, do, o, lse, *, name):
    s = qp.shape[0]
    tq = min(s, ATTN_BWD_TQ)
    nq = s // tq
    over_rows = (((0,), (0,)), ((), ()))

    def body(q_ref, k_ref, v_ref, do_ref, o_ref, lse_ref, dq_ref, dk_ref, dv_ref, p_all, ds_all, q_all, do_all, dk_acc,
             dv_acc):
        i = pl.program_id(1)

        @pl.when(i == 0)
        def _():
            dk_acc[...] = jnp.zeros_like(dk_acc)
            dv_acc[...] = jnp.zeros_like(dv_acc)

        k, v = k_ref[...], v_ref[...]
        for g in range(GROUP):
            sl = slice(g * HEAD_DIM, (g + 1) * HEAD_DIM)
            rows = slice(g * tq, (g + 1) * tq)
            qg, dog = q_ref[:, sl], do_ref[:, sl]
            dd = jnp.sum(dog.astype(F32) * o_ref[:, sl].astype(F32), axis=-1, keepdims=True)
            p = jnp.exp(_dot(qg, k, _NT) - lse_ref[:, g:g + 1])
            ds = (p * (_dot(dog, v, _NT) - dd)).astype(k.dtype)
            dq_ref[:, sl] = _dot(ds, k)
            p_all[rows, :] = p.astype(p_all.dtype)
            ds_all[rows, :] = ds
            q_all[rows, :] = qg
            do_all[rows, :] = dog
        dv_acc[...] += _dot(p_all[...], do_all[...], over_rows)
        dk_acc[...] += _dot(ds_all[...], q_all[...], over_rows)

        @pl.when(i == nq - 1)
        def _():
            dk_ref[...] = dk_acc[...]
            dv_ref[...] = dv_acc[...].astype(dv_ref.dtype)

    qspec = pl.BlockSpec((tq, _GW), lambda j, i: (i, j))
    kspec = pl.BlockSpec((s, HEAD_DIM), lambda j, i: (0, j))
    lspec = pl.BlockSpec((None, tq, GROUP), lambda j, i: (j, i, 0))
    return pl.pallas_call(
        body, name=name, grid=(N_KV_HEADS, nq), in_specs=[qspec, kspec, kspec, qspec, qspec, lspec],
        out_specs=[qspec, kspec, kspec],
        out_shape=[_sds((s, ATTN_WIDTH), F32), _sds((s, KV_WIDTH), F32), _sds((s, KV_WIDTH), MXU_DTYPE)],
        scratch_shapes=[pltpu.VMEM((GROUP * tq, s), MXU_DTYPE), pltpu.VMEM((GROUP * tq, s), MXU_DTYPE),
                        pltpu.VMEM((GROUP * tq, HEAD_DIM), MXU_DTYPE), pltpu.VMEM((GROUP * tq, HEAD_DIM), MXU_DTYPE),
                        pltpu.VMEM((s, HEAD_DIM), F32), pltpu.VMEM((s, HEAD_DIM), F32)],
        compiler_params=_params(("parallel", "arbitrary")),
    )(qp, kp, vb, do, o, lse)


GATE_TW = 1024


def gate_fwd(ya, yb, z, bias, *, name):
    s = z.shape[0]
    ts = min(s, 512)

    def body(ya_ref, yb_ref, ga_ref, gb_ref, ba_ref, bb_ref, o_ref):
        sa = jax.nn.sigmoid(ga_ref[...] + ba_ref[...])
        sb = jax.nn.sigmoid(gb_ref[...] + bb_ref[...])
        o_ref[...] = (sa * ya_ref[...] + sb * yb_ref[...]).astype(o_ref.dtype)

    tile = pl.BlockSpec((ts, GATE_TW), lambda i, j: (i, j))
    ga = pl.BlockSpec((ts, GATE_TW), lambda i, j: (i, OFF_GA // GATE_TW + j))
    gb = pl.BlockSpec((ts, GATE_TW), lambda i, j: (i, OFF_GB // GATE_TW + j))
    ba = pl.BlockSpec((1, GATE_TW), lambda i, j: (0, j))
    bb = pl.BlockSpec((1, GATE_TW), lambda i, j: (0, D_MODEL // GATE_TW + j))
    return pl.pallas_call(
        body, name=name, grid=(s // ts, D_MODEL // GATE_TW), in_specs=[tile, tile, ga, gb, ba, bb], out_specs=tile,
        out_shape=_sds((s, D_MODEL), MXU_DTYPE), compiler_params=_params(("parallel", "parallel")),
    )(ya, yb, z, z, bias, bias)


def gate_bwd(dmi, ya, yb, z, bias, *, name):
    s = z.shape[0]
    ts = min(s, 512)
    nt = s // ts

    def body(d_ref, ya_ref, yb_ref, ga_ref, gb_ref, ba_ref, bb_ref, dya_ref, dyb_ref, dga_ref, dgb_ref, dba_ref,
             dbb_ref, acc_a, acc_b):
        d = d_ref[...]
        sa = jax.nn.sigmoid(ga_ref[...] + ba_ref[...])
        sb = jax.nn.sigmoid(gb_ref[...] + bb_ref[...])
        dya_ref[...] = (d * sa).astype(dya_ref.dtype)
        dyb_ref[...] = (d * sb).astype(dyb_ref.dtype)
        dga = d * ya_ref[...] * sa * (1.0 - sa)
        dgb = d * yb_ref[...] * sb * (1.0 - sb)
        dga_ref[...] = dga.astype(dga_ref.dtype)
        dgb_ref[...] = dgb.astype(dgb_ref.dtype)
        i = pl.program_id(1)

        @pl.when(i == 0)
        def _():
            acc_a[...] = jnp.zeros_like(acc_a)
            acc_b[...] = jnp.zeros_like(acc_b)

        acc_a[...] += _fold8(dga)
        acc_b[...] += _fold8(dgb)

        @pl.when(i == nt - 1)
        def _():
            dba_ref[...] = acc_a[...].sum(axis=0, keepdims=True)
            dbb_ref[...] = acc_b[...].sum(axis=0, keepdims=True)

    tile = pl.BlockSpec((ts, GATE_TW), lambda j, i: (i, j))
    ga = pl.BlockSpec((ts, GATE_TW), lambda j, i: (i, OFF_GA // GATE_TW + j))
    gb = pl.BlockSpec((ts, GATE_TW), lambda j, i: (i, OFF_GB // GATE_TW + j))
    ba = pl.BlockSpec((1, GATE_TW), lambda j, i: (0, j))
    bb = pl.BlockSpec((1, GATE_TW), lambda j, i: (0, D_MODEL // GATE_TW + j))
    acc = pltpu.VMEM((SUBLANE, GATE_TW), F32)
    return pl.pallas_call(
        body, name=name, grid=(D_MODEL // GATE_TW, nt), in_specs=[tile, tile, tile, ga, gb, ba, bb],
        out_specs=[tile, tile, tile, tile, ba, ba],
        out_shape=[_sds((s, D_MODEL), MXU_DTYPE)] * 4 + [_sds((1, D_MODEL), F32)] * 2,
        scratch_shapes=[acc, acc], compiler_params=_params(("parallel", "arbitrary")),
    )(dmi, ya, yb, z, z, bias, bias)


HBM = pl.BlockSpec(memory_space=pltpu.HBM)
SEM = pl.BlockSpec(memory_space=pltpu.SEMAPHORE)
EFFECT = pltpu.SideEffectType.DATAFLOW_SIDE_EFFECTING
TOKEN = (SUBLANE, LANE)


def _place():
    return lax.axis_index("x"), lax.axis_index("y"), lax.axis_index("c")


def _window(ref, shard_shape, axis, d):
    r, c = shard_shape
    lead = (slice(None),) * (len(ref.shape) - 2)
    if axis == 0:
        return ref.at[lead + (pl.ds(pl.multiple_of(d * r, SUBLANE), r), slice(None))]
    return ref.at[lead + (slice(None), pl.ds(pl.multiple_of(d * c, LANE), c))]


def _hbm(a):
    return pltpu.with_memory_space_constraint(a, pltpu.HBM)


def _remote(src, dst, send_sems, recv_sems, i, to):
    return pltpu.make_async_remote_copy(src_ref=src, dst_ref=dst, send_sem=send_sems.at[i], recv_sem=recv_sems.at[i],
                                        device_id=to, device_id_type=MESH)


def cast_into_full(w, layer, axis, me, dtype, *, name):
    _, r, c = w.shape
    tr = min(r, 256)
    nr = r // tr
    in_spec = pl.BlockSpec((None, tr, c), lambda i, me_ref: (layer, i, 0))
    if axis == 0:
        out_spec = pl.BlockSpec((tr, c), lambda i, me_ref: (me_ref[0] * nr + i, 0))
        shape = (N_DEV * r, c)
    else:
        out_spec = pl.BlockSpec((tr, c), lambda i, me_ref: (i, me_ref[0]))
        shape = (r, N_DEV * c)

    def body(me_ref, w_ref, o_ref):
        o_ref[...] = w_ref[...].astype(o_ref.dtype)

    return pl.pallas_call(
        body, name=name,
        grid_spec=pltpu.PrefetchScalarGridSpec(num_scalar_prefetch=1, grid=(nr,), in_specs=[in_spec], out_specs=out_spec),
        out_shape=_sds(shape, dtype), compiler_params=_params(("parallel",)),
    )(me, w)


class _GatherPlan:
    def __init__(self, fulls, shard_shapes, axes):
        x, y, c = _place()
        self.n = len(fulls)
        self.me, self.sibling = (x, y, c), (x, y, 1 - c)
        self.chips = [(1 - x, y), (x, 1 - y), (1 - x, 1 - y)]
        self.win = lambda a, p: _window(fulls[a], shard_shapes[a], axes[a], 4 * p[0] + 2 * p[1] + p[2])

    def first(self, send_sems, recv_sems):
        out = []
        for a in range(self.n):
            mine = self.win(a, self.me)
            out.append(_remote(mine, mine, send_sems, recv_sems, 4 * a, self.sibling))
            out += [_remote(mine, mine, send_sems, recv_sems, 4 * a + 1 + j, (*chip, self.me[2]))
                    for j, chip in enumerate(self.chips)]
        return out

    def first_arrivals(self, send_sems, recv_sems):
        c = self.me[2]
        out = []
        for a in range(self.n):
            blocks = [self.sibling] + [(*chip, c) for chip in self.chips]
            out += [_remote(self.win(a, b), self.win(a, b), send_sems, recv_sems, 4 * a + k, self.me)
                    for k, b in enumerate(blocks)]
        return out

    def passed(self, send_sems, recv_sems):
        c = self.me[2]
        return [_remote(self.win(a, (*chip, c)), self.win(a, (*chip, c)), send_sems, recv_sems, 3 * a + j, self.sibling)
                for a in range(self.n) for j, chip in enumerate(self.chips)]

    def passed_arrivals(self, send_sems, recv_sems):
        c = self.me[2]
        return [_remote(self.win(a, (*chip, 1 - c)), self.win(a, (*chip, 1 - c)), send_sems, recv_sems, 3 * a + j, self.me)
                for a in range(self.n) for j, chip in enumerate(self.chips)]


def gather_start(fulls, after, shard_shapes, axes, *, name):
    n = len(fulls)

    def body(*refs):
        ins, send_sems, recv_sems, token = refs[:n], refs[n + 1], refs[n + 2], refs[-1]
        for cp in _GatherPlan(ins, shard_shapes, axes).first(send_sems, recv_sems):
            cp.start()
        token[...] = jnp.zeros_like(token)

    res = pl.pallas_call(
        body, name=name, in_specs=[HBM] * n + [ANY],
        out_shape=(pltpu.SemaphoreType.DMA((4 * n,)), pltpu.SemaphoreType.DMA((4 * n,)),
                   *[pltpu.HBM(f.shape, f.dtype) for f in fulls], _sds(TOKEN, F32)),
        out_specs=(SEM, SEM, *[HBM] * n, pl.BlockSpec(memory_space=pltpu.VMEM)),
        input_output_aliases={a: 2 + a for a in range(n)}, compiler_params=pltpu.CompilerParams(has_side_effects=EFFECT),
    )(*[_hbm(f) for f in fulls], after)
    return res[0], res[1], list(res[2:2 + n]), res[-1]


def gather_pass(send_sems, recv_sems, fulls, after, shard_shapes, axes, *, name):
    n = len(fulls)

    def body(*refs):
        ins, s1, r1 = refs[:n], refs[n], refs[n + 1]
        s2, r2, token = refs[n + 3], refs[n + 4], refs[-1]
        plan = _GatherPlan(ins, shard_shapes, axes)
        for cp in plan.first_arrivals(s1, r1):
            cp.wait_recv()
        for cp in plan.first(s1, r1):
            cp.wait_send()
        for cp in plan.passed(s2, r2):
            cp.start()
        token[...] = jnp.zeros_like(token)

    res = pl.pallas_call(
        body, name=name, in_specs=[HBM] * n + [SEM, SEM, ANY],
        out_shape=(pltpu.SemaphoreType.DMA((3 * n,)), pltpu.SemaphoreType.DMA((3 * n,)),
                   *[pltpu.HBM(f.shape, f.dtype) for f in fulls], _sds(TOKEN, F32)),
        out_specs=(SEM, SEM, *[HBM] * n, pl.BlockSpec(memory_space=pltpu.VMEM)),
        input_output_aliases={a: 2 + a for a in range(n)}, compiler_params=pltpu.CompilerParams(has_side_effects=EFFECT),
    )(*fulls, send_sems, recv_sems, after)
    return res[0], res[1], list(res[2:2 + n]), res[-1]


def gather_wait(send_sems, recv_sems, fulls, after, shard_shapes, axes, *, name):
    n = len(fulls)

    def body(*refs):
        ins, s2, r2 = refs[:n], refs[n], refs[n + 1]
        plan = _GatherPlan(ins, shard_shapes, axes)
        for cp in plan.passed_arrivals(s2, r2):
            cp.wait_recv()
        for cp in plan.passed(s2, r2):
            cp.wait_send()

    return list(pl.pallas_call(
        body, name=name, in_specs=[HBM] * n + [SEM, SEM, ANY], out_shape=tuple(pltpu.HBM(f.shape, f.dtype) for f in fulls),
        out_specs=tuple([HBM] * n), input_output_aliases={a: a for a in range(n)},
        compiler_params=pltpu.CompilerParams(has_side_effects=EFFECT),
    )(*fulls, send_sems, recv_sems, after))


def _pair_copies(grads, lands, shard_shapes, axes, send_sems, recv_sems):
    x, y, c = _place()
    return [_remote(_window(grads[a], shard_shapes[a], axes[a], 2 * q + (1 - c)), lands[a].at[q], send_sems, recv_sems,
                    N_CHIP * a + q, (x, y, 1 - c))
            for a in range(len(grads)) for q in range(N_CHIP)]


def _chip_sends(sums, lands, send_sems, recv_sems):
    x, y, c = _place()
    return [_remote(sums[a].at[2 * px + py], lands[a].at[2 * x + y], send_sems, recv_sems, 3 * a + j, (px, py, c))
            for a in range(len(sums)) for j, (px, py) in enumerate([(1 - x, y), (x, 1 - y), (1 - x, 1 - y)])]


def _chip_arrivals(sums, lands, send_sems, recv_sems):
    x, y, c = _place()
    return [_remote(sums[a].at[2 * x + y], lands[a].at[2 * px + py], send_sems, recv_sems, 3 * a + j, (x, y, c))
            for a in range(len(sums)) for j, (px, py) in enumerate([(1 - x, y), (x, 1 - y), (1 - x, 1 - y)])]


def exchange_start(srcs, after, land_shapes, make_sends, per_array, *, name):
    n = len(srcs)
    lands = [lax.empty(s, a.dtype) for s, a in zip(land_shapes, srcs)]

    def body(*refs):
        ins, zones = refs[:n], refs[n:2 * n]
        send_sems, recv_sems, token = refs[2 * n + 1], refs[2 * n + 2], refs[-1]
        for cp in make_sends(ins, zones, send_sems, recv_sems):
            cp.start()
        token[...] = jnp.zeros_like(token)

    res = pl.pallas_call(
        body, name=name, in_specs=[HBM] * (2 * n) + [ANY],
        out_shape=(pltpu.SemaphoreType.DMA((per_array * n,)), pltpu.SemaphoreType.DMA((per_array * n,)),
                   *[pltpu.HBM(a.shape, a.dtype) for a in srcs], *[pltpu.HBM(a.shape, a.dtype) for a in lands],
                   _sds(TOKEN, F32)),
        out_specs=(SEM, SEM, *[HBM] * (2 * n), pl.BlockSpec(memory_space=pltpu.VMEM)),
        input_output_aliases={a: 2 + a for a in range(2 * n)}, compiler_params=pltpu.CompilerParams(has_side_effects=EFFECT),
    )(*[_hbm(a) for a in srcs], *[_hbm(a) for a in lands], after)
    return res[0], res[1], list(res[2:2 + n]), list(res[2 + n:2 + 2 * n]), res[-1]


def exchange_wait(send_sems, recv_sems, srcs, lands, after, make_waits, *, name):
    n = len(srcs)

    def body(*refs):
        ins, zones, s, r = refs[:n], refs[n:2 * n], refs[2 * n], refs[2 * n + 1]
        sends, arrivals = make_waits(ins, zones, s, r)
        for cp in arrivals:
            cp.wait_recv()
        for cp in sends:
            cp.wait_send()

    res = pl.pallas_call(
        body, name=name, in_specs=[HBM] * (2 * n) + [SEM, SEM, ANY],
        out_shape=tuple(pltpu.HBM(a.shape, a.dtype) for a in (*srcs, *lands)), out_specs=tuple([HBM] * (2 * n)),
        input_output_aliases={a: a for a in range(2 * n)}, compiler_params=pltpu.CompilerParams(has_side_effects=EFFECT),
    )(*srcs, *lands, send_sems, recv_sems, after)
    return list(res[:n]), list(res[n:])


def pair_sum(grad, recv, shard_shape, axis, core, *, name):
    r, c = shard_shape
    tr = min(r, 256)
    nr = r // tr
    if axis == 0:
        gspec = pl.BlockSpec((tr, c), lambda q, i, cref: ((2 * q + cref[0]) * nr + i, 0))
    else:
        gspec = pl.BlockSpec((tr, c), lambda q, i, cref: (i, 2 * q + cref[0]))
    rspec = pl.BlockSpec((None, tr, c), lambda q, i, cref: (q, i, 0))

    def body(c_ref, g_ref, r_ref, o_ref):
        o_ref[...] = (g_ref[...].astype(F32) + r_ref[...].astype(F32)).astype(o_ref.dtype)

    return pl.pallas_call(
        body, name=name,
        grid_spec=pltpu.PrefetchScalarGridSpec(num_scalar_prefetch=1, grid=(N_CHIP, nr), in_specs=[gspec, rspec],
                                               out_specs=rspec),
        out_shape=_sds((N_CHIP, r, c), recv.dtype), compiler_params=_params(("parallel", "parallel")),
    )(core, grad, recv)


def all_reduce_small(part, *, name):
    r, c = part.shape

    def body(p_ref, o_ref, gath_ref, send_sems, recv_sems):
        x, y, cc = _place()
        me = 4 * x + 2 * y + cc
        gath_ref[me] = p_ref[...]
        peers = [(1 - x if k & 4 else x, 1 - y if k & 2 else y, 1 - cc if k & 1 else cc) for k in range(1, N_DEV)]
        copies = [_remote(p_ref, gath_ref.at[me], send_sems, recv_sems, i, peer) for i, peer in enumerate(peers)]
        for cp in copies:
            cp.start()
        for i, (px, py, pc) in enumerate(peers):
            _remote(p_ref, gath_ref.at[4 * px + 2 * py + pc], send_sems, recv_sems, i, (x, y, cc)).wait_recv()
        for cp in copies:
            cp.wait_send()
        acc = gath_ref[0]
        for d in range(1, N_DEV):
            acc = acc + gath_ref[d]
        o_ref[...] = acc

    vm = pl.BlockSpec(memory_space=pltpu.VMEM)
    return pl.pallas_call(
        body, name=name, in_specs=[vm], out_specs=vm, out_shape=_sds((r, c), F32),
        scratch_shapes=[pltpu.VMEM((N_DEV, r, c), F32), pltpu.SemaphoreType.DMA((N_DEV - 1,)),
                        pltpu.SemaphoreType.DMA((N_DEV - 1,))],
    )(part)


def _adamw(w, g, m, v):
    m = ADAM_B1 * m + (1.0 - ADAM_B1) * g
    v = ADAM_B2 * v + (1.0 - ADAM_B2) * (g * g)
    m_hat = m / (1.0 - ADAM_B1 ** ADAM_STEP)
    v_hat = v / (1.0 - ADAM_B2 ** ADAM_STEP)
    delta = -ADAM_LR * (m_hat / (jnp.sqrt(v_hat) + ADAM_EPS) + ADAM_WD * w)
    return delta, m, v


def reduce_adam(own, landed, chip, w, m, v, layer, outs, *, name):
    _, r, c = w.shape
    tr = min(r, 128)
    first = outs is None

    def body(chip_ref, own_ref, l1_ref, l2_ref, l3_ref, w_ref, m_ref, v_ref, *rest):
        g_out, d_out, m_out, v_out = rest[-4:]
        g = own_ref[...].astype(F32) + l1_ref[...].astype(F32) + l2_ref[...].astype(F32) + l3_ref[...].astype(F32)
        d, mn, vn = _adamw(w_ref[...], g, m_ref[...], v_ref[...])
        g_out[...] = g
        d_out[...] = d
        m_out[...] = mn
        v_out[...] = vn

    spec = pl.BlockSpec((None, tr, c), lambda i, chip_ref: (layer, i, 0))

    def slot(step):
        return pl.BlockSpec((None, tr, c), lambda i, chip_ref: ((chip_ref[0] + step) % N_CHIP, i, 0))

    n_in = 8
    return pl.pallas_call(
        body, name=name,
        grid_spec=pltpu.PrefetchScalarGridSpec(
            num_scalar_prefetch=1, grid=(r // tr,),
            in_specs=[slot(0), slot(1), slot(2), slot(3), spec, spec, spec] + ([] if first else [ANY] * 4),
            out_specs=[spec] * 4),
        out_shape=[_sds(w.shape, F32)] * 4, input_output_aliases={} if first else {n_in + i: i for i in range(4)},
        compiler_params=_params(("parallel",)),
    )(chip, own, landed, landed, landed, w, m, v, *([] if first else outs))


def adam_small(g, w, m, v, *, name):
    def body(g_ref, w_ref, m_ref, v_ref, d_out, m_out, v_out):
        d, mn, vn = _adamw(w_ref[...], g_ref[...], m_ref[...], v_ref[...])
        d_out[...] = d
        m_out[...] = mn
        v_out[...] = vn

    return pl.pallas_call(body, name=name, out_shape=[_sds(w.shape, F32)] * 3)(g, w, m, v)


BIG = ("w_in", "w_out_conv", "w_out_attn", "w_merge", "w_up", "w_down")
GATHERED = BIG + ("conv_w",)
BIG_AXIS = {"w_in": 1, "w_out_conv": 0, "w_out_attn": 0, "w_merge": 0, "w_up": 1, "w_down": 0, "conv_w": 1}
FIRST_GATHER_GROUPS = (("w_in", "conv_w"), ("w_out_conv", "w_out_attn", "w_merge"), ("w_up", "w_down"))
LAST_REDUCE_GROUPS = (("w_down", "w_up"), ("w_merge", "w_out_conv", "w_out_attn"), ("w_in",))


class _Gather:
    def __init__(self, weights, keys, layer, me, after, tag):
        self.keys, self.tag = keys, tag
        self.shapes = [weights[k].shape[1:] for k in keys]
        self.axes = [BIG_AXIS[k] for k in keys]
        own = [cast_into_full(weights[k], layer, ax, me, F32 if k == "conv_w" else MXU_DTYPE, name="cast_" + k)
               for k, ax in zip(keys, self.axes)]
        self.send, self.recv, self.fulls, self.token = gather_start(own, after, self.shapes, self.axes,
                                                                    name="gather_start_" + tag)

    def pass_on(self, after):
        self.send, self.recv, self.fulls, self.token = gather_pass(self.send, self.recv, self.fulls, after, self.shapes,
                                                                   self.axes, name="gather_pass_" + self.tag)
        return self.token

    def wait(self, after):
        fulls = gather_wait(self.send, self.recv, self.fulls, after, self.shapes, self.axes, name="gather_wait_" + self.tag)
        return dict(zip(self.keys, fulls))


class _Reduce:
    def __init__(self, weights, keys, tag):
        self.keys, self.tag = keys, tag
        self.shapes = [weights[k].shape[1:] for k in keys]
        self.axes = [BIG_AXIS[k] for k in keys]
        self.pair_shapes = [(N_CHIP, *shp) for shp in self.shapes]

    def _pair(self, i, z, ss, rs):
        return _pair_copies(i, z, self.shapes, self.axes, ss, rs)

    def begin(self, grads, after):
        self.send, self.recv, self.src, self.land, self.token = exchange_start(
            [grads[k] for k in self.keys], after, self.pair_shapes, self._pair, N_CHIP, name="rs_pair_start_" + self.tag)
        return self.token

    def middle(self, after, core):
        both = lambda i, z, ss, rs: (self._pair(i, z, ss, rs),) * 2
        grads, from_sibling = exchange_wait(self.send, self.recv, self.src, self.land, after, both,
                                            name="rs_pair_wait_" + self.tag)
        sums = [pair_sum(g, rcv, shp, ax, core, name="pair_sum_" + k)
                for k, g, rcv, shp, ax in zip(self.keys, grads, from_sibling, self.shapes, self.axes)]
        self.send, self.recv, self.src, self.land, self.token = exchange_start(
            sums, after, self.pair_shapes, _chip_sends, 3, name="rs_chips_start_" + self.tag)
        return self.token

    def end(self, after):
        both = lambda i, z, ss, rs: (_chip_sends(i, z, ss, rs), _chip_arrivals(i, z, ss, rs))
        sums, landed = exchange_wait(self.send, self.recv, self.src, self.land, after, both,
                                     name="rs_chips_wait_" + self.tag)
        return list(zip(self.keys, sums, landed))
SMALL = (("norm_mix_pre", D_MODEL), ("gate_bias", 2 * D_MODEL), ("norm_mix_post", D_MODEL), ("norm_mlp_pre", D_MODEL),
         ("norm_mlp_post", D_MODEL), ("q_norm", HEAD_DIM), ("k_norm", HEAD_DIM))
SMALL_WIDTH = sum(w for _, w in SMALL)
WEIGHTS = ("norm_mix_pre", "w_in", "gate_bias", "conv_w", "q_norm", "k_norm", "w_out_conv", "w_out_attn", "w_merge",
           "norm_mix_post", "norm_mlp_pre", "w_up", "w_down", "norm_mlp_post")


def kernel(x, norm_mix_pre, w_in, gate_bias, conv_w, q_norm, k_norm, w_out_conv, w_out_attn, w_merge, norm_mix_post, norm_mlp_pre, w_up, w_down, norm_mlp_post, loss_target, m_norm_mix_pre, m_w_in, m_gate_bias, m_conv_w, m_q_norm, m_k_norm, m_w_out_conv, m_w_out_attn, m_w_merge, m_norm_mix_post, m_norm_mlp_pre, m_w_up, m_w_down, m_norm_mlp_post, v_norm_mix_pre, v_w_in, v_gate_bias, v_conv_w, v_q_norm, v_k_norm, v_w_out_conv, v_w_out_attn, v_w_merge, v_norm_mix_post, v_norm_mlp_pre, v_w_up, v_w_down, v_norm_mlp_post):
    w = dict(norm_mix_pre=norm_mix_pre, w_in=w_in, gate_bias=gate_bias, conv_w=conv_w, q_norm=q_norm, k_norm=k_norm,
             w_out_conv=w_out_conv, w_out_attn=w_out_attn, w_merge=w_merge, norm_mix_post=norm_mix_post,
             norm_mlp_pre=norm_mlp_pre, w_up=w_up, w_down=w_down, norm_mlp_post=norm_mlp_post)
    mom = dict(norm_mix_pre=m_norm_mix_pre, w_in=m_w_in, gate_bias=m_gate_bias, conv_w=m_conv_w, q_norm=m_q_norm,
               k_norm=m_k_norm, w_out_conv=m_w_out_conv, w_out_attn=m_w_out_attn, w_merge=m_w_merge,
               norm_mix_post=m_norm_mix_post, norm_mlp_pre=m_norm_mlp_pre, w_up=m_w_up, w_down=m_w_down,
               norm_mlp_post=m_norm_mlp_post)
    var = dict(norm_mix_pre=v_norm_mix_pre, w_in=v_w_in, gate_bias=v_gate_bias, conv_w=v_conv_w, q_norm=v_q_norm,
               k_norm=v_k_norm, w_out_conv=v_w_out_conv, w_out_attn=v_w_out_attn, w_merge=v_w_merge,
               norm_mix_post=v_norm_mix_post, norm_mlp_pre=v_norm_mlp_pre, w_up=v_w_up, w_down=v_w_down,
               norm_mlp_post=v_norm_mlp_post)
    depth = w_in.shape[0]
    s = x.shape[1]
    xs = x.reshape(s, D_MODEL)
    target = loss_target.reshape(s, D_MODEL)
    x_idx, y_idx, c_idx = _place()
    as_operand = lambda i: jnp.reshape(i, (1,)).astype(jnp.int32)
    core, chip, me = as_operand(c_idx), as_operand(2 * x_idx + y_idx), as_operand(4 * x_idx + 2 * y_idx + c_idx)
    cos_t, sin_t = rope_tables(s)

    def vec(name, l):
        return w[name][l].reshape(1, -1)

    saved = []
    h = rms_fwd(xs, vec("norm_mix_pre", 0), name="rms_first")
    first, after = [], h
    for i, keys in enumerate(FIRST_GATHER_GROUPS):
        first.append(_Gather(w, keys, 0, me, after, f"0{'abc'[i]}"))
        after = first[-1].token
    full = first[0].wait(first[0].pass_on(after))
    layers = []
    for l in range(depth):
        layers.append(full)
        nxt = l + 1 < depth
        if nxt:
            coming = _Gather(w, GATHERED, l + 1, me, full["w_in"], str(l + 1))
        z = matmul(h, full["w_in"], mode="nn", m=s, n=IN_WIDTH, k=D_MODEL, out_dtypes=[F32], name="mm_in",
                   after=[coming.token] if nxt else [])
        t = conv_fwd(z, full["conv_w"], name="conv_fwd")
        qp, kp, vb = qk_prep_fwd(z, vec("q_norm", l), vec("k_norm", l), cos_t, sin_t, name="qk_fwd")
        o, lse = attn_fwd(qp, kp, vb, name="attn_fwd", after=[first[1].pass_on(qp)] if l == 0 else [])
        if l == 0:
            full.update(first[1].wait(o))
        ya = matmul(t, full["w_out_conv"], mode="nn", m=s, n=D_MODEL, k=CONV_WIDTH, out_dtypes=[F32], name="mm_out_conv",
                    after=[first[2].pass_on(o)] if l == 0 else [])
        yb = matmul(o, full["w_out_attn"], mode="nn", m=s, n=D_MODEL, k=ATTN_WIDTH, out_dtypes=[F32], name="mm_out_attn")
        mi = gate_fwd(ya, yb, z, vec("gate_bias", l), name="gate_fwd")
        mixed = matmul(mi, full["w_merge"], mode="nn", m=s, n=D_MODEL, k=D_MODEL, out_dtypes=[F32], name="mm_merge")
        if l == 0:
            full.update(first[2].wait(mixed))
        x_mid, h2 = rms_residual_fwd(xs, mixed, vec("norm_mix_post", l), vec("norm_mlp_pre", l), name="res_mix")
        act, r = matmul(h2, full["w_up"], mode="nn", m=s, n=D_FF, k=D_MODEL, out_dtypes=[MXU_DTYPE, MXU_DTYPE],
                        name="mm_up", epilogue=lambda acc: (acc, jnp.square(jnp.maximum(acc, 0.0))))
        f = matmul(r, full["w_down"], mode="nn", m=s, n=D_MODEL, k=D_FF, out_dtypes=[F32], name="mm_down",
                   after=[coming.pass_on(r)] if nxt else [])
        g_next = vec("norm_mix_pre", l + 1) if nxt else None
        x_out, h_next = rms_residual_fwd(x_mid, f, vec("norm_mlp_post", l), g_next, name="res_mlp" if nxt else "res_last")
        saved.append(dict(x_in=xs, h=h, z=z, t=t, qp=qp, kp=kp, vb=vb, o=o, lse=lse, ya=ya, yb=yb, mi=mi, mixed=mixed,
                          x_mid=x_mid, h2=h2, act=act, r=r, f=f))
        if nxt:
            full = coming.wait(x_out)
        xs, h = x_out, h_next

    dx, loss_part = loss_and_grad(xs, target, name="loss")
    loss = lax.psum(jnp.sum(loss_part), ("x", "y", "c"))

    small_rows = [None] * depth
    conv_rows = [None] * depth
    out_g, out_d, out_m, out_v = {}, {}, {}, {}
    big_outs = {k: None for k in BIG}
    pending = None
    last = []

    def finish(reduction, layer, after):
        for k, own, landed in reduction.end(after):
            big_outs[k] = reduce_adam(own, landed, chip, w[k], mom[k], var[k], layer, big_outs[k], name="adam_" + k)

    for l in reversed(range(depth)):
        sv, full = saved[l], layers[l]
        grads = {}
        groups = l == 0

        def wgrad(key, lhs, rhs, m, n, after=()):
            grads[key] = matmul(lhs, rhs, mode="tn", m=m, n=n, k=s, tk=s, out_dtypes=[COMM_DTYPE], name="wg_" + key,
                                after=after)

        def begin_group(i, after):
            last.append(_Reduce(w, LAST_REDUCE_GROUPS[i], f"{l}{'abc'[i]}"))
            return [last[i].begin(grads, after)]

        df, dg_mlp_post = rms_bwd(sv["f"], vec("norm_mlp_post", l), dx, None, out_dtype=MXU_DTYPE, name="rmsb_mlp_post",
                                  after=[pending.token] if pending else [])
        da = matmul(df, full["w_down"], mode="nt", m=s, n=D_FF, k=D_MODEL, out_dtypes=[MXU_DTYPE], name="mm_d_down",
                    extra=(sv["act"],), epilogue=lambda acc, a: (acc * (2.0 * jnp.maximum(a.astype(F32), 0.0)),))
        tokens = [pending.middle(da, core)] if pending else []
        wgrad("w_down", sv["r"], df, D_FF, D_MODEL)
        dh2 = matmul(da, full["w_up"], mode="nt", m=s, n=D_MODEL, k=D_FF, out_dtypes=[F32], name="mm_d_up", after=tokens)
        wgrad("w_up", sv["h2"], da, D_MODEL, D_FF)
        tokens = begin_group(0, dh2) if groups else []
        dx_mid, dg_mlp_pre = rms_bwd(sv["x_mid"], vec("norm_mlp_pre", l), dh2, dx, out_dtype=F32, name="rmsb_mlp_pre",
                                     after=tokens)
        dmixed, dg_mix_post = rms_bwd(sv["mixed"], vec("norm_mix_post", l), dx_mid, None, out_dtype=MXU_DTYPE,
                                      name="rmsb_mix_post")
        dmi = matmul(dmixed, full["w_merge"], mode="nt", m=s, n=D_MODEL, k=D_MODEL, out_dtypes=[F32], name="mm_d_merge")
        tokens = [last[0].middle(dmi, core)] if groups else []
        wgrad("w_merge", sv["mi"], dmixed, D_MODEL, D_MODEL, after=tokens)
        dya, dyb, dga, dgb, dba, dbb = gate_bwd(dmi, sv["ya"], sv["yb"], sv["z"], vec("gate_bias", l), name="gate_bwd")
        wgrad("w_out_conv", sv["t"], dya, CONV_WIDTH, D_MODEL)
        wgrad("w_out_attn", sv["o"], dyb, ATTN_WIDTH, D_MODEL)
        tokens = begin_group(1, dyb) if groups else []
        dt = matmul(dya, full["w_out_conv"], mode="nt", m=s, n=CONV_WIDTH, k=D_MODEL, out_dtypes=[F32],
                    name="mm_d_out_conv", after=tokens)
        do = matmul(dyb, full["w_out_attn"], mode="nt", m=s, n=ATTN_WIDTH, k=D_MODEL, out_dtypes=[MXU_DTYPE],
                    name="mm_d_out_attn")
        tokens = [last[1].middle(do, core)] if groups else []
        dcb, dcc, dci, dconv_w = conv_bwd(dt, sv["z"], full["conv_w"], name="conv_bwd", after=tokens)
        dqp, dkp, dv = attn_bwd(sv["qp"], sv["kp"], sv["vb"], do, sv["o"], sv["lse"], name="attn_bwd")
        dq, dk, dqn, dkn = qk_prep_bwd(dqp, dkp, sv["z"], vec("q_norm", l), vec("k_norm", l), cos_t, sin_t, name="qk_bwd")
        dz = jnp.concatenate([dcb, dcc, dci, dq, dk, dv, dga, dgb], axis=1)
        wgrad("w_in", sv["h"], dz, D_MODEL, IN_WIDTH)
        tokens = begin_group(2, dz) if groups else []
        dh = matmul(dz, full["w_in"], mode="nt", m=s, n=D_MODEL, k=IN_WIDTH, tk=IN_WIDTH // 4, out_dtypes=[F32],
                    name="mm_d_in", after=tokens)
        dx, dg_mix_pre = rms_bwd(sv["x_in"], vec("norm_mix_pre", l), dh, dx_mid, out_dtype=F32, name="rmsb_mix_pre")
        small_rows[l] = jnp.concatenate([dg_mix_pre, dba, dbb, dg_mix_post, dg_mlp_pre, dg_mlp_post, dqn, dkn], axis=1)
        conv_rows[l] = dconv_w.reshape(1, 3 * CONV_WIDTH)
        if pending:
            finish(pending, l + 1, dx)
        pending = None
        if not groups:
            pending = _Reduce(w, BIG, str(l))
            pending.begin(grads, dx)

    grad_x = dx.reshape(1, s, D_MODEL)

    small_part = jnp.concatenate([jnp.concatenate(small_rows, axis=0), jnp.concatenate(conv_rows, axis=0)], axis=1)
    small_sum = all_reduce_small(small_part, name="allreduce_small")
    pack = lambda src: jnp.concatenate([src[k].reshape(depth, wd) for k, wd in SMALL], axis=1)
    g_small = small_sum[:, :SMALL_WIDTH]
    d_small, m_small, v_small = adam_small(g_small, pack(w), pack(mom), pack(var), name="adam_small")
    off = 0
    for k, wd in SMALL:
        for dst, src in ((out_g, g_small), (out_d, d_small), (out_m, m_small), (out_v, v_small)):
            dst[k] = src[:, off:off + wd]
        off += wd
    cshard = CONV_WIDTH // N_DEV
    g_conv = lax.dynamic_slice_in_dim(small_sum[:, SMALL_WIDTH:].reshape(depth, 3, CONV_WIDTH), me[0] * cshard, cshard,
                                      axis=2).reshape(depth, 3 * cshard)
    flat = lambda a: a.reshape(depth, 3 * cshard)
    d_conv, m_conv, v_conv = adam_small(g_conv, flat(conv_w), flat(m_conv_w), flat(v_conv_w), name="adam_conv")
    for dst, src in ((out_g, g_conv), (out_d, d_conv), (out_m, m_conv), (out_v, v_conv)):
        dst["conv_w"] = src.reshape(depth, 3, cshard)

    token = last[2].middle(d_small, core)
    for reduction in last:
        finish(reduction, 0, token)
    for k in BIG:
        out_g[k], out_d[k], out_m[k], out_v[k] = big_outs[k]

    return (loss, grad_x, *[out_g[k] for k in WEIGHTS], *[out_d[k] for k in WEIGHTS], *[out_m[k] for k in WEIGHTS],
            *[out_v[k] for k in WEIGHTS])
```

```python
import math

import jax
import jax.numpy as jnp
from jax import lax
from jax.experimental import pallas as pl
from jax.experimental.pallas import tpu as pltpu

F32 = jnp.float32
MXU_DTYPE = jnp.bfloat16
COMM_DTYPE = jnp.bfloat16
ACT_DTYPE = jnp.bfloat16

D_MODEL = 2048
HEAD_DIM = 128
N_Q_HEADS = 16
N_KV_HEADS = 4
GROUP = N_Q_HEADS // N_KV_HEADS
ATTN_WIDTH = N_Q_HEADS * HEAD_DIM
KV_WIDTH = N_KV_HEADS * HEAD_DIM
CONV_WIDTH = D_MODEL
D_FF = 4 * D_MODEL
GRID_W = 64
ROPE_THETA = 10000.0
RMS_EPS = 1e-6
IN_WIDTH = 3 * CONV_WIDTH + ATTN_WIDTH + 2 * KV_WIDTH + 2 * D_MODEL
OFF_CB, OFF_CC, OFF_CI = 0, CONV_WIDTH, 2 * CONV_WIDTH
OFF_Q = 3 * CONV_WIDTH
OFF_K = OFF_Q + ATTN_WIDTH
OFF_V = OFF_K + KV_WIDTH
OFF_GA = OFF_V + KV_WIDTH
OFF_GB = OFF_GA + D_MODEL
ATTN_SCALE = 1.0 / math.sqrt(HEAD_DIM)

ADAM_LR, ADAM_B1, ADAM_B2, ADAM_EPS, ADAM_WD, ADAM_STEP = 0.001, 0.9, 0.999, 1e-08, 0.01, 10

N_DEV = 8
N_CHIP = 4
LANE = 128
SUBLANE = 8
VMEM_LIMIT = 48 * 1024 * 1024
MESH = pl.DeviceIdType.MESH
ANY = pl.BlockSpec(memory_space=pl.ANY)


def _sds(shape, dtype):
    return jax.ShapeDtypeStruct(tuple(shape), dtype)


def _params(sem, vmem=VMEM_LIMIT):
    return pltpu.CompilerParams(dimension_semantics=sem, vmem_limit_bytes=vmem)


def _rows(ts, w, col=0):
    return pl.BlockSpec((ts, w), lambda i: (i, col))


def _fixed(shape):
    return pl.BlockSpec(shape, lambda *_: (0,) * len(shape))


def _fold8(v):
    ts, w = v.shape
    return v.reshape(ts // SUBLANE, SUBLANE, w).sum(axis=0)


def matmul(a, b, *, mode, m, n, k, name, out_dtypes, tm=1024, tn=1024, tk=2048, epilogue=None, extra=(), after=()):
    tm, tn, tk = min(tm, m), min(tn, n), min(tk, k)
    nm, nn, nk = m // tm, n // tn, k // tk
    assert nm * tm == m and nn * tn == n and nk * tk == k, (name, m, n, k, tm, tn, tk)
    if mode == "tn":
        a_spec = pl.BlockSpec((tk, tm), lambda i, j, kk: (kk, i))
        dims = (((0,), (0,)), ((), ()))
    else:
        a_spec = pl.BlockSpec((tm, tk), lambda i, j, kk: (i, kk))
        dims = (((1,), (1 if mode == "nt" else 0,)), ((), ()))
    if mode == "nt":
        b_spec = pl.BlockSpec((tn, tk), lambda i, j, kk: (j, kk))
    else:
        b_spec = pl.BlockSpec((tk, tn), lambda i, j, kk: (kk, j))
    tile = pl.BlockSpec((tm, tn), lambda i, j, kk: (i, j))
    n_out, n_extra, n_after = len(out_dtypes), len(extra), len(after)
    if epilogue is None:
        epilogue = lambda acc: (acc,)

    def body(a_ref, b_ref, *rest):
        extra_refs = rest[:n_extra]
        outs = rest[n_extra + n_after:][:n_out]
        part = lax.dot_general(a_ref[...].astype(MXU_DTYPE), b_ref[...].astype(MXU_DTYPE), dims,
                               preferred_element_type=F32)

        def finish(acc):
            for o_ref, val in zip(outs, epilogue(acc, *[r[...] for r in extra_refs])):
                o_ref[...] = val.astype(o_ref.dtype)

        if nk == 1:
            finish(part)
        else:
            acc_ref = rest[-1]
            kk = pl.program_id(2)

            @pl.when(kk == 0)
            def _():
                acc_ref[...] = part

            @pl.when(kk > 0)
            def _():
                acc_ref[...] += part

            @pl.when(kk == nk - 1)
            def _():
                finish(acc_ref[...])

    res = pl.pallas_call(
        body, name=name, grid=(nm, nn, nk), in_specs=[a_spec, b_spec] + [tile] * n_extra + [ANY] * n_after,
        out_specs=[tile] * n_out, out_shape=[_sds((m, n), dt) for dt in out_dtypes],
        scratch_shapes=[pltpu.VMEM((tm, tn), F32)] if nk > 1 else [],
        compiler_params=_params(("parallel", "parallel", "arbitrary")),
    )(a, b, *extra, *after)
    return res[0] if n_out == 1 else res


def _rstd(x):
    return lax.rsqrt(jnp.mean(x * x, axis=-1, keepdims=True) + RMS_EPS)


def _rms_bwd(x, g, dy):
    rstd = _rstd(x)
    xh = x * rstd
    gy = dy * g
    dx = rstd * (gy - xh * jnp.mean(gy * xh, axis=-1, keepdims=True))
    return dx, dy * xh


def rms_fwd(x, g, *, name):
    s = x.shape[0]
    ts = min(s, 512)

    def body(x_ref, g_ref, h_ref):
        xv = x_ref[...]
        h_ref[...] = (xv * _rstd(xv) * g_ref[...]).astype(h_ref.dtype)

    return pl.pallas_call(
        body, name=name, grid=(s // ts,), in_specs=[_rows(ts, D_MODEL), _fixed((1, D_MODEL))],
        out_specs=_rows(ts, D_MODEL), out_shape=_sds((s, D_MODEL), MXU_DTYPE), compiler_params=_params(("parallel",)),
    )(x, g)


def rms_residual_fwd(x, y, g_post, g_next, *, name):
    s = x.shape[0]
    ts = min(s, 512)
    with_next = g_next is not None

    def body(x_ref, y_ref, gp_ref, *rest):
        yv = y_ref[...].astype(F32)
        xn = x_ref[...] + yv * _rstd(yv) * gp_ref[...]
        if with_next:
            gn_ref, xo_ref, h_ref = rest
            h_ref[...] = (xn * _rstd(xn) * gn_ref[...]).astype(h_ref.dtype)
        else:
            (xo_ref,) = rest
        xo_ref[...] = xn

    gspec = _fixed((1, D_MODEL))
    res = pl.pallas_call(
        body, name=name, grid=(s // ts,),
        in_specs=[_rows(ts, D_MODEL), _rows(ts, D_MODEL), gspec] + [gspec] * with_next,
        out_specs=[_rows(ts, D_MODEL)] * (1 + with_next),
        out_shape=[_sds((s, D_MODEL), F32)] + [_sds((s, D_MODEL), MXU_DTYPE)] * with_next,
        compiler_params=_params(("parallel",)),
    )(x, y, g_post, *([g_next] if with_next else []))
    return (res[0], res[1]) if with_next else (res[0], None)


def rms_bwd(x, g, dy, residual, *, out_dtype, name, after=()):
    s = x.shape[0]
    ts = min(s, 256)
    nt = s // ts
    with_res = residual is not None

    def body(x_ref, g_ref, dy_ref, *rest):
        dx_ref, dg_ref, acc_ref = rest[-3:]
        dx, dg_rows = _rms_bwd(x_ref[...].astype(F32), g_ref[...], dy_ref[...].astype(F32))
        if with_res:
            dx = dx + rest[0][...]
        dx_ref[...] = dx.astype(dx_ref.dtype)
        i = pl.program_id(0)

        @pl.when(i == 0)
        def _():
            acc_ref[...] = jnp.zeros_like(acc_ref)

        acc_ref[...] += _fold8(dg_rows)

        @pl.when(i == nt - 1)
        def _():
            dg_ref[...] = acc_ref[...].sum(axis=0, keepdims=True)

    return pl.pallas_call(
        body, name=name, grid=(nt,),
        in_specs=[_rows(ts, D_MODEL), _fixed((1, D_MODEL)), _rows(ts, D_MODEL)] + [_rows(ts, D_MODEL)] * with_res
        + [ANY] * len(after),
        out_specs=[_rows(ts, D_MODEL), _fixed((1, D_MODEL))],
        out_shape=[_sds((s, D_MODEL), out_dtype), _sds((1, D_MODEL), F32)],
        scratch_shapes=[pltpu.VMEM((SUBLANE, D_MODEL), F32)], compiler_params=_params(("arbitrary",)),
    )(x, g, dy, *([residual] if with_res else []), *after)


def loss_and_grad(y, target, *, name):
    s = y.shape[0]
    ts = min(s, 512)
    nt = s // ts

    def body(y_ref, t_ref, dy_ref, part_ref):
        e = y_ref[...] - t_ref[...]
        dy_ref[...] = e * (1.0 / D_MODEL)
        sq = _fold8(e * e)
        lanes = sq[:, 0:LANE]
        for j in range(1, D_MODEL // LANE):
            lanes = lanes + sq[:, j * LANE:(j + 1) * LANE]
        i = pl.program_id(0)

        @pl.when(i == 0)
        def _():
            part_ref[...] = jnp.zeros_like(part_ref)

        part_ref[...] += lanes * (0.5 / D_MODEL)

    return pl.pallas_call(
        body, name=name, grid=(nt,), in_specs=[_rows(ts, D_MODEL), _rows(ts, D_MODEL)],
        out_specs=[_rows(ts, D_MODEL), _fixed((SUBLANE, LANE))],
        out_shape=[_sds((s, D_MODEL), F32), _sds((SUBLANE, LANE), F32)], compiler_params=_params(("arbitrary",)),
    )(y, target)


CONV_TC = LANE


def _conv_taps(u, s):
    row = lax.broadcasted_iota(jnp.int32, u.shape, 0)
    prev = jnp.where(row == 0, 0.0, pltpu.roll(u, 1, 0))
    nxt = jnp.where(row == s - 1, 0.0, pltpu.roll(u, s - 1, 0))
    return prev, nxt


def _zcol(s, off):
    return pl.BlockSpec((s, CONV_TC), lambda j: (0, off // CONV_TC + j))


def conv_fwd(z, w3, *, name):
    s = z.shape[0]

    def body(cb_ref, cc_ref, ci_ref, w_ref, t_ref):
        u = cc_ref[...].astype(F32) * ci_ref[...].astype(F32)
        prev, nxt = _conv_taps(u, s)
        w = w_ref[...]
        conv = w[0:1] * prev + w[1:2] * u + w[2:3] * nxt
        t_ref[...] = (cb_ref[...].astype(F32) * conv).astype(t_ref.dtype)

    return pl.pallas_call(
        body, name=name, grid=(CONV_WIDTH // CONV_TC,),
        in_specs=[_zcol(s, OFF_CB), _zcol(s, OFF_CC), _zcol(s, OFF_CI), pl.BlockSpec((3, CONV_TC), lambda j: (0, j))],
        out_specs=pl.BlockSpec((s, CONV_TC), lambda j: (0, j)), out_shape=_sds((s, CONV_WIDTH), MXU_DTYPE),
        compiler_params=_params(("parallel",)),
    )(z, z, z, w3)


def conv_bwd(dt, z, w3, *, name, after=()):
    s = z.shape[0]

    def body(dt_ref, cb_ref, cc_ref, ci_ref, w_ref, *rest):
        dcb_ref, dcc_ref, dci_ref, dw_ref = rest[-4:]
        cc, ci = cc_ref[...].astype(F32), ci_ref[...].astype(F32)
        u = cc * ci
        prev, nxt = _conv_taps(u, s)
        w = w_ref[...]
        dtv = dt_ref[...].astype(F32)
        dcb_ref[...] = (dtv * (w[0:1] * prev + w[1:2] * u + w[2:3] * nxt)).astype(dcb_ref.dtype)
        dconv = dtv * cb_ref[...].astype(F32)
        dprev, dnxt = _conv_taps(dconv, s)
        du = w[0:1] * dnxt + w[1:2] * dconv + w[2:3] * dprev
        dcc_ref[...] = (du * ci).astype(dcc_ref.dtype)
        dci_ref[...] = (du * cc).astype(dci_ref.dtype)
        dw_ref[0:1, :] = jnp.sum(dconv * prev, axis=0, keepdims=True)
        dw_ref[1:2, :] = jnp.sum(dconv * u, axis=0, keepdims=True)
        dw_ref[2:3, :] = jnp.sum(dconv * nxt, axis=0, keepdims=True)

    col = pl.BlockSpec((s, CONV_TC), lambda j: (0, j))
    wspec = pl.BlockSpec((3, CONV_TC), lambda j: (0, j))
    return pl.pallas_call(
        body, name=name, grid=(CONV_WIDTH // CONV_TC,),
        in_specs=[col, _zcol(s, OFF_CB), _zcol(s, OFF_CC), _zcol(s, OFF_CI), wspec] + [ANY] * len(after),
        out_specs=[col, col, col, wspec],
        out_shape=[_sds((s, CONV_WIDTH), MXU_DTYPE)] * 3 + [_sds((3, CONV_WIDTH), F32)],
        compiler_params=_params(("parallel",)),
    )(dt, z, z, z, w3, *after)


def rope_tables(s):
    n_freq = HEAD_DIM // 4
    t = jnp.arange(s, dtype=jnp.int32)
    inv_freq = ROPE_THETA ** (-jnp.arange(0, HEAD_DIM // 2, 2, dtype=F32) / (HEAD_DIM // 2))
    ang_r = (t // GRID_W).astype(F32)[:, None] * inv_freq
    ang_c = (t % GRID_W).astype(F32)[:, None] * inv_freq
    cos_t = jnp.concatenate([jnp.cos(ang_r)] * 2 + [jnp.cos(ang_c)] * 2, axis=1)
    sin_t = jnp.concatenate([-jnp.sin(ang_r), jnp.sin(ang_r), -jnp.sin(ang_c), jnp.sin(ang_c)], axis=1)
    assert cos_t.shape == (s, 4 * n_freq)
    return cos_t, sin_t


def _swap_halves(v):
    lane = lax.broadcasted_iota(jnp.int32, v.shape, 1)
    return jnp.where(lane % 64 < 32, pltpu.roll(v, HEAD_DIM - 32, 1), pltpu.roll(v, 32, 1))


def qk_prep_fwd(z, qn, kn, cos_t, sin_t, *, name):
    s = z.shape[0]
    ts = min(s, 256)

    def body(q_ref, k_ref, v_ref, qn_ref, kn_ref, c_ref, s_ref, qo_ref, ko_ref, vo_ref):
        cs, sn = c_ref[...], s_ref[...]

        def head(x, g, scale):
            n = x * _rstd(x) * g
            return (n * cs + _swap_halves(n) * sn) * scale

        for h in range(N_Q_HEADS):
            sl = slice(h * HEAD_DIM, (h + 1) * HEAD_DIM)
            qo_ref[:, sl] = head(q_ref[:, sl].astype(F32), qn_ref[...], ATTN_SCALE).astype(qo_ref.dtype)
        for h in range(N_KV_HEADS):
            sl = slice(h * HEAD_DIM, (h + 1) * HEAD_DIM)
            ko_ref[:, sl] = head(k_ref[:, sl].astype(F32), kn_ref[...], 1.0).astype(ko_ref.dtype)
        vo_ref[...] = v_ref[...].astype(vo_ref.dtype)

    tab = _rows(ts, HEAD_DIM)
    gsp = _fixed((1, HEAD_DIM))
    return pl.pallas_call(
        body, name=name, grid=(s // ts,),
        in_specs=[_rows(ts, ATTN_WIDTH, OFF_Q // ATTN_WIDTH), _rows(ts, KV_WIDTH, OFF_K // KV_WIDTH),
                  _rows(ts, KV_WIDTH, OFF_V // KV_WIDTH), gsp, gsp, tab, tab],
        out_specs=[_rows(ts, ATTN_WIDTH), _rows(ts, KV_WIDTH), _rows(ts, KV_WIDTH)],
        out_shape=[_sds((s, ATTN_WIDTH), MXU_DTYPE), _sds((s, KV_WIDTH), MXU_DTYPE), _sds((s, KV_WIDTH), MXU_DTYPE)],
        compiler_params=_params(("parallel",)),
    )(z, z, z, qn, kn, cos_t, sin_t)


def qk_prep_bwd(dqp, dkp, z, qn, kn, cos_t, sin_t, *, name):
    s = z.shape[0]
    ts = min(s, 256)
    nt = s // ts

    def body(dq_ref, dk_ref, q_ref, k_ref, qn_ref, kn_ref, c_ref, s_ref, dqo_ref, dko_ref, dqn_ref, dkn_ref,
             qacc_ref, kacc_ref):
        cs, sn = c_ref[...], s_ref[...]
        i = pl.program_id(0)

        @pl.when(i == 0)
        def _():
            qacc_ref[...] = jnp.zeros_like(qacc_ref)
            kacc_ref[...] = jnp.zeros_like(kacc_ref)

        def head(x, g, dout, scale):
            d = dout.astype(F32) * scale
            dn = d * cs + _swap_halves(d * sn)
            return _rms_bwd(x, g, dn)

        qacc = jnp.zeros((SUBLANE, HEAD_DIM), F32)
        for h in range(N_Q_HEADS):
            sl = slice(h * HEAD_DIM, (h + 1) * HEAD_DIM)
            dx, dg_rows = head(q_ref[:, sl].astype(F32), qn_ref[...], dq_ref[:, sl], ATTN_SCALE)
            dqo_ref[:, sl] = dx.astype(dqo_ref.dtype)
            qacc = qacc + _fold8(dg_rows)
        kacc = jnp.zeros((SUBLANE, HEAD_DIM), F32)
        for h in range(N_KV_HEADS):
            sl = slice(h * HEAD_DIM, (h + 1) * HEAD_DIM)
            dx, dg_rows = head(k_ref[:, sl].astype(F32), kn_ref[...], dk_ref[:, sl], 1.0)
            dko_ref[:, sl] = dx.astype(dko_ref.dtype)
            kacc = kacc + _fold8(dg_rows)
        qacc_ref[...] += qacc
        kacc_ref[...] += kacc

        @pl.when(i == nt - 1)
        def _():
            dqn_ref[...] = qacc_ref[...].sum(axis=0, keepdims=True)
            dkn_ref[...] = kacc_ref[...].sum(axis=0, keepdims=True)

    tab = _rows(ts, HEAD_DIM)
    gsp = _fixed((1, HEAD_DIM))
    return pl.pallas_call(
        body, name=name, grid=(nt,),
        in_specs=[_rows(ts, ATTN_WIDTH), _rows(ts, KV_WIDTH), _rows(ts, ATTN_WIDTH, OFF_Q // ATTN_WIDTH),
                  _rows(ts, KV_WIDTH, OFF_K // KV_WIDTH), gsp, gsp, tab, tab],
        out_specs=[_rows(ts, ATTN_WIDTH), _rows(ts, KV_WIDTH), gsp, gsp],
        out_shape=[_sds((s, ATTN_WIDTH), MXU_DTYPE), _sds((s, KV_WIDTH), MXU_DTYPE), _sds((1, HEAD_DIM), F32),
                   _sds((1, HEAD_DIM), F32)],
        scratch_shapes=[pltpu.VMEM((SUBLANE, HEAD_DIM), F32)] * 2, compiler_params=_params(("arbitrary",)),
    )(dqp, dkp, z, z, qn, kn, cos_t, sin_t)


_NT = (((1,), (1,)), ((), ()))
_GW = GROUP * HEAD_DIM


def _dot(a, b, dims=(((1,), (0,)), ((), ()))):
    return lax.dot_general(a, b, dims, preferred_element_type=F32)


def attn_fwd(qp, kp, vb, *, name, after=()):
    s = qp.shape[0]
    tq = min(s, 1024)
    rows = min(tq, 128)

    def body(q_ref, k_ref, v_ref, *rest):
        o_ref, lse_ref, v_ones = rest[-3:]

        @pl.when(pl.program_id(1) == 0)
        def _():
            v_ones[:, :HEAD_DIM] = v_ref[...]
            v_ones[:, HEAD_DIM:] = jnp.ones((s, HEAD_DIM), v_ones.dtype)

        k = k_ref[...]
        for g in range(GROUP):
            sl = slice(g * HEAD_DIM, (g + 1) * HEAD_DIM)
            for r0 in range(0, tq, rows):
                rs = slice(r0, r0 + rows)
                sc = _dot(q_ref[rs, sl], k, _NT)
                mx = jnp.max(sc, axis=-1, keepdims=True)
                p = jnp.exp(sc - mx).astype(v_ones.dtype)
                o_den = _dot(p, v_ones[...])
                den = o_den[:, HEAD_DIM:HEAD_DIM + 1]
                o_ref[rs, sl] = (o_den[:, :HEAD_DIM] / den).astype(o_ref.dtype)
                lse_ref[rs, g:g + 1] = mx + jnp.log(den)

    return pl.pallas_call(
        body, name=name, grid=(N_KV_HEADS, s // tq),
        in_specs=[pl.BlockSpec((tq, _GW), lambda j, i: (i, j)), pl.BlockSpec((s, HEAD_DIM), lambda j, i: (0, j)),
                  pl.BlockSpec((s, HEAD_DIM), lambda j, i: (0, j))] + [ANY] * len(after),
        out_specs=[pl.BlockSpec((tq, _GW), lambda j, i: (i, j)), pl.BlockSpec((None, tq, GROUP), lambda j, i: (j, i, 0))],
        out_shape=[_sds((s, ATTN_WIDTH), MXU_DTYPE), _sds((N_KV_HEADS, s, GROUP), F32)],
        scratch_shapes=[pltpu.VMEM((s, 2 * HEAD_DIM), MXU_DTYPE)], compiler_params=_params(("parallel", "arbitrary")),
    )(qp, kp, vb, *after)


ATTN_BWD_TQ = 256


def attn_bwd(qp, kp, vb, do, o, lse, *, name):
    s = qp.shape[0]
    tq = min(s, ATTN_BWD_TQ)
    nq = s // tq
    over_rows = (((0,), (0,)), ((), ()))

    def body(q_ref, k_ref, v_ref, do_ref, o_ref, lse_ref, dq_ref, dk_ref, dv_ref, p_all, ds_all, q_all, do_all, dk_acc,
             dv_acc):
        i = pl.program_id(1)

        @pl.when(i == 0)
        def _():
            dk_acc[...] = jnp.zeros_like(dk_acc)
            dv_acc[...] = jnp.zeros_like(dv_acc)

        k, v = k_ref[...], v_ref[...]
        for g in range(GROUP):
            sl = slice(g * HEAD_DIM, (g + 1) * HEAD_DIM)
            rows = slice(g * tq, (g + 1) * tq)
            qg, dog = q_ref[:, sl], do_ref[:, sl]
            dd = jnp.sum(dog.astype(F32) * o_ref[:, sl].astype(F32), axis=-1, keepdims=True)
            p = jnp.exp(_dot(qg, k, _NT) - lse_ref[:, g:g + 1])
            ds = (p * (_dot(dog, v, _NT) - dd)).astype(k.dtype)
            dq_ref[:, sl] = _dot(ds, k).astype(dq_ref.dtype)
            p_all[rows, :] = p.astype(p_all.dtype)
            ds_all[rows, :] = ds
            q_all[rows, :] = qg
            do_all[rows, :] = dog
        dv_acc[...] += _dot(p_all[...], do_all[...], over_rows)
        dk_acc[...] += _dot(ds_all[...], q_all[...], over_rows)

        @pl.when(i == nq - 1)
        def _():
            dk_ref[...] = dk_acc[...].astype(dk_ref.dtype)
            dv_ref[...] = dv_acc[...].astype(dv_ref.dtype)

    qspec = pl.BlockSpec((tq, _GW), lambda j, i: (i, j))
    kspec = pl.BlockSpec((s, HEAD_DIM), lambda j, i: (0, j))
    lspec = pl.BlockSpec((None, tq, GROUP), lambda j, i: (j, i, 0))
    return pl.pallas_call(
        body, name=name, grid=(N_KV_HEADS, nq), in_specs=[qspec, kspec, kspec, qspec, qspec, lspec],
        out_specs=[qspec, kspec, kspec],
        out_shape=[_sds((s, ATTN_WIDTH), ACT_DTYPE), _sds((s, KV_WIDTH), ACT_DTYPE), _sds((s, KV_WIDTH), MXU_DTYPE)],
        scratch_shapes=[pltpu.VMEM((GROUP * tq, s), MXU_DTYPE), pltpu.VMEM((GROUP * tq, s), MXU_DTYPE),
                        pltpu.VMEM((GROUP * tq, HEAD_DIM), MXU_DTYPE), pltpu.VMEM((GROUP * tq, HEAD_DIM), MXU_DTYPE),
                        pltpu.VMEM((s, HEAD_DIM), F32), pltpu.VMEM((s, HEAD_DIM), F32)],
        compiler_params=_params(("parallel", "arbitrary")),
    )(qp, kp, vb, do, o, lse)


GATE_TW = 1024


def gate_fwd(ya, yb, z, bias, *, name):
    s = z.shape[0]
    ts = min(s, 512)

    def body(ya_ref, yb_ref, ga_ref, gb_ref, ba_ref, bb_ref, o_ref):
        sa = jax.nn.sigmoid(ga_ref[...].astype(F32) + ba_ref[...])
        sb = jax.nn.sigmoid(gb_ref[...].astype(F32) + bb_ref[...])
        o_ref[...] = (sa * ya_ref[...].astype(F32) + sb * yb_ref[...].astype(F32)).astype(o_ref.dtype)

    tile = pl.BlockSpec((ts, GATE_TW), lambda i, j: (i, j))
    ga = pl.BlockSpec((ts, GATE_TW), lambda i, j: (i, OFF_GA // GATE_TW + j))
    gb = pl.BlockSpec((ts, GATE_TW), lambda i, j: (i, OFF_GB // GATE_TW + j))
    ba = pl.BlockSpec((1, GATE_TW), lambda i, j: (0, j))
    bb = pl.BlockSpec((1, GATE_TW), lambda i, j: (0, D_MODEL // GATE_TW + j))
    return pl.pallas_call(
        body, name=name, grid=(s // ts, D_MODEL // GATE_TW), in_specs=[tile, tile, ga, gb, ba, bb], out_specs=tile,
        out_shape=_sds((s, D_MODEL), MXU_DTYPE), compiler_params=_params(("parallel", "parallel")),
    )(ya, yb, z, z, bias, bias)


def gate_bwd(dmi, ya, yb, z, bias, *, name):
    s = z.shape[0]
    ts = min(s, 512)
    nt = s // ts

    def body(d_ref, ya_ref, yb_ref, ga_ref, gb_ref, ba_ref, bb_ref, dya_ref, dyb_ref, dga_ref, dgb_ref, dba_ref,
             dbb_ref, acc_a, acc_b):
        d = d_ref[...].astype(F32)
        sa = jax.nn.sigmoid(ga_ref[...].astype(F32) + ba_ref[...])
        sb = jax.nn.sigmoid(gb_ref[...].astype(F32) + bb_ref[...])
        dya_ref[...] = (d * sa).astype(dya_ref.dtype)
        dyb_ref[...] = (d * sb).astype(dyb_ref.dtype)
        dga = d * ya_ref[...].astype(F32) * sa * (1.0 - sa)
        dgb = d * yb_ref[...].astype(F32) * sb * (1.0 - sb)
        dga_ref[...] = dga.astype(dga_ref.dtype)
        dgb_ref[...] = dgb.astype(dgb_ref.dtype)
        i = pl.program_id(1)

        @pl.when(i == 0)
        def _():
            acc_a[...] = jnp.zeros_like(acc_a)
            acc_b[...] = jnp.zeros_like(acc_b)

        acc_a[...] += _fold8(dga)
        acc_b[...] += _fold8(dgb)

        @pl.when(i == nt - 1)
        def _():
            dba_ref[...] = acc_a[...].sum(axis=0, keepdims=True)
            dbb_ref[...] = acc_b[...].sum(axis=0, keepdims=True)

    tile = pl.BlockSpec((ts, GATE_TW), lambda j, i: (i, j))
    ga = pl.BlockSpec((ts, GATE_TW), lambda j, i: (i, OFF_GA // GATE_TW + j))
    gb = pl.BlockSpec((ts, GATE_TW), lambda j, i: (i, OFF_GB // GATE_TW + j))
    ba = pl.BlockSpec((1, GATE_TW), lambda j, i: (0, j))
    bb = pl.BlockSpec((1, GATE_TW), lambda j, i: (0, D_MODEL // GATE_TW + j))
    acc = pltpu.VMEM((SUBLANE, GATE_TW), F32)
    return pl.pallas_call(
        body, name=name, grid=(D_MODEL // GATE_TW, nt), in_specs=[tile, tile, tile, ga, gb, ba, bb],
        out_specs=[tile, tile, tile, tile, ba, ba],
        out_shape=[_sds((s, D_MODEL), MXU_DTYPE)] * 4 + [_sds((1, D_MODEL), F32)] * 2,
        scratch_shapes=[acc, acc], compiler_params=_params(("parallel", "arbitrary")),
    )(dmi, ya, yb, z, z, bias, bias)


HBM = pl.BlockSpec(memory_space=pltpu.HBM)
SEM = pl.BlockSpec(memory_space=pltpu.SEMAPHORE)
EFFECT = pltpu.SideEffectType.DATAFLOW_SIDE_EFFECTING
TOKEN = (SUBLANE, LANE)


def _place():
    return lax.axis_index("x"), lax.axis_index("y"), lax.axis_index("c")


def _window(ref, shard_shape, axis, d):
    r, c = shard_shape
    lead = (slice(None),) * (len(ref.shape) - 2)
    if axis == 0:
        return ref.at[lead + (pl.ds(pl.multiple_of(d * r, SUBLANE), r), slice(None))]
    return ref.at[lead + (slice(None), pl.ds(pl.multiple_of(d * c, LANE), c))]


def _hbm(a):
    return pltpu.with_memory_space_constraint(a, pltpu.HBM)


def _remote(src, dst, send_sems, recv_sems, i, to):
    return pltpu.make_async_remote_copy(src_ref=src, dst_ref=dst, send_sem=send_sems.at[i], recv_sem=recv_sems.at[i],
                                        device_id=to, device_id_type=MESH)


def cast_into_full(w, layer, axis, me, dtype, *, name):
    _, r, c = w.shape
    tr = min(r, 256)
    nr = r // tr
    in_spec = pl.BlockSpec((None, tr, c), lambda i, me_ref: (layer, i, 0))
    if axis == 0:
        out_spec = pl.BlockSpec((tr, c), lambda i, me_ref: (me_ref[0] * nr + i, 0))
        shape = (N_DEV * r, c)
    else:
        out_spec = pl.BlockSpec((tr, c), lambda i, me_ref: (i, me_ref[0]))
        shape = (r, N_DEV * c)

    def body(me_ref, w_ref, o_ref):
        o_ref[...] = w_ref[...].astype(o_ref.dtype)

    return pl.pallas_call(
        body, name=name,
        grid_spec=pltpu.PrefetchScalarGridSpec(num_scalar_prefetch=1, grid=(nr,), in_specs=[in_spec], out_specs=out_spec),
        out_shape=_sds(shape, dtype), compiler_params=_params(("parallel",)),
    )(me, w)


class _GatherPlan:
    def __init__(self, fulls, shard_shapes, axes):
        x, y, c = _place()
        self.n = len(fulls)
        self.me, self.sibling = (x, y, c), (x, y, 1 - c)
        self.chips = [(1 - x, y), (x, 1 - y), (1 - x, 1 - y)]
        self.win = lambda a, p: _window(fulls[a], shard_shapes[a], axes[a], 4 * p[0] + 2 * p[1] + p[2])

    def first(self, send_sems, recv_sems):
        out = []
        for a in range(self.n):
            mine = self.win(a, self.me)
            out.append(_remote(mine, mine, send_sems, recv_sems, 4 * a, self.sibling))
            out += [_remote(mine, mine, send_sems, recv_sems, 4 * a + 1 + j, (*chip, self.me[2]))
                    for j, chip in enumerate(self.chips)]
        return out

    def first_arrivals(self, send_sems, recv_sems):
        c = self.me[2]
        out = []
        for a in range(self.n):
            blocks = [self.sibling] + [(*chip, c) for chip in self.chips]
            out += [_remote(self.win(a, b), self.win(a, b), send_sems, recv_sems, 4 * a + k, self.me)
                    for k, b in enumerate(blocks)]
        return out

    def passed(self, send_sems, recv_sems):
        c = self.me[2]
        return [_remote(self.win(a, (*chip, c)), self.win(a, (*chip, c)), send_sems, recv_sems, 3 * a + j, self.sibling)
                for a in range(self.n) for j, chip in enumerate(self.chips)]

    def passed_arrivals(self, send_sems, recv_sems):
        c = self.me[2]
        return [_remote(self.win(a, (*chip, 1 - c)), self.win(a, (*chip, 1 - c)), send_sems, recv_sems, 3 * a + j, self.me)
                for a in range(self.n) for j, chip in enumerate(self.chips)]


def gather_start(fulls, after, shard_shapes, axes, *, name):
    n = len(fulls)

    def body(*refs):
        ins, send_sems, recv_sems, token = refs[:n], refs[n + 1], refs[n + 2], refs[-1]
        for cp in _GatherPlan(ins, shard_shapes, axes).first(send_sems, recv_sems):
            cp.start()
        token[...] = jnp.zeros_like(token)

    res = pl.pallas_call(
        body, name=name, in_specs=[HBM] * n + [ANY],
        out_shape=(pltpu.SemaphoreType.DMA((4 * n,)), pltpu.SemaphoreType.DMA((4 * n,)),
                   *[pltpu.HBM(f.shape, f.dtype) for f in fulls], _sds(TOKEN, F32)),
        out_specs=(SEM, SEM, *[HBM] * n, pl.BlockSpec(memory_space=pltpu.VMEM)),
        input_output_aliases={a: 2 + a for a in range(n)}, compiler_params=pltpu.CompilerParams(has_side_effects=EFFECT),
    )(*[_hbm(f) for f in fulls], after)
    return res[0], res[1], list(res[2:2 + n]), res[-1]


def gather_pass(send_sems, recv_sems, fulls, after, shard_shapes, axes, *, name):
    n = len(fulls)

    def body(*refs):
        ins, s1, r1 = refs[:n], refs[n], refs[n + 1]
        s2, r2, token = refs[n + 3], refs[n + 4], refs[-1]
        plan = _GatherPlan(ins, shard_shapes, axes)
        for cp in plan.first_arrivals(s1, r1):
            cp.wait_recv()
        for cp in plan.first(s1, r1):
            cp.wait_send()
        for cp in plan.passed(s2, r2):
            cp.start()
        token[...] = jnp.zeros_like(token)

    res = pl.pallas_call(
        body, name=name, in_specs=[HBM] * n + [SEM, SEM, ANY],
        out_shape=(pltpu.SemaphoreType.DMA((3 * n,)), pltpu.SemaphoreType.DMA((3 * n,)),
                   *[pltpu.HBM(f.shape, f.dtype) for f in fulls], _sds(TOKEN, F32)),
        out_specs=(SEM, SEM, *[HBM] * n, pl.BlockSpec(memory_space=pltpu.VMEM)),
        input_output_aliases={a: 2 + a for a in range(n)}, compiler_params=pltpu.CompilerParams(has_side_effects=EFFECT),
    )(*fulls, send_sems, recv_sems, after)
    return res[0], res[1], list(res[2:2 + n]), res[-1]


def gather_wait(send_sems, recv_sems, fulls, after, shard_shapes, axes, *, name):
    n = len(fulls)

    def body(*refs):
        ins, s2, r2 = refs[:n], refs[n], refs[n + 1]
        plan = _GatherPlan(ins, shard_shapes, axes)
        for cp in plan.passed_arrivals(s2, r2):
            cp.wait_recv()
        for cp in plan.passed(s2, r2):
            cp.wait_send()

    return list(pl.pallas_call(
        body, name=name, in_specs=[HBM] * n + [SEM, SEM, ANY], out_shape=tuple(pltpu.HBM(f.shape, f.dtype) for f in fulls),
        out_specs=tuple([HBM] * n), input_output_aliases={a: a for a in range(n)},
        compiler_params=pltpu.CompilerParams(has_side_effects=EFFECT),
    )(*fulls, send_sems, recv_sems, after))


def _pair_copies(grads, lands, shard_shapes, axes, send_sems, recv_sems):
    x, y, c = _place()
    return [_remote(_window(grads[a], shard_shapes[a], axes[a], 2 * q + (1 - c)), lands[a].at[q], send_sems, recv_sems,
                    N_CHIP * a + q, (x, y, 1 - c))
            for a in range(len(grads)) for q in range(N_CHIP)]


def _chip_sends(sums, lands, send_sems, recv_sems):
    x, y, c = _place()
    return [_remote(sums[a].at[2 * px + py], lands[a].at[2 * x + y], send_sems, recv_sems, 3 * a + j, (px, py, c))
            for a in range(len(sums)) for j, (px, py) in enumerate([(1 - x, y), (x, 1 - y), (1 - x, 1 - y)])]


def _chip_arrivals(sums, lands, send_sems, recv_sems):
    x, y, c = _place()
    return [_remote(sums[a].at[2 * x + y], lands[a].at[2 * px + py], send_sems, recv_sems, 3 * a + j, (x, y, c))
            for a in range(len(sums)) for j, (px, py) in enumerate([(1 - x, y), (x, 1 - y), (1 - x, 1 - y)])]


def exchange_start(srcs, after, land_shapes, make_sends, per_array, *, name):
    n = len(srcs)
    lands = [lax.empty(s, a.dtype) for s, a in zip(land_shapes, srcs)]

    def body(*refs):
        ins, zones = refs[:n], refs[n:2 * n]
        send_sems, recv_sems, token = refs[2 * n + 1], refs[2 * n + 2], refs[-1]
        for cp in make_sends(ins, zones, send_sems, recv_sems):
            cp.start()
        token[...] = jnp.zeros_like(token)

    res = pl.pallas_call(
        body, name=name, in_specs=[HBM] * (2 * n) + [ANY],
        out_shape=(pltpu.SemaphoreType.DMA((per_array * n,)), pltpu.SemaphoreType.DMA((per_array * n,)),
                   *[pltpu.HBM(a.shape, a.dtype) for a in srcs], *[pltpu.HBM(a.shape, a.dtype) for a in lands],
                   _sds(TOKEN, F32)),
        out_specs=(SEM, SEM, *[HBM] * (2 * n), pl.BlockSpec(memory_space=pltpu.VMEM)),
        input_output_aliases={a: 2 + a for a in range(2 * n)}, compiler_params=pltpu.CompilerParams(has_side_effects=EFFECT),
    )(*[_hbm(a) for a in srcs], *[_hbm(a) for a in lands], after)
    return res[0], res[1], list(res[2:2 + n]), list(res[2 + n:2 + 2 * n]), res[-1]


def exchange_wait(send_sems, recv_sems, srcs, lands, after, make_waits, *, name):
    n = len(srcs)

    def body(*refs):
        ins, zones, s, r = refs[:n], refs[n:2 * n], refs[2 * n], refs[2 * n + 1]
        sends, arrivals = make_waits(ins, zones, s, r)
        for cp in arrivals:
            cp.wait_recv()
        for cp in sends:
            cp.wait_send()

    res = pl.pallas_call(
        body, name=name, in_specs=[HBM] * (2 * n) + [SEM, SEM, ANY],
        out_shape=tuple(pltpu.HBM(a.shape, a.dtype) for a in (*srcs, *lands)), out_specs=tuple([HBM] * (2 * n)),
        input_output_aliases={a: a for a in range(2 * n)}, compiler_params=pltpu.CompilerParams(has_side_effects=EFFECT),
    )(*srcs, *lands, send_sems, recv_sems, after)
    return list(res[:n]), list(res[n:])


def pair_sum(grad, recv, shard_shape, axis, core, *, name):
    r, c = shard_shape
    tr = min(r, 256)
    nr = r // tr
    if axis == 0:
        gspec = pl.BlockSpec((tr, c), lambda q, i, cref: ((2 * q + cref[0]) * nr + i, 0))
    else:
        gspec = pl.BlockSpec((tr, c), lambda q, i, cref: (i, 2 * q + cref[0]))
    rspec = pl.BlockSpec((None, tr, c), lambda q, i, cref: (q, i, 0))

    def body(c_ref, g_ref, r_ref, o_ref):
        o_ref[...] = (g_ref[...].astype(F32) + r_ref[...].astype(F32)).astype(o_ref.dtype)

    return pl.pallas_call(
        body, name=name,
        grid_spec=pltpu.PrefetchScalarGridSpec(num_scalar_prefetch=1, grid=(N_CHIP, nr), in_specs=[gspec, rspec],
                                               out_specs=rspec),
        out_shape=_sds((N_CHIP, r, c), recv.dtype), compiler_params=_params(("parallel", "parallel")),
    )(core, grad, recv)


def all_reduce_small(part, *, name):
    r, c = part.shape

    def body(p_ref, o_ref, gath_ref, send_sems, recv_sems):
        x, y, cc = _place()
        me = 4 * x + 2 * y + cc
        gath_ref[me] = p_ref[...]
        peers = [(1 - x if k & 4 else x, 1 - y if k & 2 else y, 1 - cc if k & 1 else cc) for k in range(1, N_DEV)]
        copies = [_remote(p_ref, gath_ref.at[me], send_sems, recv_sems, i, peer) for i, peer in enumerate(peers)]
        for cp in copies:
            cp.start()
        for i, (px, py, pc) in enumerate(peers):
            _remote(p_ref, gath_ref.at[4 * px + 2 * py + pc], send_sems, recv_sems, i, (x, y, cc)).wait_recv()
        for cp in copies:
            cp.wait_send()
        acc = gath_ref[0]
        for d in range(1, N_DEV):
            acc = acc + gath_ref[d]
        o_ref[...] = acc

    vm = pl.BlockSpec(memory_space=pltpu.VMEM)
    return pl.pallas_call(
        body, name=name, in_specs=[vm], out_specs=vm, out_shape=_sds((r, c), F32),
        scratch_shapes=[pltpu.VMEM((N_DEV, r, c), F32), pltpu.SemaphoreType.DMA((N_DEV - 1,)),
                        pltpu.SemaphoreType.DMA((N_DEV - 1,))],
    )(part)


def _adamw(w, g, m, v):
    m = ADAM_B1 * m + (1.0 - ADAM_B1) * g
    v = ADAM_B2 * v + (1.0 - ADAM_B2) * (g * g)
    m_hat = m / (1.0 - ADAM_B1 ** ADAM_STEP)
    v_hat = v / (1.0 - ADAM_B2 ** ADAM_STEP)
    delta = -ADAM_LR * (m_hat / (jnp.sqrt(v_hat) + ADAM_EPS) + ADAM_WD * w)
    return delta, m, v


def reduce_adam(own, landed, chip, w, m, v, layer, outs, *, name):
    _, r, c = w.shape
    tr = min(r, 128)
    first = outs is None

    def body(chip_ref, own_ref, l1_ref, l2_ref, l3_ref, w_ref, m_ref, v_ref, *rest):
        g_out, d_out, m_out, v_out = rest[-4:]
        g = own_ref[...].astype(F32) + l1_ref[...].astype(F32) + l2_ref[...].astype(F32) + l3_ref[...].astype(F32)
        d, mn, vn = _adamw(w_ref[...], g, m_ref[...], v_ref[...])
        g_out[...] = g
        d_out[...] = d
        m_out[...] = mn
        v_out[...] = vn

    spec = pl.BlockSpec((None, tr, c), lambda i, chip_ref: (layer, i, 0))

    def slot(step):
        return pl.BlockSpec((None, tr, c), lambda i, chip_ref: ((chip_ref[0] + step) % N_CHIP, i, 0))

    n_in = 8
    return pl.pallas_call(
        body, name=name,
        grid_spec=pltpu.PrefetchScalarGridSpec(
            num_scalar_prefetch=1, grid=(r // tr,),
            in_specs=[slot(0), slot(1), slot(2), slot(3), spec, spec, spec] + ([] if first else [ANY] * 4),
            out_specs=[spec] * 4),
        out_shape=[_sds(w.shape, F32)] * 4, input_output_aliases={} if first else {n_in + i: i for i in range(4)},
        compiler_params=_params(("parallel",)),
    )(chip, own, landed, landed, landed, w, m, v, *([] if first else outs))


def adam_small(g, w, m, v, *, name):
    def body(g_ref, w_ref, m_ref, v_ref, d_out, m_out, v_out):
        d, mn, vn = _adamw(w_ref[...], g_ref[...], m_ref[...], v_ref[...])
        d_out[...] = d
        m_out[...] = mn
        v_out[...] = vn

    return pl.pallas_call(body, name=name, out_shape=[_sds(w.shape, F32)] * 3)(g, w, m, v)


BIG = ("w_in", "w_out_conv", "w_out_attn", "w_merge", "w_up", "w_down")
GATHERED = BIG + ("conv_w",)
BIG_AXIS = {"w_in": 1, "w_out_conv": 0, "w_out_attn": 0, "w_merge": 0, "w_up": 1, "w_down": 0, "conv_w": 1}
FIRST_GATHER_GROUPS = (("w_in", "conv_w"), ("w_out_conv", "w_out_attn", "w_merge"), ("w_up", "w_down"))
LAST_REDUCE_GROUPS = (("w_down", "w_up"), ("w_merge", "w_out_conv", "w_out_attn"), ("w_in",))


class _Gather:
    def __init__(self, weights, keys, layer, me, after, tag):
        self.keys, self.tag = keys, tag
        self.shapes = [weights[k].shape[1:] for k in keys]
        self.axes = [BIG_AXIS[k] for k in keys]
        own = [cast_into_full(weights[k], layer, ax, me, F32 if k == "conv_w" else MXU_DTYPE, name="cast_" + k)
               for k, ax in zip(keys, self.axes)]
        self.send, self.recv, self.fulls, self.token = gather_start(own, after, self.shapes, self.axes,
                                                                    name="gather_start_" + tag)

    def pass_on(self, after):
        self.send, self.recv, self.fulls, self.token = gather_pass(self.send, self.recv, self.fulls, after, self.shapes,
                                                                   self.axes, name="gather_pass_" + self.tag)
        return self.token

    def wait(self, after):
        fulls = gather_wait(self.send, self.recv, self.fulls, after, self.shapes, self.axes, name="gather_wait_" + self.tag)
        return dict(zip(self.keys, fulls))


class _Reduce:
    def __init__(self, weights, keys, tag):
        self.keys, self.tag = keys, tag
        self.shapes = [weights[k].shape[1:] for k in keys]
        self.axes = [BIG_AXIS[k] for k in keys]
        self.pair_shapes = [(N_CHIP, *shp) for shp in self.shapes]

    def _pair(self, i, z, ss, rs):
        return _pair_copies(i, z, self.shapes, self.axes, ss, rs)

    def begin(self, grads, after):
        self.send, self.recv, self.src, self.land, self.token = exchange_start(
            [grads[k] for k in self.keys], after, self.pair_shapes, self._pair, N_CHIP, name="rs_pair_start_" + self.tag)
        return self.token

    def middle(self, after, core):
        both = lambda i, z, ss, rs: (self._pair(i, z, ss, rs),) * 2
        grads, from_sibling = exchange_wait(self.send, self.recv, self.src, self.land, after, both,
                                            name="rs_pair_wait_" + self.tag)
        sums = [pair_sum(g, rcv, shp, ax, core, name="pair_sum_" + k)
                for k, g, rcv, shp, ax in zip(self.keys, grads, from_sibling, self.shapes, self.axes)]
        self.send, self.recv, self.src, self.land, self.token = exchange_start(
            sums, after, self.pair_shapes, _chip_sends, 3, name="rs_chips_start_" + self.tag)
        return self.token

    def end(self, after):
        both = lambda i, z, ss, rs: (_chip_sends(i, z, ss, rs), _chip_arrivals(i, z, ss, rs))
        sums, landed = exchange_wait(self.send, self.recv, self.src, self.land, after, both,
                                     name="rs_chips_wait_" + self.tag)
        return list(zip(self.keys, sums, landed))
SMALL = (("norm_mix_pre", D_MODEL), ("gate_bias", 2 * D_MODEL), ("norm_mix_post", D_MODEL), ("norm_mlp_pre", D_MODEL),
         ("norm_mlp_post", D_MODEL), ("q_norm", HEAD_DIM), ("k_norm", HEAD_DIM))
SMALL_WIDTH = sum(w for _, w in SMALL)
WEIGHTS = ("norm_mix_pre", "w_in", "gate_bias", "conv_w", "q_norm", "k_norm", "w_out_conv", "w_out_attn", "w_merge",
           "norm_mix_post", "norm_mlp_pre", "w_up", "w_down", "norm_mlp_post")


def kernel(x, norm_mix_pre, w_in, gate_bias, conv_w, q_norm, k_norm, w_out_conv, w_out_attn, w_merge, norm_mix_post, norm_mlp_pre, w_up, w_down, norm_mlp_post, loss_target, m_norm_mix_pre, m_w_in, m_gate_bias, m_conv_w, m_q_norm, m_k_norm, m_w_out_conv, m_w_out_attn, m_w_merge, m_norm_mix_post, m_norm_mlp_pre, m_w_up, m_w_down, m_norm_mlp_post, v_norm_mix_pre, v_w_in, v_gate_bias, v_conv_w, v_q_norm, v_k_norm, v_w_out_conv, v_w_out_attn, v_w_merge, v_norm_mix_post, v_norm_mlp_pre, v_w_up, v_w_down, v_norm_mlp_post):
    w = dict(norm_mix_pre=norm_mix_pre, w_in=w_in, gate_bias=gate_bias, conv_w=conv_w, q_norm=q_norm, k_norm=k_norm,
             w_out_conv=w_out_conv, w_out_attn=w_out_attn, w_merge=w_merge, norm_mix_post=norm_mix_post,
             norm_mlp_pre=norm_mlp_pre, w_up=w_up, w_down=w_down, norm_mlp_post=norm_mlp_post)
    mom = dict(norm_mix_pre=m_norm_mix_pre, w_in=m_w_in, gate_bias=m_gate_bias, conv_w=m_conv_w, q_norm=m_q_norm,
               k_norm=m_k_norm, w_out_conv=m_w_out_conv, w_out_attn=m_w_out_attn, w_merge=m_w_merge,
               norm_mix_post=m_norm_mix_post, norm_mlp_pre=m_norm_mlp_pre, w_up=m_w_up, w_down=m_w_down,
               norm_mlp_post=m_norm_mlp_post)
    var = dict(norm_mix_pre=v_norm_mix_pre, w_in=v_w_in, gate_bias=v_gate_bias, conv_w=v_conv_w, q_norm=v_q_norm,
               k_norm=v_k_norm, w_out_conv=v_w_out_conv, w_out_attn=v_w_out_attn, w_merge=v_w_merge,
               norm_mix_post=v_norm_mix_post, norm_mlp_pre=v_norm_mlp_pre, w_up=v_w_up, w_down=v_w_down,
               norm_mlp_post=v_norm_mlp_post)
    depth = w_in.shape[0]
    s = x.shape[1]
    xs = x.reshape(s, D_MODEL)
    target = loss_target.reshape(s, D_MODEL)
    x_idx, y_idx, c_idx = _place()
    as_operand = lambda i: jnp.reshape(i, (1,)).astype(jnp.int32)
    core, chip, me = as_operand(c_idx), as_operand(2 * x_idx + y_idx), as_operand(4 * x_idx + 2 * y_idx + c_idx)
    cos_t, sin_t = rope_tables(s)

    def vec(name, l):
        return w[name][l].reshape(1, -1)

    saved = []
    h = rms_fwd(xs, vec("norm_mix_pre", 0), name="rms_first")
    first, after = [], h
    for i, keys in enumerate(FIRST_GATHER_GROUPS):
        first.append(_Gather(w, keys, 0, me, after, f"0{'abc'[i]}"))
        after = first[-1].token
    full = first[0].wait(first[0].pass_on(after))
    layers = []
    for l in range(depth):
        layers.append(full)
        nxt = l + 1 < depth
        if nxt:
            coming = _Gather(w, GATHERED, l + 1, me, full["w_in"], str(l + 1))
        z = matmul(h, full["w_in"], mode="nn", m=s, n=IN_WIDTH, k=D_MODEL, out_dtypes=[ACT_DTYPE], name="mm_in",
                   after=[coming.token] if nxt else [])
        t = conv_fwd(z, full["conv_w"], name="conv_fwd")
        qp, kp, vb = qk_prep_fwd(z, vec("q_norm", l), vec("k_norm", l), cos_t, sin_t, name="qk_fwd")
        o, lse = attn_fwd(qp, kp, vb, name="attn_fwd", after=[first[1].pass_on(qp)] if l == 0 else [])
        if l == 0:
            full.update(first[1].wait(o))
        ya = matmul(t, full["w_out_conv"], mode="nn", m=s, n=D_MODEL, k=CONV_WIDTH, out_dtypes=[ACT_DTYPE], name="mm_out_conv",
                    after=[first[2].pass_on(o)] if l == 0 else [])
        yb = matmul(o, full["w_out_attn"], mode="nn", m=s, n=D_MODEL, k=ATTN_WIDTH, out_dtypes=[ACT_DTYPE], name="mm_out_attn")
        mi = gate_fwd(ya, yb, z, vec("gate_bias", l), name="gate_fwd")
        mixed = matmul(mi, full["w_merge"], mode="nn", m=s, n=D_MODEL, k=D_MODEL, out_dtypes=[ACT_DTYPE], name="mm_merge")
        if l == 0:
            full.update(first[2].wait(mixed))
        x_mid, h2 = rms_residual_fwd(xs, mixed, vec("norm_mix_post", l), vec("norm_mlp_pre", l), name="res_mix")
        act, r = matmul(h2, full["w_up"], mode="nn", m=s, n=D_FF, k=D_MODEL, out_dtypes=[MXU_DTYPE, MXU_DTYPE],
                        name="mm_up", epilogue=lambda acc: (acc, jnp.square(jnp.maximum(acc, 0.0))))
        f = matmul(r, full["w_down"], mode="nn", m=s, n=D_MODEL, k=D_FF, out_dtypes=[ACT_DTYPE], name="mm_down",
                   after=[coming.pass_on(r)] if nxt else [])
        g_next = vec("norm_mix_pre", l + 1) if nxt else None
        x_out, h_next = rms_residual_fwd(x_mid, f, vec("norm_mlp_post", l), g_next, name="res_mlp" if nxt else "res_last")
        saved.append(dict(x_in=xs, h=h, z=z, t=t, qp=qp, kp=kp, vb=vb, o=o, lse=lse, ya=ya, yb=yb, mi=mi, mixed=mixed,
                          x_mid=x_mid, h2=h2, act=act, r=r, f=f))
        if nxt:
            full = coming.wait(x_out)
        xs, h = x_out, h_next

    dx, loss_part = loss_and_grad(xs, target, name="loss")
    loss = lax.psum(jnp.sum(loss_part), ("x", "y", "c"))

    small_rows = [None] * depth
    conv_rows = [None] * depth
    out_g, out_d, out_m, out_v = {}, {}, {}, {}
    big_outs = {k: None for k in BIG}
    pending = None
    last = []

    def finish(reduction, layer, after):
        for k, own, landed in reduction.end(after):
            big_outs[k] = reduce_adam(own, landed, chip, w[k], mom[k], var[k], layer, big_outs[k], name="adam_" + k)

    for l in reversed(range(depth)):
        sv, full = saved[l], layers[l]
        grads = {}
        groups = l == 0

        def wgrad(key, lhs, rhs, m, n, after=()):
            grads[key] = matmul(lhs, rhs, mode="tn", m=m, n=n, k=s, tk=s, out_dtypes=[COMM_DTYPE], name="wg_" + key,
                                after=after)

        def begin_group(i, after):
            last.append(_Reduce(w, LAST_REDUCE_GROUPS[i], f"{l}{'abc'[i]}"))
            return [last[i].begin(grads, after)]

        df, dg_mlp_post = rms_bwd(sv["f"], vec("norm_mlp_post", l), dx, None, out_dtype=MXU_DTYPE, name="rmsb_mlp_post",
                                  after=[pending.token] if pending else [])
        da = matmul(df, full["w_down"], mode="nt", m=s, n=D_FF, k=D_MODEL, out_dtypes=[MXU_DTYPE], name="mm_d_down",
                    extra=(sv["act"],), epilogue=lambda acc, a: (acc * (2.0 * jnp.maximum(a.astype(F32), 0.0)),))
        tokens = [pending.middle(da, core)] if pending else []
        wgrad("w_down", sv["r"], df, D_FF, D_MODEL)
        dh2 = matmul(da, full["w_up"], mode="nt", m=s, n=D_MODEL, k=D_FF, out_dtypes=[ACT_DTYPE], name="mm_d_up", after=tokens)
        wgrad("w_up", sv["h2"], da, D_MODEL, D_FF)
        tokens = begin_group(0, dh2) if groups else []
        dx_mid, dg_mlp_pre = rms_bwd(sv["x_mid"], vec("norm_mlp_pre", l), dh2, dx, out_dtype=F32, name="rmsb_mlp_pre",
                                     after=tokens)
        dmixed, dg_mix_post = rms_bwd(sv["mixed"], vec("norm_mix_post", l), dx_mid, None, out_dtype=MXU_DTYPE,
                                      name="rmsb_mix_post")
        dmi = matmul(dmixed, full["w_merge"], mode="nt", m=s, n=D_MODEL, k=D_MODEL, out_dtypes=[ACT_DTYPE], name="mm_d_merge")
        tokens = [last[0].middle(dmi, core)] if groups else []
        wgrad("w_merge", sv["mi"], dmixed, D_MODEL, D_MODEL, after=tokens)
        dya, dyb, dga, dgb, dba, dbb = gate_bwd(dmi, sv["ya"], sv["yb"], sv["z"], vec("gate_bias", l), name="gate_bwd")
        wgrad("w_out_conv", sv["t"], dya, CONV_WIDTH, D_MODEL)
        wgrad("w_out_attn", sv["o"], dyb, ATTN_WIDTH, D_MODEL)
        tokens = begin_group(1, dyb) if groups else []
        dt = matmul(dya, full["w_out_conv"], mode="nt", m=s, n=CONV_WIDTH, k=D_MODEL, out_dtypes=[ACT_DTYPE],
                    name="mm_d_out_conv", after=tokens)
        do = matmul(dyb, full["w_out_attn"], mode="nt", m=s, n=ATTN_WIDTH, k=D_MODEL, out_dtypes=[MXU_DTYPE],
                    name="mm_d_out_attn")
        tokens = [last[1].middle(do, core)] if groups else []
        dcb, dcc, dci, dconv_w = conv_bwd(dt, sv["z"], full["conv_w"], name="conv_bwd", after=tokens)
        dqp, dkp, dv = attn_bwd(sv["qp"], sv["kp"], sv["vb"], do, sv["o"], sv["lse"], name="attn_bwd")
        dq, dk, dqn, dkn = qk_prep_bwd(dqp, dkp, sv["z"], vec("q_norm", l), vec("k_norm", l), cos_t, sin_t, name="qk_bwd")
        dz = jnp.concatenate([dcb, dcc, dci, dq, dk, dv, dga, dgb], axis=1)
        wgrad("w_in", sv["h"], dz, D_MODEL, IN_WIDTH)
        tokens = begin_group(2, dz) if groups else []
        dh = matmul(dz, full["w_in"], mode="nt", m=s, n=D_MODEL, k=IN_WIDTH, tk=IN_WIDTH // 4, out_dtypes=[ACT_DTYPE],
                    name="mm_d_in", after=tokens)
        dx, dg_mix_pre = rms_bwd(sv["x_in"], vec("norm_mix_pre", l), dh, dx_mid, out_dtype=F32, name="rmsb_mix_pre")
        small_rows[l] = jnp.concatenate([dg_mix_pre, dba, dbb, dg_mix_post, dg_mlp_pre, dg_mlp_post, dqn, dkn], axis=1)
        conv_rows[l] = dconv_w.reshape(1, 3 * CONV_WIDTH)
        if pending:
            finish(pending, l + 1, dx)
        pending = None
        if not groups:
            pending = _Reduce(w, BIG, str(l))
            pending.begin(grads, dx)

    grad_x = dx.reshape(1, s, D_MODEL)

    small_part = jnp.concatenate([jnp.concatenate(small_rows, axis=0), jnp.concatenate(conv_rows, axis=0)], axis=1)
    small_sum = all_reduce_small(small_part, name="allreduce_small")
    pack = lambda src: jnp.concatenate([src[k].reshape(depth, wd) for k, wd in SMALL], axis=1)
    g_small = small_sum[:, :SMALL_WIDTH]
    d_small, m_small, v_small = adam_small(g_small, pack(w), pack(mom), pack(var), name="adam_small")
    off = 0
    for k, wd in SMALL:
        for dst, src in ((out_g, g_small), (out_d, d_small), (out_m, m_small), (out_v, v_small)):
            dst[k] = src[:, off:off + wd]
        off += wd
    cshard = CONV_WIDTH // N_DEV
    g_conv = lax.dynamic_slice_in_dim(small_sum[:, SMALL_WIDTH:].reshape(depth, 3, CONV_WIDTH), me[0] * cshard, cshard,
                                      axis=2).reshape(depth, 3 * cshard)
    flat = lambda a: a.reshape(depth, 3 * cshard)
    d_conv, m_conv, v_conv = adam_small(g_conv, flat(conv_w), flat(m_conv_w), flat(v_conv_w), name="adam_conv")
    for dst, src in ((out_g, g_conv), (out_d, d_conv), (out_m, m_conv), (out_v, v_conv)):
        dst["conv_w"] = src.reshape(depth, 3, cshard)

    token = last[2].middle(d_small, core)
    for reduction in last:
        finish(reduction, 0, token)
    for k in BIG:
        out_g[k], out_d[k], out_m[k], out_v[k] = big_outs[k]

    return (loss, grad_x, *[out_g[k] for k in WEIGHTS], *[out_d[k] for k in WEIGHTS], *[out_m[k] for k in WEIGHTS],
            *[out_v[k] for k in WEIGHTS])
```

```python
import math

import jax
import jax.numpy as jnp
from jax import lax
from jax.experimental import pallas as pl
from jax.experimental.pallas import tpu as pltpu

F32 = jnp.float32
MXU_DTYPE = jnp.bfloat16
COMM_DTYPE = jnp.bfloat16
ACT_DTYPE = jnp.bfloat16

D_MODEL = 2048
HEAD_DIM = 128
N_Q_HEADS = 16
N_KV_HEADS = 4
GROUP = N_Q_HEADS // N_KV_HEADS
ATTN_WIDTH = N_Q_HEADS * HEAD_DIM
KV_WIDTH = N_KV_HEADS * HEAD_DIM
CONV_WIDTH = D_MODEL
D_FF = 4 * D_MODEL
GRID_W = 64
ROPE_THETA = 10000.0
RMS_EPS = 1e-6
IN_WIDTH = 3 * CONV_WIDTH + ATTN_WIDTH + 2 * KV_WIDTH + 2 * D_MODEL
OFF_CB, OFF_CC, OFF_CI = 0, CONV_WIDTH, 2 * CONV_WIDTH
OFF_Q = 3 * CONV_WIDTH
OFF_K = OFF_Q + ATTN_WIDTH
OFF_V = OFF_K + KV_WIDTH
OFF_GA = OFF_V + KV_WIDTH
OFF_GB = OFF_GA + D_MODEL
ATTN_SCALE = 1.0 / math.sqrt(HEAD_DIM)

ADAM_LR, ADAM_B1, ADAM_B2, ADAM_EPS, ADAM_WD, ADAM_STEP = 0.001, 0.9, 0.999, 1e-08, 0.01, 10

N_DEV = 8
N_CHIP = 4
LANE = 128
SUBLANE = 8
VMEM_LIMIT = 48 * 1024 * 1024
VMEM_LIMIT_DEEP_K = 56 * 1024 * 1024
MESH = pl.DeviceIdType.MESH
ANY = pl.BlockSpec(memory_space=pl.ANY)


def _sds(shape, dtype):
    return jax.ShapeDtypeStruct(tuple(shape), dtype)


def _params(sem, vmem=VMEM_LIMIT):
    return pltpu.CompilerParams(dimension_semantics=sem, vmem_limit_bytes=vmem)


def _rows(ts, w, col=0):
    return pl.BlockSpec((ts, w), lambda i: (i, col))


def _fixed(shape):
    return pl.BlockSpec(shape, lambda *_: (0,) * len(shape))


def _fold8(v):
    ts, w = v.shape
    return v.reshape(ts // SUBLANE, SUBLANE, w).sum(axis=0)


def matmul(a, b, *, mode, m, n, k, name, out_dtypes, tm=1024, tn=1024, tk=2048, epilogue=None, extra=(), after=(),
           vmem=VMEM_LIMIT):
    tm, tn, tk = min(tm, m), min(tn, n), min(tk, k)
    nm, nn, nk = m // tm, n // tn, k // tk
    assert nm * tm == m and nn * tn == n and nk * tk == k, (name, m, n, k, tm, tn, tk)
    if mode == "tn":
        a_spec = pl.BlockSpec((tk, tm), lambda i, j, kk: (kk, i))
        dims = (((0,), (0,)), ((), ()))
    else:
        a_spec = pl.BlockSpec((tm, tk), lambda i, j, kk: (i, kk))
        dims = (((1,), (1 if mode == "nt" else 0,)), ((), ()))
    if mode == "nt":
        b_spec = pl.BlockSpec((tn, tk), lambda i, j, kk: (j, kk))
    else:
        b_spec = pl.BlockSpec((tk, tn), lambda i, j, kk: (kk, j))
    tile = pl.BlockSpec((tm, tn), lambda i, j, kk: (i, j))
    n_out, n_extra, n_after = len(out_dtypes), len(extra), len(after)
    if epilogue is None:
        epilogue = lambda acc: (acc,)

    def body(a_ref, b_ref, *rest):
        extra_refs = rest[:n_extra]
        outs = rest[n_extra + n_after:][:n_out]
        part = lax.dot_general(a_ref[...].astype(MXU_DTYPE), b_ref[...].astype(MXU_DTYPE), dims,
                               preferred_element_type=F32)

        def finish(acc):
            for o_ref, val in zip(outs, epilogue(acc, *[r[...] for r in extra_refs])):
                o_ref[...] = val.astype(o_ref.dtype)

        if nk == 1:
            finish(part)
        else:
            acc_ref = rest[-1]
            kk = pl.program_id(2)

            @pl.when(kk == 0)
            def _():
                acc_ref[...] = part

            @pl.when(kk > 0)
            def _():
                acc_ref[...] += part

            @pl.when(kk == nk - 1)
            def _():
                finish(acc_ref[...])

    res = pl.pallas_call(
        body, name=name, grid=(nm, nn, nk), in_specs=[a_spec, b_spec] + [tile] * n_extra + [ANY] * n_after,
        out_specs=[tile] * n_out, out_shape=[_sds((m, n), dt) for dt in out_dtypes],
        scratch_shapes=[pltpu.VMEM((tm, tn), F32)] if nk > 1 else [],
        compiler_params=_params(("parallel", "parallel", "arbitrary"), vmem),
    )(a, b, *extra, *after)
    return res[0] if n_out == 1 else res


def _rstd(x):
    return lax.rsqrt(jnp.mean(x * x, axis=-1, keepdims=True) + RMS_EPS)


def _rms_bwd(x, g, dy):
    rstd = _rstd(x)
    xh = x * rstd
    gy = dy * g
    dx = rstd * (gy - xh * jnp.mean(gy * xh, axis=-1, keepdims=True))
    return dx, dy * xh


def rms_fwd(x, g, *, name):
    s = x.shape[0]
    ts = min(s, 512)

    def body(x_ref, g_ref, h_ref):
        xv = x_ref[...]
        h_ref[...] = (xv * _rstd(xv) * g_ref[...]).astype(h_ref.dtype)

    return pl.pallas_call(
        body, name=name, grid=(s // ts,), in_specs=[_rows(ts, D_MODEL), _fixed((1, D_MODEL))],
        out_specs=_rows(ts, D_MODEL), out_shape=_sds((s, D_MODEL), MXU_DTYPE), compiler_params=_params(("parallel",)),
    )(x, g)


def rms_residual_fwd(x, y, g_post, g_next, *, name):
    s = x.shape[0]
    ts = min(s, 512)
    with_next = g_next is not None

    def body(x_ref, y_ref, gp_ref, *rest):
        yv = y_ref[...].astype(F32)
        xn = x_ref[...] + yv * _rstd(yv) * gp_ref[...]
        if with_next:
            gn_ref, xo_ref, h_ref = rest
            h_ref[...] = (xn * _rstd(xn) * gn_ref[...]).astype(h_ref.dtype)
        else:
            (xo_ref,) = rest
        xo_ref[...] = xn

    gspec = _fixed((1, D_MODEL))
    res = pl.pallas_call(
        body, name=name, grid=(s // ts,),
        in_specs=[_rows(ts, D_MODEL), _rows(ts, D_MODEL), gspec] + [gspec] * with_next,
        out_specs=[_rows(ts, D_MODEL)] * (1 + with_next),
        out_shape=[_sds((s, D_MODEL), F32)] + [_sds((s, D_MODEL), MXU_DTYPE)] * with_next,
        compiler_params=_params(("parallel",)),
    )(x, y, g_post, *([g_next] if with_next else []))
    return (res[0], res[1]) if with_next else (res[0], None)


def rms_bwd(x, g, dy, residual, *, out_dtype, name):
    s = x.shape[0]
    ts = min(s, 256)
    nt = s // ts
    with_res = residual is not None

    def body(x_ref, g_ref, dy_ref, *rest):
        dx_ref, dg_ref, acc_ref = rest[-3:]
        dx, dg_rows = _rms_bwd(x_ref[...].astype(F32), g_ref[...], dy_ref[...].astype(F32))
        if with_res:
            dx = dx + rest[0][...]
        dx_ref[...] = dx.astype(dx_ref.dtype)
        i = pl.program_id(0)

        @pl.when(i == 0)
        def _():
            acc_ref[...] = jnp.zeros_like(acc_ref)

        acc_ref[...] += _fold8(dg_rows)

        @pl.when(i == nt - 1)
        def _():
            dg_ref[...] = acc_ref[...].sum(axis=0, keepdims=True)

    return pl.pallas_call(
        body, name=name, grid=(nt,),
        in_specs=[_rows(ts, D_MODEL), _fixed((1, D_MODEL)), _rows(ts, D_MODEL)] + [_rows(ts, D_MODEL)] * with_res,
        out_specs=[_rows(ts, D_MODEL), _fixed((1, D_MODEL))],
        out_shape=[_sds((s, D_MODEL), out_dtype), _sds((1, D_MODEL), F32)],
        scratch_shapes=[pltpu.VMEM((SUBLANE, D_MODEL), F32)], compiler_params=_params(("arbitrary",)),
    )(x, g, dy, *([residual] if with_res else []))


def rms_bwd_pair(xa, ga, dya, residual, xb, gb, *, name):
    s = xa.shape[0]
    ts = min(s, 256)
    nt = s // ts

    def body(xa_ref, ga_ref, dya_ref, r_ref, xb_ref, gb_ref, d1_ref, d2_ref, dga_ref, dgb_ref, acc_a, acc_b):
        d1, rows_a = _rms_bwd(xa_ref[...].astype(F32), ga_ref[...], dya_ref[...].astype(F32))
        d1 = d1 + r_ref[...]
        d1_ref[...] = d1
        d2, rows_b = _rms_bwd(xb_ref[...].astype(F32), gb_ref[...], d1)
        d2_ref[...] = d2.astype(d2_ref.dtype)
        i = pl.program_id(0)

        @pl.when(i == 0)
        def _():
            acc_a[...] = jnp.zeros_like(acc_a)
            acc_b[...] = jnp.zeros_like(acc_b)

        acc_a[...] += _fold8(rows_a)
        acc_b[...] += _fold8(rows_b)

        @pl.when(i == nt - 1)
        def _():
            dga_ref[...] = acc_a[...].sum(axis=0, keepdims=True)
            dgb_ref[...] = acc_b[...].sum(axis=0, keepdims=True)

    row, gain = _rows(ts, D_MODEL), _fixed((1, D_MODEL))
    return pl.pallas_call(
        body, name=name, grid=(nt,), in_specs=[row, gain, row, row, row, gain], out_specs=[row, row, gain, gain],
        out_shape=[_sds((s, D_MODEL), F32), _sds((s, D_MODEL), MXU_DTYPE), _sds((1, D_MODEL), F32), _sds((1, D_MODEL), F32)],
        scratch_shapes=[pltpu.VMEM((SUBLANE, D_MODEL), F32)] * 2, compiler_params=_params(("arbitrary",)),
    )(xa, ga, dya, residual, xb, gb)


def loss_and_grad(y, target, *, name):
    s = y.shape[0]
    ts = min(s, 512)
    nt = s // ts

    def body(y_ref, t_ref, dy_ref, part_ref):
        e = y_ref[...] - t_ref[...]
        dy_ref[...] = e * (1.0 / D_MODEL)
        sq = _fold8(e * e)
        lanes = sq[:, 0:LANE]
        for j in range(1, D_MODEL // LANE):
            lanes = lanes + sq[:, j * LANE:(j + 1) * LANE]
        i = pl.program_id(0)

        @pl.when(i == 0)
        def _():
            part_ref[...] = jnp.zeros_like(part_ref)

        part_ref[...] += lanes * (0.5 / D_MODEL)

    return pl.pallas_call(
        body, name=name, grid=(nt,), in_specs=[_rows(ts, D_MODEL), _rows(ts, D_MODEL)],
        out_specs=[_rows(ts, D_MODEL), _fixed((SUBLANE, LANE))],
        out_shape=[_sds((s, D_MODEL), F32), _sds((SUBLANE, LANE), F32)], compiler_params=_params(("arbitrary",)),
    )(y, target)


CONV_TC = LANE


def _conv_taps(u, s):
    row = lax.broadcasted_iota(jnp.int32, u.shape, 0)
    prev = jnp.where(row == 0, 0.0, pltpu.roll(u, 1, 0))
    nxt = jnp.where(row == s - 1, 0.0, pltpu.roll(u, s - 1, 0))
    return prev, nxt


def _zcol(s, off):
    return pl.BlockSpec((s, CONV_TC), lambda j: (0, off // CONV_TC + j))


def conv_fwd(z, w3, *, name):
    s = z.shape[0]

    def body(cb_ref, cc_ref, ci_ref, w_ref, t_ref):
        u = cc_ref[...].astype(F32) * ci_ref[...].astype(F32)
        prev, nxt = _conv_taps(u, s)
        w = w_ref[...]
        conv = w[0:1] * prev + w[1:2] * u + w[2:3] * nxt
        t_ref[...] = (cb_ref[...].astype(F32) * conv).astype(t_ref.dtype)

    return pl.pallas_call(
        body, name=name, grid=(CONV_WIDTH // CONV_TC,),
        in_specs=[_zcol(s, OFF_CB), _zcol(s, OFF_CC), _zcol(s, OFF_CI), pl.BlockSpec((3, CONV_TC), lambda j: (0, j))],
        out_specs=pl.BlockSpec((s, CONV_TC), lambda j: (0, j)), out_shape=_sds((s, CONV_WIDTH), MXU_DTYPE),
        compiler_params=_params(("parallel",)),
    )(z, z, z, w3)


def conv_bwd(dt, z, w3, *, name, after=()):
    s = z.shape[0]

    def body(dt_ref, cb_ref, cc_ref, ci_ref, w_ref, *rest):
        dcb_ref, dcc_ref, dci_ref, dw_ref = rest[-4:]
        cc, ci = cc_ref[...].astype(F32), ci_ref[...].astype(F32)
        u = cc * ci
        prev, nxt = _conv_taps(u, s)
        w = w_ref[...]
        dtv = dt_ref[...].astype(F32)
        dcb_ref[...] = (dtv * (w[0:1] * prev + w[1:2] * u + w[2:3] * nxt)).astype(dcb_ref.dtype)
        dconv = dtv * cb_ref[...].astype(F32)
        dprev, dnxt = _conv_taps(dconv, s)
        du = w[0:1] * dnxt + w[1:2] * dconv + w[2:3] * dprev
        dcc_ref[...] = (du * ci).astype(dcc_ref.dtype)
        dci_ref[...] = (du * cc).astype(dci_ref.dtype)
        dw_ref[0:1, :] = jnp.sum(dconv * prev, axis=0, keepdims=True)
        dw_ref[1:2, :] = jnp.sum(dconv * u, axis=0, keepdims=True)
        dw_ref[2:3, :] = jnp.sum(dconv * nxt, axis=0, keepdims=True)

    col = pl.BlockSpec((s, CONV_TC), lambda j: (0, j))
    wspec = pl.BlockSpec((3, CONV_TC), lambda j: (0, j))
    return pl.pallas_call(
        body, name=name, grid=(CONV_WIDTH // CONV_TC,),
        in_specs=[col, _zcol(s, OFF_CB), _zcol(s, OFF_CC), _zcol(s, OFF_CI), wspec] + [ANY] * len(after),
        out_specs=[col, col, col, wspec],
        out_shape=[_sds((s, CONV_WIDTH), MXU_DTYPE)] * 3 + [_sds((3, CONV_WIDTH), F32)],
        compiler_params=_params(("parallel",)),
    )(dt, z, z, z, w3, *after)


def rope_tables(s):
    n_freq = HEAD_DIM // 4
    t = jnp.arange(s, dtype=jnp.int32)
    inv_freq = ROPE_THETA ** (-jnp.arange(0, HEAD_DIM // 2, 2, dtype=F32) / (HEAD_DIM // 2))
    ang_r = (t // GRID_W).astype(F32)[:, None] * inv_freq
    ang_c = (t % GRID_W).astype(F32)[:, None] * inv_freq
    cos_t = jnp.concatenate([jnp.cos(ang_r)] * 2 + [jnp.cos(ang_c)] * 2, axis=1)
    sin_t = jnp.concatenate([-jnp.sin(ang_r), jnp.sin(ang_r), -jnp.sin(ang_c), jnp.sin(ang_c)], axis=1)
    assert cos_t.shape == (s, 4 * n_freq)
    return cos_t, sin_t


def _swap_halves(v):
    lane = lax.broadcasted_iota(jnp.int32, v.shape, 1)
    return jnp.where(lane % 64 < 32, pltpu.roll(v, HEAD_DIM - 32, 1), pltpu.roll(v, 32, 1))


def qk_prep_fwd(z, qn, kn, cos_t, sin_t, *, name):
    s = z.shape[0]
    ts = min(s, 256)

    def body(q_ref, k_ref, v_ref, qn_ref, kn_ref, c_ref, s_ref, qo_ref, ko_ref, vo_ref):
        cs, sn = c_ref[...], s_ref[...]

        def head(x, g, scale):
            n = x * _rstd(x) * g
            return (n * cs + _swap_halves(n) * sn) * scale

        for h in range(N_Q_HEADS):
            sl = slice(h * HEAD_DIM, (h + 1) * HEAD_DIM)
            qo_ref[:, sl] = head(q_ref[:, sl].astype(F32), qn_ref[...], ATTN_SCALE).astype(qo_ref.dtype)
        for h in range(N_KV_HEADS):
            sl = slice(h * HEAD_DIM, (h + 1) * HEAD_DIM)
            ko_ref[:, sl] = head(k_ref[:, sl].astype(F32), kn_ref[...], 1.0).astype(ko_ref.dtype)
        vo_ref[...] = v_ref[...].astype(vo_ref.dtype)

    tab = _rows(ts, HEAD_DIM)
    gsp = _fixed((1, HEAD_DIM))
    return pl.pallas_call(
        body, name=name, grid=(s // ts,),
        in_specs=[_rows(ts, ATTN_WIDTH, OFF_Q // ATTN_WIDTH), _rows(ts, KV_WIDTH, OFF_K // KV_WIDTH),
                  _rows(ts, KV_WIDTH, OFF_V // KV_WIDTH), gsp, gsp, tab, tab],
        out_specs=[_rows(ts, ATTN_WIDTH), _rows(ts, KV_WIDTH), _rows(ts, KV_WIDTH)],
        out_shape=[_sds((s, ATTN_WIDTH), MXU_DTYPE), _sds((s, KV_WIDTH), MXU_DTYPE), _sds((s, KV_WIDTH), MXU_DTYPE)],
        compiler_params=_params(("parallel",)),
    )(z, z, z, qn, kn, cos_t, sin_t)


def qk_prep_bwd(dqp, dkp, z, qn, kn, cos_t, sin_t, *, name):
    s = z.shape[0]
    ts = min(s, 256)
    nt = s // ts

    def body(dq_ref, dk_ref, q_ref, k_ref, qn_ref, kn_ref, c_ref, s_ref, dqo_ref, dko_ref, dqn_ref, dkn_ref,
             qacc_ref, kacc_ref):
        cs, sn = c_ref[...], s_ref[...]
        i = pl.program_id(0)

        @pl.when(i == 0)
        def _():
            qacc_ref[...] = jnp.zeros_like(qacc_ref)
            kacc_ref[...] = jnp.zeros_like(kacc_ref)

        def head(x, g, dout, scale):
            d = dout.astype(F32) * scale
            dn = d * cs + _swap_halves(d * sn)
            return _rms_bwd(x, g, dn)

        qacc = jnp.zeros((SUBLANE, HEAD_DIM), F32)
        for h in range(N_Q_HEADS):
            sl = slice(h * HEAD_DIM, (h + 1) * HEAD_DIM)
            dx, dg_rows = head(q_ref[:, sl].astype(F32), qn_ref[...], dq_ref[:, sl], ATTN_SCALE)
            dqo_ref[:, sl] = dx.astype(dqo_ref.dtype)
            qacc = qacc + _fold8(dg_rows)
        kacc = jnp.zeros((SUBLANE, HEAD_DIM), F32)
        for h in range(N_KV_HEADS):
            sl = slice(h * HEAD_DIM, (h + 1) * HEAD_DIM)
            dx, dg_rows = head(k_ref[:, sl].astype(F32), kn_ref[...], dk_ref[:, sl], 1.0)
            dko_ref[:, sl] = dx.astype(dko_ref.dtype)
            kacc = kacc + _fold8(dg_rows)
        qacc_ref[...] += qacc
        kacc_ref[...] += kacc

        @pl.when(i == nt - 1)
        def _():
            dqn_ref[...] = qacc_ref[...].sum(axis=0, keepdims=True)
            dkn_ref[...] = kacc_ref[...].sum(axis=0, keepdims=True)

    tab = _rows(ts, HEAD_DIM)
    gsp = _fixed((1, HEAD_DIM))
    return pl.pallas_call(
        body, name=name, grid=(nt,),
        in_specs=[_rows(ts, ATTN_WIDTH), _rows(ts, KV_WIDTH), _rows(ts, ATTN_WIDTH, OFF_Q // ATTN_WIDTH),
                  _rows(ts, KV_WIDTH, OFF_K // KV_WIDTH), gsp, gsp, tab, tab],
        out_specs=[_rows(ts, ATTN_WIDTH), _rows(ts, KV_WIDTH), gsp, gsp],
        out_shape=[_sds((s, ATTN_WIDTH), MXU_DTYPE), _sds((s, KV_WIDTH), MXU_DTYPE), _sds((1, HEAD_DIM), F32),
                   _sds((1, HEAD_DIM), F32)],
        scratch_shapes=[pltpu.VMEM((SUBLANE, HEAD_DIM), F32)] * 2, compiler_params=_params(("arbitrary",)),
    )(dqp, dkp, z, z, qn, kn, cos_t, sin_t)


_NT = (((1,), (1,)), ((), ()))
_GW = GROUP * HEAD_DIM


def _dot(a, b, dims=(((1,), (0,)), ((), ()))):
    return lax.dot_general(a, b, dims, preferred_element_type=F32)


def attn_fwd(qp, kp, vb, *, name, after=()):
    s = qp.shape[0]
    tq = min(s, 1024)
    rows = min(tq, 128)

    def body(q_ref, k_ref, v_ref, *rest):
        o_ref, lse_ref, v_ones = rest[-3:]

        @pl.when(pl.program_id(1) == 0)
        def _():
            v_ones[:, :HEAD_DIM] = v_ref[...]
            v_ones[:, HEAD_DIM:] = jnp.ones((s, HEAD_DIM), v_ones.dtype)

        k = k_ref[...]
        for g in range(GROUP):
            sl = slice(g * HEAD_DIM, (g + 1) * HEAD_DIM)
            for r0 in range(0, tq, rows):
                rs = slice(r0, r0 + rows)
                sc = _dot(q_ref[rs, sl], k, _NT)
                mx = jnp.max(sc, axis=-1, keepdims=True)
                p = jnp.exp(sc - mx).astype(v_ones.dtype)
                o_den = _dot(p, v_ones[...])
                den = o_den[:, HEAD_DIM:HEAD_DIM + 1]
                o_ref[rs, sl] = (o_den[:, :HEAD_DIM] / den).astype(o_ref.dtype)
                lse_ref[rs, g:g + 1] = mx + jnp.log(den)

    return pl.pallas_call(
        body, name=name, grid=(N_KV_HEADS, s // tq),
        in_specs=[pl.BlockSpec((tq, _GW), lambda j, i: (i, j)), pl.BlockSpec((s, HEAD_DIM), lambda j, i: (0, j)),
                  pl.BlockSpec((s, HEAD_DIM), lambda j, i: (0, j))] + [ANY] * len(after),
        out_specs=[pl.BlockSpec((tq, _GW), lambda j, i: (i, j)), pl.BlockSpec((None, tq, GROUP), lambda j, i: (j, i, 0))],
        out_shape=[_sds((s, ATTN_WIDTH), MXU_DTYPE), _sds((N_KV_HEADS, s, GROUP), F32)],
        scratch_shapes=[pltpu.VMEM((s, 2 * HEAD_DIM), MXU_DTYPE)], compiler_params=_params(("parallel", "arbitrary")),
    )(qp, kp, vb, *after)


ATTN_BWD_TQ = 256


def attn_bwd(qp, kp, vb, do, o, lse, *, name):
    s = qp.shape[0]
    tq = min(s, ATTN_BWD_TQ)
    nq = s // tq
    over_rows = (((0,), (0,)), ((), ()))

    def body(q_ref, k_ref, v_ref, do_ref, o_ref, lse_ref, dq_ref, dk_ref, dv_ref, p_all, ds_all, q_all, do_all, dk_acc,
             dv_acc):
        i = pl.program_id(1)

        @pl.when(i == 0)
        def _():
            dk_acc[...] = jnp.zeros_like(dk_acc)
            dv_acc[...] = jnp.zeros_like(dv_acc)

        k, v = k_ref[...], v_ref[...]
        for g in range(GROUP):
            sl = slice(g * HEAD_DIM, (g + 1) * HEAD_DIM)
            rows = slice(g * tq, (g + 1) * tq)
            qg, dog = q_ref[:, sl], do_ref[:, sl]
            dd = jnp.sum(dog.astype(F32) * o_ref[:, sl].astype(F32), axis=-1, keepdims=True)
            p = jnp.exp(_dot(qg, k, _NT) - lse_ref[:, g:g + 1])
            ds = (p * (_dot(dog, v, _NT) - dd)).astype(k.dtype)
            dq_ref[:, sl] = _dot(ds, k).astype(dq_ref.dtype)
            p_all[rows, :] = p.astype(p_all.dtype)
            ds_all[rows, :] = ds
            q_all[rows, :] = qg
            do_all[rows, :] = dog
        dv_acc[...] += _dot(p_all[...], do_all[...], over_rows)
        dk_acc[...] += _dot(ds_all[...], q_all[...], over_rows)

        @pl.when(i == nq - 1)
        def _():
            dk_ref[...] = dk_acc[...].astype(dk_ref.dtype)
            dv_ref[...] = dv_acc[...].astype(dv_ref.dtype)

    qspec = pl.BlockSpec((tq, _GW), lambda j, i: (i, j))
    kspec = pl.BlockSpec((s, HEAD_DIM), lambda j, i: (0, j))
    lspec = pl.BlockSpec((None, tq, GROUP), lambda j, i: (j, i, 0))
    return pl.pallas_call(
        body, name=name, grid=(N_KV_HEADS, nq), in_specs=[qspec, kspec, kspec, qspec, qspec, lspec],
        out_specs=[qspec, kspec, kspec],
        out_shape=[_sds((s, ATTN_WIDTH), ACT_DTYPE), _sds((s, KV_WIDTH), ACT_DTYPE), _sds((s, KV_WIDTH), MXU_DTYPE)],
        scratch_shapes=[pltpu.VMEM((GROUP * tq, s), MXU_DTYPE), pltpu.VMEM((GROUP * tq, s), MXU_DTYPE),
                        pltpu.VMEM((GROUP * tq, HEAD_DIM), MXU_DTYPE), pltpu.VMEM((GROUP * tq, HEAD_DIM), MXU_DTYPE),
                        pltpu.VMEM((s, HEAD_DIM), F32), pltpu.VMEM((s, HEAD_DIM), F32)],
        compiler_params=_params(("parallel", "arbitrary")),
    )(qp, kp, vb, do, o, lse)


GATE_TW = 1024


def gate_fwd(ya, yb, z, bias, *, name):
    s = z.shape[0]
    ts = min(s, 512)

    def body(ya_ref, yb_ref, ga_ref, gb_ref, ba_ref, bb_ref, o_ref):
        sa = jax.nn.sigmoid(ga_ref[...].astype(F32) + ba_ref[...])
        sb = jax.nn.sigmoid(gb_ref[...].astype(F32) + bb_ref[...])
        o_ref[...] = (sa * ya_ref[...].astype(F32) + sb * yb_ref[...].astype(F32)).astype(o_ref.dtype)

    tile = pl.BlockSpec((ts, GATE_TW), lambda i, j: (i, j))
    ga = pl.BlockSpec((ts, GATE_TW), lambda i, j: (i, OFF_GA // GATE_TW + j))
    gb = pl.BlockSpec((ts, GATE_TW), lambda i, j: (i, OFF_GB // GATE_TW + j))
    ba = pl.BlockSpec((1, GATE_TW), lambda i, j: (0, j))
    bb = pl.BlockSpec((1, GATE_TW), lambda i, j: (0, D_MODEL // GATE_TW + j))
    return pl.pallas_call(
        body, name=name, grid=(s // ts, D_MODEL // GATE_TW), in_specs=[tile, tile, ga, gb, ba, bb], out_specs=tile,
        out_shape=_sds((s, D_MODEL), MXU_DTYPE), compiler_params=_params(("parallel", "parallel")),
    )(ya, yb, z, z, bias, bias)


def gate_bwd(dmi, ya, yb, z, bias, *, name):
    s = z.shape[0]
    ts = min(s, 512)
    nt = s // ts

    def body(d_ref, ya_ref, yb_ref, ga_ref, gb_ref, ba_ref, bb_ref, dya_ref, dyb_ref, dga_ref, dgb_ref, dba_ref,
             dbb_ref, acc_a, acc_b):
        d = d_ref[...].astype(F32)
        sa = jax.nn.sigmoid(ga_ref[...].astype(F32) + ba_ref[...])
        sb = jax.nn.sigmoid(gb_ref[...].astype(F32) + bb_ref[...])
        dya_ref[...] = (d * sa).astype(dya_ref.dtype)
        dyb_ref[...] = (d * sb).astype(dyb_ref.dtype)
        dga = d * ya_ref[...].astype(F32) * sa * (1.0 - sa)
        dgb = d * yb_ref[...].astype(F32) * sb * (1.0 - sb)
        dga_ref[...] = dga.astype(dga_ref.dtype)
        dgb_ref[...] = dgb.astype(dgb_ref.dtype)
        i = pl.program_id(1)

        @pl.when(i == 0)
        def _():
            acc_a[...] = jnp.zeros_like(acc_a)
            acc_b[...] = jnp.zeros_like(acc_b)

        acc_a[...] += _fold8(dga)
        acc_b[...] += _fold8(dgb)

        @pl.when(i == nt - 1)
        def _():
            dba_ref[...] = acc_a[...].sum(axis=0, keepdims=True)
            dbb_ref[...] = acc_b[...].sum(axis=0, keepdims=True)

    tile = pl.BlockSpec((ts, GATE_TW), lambda j, i: (i, j))
    ga = pl.BlockSpec((ts, GATE_TW), lambda j, i: (i, OFF_GA // GATE_TW + j))
    gb = pl.BlockSpec((ts, GATE_TW), lambda j, i: (i, OFF_GB // GATE_TW + j))
    ba = pl.BlockSpec((1, GATE_TW), lambda j, i: (0, j))
    bb = pl.BlockSpec((1, GATE_TW), lambda j, i: (0, D_MODEL // GATE_TW + j))
    acc = pltpu.VMEM((SUBLANE, GATE_TW), F32)
    return pl.pallas_call(
        body, name=name, grid=(D_MODEL // GATE_TW, nt), in_specs=[tile, tile, tile, ga, gb, ba, bb],
        out_specs=[tile, tile, tile, tile, ba, ba],
        out_shape=[_sds((s, D_MODEL), MXU_DTYPE)] * 4 + [_sds((1, D_MODEL), F32)] * 2,
        scratch_shapes=[acc, acc], compiler_params=_params(("parallel", "arbitrary")),
    )(dmi, ya, yb, z, z, bias, bias)


HBM = pl.BlockSpec(memory_space=pltpu.HBM)
SEM = pl.BlockSpec(memory_space=pltpu.SEMAPHORE)
EFFECT = pltpu.SideEffectType.DATAFLOW_SIDE_EFFECTING
TOKEN = (SUBLANE, LANE)


def _place():
    return lax.axis_index("x"), lax.axis_index("y"), lax.axis_index("c")


def _window(ref, shard_shape, axis, d):
    r, c = shard_shape
    lead = (slice(None),) * (len(ref.shape) - 2)
    if axis == 0:
        return ref.at[lead + (pl.ds(pl.multiple_of(d * r, SUBLANE), r), slice(None))]
    return ref.at[lead + (slice(None), pl.ds(pl.multiple_of(d * c, LANE), c))]


def _hbm(a):
    return pltpu.with_memory_space_constraint(a, pltpu.HBM)


def _remote(src, dst, send_sems, recv_sems, i, to):
    return pltpu.make_async_remote_copy(src_ref=src, dst_ref=dst, send_sem=send_sems.at[i], recv_sem=recv_sems.at[i],
                                        device_id=to, device_id_type=MESH)


def cast_into_full(w, layer, axis, me, dtype, *, name):
    _, r, c = w.shape
    tr = min(r, 256)
    nr = r // tr
    in_spec = pl.BlockSpec((None, tr, c), lambda i, me_ref: (layer, i, 0))
    if axis == 0:
        out_spec = pl.BlockSpec((tr, c), lambda i, me_ref: (me_ref[0] * nr + i, 0))
        shape = (N_DEV * r, c)
    else:
        out_spec = pl.BlockSpec((tr, c), lambda i, me_ref: (i, me_ref[0]))
        shape = (r, N_DEV * c)

    def body(me_ref, w_ref, o_ref):
        o_ref[...] = w_ref[...].astype(o_ref.dtype)

    return pl.pallas_call(
        body, name=name,
        grid_spec=pltpu.PrefetchScalarGridSpec(num_scalar_prefetch=1, grid=(nr,), in_specs=[in_spec], out_specs=out_spec),
        out_shape=_sds(shape, dtype), compiler_params=_params(("parallel",)),
    )(me, w)


class _GatherPlan:
    def __init__(self, fulls, shard_shapes, axes):
        x, y, c = _place()
        self.n = len(fulls)
        self.me, self.sibling = (x, y, c), (x, y, 1 - c)
        self.chips = [(1 - x, y), (x, 1 - y), (1 - x, 1 - y)]
        self.win = lambda a, p: _window(fulls[a], shard_shapes[a], axes[a], 4 * p[0] + 2 * p[1] + p[2])

    def first(self, send_sems, recv_sems):
        out = []
        for a in range(self.n):
            mine = self.win(a, self.me)
            out.append(_remote(mine, mine, send_sems, recv_sems, 4 * a, self.sibling))
            out += [_remote(mine, mine, send_sems, recv_sems, 4 * a + 1 + j, (*chip, self.me[2]))
                    for j, chip in enumerate(self.chips)]
        return out

    def first_arrivals(self, send_sems, recv_sems):
        c = self.me[2]
        out = []
        for a in range(self.n):
            blocks = [self.sibling] + [(*chip, c) for chip in self.chips]
            out += [_remote(self.win(a, b), self.win(a, b), send_sems, recv_sems, 4 * a + k, self.me)
                    for k, b in enumerate(blocks)]
        return out

    def passed(self, send_sems, recv_sems):
        c = self.me[2]
        return [_remote(self.win(a, (*chip, c)), self.win(a, (*chip, c)), send_sems, recv_sems, 3 * a + j, self.sibling)
                for a in range(self.n) for j, chip in enumerate(self.chips)]

    def passed_arrivals(self, send_sems, recv_sems):
        c = self.me[2]
        return [_remote(self.win(a, (*chip, 1 - c)), self.win(a, (*chip, 1 - c)), send_sems, recv_sems, 3 * a + j, self.me)
                for a in range(self.n) for j, chip in enumerate(self.chips)]


def gather_start(fulls, after, shard_shapes, axes, *, name):
    n = len(fulls)

    def body(*refs):
        ins, send_sems, recv_sems, token = refs[:n], refs[n + 1], refs[n + 2], refs[-1]
        for cp in _GatherPlan(ins, shard_shapes, axes).first(send_sems, recv_sems):
            cp.start()
        token[...] = jnp.zeros_like(token)

    res = pl.pallas_call(
        body, name=name, in_specs=[HBM] * n + [ANY],
        out_shape=(pltpu.SemaphoreType.DMA((4 * n,)), pltpu.SemaphoreType.DMA((4 * n,)),
                   *[pltpu.HBM(f.shape, f.dtype) for f in fulls], _sds(TOKEN, F32)),
        out_specs=(SEM, SEM, *[HBM] * n, pl.BlockSpec(memory_space=pltpu.VMEM)),
        input_output_aliases={a: 2 + a for a in range(n)}, compiler_params=pltpu.CompilerParams(has_side_effects=EFFECT),
    )(*[_hbm(f) for f in fulls], after)
    return res[0], res[1], list(res[2:2 + n]), res[-1]


def gather_pass(send_sems, recv_sems, fulls, after, shard_shapes, axes, *, name):
    n = len(fulls)

    def body(*refs):
        ins, s1, r1 = refs[:n], refs[n], refs[n + 1]
        s2, r2, token = refs[n + 3], refs[n + 4], refs[-1]
        plan = _GatherPlan(ins, shard_shapes, axes)
        for cp in plan.first_arrivals(s1, r1):
            cp.wait_recv()
        for cp in plan.first(s1, r1):
            cp.wait_send()
        for cp in plan.passed(s2, r2):
            cp.start()
        token[...] = jnp.zeros_like(token)

    res = pl.pallas_call(
        body, name=name, in_specs=[HBM] * n + [SEM, SEM, ANY],
        out_shape=(pltpu.SemaphoreType.DMA((3 * n,)), pltpu.SemaphoreType.DMA((3 * n,)),
                   *[pltpu.HBM(f.shape, f.dtype) for f in fulls], _sds(TOKEN, F32)),
        out_specs=(SEM, SEM, *[HBM] * n, pl.BlockSpec(memory_space=pltpu.VMEM)),
        input_output_aliases={a: 2 + a for a in range(n)}, compiler_params=pltpu.CompilerParams(has_side_effects=EFFECT),
    )(*fulls, send_sems, recv_sems, after)
    return res[0], res[1], list(res[2:2 + n]), res[-1]


def gather_wait(send_sems, recv_sems, fulls, after, shard_shapes, axes, *, name):
    n = len(fulls)

    def body(*refs):
        ins, s2, r2 = refs[:n], refs[n], refs[n + 1]
        plan = _GatherPlan(ins, shard_shapes, axes)
        for cp in plan.passed_arrivals(s2, r2):
            cp.wait_recv()
        for cp in plan.passed(s2, r2):
            cp.wait_send()

    return list(pl.pallas_call(
        body, name=name, in_specs=[HBM] * n + [SEM, SEM, ANY], out_shape=tuple(pltpu.HBM(f.shape, f.dtype) for f in fulls),
        out_specs=tuple([HBM] * n), input_output_aliases={a: a for a in range(n)},
        compiler_params=pltpu.CompilerParams(has_side_effects=EFFECT),
    )(*fulls, send_sems, recv_sems, after))


def _pair_copies(grads, lands, shard_shapes, axes, send_sems, recv_sems):
    x, y, c = _place()
    return [_remote(_window(grads[a], shard_shapes[a], axes[a], 2 * q + (1 - c)), lands[a].at[q], send_sems, recv_sems,
                    N_CHIP * a + q, (x, y, 1 - c))
            for a in range(len(grads)) for q in range(N_CHIP)]


def _chip_sends(sums, lands, send_sems, recv_sems):
    x, y, c = _place()
    return [_remote(sums[a].at[2 * px + py], lands[a].at[2 * x + y], send_sems, recv_sems, 3 * a + j, (px, py, c))
            for a in range(len(sums)) for j, (px, py) in enumerate([(1 - x, y), (x, 1 - y), (1 - x, 1 - y)])]


def _chip_arrivals(sums, lands, send_sems, recv_sems):
    x, y, c = _place()
    return [_remote(sums[a].at[2 * x + y], lands[a].at[2 * px + py], send_sems, recv_sems, 3 * a + j, (x, y, c))
            for a in range(len(sums)) for j, (px, py) in enumerate([(1 - x, y), (x, 1 - y), (1 - x, 1 - y)])]


def exchange_start(srcs, after, land_shapes, make_sends, per_array, *, name):
    n = len(srcs)
    lands = [lax.empty(s, a.dtype) for s, a in zip(land_shapes, srcs)]

    def body(*refs):
        ins, zones = refs[:n], refs[n:2 * n]
        send_sems, recv_sems, token = refs[2 * n + 1], refs[2 * n + 2], refs[-1]
        for cp in make_sends(ins, zones, send_sems, recv_sems):
            cp.start()
        token[...] = jnp.zeros_like(token)

    res = pl.pallas_call(
        body, name=name, in_specs=[HBM] * (2 * n) + [ANY],
        out_shape=(pltpu.SemaphoreType.DMA((per_array * n,)), pltpu.SemaphoreType.DMA((per_array * n,)),
                   *[pltpu.HBM(a.shape, a.dtype) for a in srcs], *[pltpu.HBM(a.shape, a.dtype) for a in lands],
                   _sds(TOKEN, F32)),
        out_specs=(SEM, SEM, *[HBM] * (2 * n), pl.BlockSpec(memory_space=pltpu.VMEM)),
        input_output_aliases={a: 2 + a for a in range(2 * n)}, compiler_params=pltpu.CompilerParams(has_side_effects=EFFECT),
    )(*[_hbm(a) for a in srcs], *[_hbm(a) for a in lands], after)
    return res[0], res[1], list(res[2:2 + n]), list(res[2 + n:2 + 2 * n]), res[-1]


def exchange_wait(send_sems, recv_sems, srcs, lands, after, make_waits, *, name):
    n = len(srcs)

    def body(*refs):
        ins, zones, s, r = refs[:n], refs[n:2 * n], refs[2 * n], refs[2 * n + 1]
        sends, arrivals = make_waits(ins, zones, s, r)
        for cp in arrivals:
            cp.wait_recv()
        for cp in sends:
            cp.wait_send()

    res = pl.pallas_call(
        body, name=name, in_specs=[HBM] * (2 * n) + [SEM, SEM, ANY],
        out_shape=tuple(pltpu.HBM(a.shape, a.dtype) for a in (*srcs, *lands)), out_specs=tuple([HBM] * (2 * n)),
        input_output_aliases={a: a for a in range(2 * n)}, compiler_params=pltpu.CompilerParams(has_side_effects=EFFECT),
    )(*srcs, *lands, send_sems, recv_sems, after)
    return list(res[:n]), list(res[n:])


def pair_sum(grad, recv, shard_shape, axis, core, *, name):
    r, c = shard_shape
    tr = min(r, 256)
    nr = r // tr
    if axis == 0:
        gspec = pl.BlockSpec((tr, c), lambda q, i, cref: ((2 * q + cref[0]) * nr + i, 0))
    else:
        gspec = pl.BlockSpec((tr, c), lambda q, i, cref: (i, 2 * q + cref[0]))
    rspec = pl.BlockSpec((None, tr, c), lambda q, i, cref: (q, i, 0))

    def body(c_ref, g_ref, r_ref, o_ref):
        o_ref[...] = (g_ref[...].astype(F32) + r_ref[...].astype(F32)).astype(o_ref.dtype)

    return pl.pallas_call(
        body, name=name,
        grid_spec=pltpu.PrefetchScalarGridSpec(num_scalar_prefetch=1, grid=(N_CHIP, nr), in_specs=[gspec, rspec],
                                               out_specs=rspec),
        out_shape=_sds((N_CHIP, r, c), recv.dtype), compiler_params=_params(("parallel", "parallel")),
    )(core, grad, recv)


def all_reduce_small(part, *, name):
    r, c = part.shape

    def body(p_ref, o_ref, gath_ref, send_sems, recv_sems):
        x, y, cc = _place()
        me = 4 * x + 2 * y + cc
        gath_ref[me] = p_ref[...]
        peers = [(1 - x if k & 4 else x, 1 - y if k & 2 else y, 1 - cc if k & 1 else cc) for k in range(1, N_DEV)]
        copies = [_remote(p_ref, gath_ref.at[me], send_sems, recv_sems, i, peer) for i, peer in enumerate(peers)]
        for cp in copies:
            cp.start()
        for i, (px, py, pc) in enumerate(peers):
            _remote(p_ref, gath_ref.at[4 * px + 2 * py + pc], send_sems, recv_sems, i, (x, y, cc)).wait_recv()
        for cp in copies:
            cp.wait_send()
        acc = gath_ref[0]
        for d in range(1, N_DEV):
            acc = acc + gath_ref[d]
        o_ref[...] = acc

    vm = pl.BlockSpec(memory_space=pltpu.VMEM)
    return pl.pallas_call(
        body, name=name, in_specs=[vm], out_specs=vm, out_shape=_sds((r, c), F32),
        scratch_shapes=[pltpu.VMEM((N_DEV, r, c), F32), pltpu.SemaphoreType.DMA((N_DEV - 1,)),
                        pltpu.SemaphoreType.DMA((N_DEV - 1,))],
    )(part)


def _adamw(w, g, m, v):
    m = ADAM_B1 * m + (1.0 - ADAM_B1) * g
    v = ADAM_B2 * v + (1.0 - ADAM_B2) * (g * g)
    m_hat = m / (1.0 - ADAM_B1 ** ADAM_STEP)
    v_hat = v / (1.0 - ADAM_B2 ** ADAM_STEP)
    delta = -ADAM_LR * (m_hat / (jnp.sqrt(v_hat) + ADAM_EPS) + ADAM_WD * w)
    return delta, m, v


def reduce_adam(own, landed, chip, w, m, v, layer, outs, *, name):
    _, r, c = w.shape
    tr = min(r, 128)
    first = outs is None

    def body(chip_ref, own_ref, l1_ref, l2_ref, l3_ref, w_ref, m_ref, v_ref, *rest):
        g_out, d_out, m_out, v_out = rest[-4:]
        g = own_ref[...].astype(F32) + l1_ref[...].astype(F32) + l2_ref[...].astype(F32) + l3_ref[...].astype(F32)
        d, mn, vn = _adamw(w_ref[...], g, m_ref[...], v_ref[...])
        g_out[...] = g
        d_out[...] = d
        m_out[...] = mn
        v_out[...] = vn

    spec = pl.BlockSpec((None, tr, c), lambda i, chip_ref: (layer, i, 0))

    def slot(step):
        return pl.BlockSpec((None, tr, c), lambda i, chip_ref: ((chip_ref[0] + step) % N_CHIP, i, 0))

    n_in = 8
    return pl.pallas_call(
        body, name=name,
        grid_spec=pltpu.PrefetchScalarGridSpec(
            num_scalar_prefetch=1, grid=(r // tr,),
            in_specs=[slot(0), slot(1), slot(2), slot(3), spec, spec, spec] + ([] if first else [ANY] * 4),
            out_specs=[spec] * 4),
        out_shape=[_sds(w.shape, F32)] * 4, input_output_aliases={} if first else {n_in + i: i for i in range(4)},
        compiler_params=_params(("parallel",)),
    )(chip, own, landed, landed, landed, w, m, v, *([] if first else outs))


def adam_small(g, w, m, v, *, name):
    def body(g_ref, w_ref, m_ref, v_ref, d_out, m_out, v_out):
        d, mn, vn = _adamw(w_ref[...], g_ref[...], m_ref[...], v_ref[...])
        d_out[...] = d
        m_out[...] = mn
        v_out[...] = vn

    return pl.pallas_call(body, name=name, out_shape=[_sds(w.shape, F32)] * 3)(g, w, m, v)


BIG = ("w_in", "w_out_conv", "w_out_attn", "w_merge", "w_up", "w_down")
GATHERED = BIG + ("conv_w",)
BIG_AXIS = {"w_in": 1, "w_out_conv": 0, "w_out_attn": 0, "w_merge": 0, "w_up": 1, "w_down": 0, "conv_w": 1}
FIRST_GATHER_GROUPS = (("w_in", "conv_w"), ("w_out_conv", "w_out_attn", "w_merge"), ("w_up", "w_down"))
LAST_REDUCE_GROUPS = (("w_down", "w_up"), ("w_merge", "w_out_conv", "w_out_attn"), ("w_in",))


class _Gather:
    def __init__(self, weights, keys, layer, me, after, tag):
        self.keys, self.tag = keys, tag
        self.shapes = [weights[k].shape[1:] for k in keys]
        self.axes = [BIG_AXIS[k] for k in keys]
        own = [cast_into_full(weights[k], layer, ax, me, F32 if k == "conv_w" else MXU_DTYPE, name="cast_" + k)
               for k, ax in zip(keys, self.axes)]
        self.send, self.recv, self.fulls, self.token = gather_start(own, after, self.shapes, self.axes,
                                                                    name="gather_start_" + tag)

    def pass_on(self, after):
        self.send, self.recv, self.fulls, self.token = gather_pass(self.send, self.recv, self.fulls, after, self.shapes,
                                                                   self.axes, name="gather_pass_" + self.tag)
        return self.token

    def wait(self, after):
        fulls = gather_wait(self.send, self.recv, self.fulls, after, self.shapes, self.axes, name="gather_wait_" + self.tag)
        return dict(zip(self.keys, fulls))


class _Reduce:
    def __init__(self, weights, keys, tag):
        self.keys, self.tag = keys, tag
        self.shapes = [weights[k].shape[1:] for k in keys]
        self.axes = [BIG_AXIS[k] for k in keys]
        self.pair_shapes = [(N_CHIP, *shp) for shp in self.shapes]

    def _pair(self, i, z, ss, rs):
        return _pair_copies(i, z, self.shapes, self.axes, ss, rs)

    def begin(self, grads, after):
        self.send, self.recv, self.src, self.land, self.token = exchange_start(
            [grads[k] for k in self.keys], after, self.pair_shapes, self._pair, N_CHIP, name="rs_pair_start_" + self.tag)
        return self.token

    def middle(self, after, core):
        both = lambda i, z, ss, rs: (self._pair(i, z, ss, rs),) * 2
        grads, from_sibling = exchange_wait(self.send, self.recv, self.src, self.land, after, both,
                                            name="rs_pair_wait_" + self.tag)
        sums = [pair_sum(g, rcv, shp, ax, core, name="pair_sum_" + k)
                for k, g, rcv, shp, ax in zip(self.keys, grads, from_sibling, self.shapes, self.axes)]
        self.send, self.recv, self.src, self.land, self.token = exchange_start(
            sums, after, self.pair_shapes, _chip_sends, 3, name="rs_chips_start_" + self.tag)
        return self.token

    def end(self, after):
        both = lambda i, z, ss, rs: (_chip_sends(i, z, ss, rs), _chip_arrivals(i, z, ss, rs))
        sums, landed = exchange_wait(self.send, self.recv, self.src, self.land, after, both,
                                     name="rs_chips_wait_" + self.tag)
        return list(zip(self.keys, sums, landed))
SMALL = (("norm_mix_pre", D_MODEL), ("gate_bias", 2 * D_MODEL), ("norm_mix_post", D_MODEL), ("norm_mlp_pre", D_MODEL),
         ("norm_mlp_post", D_MODEL), ("q_norm", HEAD_DIM), ("k_norm", HEAD_DIM))
SMALL_WIDTH = sum(w for _, w in SMALL)
WEIGHTS = ("norm_mix_pre", "w_in", "gate_bias", "conv_w", "q_norm", "k_norm", "w_out_conv", "w_out_attn", "w_merge",
           "norm_mix_post", "norm_mlp_pre", "w_up", "w_down", "norm_mlp_post")


def kernel(x, norm_mix_pre, w_in, gate_bias, conv_w, q_norm, k_norm, w_out_conv, w_out_attn, w_merge, norm_mix_post, norm_mlp_pre, w_up, w_down, norm_mlp_post, loss_target, m_norm_mix_pre, m_w_in, m_gate_bias, m_conv_w, m_q_norm, m_k_norm, m_w_out_conv, m_w_out_attn, m_w_merge, m_norm_mix_post, m_norm_mlp_pre, m_w_up, m_w_down, m_norm_mlp_post, v_norm_mix_pre, v_w_in, v_gate_bias, v_conv_w, v_q_norm, v_k_norm, v_w_out_conv, v_w_out_attn, v_w_merge, v_norm_mix_post, v_norm_mlp_pre, v_w_up, v_w_down, v_norm_mlp_post):
    w = dict(norm_mix_pre=norm_mix_pre, w_in=w_in, gate_bias=gate_bias, conv_w=conv_w, q_norm=q_norm, k_norm=k_norm,
             w_out_conv=w_out_conv, w_out_attn=w_out_attn, w_merge=w_merge, norm_mix_post=norm_mix_post,
             norm_mlp_pre=norm_mlp_pre, w_up=w_up, w_down=w_down, norm_mlp_post=norm_mlp_post)
    mom = dict(norm_mix_pre=m_norm_mix_pre, w_in=m_w_in, gate_bias=m_gate_bias, conv_w=m_conv_w, q_norm=m_q_norm,
               k_norm=m_k_norm, w_out_conv=m_w_out_conv, w_out_attn=m_w_out_attn, w_merge=m_w_merge,
               norm_mix_post=m_norm_mix_post, norm_mlp_pre=m_norm_mlp_pre, w_up=m_w_up, w_down=m_w_down,
               norm_mlp_post=m_norm_mlp_post)
    var = dict(norm_mix_pre=v_norm_mix_pre, w_in=v_w_in, gate_bias=v_gate_bias, conv_w=v_conv_w, q_norm=v_q_norm,
               k_norm=v_k_norm, w_out_conv=v_w_out_conv, w_out_attn=v_w_out_attn, w_merge=v_w_merge,
               norm_mix_post=v_norm_mix_post, norm_mlp_pre=v_norm_mlp_pre, w_up=v_w_up, w_down=v_w_down,
               norm_mlp_post=v_norm_mlp_post)
    depth = w_in.shape[0]
    s = x.shape[1]
    xs = x.reshape(s, D_MODEL)
    target = loss_target.reshape(s, D_MODEL)
    x_idx, y_idx, c_idx = _place()
    as_operand = lambda i: jnp.reshape(i, (1,)).astype(jnp.int32)
    core, chip, me = as_operand(c_idx), as_operand(2 * x_idx + y_idx), as_operand(4 * x_idx + 2 * y_idx + c_idx)
    cos_t, sin_t = rope_tables(s)

    def vec(name, l):
        return w[name][l].reshape(1, -1)

    saved = []
    h = rms_fwd(xs, vec("norm_mix_pre", 0), name="rms_first")
    first, after = [], h
    for i, keys in enumerate(FIRST_GATHER_GROUPS):
        first.append(_Gather(w, keys, 0, me, after, f"0{'abc'[i]}"))
        after = first[-1].token
    full = first[0].wait(first[0].pass_on(after))
    layers = []
    for l in range(depth):
        layers.append(full)
        nxt = l + 1 < depth
        if nxt:
            coming = _Gather(w, GATHERED, l + 1, me, full["w_in"], str(l + 1))
        z = matmul(h, full["w_in"], mode="nn", m=s, n=IN_WIDTH, k=D_MODEL, out_dtypes=[ACT_DTYPE], name="mm_in",
                   after=[coming.token] if nxt else [])
        t = conv_fwd(z, full["conv_w"], name="conv_fwd")
        qp, kp, vb = qk_prep_fwd(z, vec("q_norm", l), vec("k_norm", l), cos_t, sin_t, name="qk_fwd")
        o, lse = attn_fwd(qp, kp, vb, name="attn_fwd", after=[first[1].pass_on(qp)] if l == 0 else [])
        if l == 0:
            full.update(first[1].wait(o))
        ya = matmul(t, full["w_out_conv"], mode="nn", m=s, n=D_MODEL, k=CONV_WIDTH, out_dtypes=[ACT_DTYPE], name="mm_out_conv",
                    after=[first[2].pass_on(o)] if l == 0 else [])
        yb = matmul(o, full["w_out_attn"], mode="nn", m=s, n=D_MODEL, k=ATTN_WIDTH, out_dtypes=[ACT_DTYPE], name="mm_out_attn")
        mi = gate_fwd(ya, yb, z, vec("gate_bias", l), name="gate_fwd")
        mixed = matmul(mi, full["w_merge"], mode="nn", m=s, n=D_MODEL, k=D_MODEL, out_dtypes=[ACT_DTYPE], name="mm_merge")
        if l == 0:
            full.update(first[2].wait(mixed))
        x_mid, h2 = rms_residual_fwd(xs, mixed, vec("norm_mix_post", l), vec("norm_mlp_pre", l), name="res_mix")
        act, r = matmul(h2, full["w_up"], mode="nn", m=s, n=D_FF, k=D_MODEL, out_dtypes=[MXU_DTYPE, MXU_DTYPE],
                        name="mm_up", epilogue=lambda acc: (acc, jnp.square(jnp.maximum(acc, 0.0))))
        f = matmul(r, full["w_down"], mode="nn", m=s, n=D_MODEL, k=D_FF, tk=D_FF // 2, vmem=VMEM_LIMIT_DEEP_K,
                   out_dtypes=[ACT_DTYPE], name="mm_down",
                   after=[coming.pass_on(r)] if nxt and l > 0 else [])
        if nxt and l == 0:
            coming.pass_on(f)
        g_next = vec("norm_mix_pre", l + 1) if nxt else None
        x_out, h_next = rms_residual_fwd(x_mid, f, vec("norm_mlp_post", l), g_next, name="res_mlp" if nxt else "res_last")
        saved.append(dict(x_in=xs, h=h, z=z, t=t, qp=qp, kp=kp, vb=vb, o=o, lse=lse, ya=ya, yb=yb, mi=mi, mixed=mixed,
                          x_mid=x_mid, h2=h2, act=act, r=r, f=f))
        if nxt:
            full = coming.wait(x_out)
        xs, h = x_out, h_next

    dx, loss_part = loss_and_grad(xs, target, name="loss")
    loss = lax.psum(jnp.sum(loss_part), ("x", "y", "c"))

    small_rows = [None] * depth
    conv_rows = [None] * depth
    out_g, out_d, out_m, out_v = {}, {}, {}, {}
    big_outs = {k: None for k in BIG}
    pending = None
    last = []
    handed_down = None

    def finish(reduction, layer, after):
        for k, own, landed in reduction.end(after):
            big_outs[k] = reduce_adam(own, landed, chip, w[k], mom[k], var[k], layer, big_outs[k], name="adam_" + k)

    for l in reversed(range(depth)):
        sv, full = saved[l], layers[l]
        grads = {}
        groups = l == 0

        def wgrad(key, lhs, rhs, m, n, after=()):
            grads[key] = matmul(lhs, rhs, mode="tn", m=m, n=n, k=s, tk=s, out_dtypes=[COMM_DTYPE], name="wg_" + key,
                                after=after)

        def begin_group(i, after):
            last.append(_Reduce(w, LAST_REDUCE_GROUPS[i], f"{l}{'abc'[i]}"))
            return [last[i].begin(grads, after)]

        if handed_down is None:
            handed_down = rms_bwd(sv["f"], vec("norm_mlp_post", l), dx, None, out_dtype=MXU_DTYPE, name="rmsb_mlp_post")
        df, dg_mlp_post = handed_down
        da = matmul(df, full["w_down"], mode="nt", m=s, n=D_FF, k=D_MODEL, out_dtypes=[MXU_DTYPE], name="mm_d_down",
                    extra=(sv["act"],), epilogue=lambda acc, a: (acc * (2.0 * jnp.maximum(a.astype(F32), 0.0)),),
                    after=[pending.token] if pending else [])
        tokens = [pending.middle(da, core)] if pending else []
        wgrad("w_down", sv["r"], df, D_FF, D_MODEL)
        dh2 = matmul(da, full["w_up"], mode="nt", m=s, n=D_MODEL, k=D_FF, tk=D_FF // 2, vmem=VMEM_LIMIT_DEEP_K,
                     out_dtypes=[ACT_DTYPE], name="mm_d_up", after=tokens)
        wgrad("w_up", sv["h2"], da, D_MODEL, D_FF)
        tokens = begin_group(0, dh2) if groups else []
        dx_mid, dmixed, dg_mlp_pre, dg_mix_post = rms_bwd_pair(
            sv["x_mid"], vec("norm_mlp_pre", l), dh2, dx, sv["mixed"], vec("norm_mix_post", l), name="rmsb_mlp_pre_mix_post")
        dmi = matmul(dmixed, full["w_merge"], mode="nt", m=s, n=D_MODEL, k=D_MODEL, out_dtypes=[ACT_DTYPE], name="mm_d_merge",
                     after=tokens)
        tokens = [last[0].middle(dmi, core)] if groups else []
        wgrad("w_merge", sv["mi"], dmixed, D_MODEL, D_MODEL, after=tokens)
        dya, dyb, dga, dgb, dba, dbb = gate_bwd(dmi, sv["ya"], sv["yb"], sv["z"], vec("gate_bias", l), name="gate_bwd")
        wgrad("w_out_conv", sv["t"], dya, CONV_WIDTH, D_MODEL)
        wgrad("w_out_attn", sv["o"], dyb, ATTN_WIDTH, D_MODEL)
        tokens = begin_group(1, dyb) if groups else []
        dt = matmul(dya, full["w_out_conv"], mode="nt", m=s, n=CONV_WIDTH, k=D_MODEL, out_dtypes=[ACT_DTYPE],
                    name="mm_d_out_conv", after=tokens)
        do = matmul(dyb, full["w_out_attn"], mode="nt", m=s, n=ATTN_WIDTH, k=D_MODEL, out_dtypes=[MXU_DTYPE],
                    name="mm_d_out_attn")
        tokens = [last[1].middle(do, core)] if groups else []
        dcb, dcc, dci, dconv_w = conv_bwd(dt, sv["z"], full["conv_w"], name="conv_bwd", after=tokens)
        dqp, dkp, dv = attn_bwd(sv["qp"], sv["kp"], sv["vb"], do, sv["o"], sv["lse"], name="attn_bwd")
        dq, dk, dqn, dkn = qk_prep_bwd(dqp, dkp, sv["z"], vec("q_norm", l), vec("k_norm", l), cos_t, sin_t, name="qk_bwd")
        dz = jnp.concatenate([dcb, dcc, dci, dq, dk, dv, dga, dgb], axis=1)
        wgrad("w_in", sv["h"], dz, D_MODEL, IN_WIDTH)
        tokens = begin_group(2, dz) if groups else []
        dh = matmul(dz, full["w_in"], mode="nt", m=s, n=D_MODEL, k=IN_WIDTH, tk=IN_WIDTH // 4, out_dtypes=[ACT_DTYPE],
                    name="mm_d_in", after=tokens)
        if l > 0:
            dx, df_below, dg_mix_pre, dg_post_below = rms_bwd_pair(
                sv["x_in"], vec("norm_mix_pre", l), dh, dx_mid, saved[l - 1]["f"], vec("norm_mlp_post", l - 1),
                name="rmsb_mix_pre_mlp_post")
            handed_down = (df_below, dg_post_below)
        else:
            dx, dg_mix_pre = rms_bwd(sv["x_in"], vec("norm_mix_pre", l), dh, dx_mid, out_dtype=F32, name="rmsb_mix_pre")
        small_rows[l] = jnp.concatenate([dg_mix_pre, dba, dbb, dg_mix_post, dg_mlp_pre, dg_mlp_post, dqn, dkn], axis=1)
        conv_rows[l] = dconv_w.reshape(1, 3 * CONV_WIDTH)
        if pending:
            finish(pending, l + 1, dx)
        pending = None
        if not groups:
            pending = _Reduce(w, BIG, str(l))
            pending.begin(grads, dx)

    grad_x = dx.reshape(1, s, D_MODEL)

    small_part = jnp.concatenate([jnp.concatenate(small_rows, axis=0), jnp.concatenate(conv_rows, axis=0)], axis=1)
    small_sum = all_reduce_small(small_part, name="allreduce_small")
    pack = lambda src: jnp.concatenate([src[k].reshape(depth, wd) for k, wd in SMALL], axis=1)
    g_small = small_sum[:, :SMALL_WIDTH]
    d_small, m_small, v_small = adam_small(g_small, pack(w), pack(mom), pack(var), name="adam_small")
    off = 0
    for k, wd in SMALL:
        for dst, src in ((out_g, g_small), (out_d, d_small), (out_m, m_small), (out_v, v_small)):
            dst[k] = src[:, off:off + wd]
        off += wd
    cshard = CONV_WIDTH // N_DEV
    g_conv = lax.dynamic_slice_in_dim(small_sum[:, SMALL_WIDTH:].reshape(depth, 3, CONV_WIDTH), me[0] * cshard, cshard,
                                      axis=2).reshape(depth, 3 * cshard)
    flat = lambda a: a.reshape(depth, 3 * cshard)
    d_conv, m_conv, v_conv = adam_small(g_conv, flat(conv_w), flat(m_conv_w), flat(v_conv_w), name="adam_conv")
    for dst, src in ((out_g, g_conv), (out_d, d_conv), (out_m, m_conv), (out_v, v_conv)):
        dst["conv_w"] = src.reshape(depth, 3, cshard)

    token = last[2].middle(d_small, core)
    for reduction in last:
        finish(reduction, 0, token)
    for k in BIG:
        out_g[k], out_d[k], out_m[k], out_v[k] = big_outs[k]

    return (loss, grad_x, *[out_g[k] for k in WEIGHTS], *[out_d[k] for k in WEIGHTS], *[out_m[k] for k in WEIGHTS],
            *[out_v[k] for k in WEIGHTS])
```

```python
import math

import jax
import jax.numpy as jnp
from jax import lax
from jax.experimental import pallas as pl
from jax.experimental.pallas import tpu as pltpu

F32 = jnp.float32
MXU_DTYPE = jnp.bfloat16
COMM_DTYPE = jnp.bfloat16
ACT_DTYPE = jnp.bfloat16

D_MODEL = 2048
HEAD_DIM = 128
N_Q_HEADS = 16
N_KV_HEADS = 4
GROUP = N_Q_HEADS // N_KV_HEADS
ATTN_WIDTH = N_Q_HEADS * HEAD_DIM
KV_WIDTH = N_KV_HEADS * HEAD_DIM
CONV_WIDTH = D_MODEL
D_FF = 4 * D_MODEL
GRID_W = 64
ROPE_THETA = 10000.0
RMS_EPS = 1e-6
IN_WIDTH = 3 * CONV_WIDTH + ATTN_WIDTH + 2 * KV_WIDTH + 2 * D_MODEL
OFF_CB, OFF_CC, OFF_CI = 0, CONV_WIDTH, 2 * CONV_WIDTH
OFF_Q = 3 * CONV_WIDTH
OFF_K = OFF_Q + ATTN_WIDTH
OFF_V = OFF_K + KV_WIDTH
OFF_GA = OFF_V + KV_WIDTH
OFF_GB = OFF_GA + D_MODEL
ATTN_SCALE = 1.0 / math.sqrt(HEAD_DIM)

ADAM_LR, ADAM_B1, ADAM_B2, ADAM_EPS, ADAM_WD, ADAM_STEP = 0.001, 0.9, 0.999, 1e-08, 0.01, 10

N_DEV = 8
N_CHIP = 4
LANE = 128
SUBLANE = 8
VMEM_LIMIT = 48 * 1024 * 1024
VMEM_LIMIT_DEEP_K = 56 * 1024 * 1024
MESH = pl.DeviceIdType.MESH
ANY = pl.BlockSpec(memory_space=pl.ANY)


def _sds(shape, dtype):
    return jax.ShapeDtypeStruct(tuple(shape), dtype)


def _params(sem, vmem=VMEM_LIMIT):
    return pltpu.CompilerParams(dimension_semantics=sem, vmem_limit_bytes=vmem)


def _rows(ts, w, col=0):
    return pl.BlockSpec((ts, w), lambda i: (i, col))


def _fixed(shape):
    return pl.BlockSpec(shape, lambda *_: (0,) * len(shape))


def _fold8(v):
    ts, w = v.shape
    return v.reshape(ts // SUBLANE, SUBLANE, w).sum(axis=0)


def matmul(a, b, *, mode, m, n, k, name, out_dtypes, tm=1024, tn=1024, tk=2048, epilogue=None, extra=(), after=(),
           vmem=VMEM_LIMIT):
    tm, tn, tk = min(tm, m), min(tn, n), min(tk, k)
    nm, nn, nk = m // tm, n // tn, k // tk
    assert nm * tm == m and nn * tn == n and nk * tk == k, (name, m, n, k, tm, tn, tk)
    if mode == "tn":
        a_spec = pl.BlockSpec((tk, tm), lambda i, j, kk: (kk, i))
        dims = (((0,), (0,)), ((), ()))
    else:
        a_spec = pl.BlockSpec((tm, tk), lambda i, j, kk: (i, kk))
        dims = (((1,), (1 if mode == "nt" else 0,)), ((), ()))
    if mode == "nt":
        b_spec = pl.BlockSpec((tn, tk), lambda i, j, kk: (j, kk))
    else:
        b_spec = pl.BlockSpec((tk, tn), lambda i, j, kk: (kk, j))
    tile = pl.BlockSpec((tm, tn), lambda i, j, kk: (i, j))
    n_out, n_extra, n_after = len(out_dtypes), len(extra), len(after)
    if epilogue is None:
        epilogue = lambda acc: (acc,)

    def body(a_ref, b_ref, *rest):
        extra_refs = rest[:n_extra]
        outs = rest[n_extra + n_after:][:n_out]
        part = lax.dot_general(a_ref[...].astype(MXU_DTYPE), b_ref[...].astype(MXU_DTYPE), dims,
                               preferred_element_type=F32)

        def finish(acc):
            for o_ref, val in zip(outs, epilogue(acc, *[r[...] for r in extra_refs])):
                o_ref[...] = val.astype(o_ref.dtype)

        if nk == 1:
            finish(part)
        else:
            acc_ref = rest[-1]
            kk = pl.program_id(2)

            @pl.when(kk == 0)
            def _():
                acc_ref[...] = part

            @pl.when(kk > 0)
            def _():
                acc_ref[...] += part

            @pl.when(kk == nk - 1)
            def _():
                finish(acc_ref[...])

    res = pl.pallas_call(
        body, name=name, grid=(nm, nn, nk), in_specs=[a_spec, b_spec] + [tile] * n_extra + [ANY] * n_after,
        out_specs=[tile] * n_out, out_shape=[_sds((m, n), dt) for dt in out_dtypes],
        scratch_shapes=[pltpu.VMEM((tm, tn), F32)] if nk > 1 else [],
        compiler_params=_params(("parallel", "parallel", "arbitrary"), vmem),
    )(a, b, *extra, *after)
    return res[0] if n_out == 1 else res


def _rstd(x):
    return lax.rsqrt(jnp.mean(x * x, axis=-1, keepdims=True) + RMS_EPS)


def _rms_bwd(x, g, dy):
    rstd = _rstd(x)
    xh = x * rstd
    gy = dy * g
    dx = rstd * (gy - xh * jnp.mean(gy * xh, axis=-1, keepdims=True))
    return dx, dy * xh


def rms_fwd(x, g, *, name):
    s = x.shape[0]
    ts = min(s, 512)

    def body(x_ref, g_ref, h_ref):
        xv = x_ref[...]
        h_ref[...] = (xv * _rstd(xv) * g_ref[...]).astype(h_ref.dtype)

    return pl.pallas_call(
        body, name=name, grid=(s // ts,), in_specs=[_rows(ts, D_MODEL), _fixed((1, D_MODEL))],
        out_specs=_rows(ts, D_MODEL), out_shape=_sds((s, D_MODEL), MXU_DTYPE), compiler_params=_params(("parallel",)),
    )(x, g)


def rms_residual_fwd(x, y, g_post, g_next, *, name):
    s = x.shape[0]
    ts = min(s, 512)
    with_next = g_next is not None

    def body(x_ref, y_ref, gp_ref, *rest):
        yv = y_ref[...].astype(F32)
        xn = x_ref[...] + yv * _rstd(yv) * gp_ref[...]
        if with_next:
            gn_ref, xo_ref, h_ref = rest
            h_ref[...] = (xn * _rstd(xn) * gn_ref[...]).astype(h_ref.dtype)
        else:
            (xo_ref,) = rest
        xo_ref[...] = xn

    gspec = _fixed((1, D_MODEL))
    res = pl.pallas_call(
        body, name=name, grid=(s // ts,),
        in_specs=[_rows(ts, D_MODEL), _rows(ts, D_MODEL), gspec] + [gspec] * with_next,
        out_specs=[_rows(ts, D_MODEL)] * (1 + with_next),
        out_shape=[_sds((s, D_MODEL), F32)] + [_sds((s, D_MODEL), MXU_DTYPE)] * with_next,
        compiler_params=_params(("parallel",)),
    )(x, y, g_post, *([g_next] if with_next else []))
    return (res[0], res[1]) if with_next else (res[0], None)


def rms_bwd(x, g, dy, residual, *, out_dtype, name):
    s = x.shape[0]
    ts = min(s, 256)
    nt = s // ts
    with_res = residual is not None

    def body(x_ref, g_ref, dy_ref, *rest):
        dx_ref, dg_ref, acc_ref = rest[-3:]
        dx, dg_rows = _rms_bwd(x_ref[...].astype(F32), g_ref[...], dy_ref[...].astype(F32))
        if with_res:
            dx = dx + rest[0][...]
        dx_ref[...] = dx.astype(dx_ref.dtype)
        i = pl.program_id(0)

        @pl.when(i == 0)
        def _():
            acc_ref[...] = jnp.zeros_like(acc_ref)

        acc_ref[...] += _fold8(dg_rows)

        @pl.when(i == nt - 1)
        def _():
            dg_ref[...] = acc_ref[...].sum(axis=0, keepdims=True)

    return pl.pallas_call(
        body, name=name, grid=(nt,),
        in_specs=[_rows(ts, D_MODEL), _fixed((1, D_MODEL)), _rows(ts, D_MODEL)] + [_rows(ts, D_MODEL)] * with_res,
        out_specs=[_rows(ts, D_MODEL), _fixed((1, D_MODEL))],
        out_shape=[_sds((s, D_MODEL), out_dtype), _sds((1, D_MODEL), F32)],
        scratch_shapes=[pltpu.VMEM((SUBLANE, D_MODEL), F32)], compiler_params=_params(("arbitrary",)),
    )(x, g, dy, *([residual] if with_res else []))


def rms_bwd_pair(xa, ga, dya, residual, xb, gb, *, name):
    s = xa.shape[0]
    ts = min(s, 256)
    nt = s // ts

    def body(xa_ref, ga_ref, dya_ref, r_ref, xb_ref, gb_ref, d1_ref, d2_ref, dga_ref, dgb_ref, acc_a, acc_b):
        d1, rows_a = _rms_bwd(xa_ref[...].astype(F32), ga_ref[...], dya_ref[...].astype(F32))
        d1 = d1 + r_ref[...]
        d1_ref[...] = d1
        d2, rows_b = _rms_bwd(xb_ref[...].astype(F32), gb_ref[...], d1)
        d2_ref[...] = d2.astype(d2_ref.dtype)
        i = pl.program_id(0)

        @pl.when(i == 0)
        def _():
            acc_a[...] = jnp.zeros_like(acc_a)
            acc_b[...] = jnp.zeros_like(acc_b)

        acc_a[...] += _fold8(rows_a)
        acc_b[...] += _fold8(rows_b)

        @pl.when(i == nt - 1)
        def _():
            dga_ref[...] = acc_a[...].sum(axis=0, keepdims=True)
            dgb_ref[...] = acc_b[...].sum(axis=0, keepdims=True)

    row, gain = _rows(ts, D_MODEL), _fixed((1, D_MODEL))
    return pl.pallas_call(
        body, name=name, grid=(nt,), in_specs=[row, gain, row, row, row, gain], out_specs=[row, row, gain, gain],
        out_shape=[_sds((s, D_MODEL), F32), _sds((s, D_MODEL), MXU_DTYPE), _sds((1, D_MODEL), F32), _sds((1, D_MODEL), F32)],
        scratch_shapes=[pltpu.VMEM((SUBLANE, D_MODEL), F32)] * 2, compiler_params=_params(("arbitrary",)),
    )(xa, ga, dya, residual, xb, gb)


def loss_and_grad(y, target, *, name):
    s = y.shape[0]
    ts = min(s, 512)
    nt = s // ts

    def body(y_ref, t_ref, dy_ref, part_ref):
        e = y_ref[...] - t_ref[...]
        dy_ref[...] = e * (1.0 / D_MODEL)
        sq = _fold8(e * e)
        lanes = sq[:, 0:LANE]
        for j in range(1, D_MODEL // LANE):
            lanes = lanes + sq[:, j * LANE:(j + 1) * LANE]
        i = pl.program_id(0)

        @pl.when(i == 0)
        def _():
            part_ref[...] = jnp.zeros_like(part_ref)

        part_ref[...] += lanes * (0.5 / D_MODEL)

    return pl.pallas_call(
        body, name=name, grid=(nt,), in_specs=[_rows(ts, D_MODEL), _rows(ts, D_MODEL)],
        out_specs=[_rows(ts, D_MODEL), _fixed((SUBLANE, LANE))],
        out_shape=[_sds((s, D_MODEL), F32), _sds((SUBLANE, LANE), F32)], compiler_params=_params(("arbitrary",)),
    )(y, target)


CONV_TC = LANE


def _conv_taps(u, s):
    row = lax.broadcasted_iota(jnp.int32, u.shape, 0)
    prev = jnp.where(row == 0, 0.0, pltpu.roll(u, 1, 0))
    nxt = jnp.where(row == s - 1, 0.0, pltpu.roll(u, s - 1, 0))
    return prev, nxt


def _zcol(s, off):
    return pl.BlockSpec((s, CONV_TC), lambda j: (0, off // CONV_TC + j))


def conv_fwd(z, w3, *, name):
    s = z.shape[0]

    def body(cb_ref, cc_ref, ci_ref, w_ref, t_ref):
        u = cc_ref[...].astype(F32) * ci_ref[...].astype(F32)
        prev, nxt = _conv_taps(u, s)
        w = w_ref[...]
        conv = w[0:1] * prev + w[1:2] * u + w[2:3] * nxt
        t_ref[...] = (cb_ref[...].astype(F32) * conv).astype(t_ref.dtype)

    return pl.pallas_call(
        body, name=name, grid=(CONV_WIDTH // CONV_TC,),
        in_specs=[_zcol(s, OFF_CB), _zcol(s, OFF_CC), _zcol(s, OFF_CI), pl.BlockSpec((3, CONV_TC), lambda j: (0, j))],
        out_specs=pl.BlockSpec((s, CONV_TC), lambda j: (0, j)), out_shape=_sds((s, CONV_WIDTH), MXU_DTYPE),
        compiler_params=_params(("parallel",)),
    )(z, z, z, w3)


def conv_bwd(dt, z, w3, *, name, after=()):
    s = z.shape[0]

    def body(dt_ref, cb_ref, cc_ref, ci_ref, w_ref, *rest):
        dcb_ref, dcc_ref, dci_ref, dw_ref = rest[-4:]
        cc, ci = cc_ref[...].astype(F32), ci_ref[...].astype(F32)
        u = cc * ci
        prev, nxt = _conv_taps(u, s)
        w = w_ref[...]
        dtv = dt_ref[...].astype(F32)
        dcb_ref[...] = (dtv * (w[0:1] * prev + w[1:2] * u + w[2:3] * nxt)).astype(dcb_ref.dtype)
        dconv = dtv * cb_ref[...].astype(F32)
        dprev, dnxt = _conv_taps(dconv, s)
        du = w[0:1] * dnxt + w[1:2] * dconv + w[2:3] * dprev
        dcc_ref[...] = (du * ci).astype(dcc_ref.dtype)
        dci_ref[...] = (du * cc).astype(dci_ref.dtype)
        dw_ref[0:1, :] = jnp.sum(dconv * prev, axis=0, keepdims=True)
        dw_ref[1:2, :] = jnp.sum(dconv * u, axis=0, keepdims=True)
        dw_ref[2:3, :] = jnp.sum(dconv * nxt, axis=0, keepdims=True)

    col = pl.BlockSpec((s, CONV_TC), lambda j: (0, j))
    wspec = pl.BlockSpec((3, CONV_TC), lambda j: (0, j))
    return pl.pallas_call(
        body, name=name, grid=(CONV_WIDTH // CONV_TC,),
        in_specs=[col, _zcol(s, OFF_CB), _zcol(s, OFF_CC), _zcol(s, OFF_CI), wspec] + [ANY] * len(after),
        out_specs=[col, col, col, wspec],
        out_shape=[_sds((s, CONV_WIDTH), MXU_DTYPE)] * 3 + [_sds((3, CONV_WIDTH), F32)],
        compiler_params=_params(("parallel",)),
    )(dt, z, z, z, w3, *after)


def rope_tables(s):
    n_freq = HEAD_DIM // 4
    t = jnp.arange(s, dtype=jnp.int32)
    inv_freq = ROPE_THETA ** (-jnp.arange(0, HEAD_DIM // 2, 2, dtype=F32) / (HEAD_DIM // 2))
    ang_r = (t // GRID_W).astype(F32)[:, None] * inv_freq
    ang_c = (t % GRID_W).astype(F32)[:, None] * inv_freq
    cos_t = jnp.concatenate([jnp.cos(ang_r)] * 2 + [jnp.cos(ang_c)] * 2, axis=1)
    sin_t = jnp.concatenate([-jnp.sin(ang_r), jnp.sin(ang_r), -jnp.sin(ang_c), jnp.sin(ang_c)], axis=1)
    assert cos_t.shape == (s, 4 * n_freq)
    return cos_t, sin_t


def _swap_halves(v):
    lane = lax.broadcasted_iota(jnp.int32, v.shape, 1)
    return jnp.where(lane % 64 < 32, pltpu.roll(v, HEAD_DIM - 32, 1), pltpu.roll(v, 32, 1))


def qk_prep_fwd(z, qn, kn, cos_t, sin_t, *, name):
    s = z.shape[0]
    ts = min(s, 512)

    def body(q_ref, k_ref, v_ref, qn_ref, kn_ref, c_ref, s_ref, qo_ref, ko_ref, vo_ref):
        cs, sn = c_ref[...], s_ref[...]

        def head(x, g, scale):
            n = x * _rstd(x) * g
            return (n * cs + _swap_halves(n) * sn) * scale

        for h in range(N_Q_HEADS):
            sl = slice(h * HEAD_DIM, (h + 1) * HEAD_DIM)
            qo_ref[:, sl] = head(q_ref[:, sl].astype(F32), qn_ref[...], ATTN_SCALE).astype(qo_ref.dtype)
        for h in range(N_KV_HEADS):
            sl = slice(h * HEAD_DIM, (h + 1) * HEAD_DIM)
            ko_ref[:, sl] = head(k_ref[:, sl].astype(F32), kn_ref[...], 1.0).astype(ko_ref.dtype)
        vo_ref[...] = v_ref[...].astype(vo_ref.dtype)

    tab = _rows(ts, HEAD_DIM)
    gsp = _fixed((1, HEAD_DIM))
    return pl.pallas_call(
        body, name=name, grid=(s // ts,),
        in_specs=[_rows(ts, ATTN_WIDTH, OFF_Q // ATTN_WIDTH), _rows(ts, KV_WIDTH, OFF_K // KV_WIDTH),
                  _rows(ts, KV_WIDTH, OFF_V // KV_WIDTH), gsp, gsp, tab, tab],
        out_specs=[_rows(ts, ATTN_WIDTH), _rows(ts, KV_WIDTH), _rows(ts, KV_WIDTH)],
        out_shape=[_sds((s, ATTN_WIDTH), MXU_DTYPE), _sds((s, KV_WIDTH), MXU_DTYPE), _sds((s, KV_WIDTH), MXU_DTYPE)],
        compiler_params=_params(("parallel",)),
    )(z, z, z, qn, kn, cos_t, sin_t)


def qk_prep_bwd(dqp, dkp, z, qn, kn, cos_t, sin_t, *, name):
    s = z.shape[0]
    ts = min(s, 512)
    nt = s // ts

    def body(dq_ref, dk_ref, q_ref, k_ref, qn_ref, kn_ref, c_ref, s_ref, dqo_ref, dko_ref, dqn_ref, dkn_ref,
             qacc_ref, kacc_ref):
        cs, sn = c_ref[...], s_ref[...]
        i = pl.program_id(0)

        @pl.when(i == 0)
        def _():
            qacc_ref[...] = jnp.zeros_like(qacc_ref)
            kacc_ref[...] = jnp.zeros_like(kacc_ref)

        def head(x, g, dout, scale):
            d = dout.astype(F32) * scale
            dn = d * cs + _swap_halves(d * sn)
            return _rms_bwd(x, g, dn)

        qacc = jnp.zeros((SUBLANE, HEAD_DIM), F32)
        for h in range(N_Q_HEADS):
            sl = slice(h * HEAD_DIM, (h + 1) * HEAD_DIM)
            dx, dg_rows = head(q_ref[:, sl].astype(F32), qn_ref[...], dq_ref[:, sl], ATTN_SCALE)
            dqo_ref[:, sl] = dx.astype(dqo_ref.dtype)
            qacc = qacc + _fold8(dg_rows)
        kacc = jnp.zeros((SUBLANE, HEAD_DIM), F32)
        for h in range(N_KV_HEADS):
            sl = slice(h * HEAD_DIM, (h + 1) * HEAD_DIM)
            dx, dg_rows = head(k_ref[:, sl].astype(F32), kn_ref[...], dk_ref[:, sl], 1.0)
            dko_ref[:, sl] = dx.astype(dko_ref.dtype)
            kacc = kacc + _fold8(dg_rows)
        qacc_ref[...] += qacc
        kacc_ref[...] += kacc

        @pl.when(i == nt - 1)
        def _():
            dqn_ref[...] = qacc_ref[...].sum(axis=0, keepdims=True)
            dkn_ref[...] = kacc_ref[...].sum(axis=0, keepdims=True)

    tab = _rows(ts, HEAD_DIM)
    gsp = _fixed((1, HEAD_DIM))
    return pl.pallas_call(
        body, name=name, grid=(nt,),
        in_specs=[_rows(ts, ATTN_WIDTH), _rows(ts, KV_WIDTH), _rows(ts, ATTN_WIDTH, OFF_Q // ATTN_WIDTH),
                  _rows(ts, KV_WIDTH, OFF_K // KV_WIDTH), gsp, gsp, tab, tab],
        out_specs=[_rows(ts, ATTN_WIDTH), _rows(ts, KV_WIDTH), gsp, gsp],
        out_shape=[_sds((s, ATTN_WIDTH), MXU_DTYPE), _sds((s, KV_WIDTH), MXU_DTYPE), _sds((1, HEAD_DIM), F32),
                   _sds((1, HEAD_DIM), F32)],
        scratch_shapes=[pltpu.VMEM((SUBLANE, HEAD_DIM), F32)] * 2, compiler_params=_params(("arbitrary",)),
    )(dqp, dkp, z, z, qn, kn, cos_t, sin_t)


_NT = (((1,), (1,)), ((), ()))
_GW = GROUP * HEAD_DIM


def _dot(a, b, dims=(((1,), (0,)), ((), ()))):
    return lax.dot_general(a, b, dims, preferred_element_type=F32)


def attn_fwd(qp, kp, vb, *, name, after=()):
    s = qp.shape[0]
    tq = min(s, 1024)
    rows = min(tq, 128)

    def body(q_ref, k_ref, v_ref, *rest):
        o_ref, lse_ref, v_ones = rest[-3:]

        @pl.when(pl.program_id(1) == 0)
        def _():
            v_ones[:, :HEAD_DIM] = v_ref[...]
            v_ones[:, HEAD_DIM:] = jnp.ones((s, HEAD_DIM), v_ones.dtype)

        k = k_ref[...]
        for g in range(GROUP):
            sl = slice(g * HEAD_DIM, (g + 1) * HEAD_DIM)
            for r0 in range(0, tq, rows):
                rs = slice(r0, r0 + rows)
                sc = _dot(q_ref[rs, sl], k, _NT)
                mx = jnp.max(sc, axis=-1, keepdims=True)
                p = jnp.exp(sc - mx).astype(v_ones.dtype)
                o_den = _dot(p, v_ones[...])
                den = o_den[:, HEAD_DIM:HEAD_DIM + 1]
                o_ref[rs, sl] = (o_den[:, :HEAD_DIM] / den).astype(o_ref.dtype)
                lse_ref[rs, g:g + 1] = mx + jnp.log(den)

    return pl.pallas_call(
        body, name=name, grid=(N_KV_HEADS, s // tq),
        in_specs=[pl.BlockSpec((tq, _GW), lambda j, i: (i, j)), pl.BlockSpec((s, HEAD_DIM), lambda j, i: (0, j)),
                  pl.BlockSpec((s, HEAD_DIM), lambda j, i: (0, j))] + [ANY] * len(after),
        out_specs=[pl.BlockSpec((tq, _GW), lambda j, i: (i, j)), pl.BlockSpec((None, tq, GROUP), lambda j, i: (j, i, 0))],
        out_shape=[_sds((s, ATTN_WIDTH), MXU_DTYPE), _sds((N_KV_HEADS, s, GROUP), F32)],
        scratch_shapes=[pltpu.VMEM((s, 2 * HEAD_DIM), MXU_DTYPE)], compiler_params=_params(("parallel", "arbitrary")),
    )(qp, kp, vb, *after)


ATTN_BWD_TQ = 256


def attn_bwd(qp, kp, vb, do, o, lse, *, name):
    s = qp.shape[0]
    tq = min(s, ATTN_BWD_TQ)
    nq = s // tq
    over_rows = (((0,), (0,)), ((), ()))

    def body(q_ref, k_ref, v_ref, do_ref, o_ref, lse_ref, dq_ref, dk_ref, dv_ref, p_all, ds_all, q_all, do_all, dk_acc,
             dv_acc):
        i = pl.program_id(1)

        @pl.when(i == 0)
        def _():
            dk_acc[...] = jnp.zeros_like(dk_acc)
            dv_acc[...] = jnp.zeros_like(dv_acc)

        k, v = k_ref[...], v_ref[...]
        for g in range(GROUP):
            sl = slice(g * HEAD_DIM, (g + 1) * HEAD_DIM)
            rows = slice(g * tq, (g + 1) * tq)
            qg, dog = q_ref[:, sl], do_ref[:, sl]
            dd = jnp.sum(dog.astype(F32) * o_ref[:, sl].astype(F32), axis=-1, keepdims=True)
            p = jnp.exp(_dot(qg, k, _NT) - lse_ref[:, g:g + 1])
            ds = (p * (_dot(dog, v, _NT) - dd)).astype(k.dtype)
            dq_ref[:, sl] = _dot(ds, k).astype(dq_ref.dtype)
            p_all[rows, :] = p.astype(p_all.dtype)
            ds_all[rows, :] = ds
            q_all[rows, :] = qg
            do_all[rows, :] = dog
        dv_acc[...] += _dot(p_all[...], do_all[...], over_rows)
        dk_acc[...] += _dot(ds_all[...], q_all[...], over_rows)

        @pl.when(i == nq - 1)
        def _():
            dk_ref[...] = dk_acc[...].astype(dk_ref.dtype)
            dv_ref[...] = dv_acc[...].astype(dv_ref.dtype)

    qspec = pl.BlockSpec((tq, _GW), lambda j, i: (i, j))
    kspec = pl.BlockSpec((s, HEAD_DIM), lambda j, i: (0, j))
    lspec = pl.BlockSpec((None, tq, GROUP), lambda j, i: (j, i, 0))
    return pl.pallas_call(
        body, name=name, grid=(N_KV_HEADS, nq), in_specs=[qspec, kspec, kspec, qspec, qspec, lspec],
        out_specs=[qspec, kspec, kspec],
        out_shape=[_sds((s, ATTN_WIDTH), ACT_DTYPE), _sds((s, KV_WIDTH), ACT_DTYPE), _sds((s, KV_WIDTH), MXU_DTYPE)],
        scratch_shapes=[pltpu.VMEM((GROUP * tq, s), MXU_DTYPE), pltpu.VMEM((GROUP * tq, s), MXU_DTYPE),
                        pltpu.VMEM((GROUP * tq, HEAD_DIM), MXU_DTYPE), pltpu.VMEM((GROUP * tq, HEAD_DIM), MXU_DTYPE),
                        pltpu.VMEM((s, HEAD_DIM), F32), pltpu.VMEM((s, HEAD_DIM), F32)],
        compiler_params=_params(("parallel", "arbitrary")),
    )(qp, kp, vb, do, o, lse)


GATE_TW = 1024


def gate_fwd(ya, yb, z, bias, *, name):
    s = z.shape[0]
    ts = min(s, 512)

    def body(ya_ref, yb_ref, ga_ref, gb_ref, ba_ref, bb_ref, o_ref):
        sa = jax.nn.sigmoid(ga_ref[...].astype(F32) + ba_ref[...])
        sb = jax.nn.sigmoid(gb_ref[...].astype(F32) + bb_ref[...])
        o_ref[...] = (sa * ya_ref[...].astype(F32) + sb * yb_ref[...].astype(F32)).astype(o_ref.dtype)

    tile = pl.BlockSpec((ts, GATE_TW), lambda i, j: (i, j))
    ga = pl.BlockSpec((ts, GATE_TW), lambda i, j: (i, OFF_GA // GATE_TW + j))
    gb = pl.BlockSpec((ts, GATE_TW), lambda i, j: (i, OFF_GB // GATE_TW + j))
    ba = pl.BlockSpec((1, GATE_TW), lambda i, j: (0, j))
    bb = pl.BlockSpec((1, GATE_TW), lambda i, j: (0, D_MODEL // GATE_TW + j))
    return pl.pallas_call(
        body, name=name, grid=(s // ts, D_MODEL // GATE_TW), in_specs=[tile, tile, ga, gb, ba, bb], out_specs=tile,
        out_shape=_sds((s, D_MODEL), MXU_DTYPE), compiler_params=_params(("parallel", "parallel")),
    )(ya, yb, z, z, bias, bias)


def gate_bwd(dmi, ya, yb, z, bias, *, name):
    s = z.shape[0]
    ts = min(s, 512)
    nt = s // ts

    def body(d_ref, ya_ref, yb_ref, ga_ref, gb_ref, ba_ref, bb_ref, dya_ref, dyb_ref, dga_ref, dgb_ref, dba_ref,
             dbb_ref, acc_a, acc_b):
        d = d_ref[...].astype(F32)
        sa = jax.nn.sigmoid(ga_ref[...].astype(F32) + ba_ref[...])
        sb = jax.nn.sigmoid(gb_ref[...].astype(F32) + bb_ref[...])
        dya_ref[...] = (d * sa).astype(dya_ref.dtype)
        dyb_ref[...] = (d * sb).astype(dyb_ref.dtype)
        dga = d * ya_ref[...].astype(F32) * sa * (1.0 - sa)
        dgb = d * yb_ref[...].astype(F32) * sb * (1.0 - sb)
        dga_ref[...] = dga.astype(dga_ref.dtype)
        dgb_ref[...] = dgb.astype(dgb_ref.dtype)
        i = pl.program_id(1)

        @pl.when(i == 0)
        def _():
            acc_a[...] = jnp.zeros_like(acc_a)
            acc_b[...] = jnp.zeros_like(acc_b)

        acc_a[...] += _fold8(dga)
        acc_b[...] += _fold8(dgb)

        @pl.when(i == nt - 1)
        def _():
            dba_ref[...] = acc_a[...].sum(axis=0, keepdims=True)
            dbb_ref[...] = acc_b[...].sum(axis=0, keepdims=True)

    tile = pl.BlockSpec((ts, GATE_TW), lambda j, i: (i, j))
    ga = pl.BlockSpec((ts, GATE_TW), lambda j, i: (i, OFF_GA // GATE_TW + j))
    gb = pl.BlockSpec((ts, GATE_TW), lambda j, i: (i, OFF_GB // GATE_TW + j))
    ba = pl.BlockSpec((1, GATE_TW), lambda j, i: (0, j))
    bb = pl.BlockSpec((1, GATE_TW), lambda j, i: (0, D_MODEL // GATE_TW + j))
    acc = pltpu.VMEM((SUBLANE, GATE_TW), F32)
    return pl.pallas_call(
        body, name=name, grid=(D_MODEL // GATE_TW, nt), in_specs=[tile, tile, tile, ga, gb, ba, bb],
        out_specs=[tile, tile, tile, tile, ba, ba],
        out_shape=[_sds((s, D_MODEL), MXU_DTYPE)] * 4 + [_sds((1, D_MODEL), F32)] * 2,
        scratch_shapes=[acc, acc], compiler_params=_params(("parallel", "arbitrary")),
    )(dmi, ya, yb, z, z, bias, bias)


HBM = pl.BlockSpec(memory_space=pltpu.HBM)
SEM = pl.BlockSpec(memory_space=pltpu.SEMAPHORE)
EFFECT = pltpu.SideEffectType.DATAFLOW_SIDE_EFFECTING
TOKEN = (SUBLANE, LANE)


def _place():
    return lax.axis_index("x"), lax.axis_index("y"), lax.axis_index("c")


def _window(ref, shard_shape, axis, d):
    r, c = shard_shape
    lead = (slice(None),) * (len(ref.shape) - 2)
    if axis == 0:
        return ref.at[lead + (pl.ds(pl.multiple_of(d * r, SUBLANE), r), slice(None))]
    return ref.at[lead + (slice(None), pl.ds(pl.multiple_of(d * c, LANE), c))]


def _hbm(a):
    return pltpu.with_memory_space_constraint(a, pltpu.HBM)


def _remote(src, dst, send_sems, recv_sems, i, to):
    return pltpu.make_async_remote_copy(src_ref=src, dst_ref=dst, send_sem=send_sems.at[i], recv_sem=recv_sems.at[i],
                                        device_id=to, device_id_type=MESH)


def cast_into_full(w, layer, axis, me, dtype, *, name):
    _, r, c = w.shape
    tr = min(r, 256)
    nr = r // tr
    in_spec = pl.BlockSpec((None, tr, c), lambda i, me_ref: (layer, i, 0))
    if axis == 0:
        out_spec = pl.BlockSpec((tr, c), lambda i, me_ref: (me_ref[0] * nr + i, 0))
        shape = (N_DEV * r, c)
    else:
        out_spec = pl.BlockSpec((tr, c), lambda i, me_ref: (i, me_ref[0]))
        shape = (r, N_DEV * c)

    def body(me_ref, w_ref, o_ref):
        o_ref[...] = w_ref[...].astype(o_ref.dtype)

    return pl.pallas_call(
        body, name=name,
        grid_spec=pltpu.PrefetchScalarGridSpec(num_scalar_prefetch=1, grid=(nr,), in_specs=[in_spec], out_specs=out_spec),
        out_shape=_sds(shape, dtype), compiler_params=_params(("parallel",)),
    )(me, w)


class _GatherPlan:
    def __init__(self, fulls, shard_shapes, axes):
        x, y, c = _place()
        self.n = len(fulls)
        self.me, self.sibling = (x, y, c), (x, y, 1 - c)
        self.chips = [(1 - x, y), (x, 1 - y), (1 - x, 1 - y)]
        self.win = lambda a, p: _window(fulls[a], shard_shapes[a], axes[a], 4 * p[0] + 2 * p[1] + p[2])

    def first(self, send_sems, recv_sems):
        out = []
        for a in range(self.n):
            mine = self.win(a, self.me)
            out.append(_remote(mine, mine, send_sems, recv_sems, 4 * a, self.sibling))
            out += [_remote(mine, mine, send_sems, recv_sems, 4 * a + 1 + j, (*chip, self.me[2]))
                    for j, chip in enumerate(self.chips)]
        return out

    def first_arrivals(self, send_sems, recv_sems):
        c = self.me[2]
        out = []
        for a in range(self.n):
            blocks = [self.sibling] + [(*chip, c) for chip in self.chips]
            out += [_remote(self.win(a, b), self.win(a, b), send_sems, recv_sems, 4 * a + k, self.me)
                    for k, b in enumerate(blocks)]
        return out

    def passed(self, send_sems, recv_sems):
        c = self.me[2]
        return [_remote(self.win(a, (*chip, c)), self.win(a, (*chip, c)), send_sems, recv_sems, 3 * a + j, self.sibling)
                for a in range(self.n) for j, chip in enumerate(self.chips)]

    def passed_arrivals(self, send_sems, recv_sems):
        c = self.me[2]
        return [_remote(self.win(a, (*chip, 1 - c)), self.win(a, (*chip, 1 - c)), send_sems, recv_sems, 3 * a + j, self.me)
                for a in range(self.n) for j, chip in enumerate(self.chips)]


def gather_start(fulls, after, shard_shapes, axes, *, name):
    n = len(fulls)

    def body(*refs):
        ins, send_sems, recv_sems, token = refs[:n], refs[n + 1], refs[n + 2], refs[-1]
        for cp in _GatherPlan(ins, shard_shapes, axes).first(send_sems, recv_sems):
            cp.start()
        token[...] = jnp.zeros_like(token)

    res = pl.pallas_call(
        body, name=name, in_specs=[HBM] * n + [ANY],
        out_shape=(pltpu.SemaphoreType.DMA((4 * n,)), pltpu.SemaphoreType.DMA((4 * n,)),
                   *[pltpu.HBM(f.shape, f.dtype) for f in fulls], _sds(TOKEN, F32)),
        out_specs=(SEM, SEM, *[HBM] * n, pl.BlockSpec(memory_space=pltpu.VMEM)),
        input_output_aliases={a: 2 + a for a in range(n)}, compiler_params=pltpu.CompilerParams(has_side_effects=EFFECT),
    )(*[_hbm(f) for f in fulls], after)
    return res[0], res[1], list(res[2:2 + n]), res[-1]


def gather_pass(send_sems, recv_sems, fulls, after, shard_shapes, axes, *, name):
    n = len(fulls)

    def body(*refs):
        ins, s1, r1 = refs[:n], refs[n], refs[n + 1]
        s2, r2, token = refs[n + 3], refs[n + 4], refs[-1]
        plan = _GatherPlan(ins, shard_shapes, axes)
        for cp in plan.first_arrivals(s1, r1):
            cp.wait_recv()
        for cp in plan.first(s1, r1):
            cp.wait_send()
        for cp in plan.passed(s2, r2):
            cp.start()
        token[...] = jnp.zeros_like(token)

    res = pl.pallas_call(
        body, name=name, in_specs=[HBM] * n + [SEM, SEM, ANY],
        out_shape=(pltpu.SemaphoreType.DMA((3 * n,)), pltpu.SemaphoreType.DMA((3 * n,)),
                   *[pltpu.HBM(f.shape, f.dtype) for f in fulls], _sds(TOKEN, F32)),
        out_specs=(SEM, SEM, *[HBM] * n, pl.BlockSpec(memory_space=pltpu.VMEM)),
        input_output_aliases={a: 2 + a for a in range(n)}, compiler_params=pltpu.CompilerParams(has_side_effects=EFFECT),
    )(*fulls, send_sems, recv_sems, after)
    return res[0], res[1], list(res[2:2 + n]), res[-1]


def gather_wait(send_sems, recv_sems, fulls, after, shard_shapes, axes, *, name):
    n = len(fulls)

    def body(*refs):
        ins, s2, r2 = refs[:n], refs[n], refs[n + 1]
        plan = _GatherPlan(ins, shard_shapes, axes)
        for cp in plan.passed_arrivals(s2, r2):
            cp.wait_recv()
        for cp in plan.passed(s2, r2):
            cp.wait_send()

    return list(pl.pallas_call(
        body, name=name, in_specs=[HBM] * n + [SEM, SEM, ANY], out_shape=tuple(pltpu.HBM(f.shape, f.dtype) for f in fulls),
        out_specs=tuple([HBM] * n), input_output_aliases={a: a for a in range(n)},
        compiler_params=pltpu.CompilerParams(has_side_effects=EFFECT),
    )(*fulls, send_sems, recv_sems, after))


def _pair_copies(grads, lands, shard_shapes, axes, send_sems, recv_sems):
    x, y, c = _place()
    return [_remote(_window(grads[a], shard_shapes[a], axes[a], 2 * q + (1 - c)), lands[a].at[q], send_sems, recv_sems,
                    N_CHIP * a + q, (x, y, 1 - c))
            for a in range(len(grads)) for q in range(N_CHIP)]


def _chip_sends(sums, lands, send_sems, recv_sems):
    x, y, c = _place()
    return [_remote(sums[a].at[2 * px + py], lands[a].at[2 * x + y], send_sems, recv_sems, 3 * a + j, (px, py, c))
            for a in range(len(sums)) for j, (px, py) in enumerate([(1 - x, y), (x, 1 - y), (1 - x, 1 - y)])]


def _chip_arrivals(sums, lands, send_sems, recv_sems):
    x, y, c = _place()
    return [_remote(sums[a].at[2 * x + y], lands[a].at[2 * px + py], send_sems, recv_sems, 3 * a + j, (x, y, c))
            for a in range(len(sums)) for j, (px, py) in enumerate([(1 - x, y), (x, 1 - y), (1 - x, 1 - y)])]


def exchange_start(srcs, after, land_shapes, make_sends, per_array, *, name):
    n = len(srcs)
    lands = [lax.empty(s, a.dtype) for s, a in zip(land_shapes, srcs)]

    def body(*refs):
        ins, zones = refs[:n], refs[n:2 * n]
        send_sems, recv_sems, token = refs[2 * n + 1], refs[2 * n + 2], refs[-1]
        for cp in make_sends(ins, zones, send_sems, recv_sems):
            cp.start()
        token[...] = jnp.zeros_like(token)

    res = pl.pallas_call(
        body, name=name, in_specs=[HBM] * (2 * n) + [ANY],
        out_shape=(pltpu.SemaphoreType.DMA((per_array * n,)), pltpu.SemaphoreType.DMA((per_array * n,)),
                   *[pltpu.HBM(a.shape, a.dtype) for a in srcs], *[pltpu.HBM(a.shape, a.dtype) for a in lands],
                   _sds(TOKEN, F32)),
        out_specs=(SEM, SEM, *[HBM] * (2 * n), pl.BlockSpec(memory_space=pltpu.VMEM)),
        input_output_aliases={a: 2 + a for a in range(2 * n)}, compiler_params=pltpu.CompilerParams(has_side_effects=EFFECT),
    )(*[_hbm(a) for a in srcs], *[_hbm(a) for a in lands], after)
    return res[0], res[1], list(res[2:2 + n]), list(res[2 + n:2 + 2 * n]), res[-1]


def exchange_wait(send_sems, recv_sems, srcs, lands, after, make_waits, *, name):
    n = len(srcs)

    def body(*refs):
        ins, zones, s, r = refs[:n], refs[n:2 * n], refs[2 * n], refs[2 * n + 1]
        sends, arrivals = make_waits(ins, zones, s, r)
        for cp in arrivals:
            cp.wait_recv()
        for cp in sends:
            cp.wait_send()

    res = pl.pallas_call(
        body, name=name, in_specs=[HBM] * (2 * n) + [SEM, SEM, ANY],
        out_shape=tuple(pltpu.HBM(a.shape, a.dtype) for a in (*srcs, *lands)), out_specs=tuple([HBM] * (2 * n)),
        input_output_aliases={a: a for a in range(2 * n)}, compiler_params=pltpu.CompilerParams(has_side_effects=EFFECT),
    )(*srcs, *lands, send_sems, recv_sems, after)
    return list(res[:n]), list(res[n:])


def pair_sum(grad, recv, shard_shape, axis, core, *, name):
    r, c = shard_shape
    tr = min(r, 1024)
    nr = r // tr
    if axis == 0:
        gspec = pl.BlockSpec((tr, c), lambda q, i, cref: ((2 * q + cref[0]) * nr + i, 0))
    else:
        gspec = pl.BlockSpec((tr, c), lambda q, i, cref: (i, 2 * q + cref[0]))
    rspec = pl.BlockSpec((None, tr, c), lambda q, i, cref: (q, i, 0))

    def body(c_ref, g_ref, r_ref, o_ref):
        o_ref[...] = (g_ref[...].astype(F32) + r_ref[...].astype(F32)).astype(o_ref.dtype)

    return pl.pallas_call(
        body, name=name,
        grid_spec=pltpu.PrefetchScalarGridSpec(num_scalar_prefetch=1, grid=(N_CHIP, nr), in_specs=[gspec, rspec],
                                               out_specs=rspec),
        out_shape=_sds((N_CHIP, r, c), recv.dtype), compiler_params=_params(("parallel", "parallel")),
    )(core, grad, recv)


def all_reduce_small(part, *, name):
    r, c = part.shape

    def body(p_ref, o_ref, gath_ref, send_sems, recv_sems):
        x, y, cc = _place()
        me = 4 * x + 2 * y + cc
        gath_ref[me] = p_ref[...]
        peers = [(1 - x if k & 4 else x, 1 - y if k & 2 else y, 1 - cc if k & 1 else cc) for k in range(1, N_DEV)]
        copies = [_remote(p_ref, gath_ref.at[me], send_sems, recv_sems, i, peer) for i, peer in enumerate(peers)]
        for cp in copies:
            cp.start()
        for i, (px, py, pc) in enumerate(peers):
            _remote(p_ref, gath_ref.at[4 * px + 2 * py + pc], send_sems, recv_sems, i, (x, y, cc)).wait_recv()
        for cp in copies:
            cp.wait_send()
        acc = gath_ref[0]
        for d in range(1, N_DEV):
            acc = acc + gath_ref[d]
        o_ref[...] = acc

    vm = pl.BlockSpec(memory_space=pltpu.VMEM)
    return pl.pallas_call(
        body, name=name, in_specs=[vm], out_specs=vm, out_shape=_sds((r, c), F32),
        scratch_shapes=[pltpu.VMEM((N_DEV, r, c), F32), pltpu.SemaphoreType.DMA((N_DEV - 1,)),
                        pltpu.SemaphoreType.DMA((N_DEV - 1,))],
    )(part)


def _adamw(w, g, m, v):
    m = ADAM_B1 * m + (1.0 - ADAM_B1) * g
    v = ADAM_B2 * v + (1.0 - ADAM_B2) * (g * g)
    m_hat = m / (1.0 - ADAM_B1 ** ADAM_STEP)
    v_hat = v / (1.0 - ADAM_B2 ** ADAM_STEP)
    delta = -ADAM_LR * (m_hat / (jnp.sqrt(v_hat) + ADAM_EPS) + ADAM_WD * w)
    return delta, m, v


def reduce_adam(own, landed, chip, w, m, v, layer, outs, *, name):
    _, r, c = w.shape
    tr = min(r, 256)
    first = outs is None

    def body(chip_ref, own_ref, l1_ref, l2_ref, l3_ref, w_ref, m_ref, v_ref, *rest):
        g_out, d_out, m_out, v_out = rest[-4:]
        g = own_ref[...].astype(F32) + l1_ref[...].astype(F32) + l2_ref[...].astype(F32) + l3_ref[...].astype(F32)
        d, mn, vn = _adamw(w_ref[...], g, m_ref[...], v_ref[...])
        g_out[...] = g
        d_out[...] = d
        m_out[...] = mn
        v_out[...] = vn

    spec = pl.BlockSpec((None, tr, c), lambda i, chip_ref: (layer, i, 0))

    def slot(step):
        return pl.BlockSpec((None, tr, c), lambda i, chip_ref: ((chip_ref[0] + step) % N_CHIP, i, 0))

    n_in = 8
    return pl.pallas_call(
        body, name=name,
        grid_spec=pltpu.PrefetchScalarGridSpec(
            num_scalar_prefetch=1, grid=(r // tr,),
            in_specs=[slot(0), slot(1), slot(2), slot(3), spec, spec, spec] + ([] if first else [ANY] * 4),
            out_specs=[spec] * 4),
        out_shape=[_sds(w.shape, F32)] * 4, input_output_aliases={} if first else {n_in + i: i for i in range(4)},
        compiler_params=_params(("parallel",)),
    )(chip, own, landed, landed, landed, w, m, v, *([] if first else outs))


def adam_small(g, w, m, v, *, name):
    def body(g_ref, w_ref, m_ref, v_ref, d_out, m_out, v_out):
        d, mn, vn = _adamw(w_ref[...], g_ref[...], m_ref[...], v_ref[...])
        d_out[...] = d
        m_out[...] = mn
        v_out[...] = vn

    return pl.pallas_call(body, name=name, out_shape=[_sds(w.shape, F32)] * 3)(g, w, m, v)


BIG = ("w_in", "w_out_conv", "w_out_attn", "w_merge", "w_up", "w_down")
GATHERED = BIG + ("conv_w",)
BIG_AXIS = {"w_in": 1, "w_out_conv": 0, "w_out_attn": 0, "w_merge": 0, "w_up": 1, "w_down": 0, "conv_w": 1}
FIRST_GATHER_GROUPS = (("w_in", "conv_w"), ("w_out_conv", "w_out_attn", "w_merge"), ("w_up", "w_down"))
LAST_REDUCE_GROUPS = (("w_down", "w_up"), ("w_merge", "w_out_conv", "w_out_attn"), ("w_in",))


class _Gather:
    def __init__(self, weights, keys, layer, me, after, tag):
        self.keys, self.tag = keys, tag
        self.shapes = [weights[k].shape[1:] for k in keys]
        self.axes = [BIG_AXIS[k] for k in keys]
        own = [cast_into_full(weights[k], layer, ax, me, F32 if k == "conv_w" else MXU_DTYPE, name="cast_" + k)
               for k, ax in zip(keys, self.axes)]
        self.send, self.recv, self.fulls, self.token = gather_start(own, after, self.shapes, self.axes,
                                                                    name="gather_start_" + tag)

    def pass_on(self, after):
        self.send, self.recv, self.fulls, self.token = gather_pass(self.send, self.recv, self.fulls, after, self.shapes,
                                                                   self.axes, name="gather_pass_" + self.tag)
        return self.token

    def wait(self, after):
        fulls = gather_wait(self.send, self.recv, self.fulls, after, self.shapes, self.axes, name="gather_wait_" + self.tag)
        return dict(zip(self.keys, fulls))


class _Reduce:
    def __init__(self, weights, keys, tag):
        self.keys, self.tag = keys, tag
        self.shapes = [weights[k].shape[1:] for k in keys]
        self.axes = [BIG_AXIS[k] for k in keys]
        self.pair_shapes = [(N_CHIP, *shp) for shp in self.shapes]

    def _pair(self, i, z, ss, rs):
        return _pair_copies(i, z, self.shapes, self.axes, ss, rs)

    def begin(self, grads, after):
        self.send, self.recv, self.src, self.land, self.token = exchange_start(
            [grads[k] for k in self.keys], after, self.pair_shapes, self._pair, N_CHIP, name="rs_pair_start_" + self.tag)
        return self.token

    def middle(self, after, core):
        both = lambda i, z, ss, rs: (self._pair(i, z, ss, rs),) * 2
        grads, from_sibling = exchange_wait(self.send, self.recv, self.src, self.land, after, both,
                                            name="rs_pair_wait_" + self.tag)
        sums = [pair_sum(g, rcv, shp, ax, core, name="pair_sum_" + k)
                for k, g, rcv, shp, ax in zip(self.keys, grads, from_sibling, self.shapes, self.axes)]
        self.send, self.recv, self.src, self.land, self.token = exchange_start(
            sums, after, self.pair_shapes, _chip_sends, 3, name="rs_chips_start_" + self.tag)
        return self.token

    def end(self, after):
        both = lambda i, z, ss, rs: (_chip_sends(i, z, ss, rs), _chip_arrivals(i, z, ss, rs))
        sums, landed = exchange_wait(self.send, self.recv, self.src, self.land, after, both,
                                     name="rs_chips_wait_" + self.tag)
        return list(zip(self.keys, sums, landed))
SMALL = (("norm_mix_pre", D_MODEL), ("gate_bias", 2 * D_MODEL), ("norm_mix_post", D_MODEL), ("norm_mlp_pre", D_MODEL),
         ("norm_mlp_post", D_MODEL), ("q_norm", HEAD_DIM), ("k_norm", HEAD_DIM))
SMALL_WIDTH = sum(w for _, w in SMALL)
WEIGHTS = ("norm_mix_pre", "w_in", "gate_bias", "conv_w", "q_norm", "k_norm", "w_out_conv", "w_out_attn", "w_merge",
           "norm_mix_post", "norm_mlp_pre", "w_up", "w_down", "norm_mlp_post")


def kernel(x, norm_mix_pre, w_in, gate_bias, conv_w, q_norm, k_norm, w_out_conv, w_out_attn, w_merge, norm_mix_post, norm_mlp_pre, w_up, w_down, norm_mlp_post, loss_target, m_norm_mix_pre, m_w_in, m_gate_bias, m_conv_w, m_q_norm, m_k_norm, m_w_out_conv, m_w_out_attn, m_w_merge, m_norm_mix_post, m_norm_mlp_pre, m_w_up, m_w_down, m_norm_mlp_post, v_norm_mix_pre, v_w_in, v_gate_bias, v_conv_w, v_q_norm, v_k_norm, v_w_out_conv, v_w_out_attn, v_w_merge, v_norm_mix_post, v_norm_mlp_pre, v_w_up, v_w_down, v_norm_mlp_post):
    w = dict(norm_mix_pre=norm_mix_pre, w_in=w_in, gate_bias=gate_bias, conv_w=conv_w, q_norm=q_norm, k_norm=k_norm,
             w_out_conv=w_out_conv, w_out_attn=w_out_attn, w_merge=w_merge, norm_mix_post=norm_mix_post,
             norm_mlp_pre=norm_mlp_pre, w_up=w_up, w_down=w_down, norm_mlp_post=norm_mlp_post)
    mom = dict(norm_mix_pre=m_norm_mix_pre, w_in=m_w_in, gate_bias=m_gate_bias, conv_w=m_conv_w, q_norm=m_q_norm,
               k_norm=m_k_norm, w_out_conv=m_w_out_conv, w_out_attn=m_w_out_attn, w_merge=m_w_merge,
               norm_mix_post=m_norm_mix_post, norm_mlp_pre=m_norm_mlp_pre, w_up=m_w_up, w_down=m_w_down,
               norm_mlp_post=m_norm_mlp_post)
    var = dict(norm_mix_pre=v_norm_mix_pre, w_in=v_w_in, gate_bias=v_gate_bias, conv_w=v_conv_w, q_norm=v_q_norm,
               k_norm=v_k_norm, w_out_conv=v_w_out_conv, w_out_attn=v_w_out_attn, w_merge=v_w_merge,
               norm_mix_post=v_norm_mix_post, norm_mlp_pre=v_norm_mlp_pre, w_up=v_w_up, w_down=v_w_down,
               norm_mlp_post=v_norm_mlp_post)
    depth = w_in.shape[0]
    s = x.shape[1]
    xs = x.reshape(s, D_MODEL)
    target = loss_target.reshape(s, D_MODEL)
    x_idx, y_idx, c_idx = _place()
    as_operand = lambda i: jnp.reshape(i, (1,)).astype(jnp.int32)
    core, chip, me = as_operand(c_idx), as_operand(2 * x_idx + y_idx), as_operand(4 * x_idx + 2 * y_idx + c_idx)
    cos_t, sin_t = rope_tables(s)

    def vec(name, l):
        return w[name][l].reshape(1, -1)

    saved = []
    h = rms_fwd(xs, vec("norm_mix_pre", 0), name="rms_first")
    first, after = [], h
    for i, keys in enumerate(FIRST_GATHER_GROUPS):
        first.append(_Gather(w, keys, 0, me, after, f"0{'abc'[i]}"))
        after = first[-1].token
    full = first[0].wait(first[0].pass_on(after))
    layers = []
    for l in range(depth):
        layers.append(full)
        nxt = l + 1 < depth
        if nxt:
            coming = _Gather(w, GATHERED, l + 1, me, full["w_in"], str(l + 1))
        z = matmul(h, full["w_in"], mode="nn", m=s, n=IN_WIDTH, k=D_MODEL, out_dtypes=[ACT_DTYPE], name="mm_in",
                   after=[coming.token] if nxt else [])
        t = conv_fwd(z, full["conv_w"], name="conv_fwd")
        qp, kp, vb = qk_prep_fwd(z, vec("q_norm", l), vec("k_norm", l), cos_t, sin_t, name="qk_fwd")
        o, lse = attn_fwd(qp, kp, vb, name="attn_fwd", after=[first[1].pass_on(qp)] if l == 0 else [])
        if l == 0:
            full.update(first[1].wait(o))
        ya = matmul(t, full["w_out_conv"], mode="nn", m=s, n=D_MODEL, k=CONV_WIDTH, out_dtypes=[ACT_DTYPE], name="mm_out_conv",
                    after=[first[2].pass_on(o)] if l == 0 else [])
        yb = matmul(o, full["w_out_attn"], mode="nn", m=s, n=D_MODEL, k=ATTN_WIDTH, out_dtypes=[ACT_DTYPE], name="mm_out_attn")
        mi = gate_fwd(ya, yb, z, vec("gate_bias", l), name="gate_fwd")
        mixed = matmul(mi, full["w_merge"], mode="nn", m=s, n=D_MODEL, k=D_MODEL, out_dtypes=[ACT_DTYPE], name="mm_merge")
        if l == 0:
            full.update(first[2].wait(mixed))
        x_mid, h2 = rms_residual_fwd(xs, mixed, vec("norm_mix_post", l), vec("norm_mlp_pre", l), name="res_mix")
        act, r = matmul(h2, full["w_up"], mode="nn", m=s, n=D_FF, k=D_MODEL, out_dtypes=[MXU_DTYPE, MXU_DTYPE],
                        name="mm_up", epilogue=lambda acc: (acc, jnp.square(jnp.maximum(acc, 0.0))))
        f = matmul(r, full["w_down"], mode="nn", m=s, n=D_MODEL, k=D_FF, tk=D_FF // 2, vmem=VMEM_LIMIT_DEEP_K,
                   out_dtypes=[ACT_DTYPE], name="mm_down",
                   after=[coming.pass_on(r)] if nxt and l > 0 else [])
        if nxt and l == 0:
            coming.pass_on(f)
        g_next = vec("norm_mix_pre", l + 1) if nxt else None
        x_out, h_next = rms_residual_fwd(x_mid, f, vec("norm_mlp_post", l), g_next, name="res_mlp" if nxt else "res_last")
        saved.append(dict(x_in=xs, h=h, z=z, t=t, qp=qp, kp=kp, vb=vb, o=o, lse=lse, ya=ya, yb=yb, mi=mi, mixed=mixed,
                          x_mid=x_mid, h2=h2, act=act, r=r, f=f))
        if nxt:
            full = coming.wait(x_out)
        xs, h = x_out, h_next

    dx, loss_part = loss_and_grad(xs, target, name="loss")
    loss = lax.psum(jnp.sum(loss_part), ("x", "y", "c"))

    small_rows = [None] * depth
    conv_rows = [None] * depth
    out_g, out_d, out_m, out_v = {}, {}, {}, {}
    big_outs = {k: None for k in BIG}
    pending = None
    last = []
    handed_down = None

    def finish(reduction, layer, after):
        for k, own, landed in reduction.end(after):
            big_outs[k] = reduce_adam(own, landed, chip, w[k], mom[k], var[k], layer, big_outs[k], name="adam_" + k)

    for l in reversed(range(depth)):
        sv, full = saved[l], layers[l]
        grads = {}
        groups = l == 0

        def wgrad(key, lhs, rhs, m, n, after=()):
            grads[key] = matmul(lhs, rhs, mode="tn", m=m, n=n, k=s, tk=s, out_dtypes=[COMM_DTYPE], name="wg_" + key,
                                after=after)

        def begin_group(i, after):
            last.append(_Reduce(w, LAST_REDUCE_GROUPS[i], f"{l}{'abc'[i]}"))
            return [last[i].begin(grads, after)]

        if handed_down is None:
            handed_down = rms_bwd(sv["f"], vec("norm_mlp_post", l), dx, None, out_dtype=MXU_DTYPE, name="rmsb_mlp_post")
        df, dg_mlp_post = handed_down
        da = matmul(df, full["w_down"], mode="nt", m=s, n=D_FF, k=D_MODEL, out_dtypes=[MXU_DTYPE], name="mm_d_down",
                    extra=(sv["act"],), epilogue=lambda acc, a: (acc * (2.0 * jnp.maximum(a.astype(F32), 0.0)),),
                    after=[pending.token] if pending else [])
        tokens = [pending.middle(da, core)] if pending else []
        wgrad("w_down", sv["r"], df, D_FF, D_MODEL)
        dh2 = matmul(da, full["w_up"], mode="nt", m=s, n=D_MODEL, k=D_FF, tk=D_FF // 2, vmem=VMEM_LIMIT_DEEP_K,
                     out_dtypes=[ACT_DTYPE], name="mm_d_up", after=tokens)
        wgrad("w_up", sv["h2"], da, D_MODEL, D_FF)
        tokens = begin_group(0, dh2) if groups else []
        dx_mid, dmixed, dg_mlp_pre, dg_mix_post = rms_bwd_pair(
            sv["x_mid"], vec("norm_mlp_pre", l), dh2, dx, sv["mixed"], vec("norm_mix_post", l), name="rmsb_mlp_pre_mix_post")
        dmi = matmul(dmixed, full["w_merge"], mode="nt", m=s, n=D_MODEL, k=D_MODEL, out_dtypes=[ACT_DTYPE], name="mm_d_merge",
                     after=tokens)
        tokens = [last[0].middle(dmi, core)] if groups else []
        wgrad("w_merge", sv["mi"], dmixed, D_MODEL, D_MODEL, after=tokens)
        dya, dyb, dga, dgb, dba, dbb = gate_bwd(dmi, sv["ya"], sv["yb"], sv["z"], vec("gate_bias", l), name="gate_bwd")
        wgrad("w_out_conv", sv["t"], dya, CONV_WIDTH, D_MODEL)
        wgrad("w_out_attn", sv["o"], dyb, ATTN_WIDTH, D_MODEL)
        tokens = begin_group(1, dyb) if groups else []
        dt = matmul(dya, full["w_out_conv"], mode="nt", m=s, n=CONV_WIDTH, k=D_MODEL, out_dtypes=[ACT_DTYPE],
                    name="mm_d_out_conv", after=tokens)
        do = matmul(dyb, full["w_out_attn"], mode="nt", m=s, n=ATTN_WIDTH, k=D_MODEL, out_dtypes=[MXU_DTYPE],
                    name="mm_d_out_attn")
        tokens = [last[1].middle(do, core)] if groups else []
        dcb, dcc, dci, dconv_w = conv_bwd(dt, sv["z"], full["conv_w"], name="conv_bwd", after=tokens)
        dqp, dkp, dv = attn_bwd(sv["qp"], sv["kp"], sv["vb"], do, sv["o"], sv["lse"], name="attn_bwd")
        dq, dk, dqn, dkn = qk_prep_bwd(dqp, dkp, sv["z"], vec("q_norm", l), vec("k_norm", l), cos_t, sin_t, name="qk_bwd")
        dz = jnp.concatenate([dcb, dcc, dci, dq, dk, dv, dga, dgb], axis=1)
        wgrad("w_in", sv["h"], dz, D_MODEL, IN_WIDTH)
        tokens = begin_group(2, dz) if groups else []
        dh = matmul(dz, full["w_in"], mode="nt", m=s, n=D_MODEL, k=IN_WIDTH, tk=IN_WIDTH // 4, out_dtypes=[ACT_DTYPE],
                    name="mm_d_in", after=tokens)
        if l > 0:
            dx, df_below, dg_mix_pre, dg_post_below = rms_bwd_pair(
                sv["x_in"], vec("norm_mix_pre", l), dh, dx_mid, saved[l - 1]["f"], vec("norm_mlp_post", l - 1),
                name="rmsb_mix_pre_mlp_post")
            handed_down = (df_below, dg_post_below)
        else:
            dx, dg_mix_pre = rms_bwd(sv["x_in"], vec("norm_mix_pre", l), dh, dx_mid, out_dtype=F32, name="rmsb_mix_pre")
        small_rows[l] = jnp.concatenate([dg_mix_pre, dba, dbb, dg_mix_post, dg_mlp_pre, dg_mlp_post, dqn, dkn], axis=1)
        conv_rows[l] = dconv_w.reshape(1, 3 * CONV_WIDTH)
        if pending:
            finish(pending, l + 1, dx)
        pending = None
        if not groups:
            pending = _Reduce(w, BIG, str(l))
            pending.begin(grads, dx)

    grad_x = dx.reshape(1, s, D_MODEL)

    small_part = jnp.concatenate([jnp.concatenate(small_rows, axis=0), jnp.concatenate(conv_rows, axis=0)], axis=1)
    small_sum = all_reduce_small(small_part, name="allreduce_small")
    pack = lambda src: jnp.concatenate([src[k].reshape(depth, wd) for k, wd in SMALL], axis=1)
    g_small = small_sum[:, :SMALL_WIDTH]
    d_small, m_small, v_small = adam_small(g_small, pack(w), pack(mom), pack(var), name="adam_small")
    off = 0
    for k, wd in SMALL:
        for dst, src in ((out_g, g_small), (out_d, d_small), (out_m, m_small), (out_v, v_small)):
            dst[k] = src[:, off:off + wd]
        off += wd
    cshard = CONV_WIDTH // N_DEV
    g_conv = lax.dynamic_slice_in_dim(small_sum[:, SMALL_WIDTH:].reshape(depth, 3, CONV_WIDTH), me[0] * cshard, cshard,
                                      axis=2).reshape(depth, 3 * cshard)
    flat = lambda a: a.reshape(depth, 3 * cshard)
    d_conv, m_conv, v_conv = adam_small(g_conv, flat(conv_w), flat(m_conv_w), flat(v_conv_w), name="adam_conv")
    for dst, src in ((out_g, g_conv), (out_d, d_conv), (out_m, m_conv), (out_v, v_conv)):
        dst["conv_w"] = src.reshape(depth, 3, cshard)

    token = last[2].middle(d_small, core)
    for reduction in last:
        finish(reduction, 0, token)
    for k in BIG:
        out_g[k], out_d[k], out_m[k], out_v[k] = big_outs[k]

    return (loss, grad_x, *[out_g[k] for k in WEIGHTS], *[out_d[k] for k in WEIGHTS], *[out_m[k] for k in WEIGHTS],
            *[out_v[k] for k in WEIGHTS])
```

```python
import math

import jax
import jax.numpy as jnp
from jax import lax
from jax.experimental import pallas as pl
from jax.experimental.pallas import tpu as pltpu

F32 = jnp.float32
MXU_DTYPE = jnp.bfloat16
COMM_DTYPE = jnp.bfloat16
ACT_DTYPE = jnp.bfloat16

D_MODEL = 2048
HEAD_DIM = 128
N_Q_HEADS = 16
N_KV_HEADS = 4
GROUP = N_Q_HEADS // N_KV_HEADS
ATTN_WIDTH = N_Q_HEADS * HEAD_DIM
KV_WIDTH = N_KV_HEADS * HEAD_DIM
CONV_WIDTH = D_MODEL
D_FF = 4 * D_MODEL
GRID_W = 64
ROPE_THETA = 10000.0
RMS_EPS = 1e-6
IN_WIDTH = 3 * CONV_WIDTH + ATTN_WIDTH + 2 * KV_WIDTH + 2 * D_MODEL
OFF_CB, OFF_CC, OFF_CI = 0, CONV_WIDTH, 2 * CONV_WIDTH
OFF_Q = 3 * CONV_WIDTH
OFF_K = OFF_Q + ATTN_WIDTH
OFF_V = OFF_K + KV_WIDTH
OFF_GA = OFF_V + KV_WIDTH
OFF_GB = OFF_GA + D_MODEL
ATTN_SCALE = 1.0 / math.sqrt(HEAD_DIM)

ADAM_LR, ADAM_B1, ADAM_B2, ADAM_EPS, ADAM_WD, ADAM_STEP = 0.001, 0.9, 0.999, 1e-08, 0.01, 10

N_DEV = 8
N_CHIP = 4
LANE = 128
SUBLANE = 8
VMEM_LIMIT = 48 * 1024 * 1024
VMEM_LIMIT_DEEP_K = 56 * 1024 * 1024
MESH = pl.DeviceIdType.MESH
ANY = pl.BlockSpec(memory_space=pl.ANY)


def _sds(shape, dtype):
    return jax.ShapeDtypeStruct(tuple(shape), dtype)


def _params(sem, vmem=VMEM_LIMIT):
    return pltpu.CompilerParams(dimension_semantics=sem, vmem_limit_bytes=vmem)


def _rows(ts, w, col=0):
    return pl.BlockSpec((ts, w), lambda i: (i, col))


def _fixed(shape):
    return pl.BlockSpec(shape, lambda *_: (0,) * len(shape))


def _fold8(v):
    ts, w = v.shape
    return v.reshape(ts // SUBLANE, SUBLANE, w).sum(axis=0)


def matmul(a, b, *, mode, m, n, k, name, out_dtypes, tm=1024, tn=1024, tk=2048, epilogue=None, extra=(), after=(),
           vmem=VMEM_LIMIT):
    tm, tn, tk = min(tm, m), min(tn, n), min(tk, k)
    nm, nn, nk = m // tm, n // tn, k // tk
    assert nm * tm == m and nn * tn == n and nk * tk == k, (name, m, n, k, tm, tn, tk)
    if mode == "tn":
        a_spec = pl.BlockSpec((tk, tm), lambda i, j, kk: (kk, i))
        dims = (((0,), (0,)), ((), ()))
    else:
        a_spec = pl.BlockSpec((tm, tk), lambda i, j, kk: (i, kk))
        dims = (((1,), (1 if mode == "nt" else 0,)), ((), ()))
    if mode == "nt":
        b_spec = pl.BlockSpec((tn, tk), lambda i, j, kk: (j, kk))
    else:
        b_spec = pl.BlockSpec((tk, tn), lambda i, j, kk: (kk, j))
    tile = pl.BlockSpec((tm, tn), lambda i, j, kk: (i, j))
    n_out, n_extra, n_after = len(out_dtypes), len(extra), len(after)
    if epilogue is None:
        epilogue = lambda acc: (acc,)

    def body(a_ref, b_ref, *rest):
        extra_refs = rest[:n_extra]
        outs = rest[n_extra + n_after:][:n_out]
        part = lax.dot_general(a_ref[...].astype(MXU_DTYPE), b_ref[...].astype(MXU_DTYPE), dims,
                               preferred_element_type=F32)

        def finish(acc):
            for o_ref, val in zip(outs, epilogue(acc, *[r[...] for r in extra_refs])):
                o_ref[...] = val.astype(o_ref.dtype)

        if nk == 1:
            finish(part)
        else:
            acc_ref = rest[-1]
            kk = pl.program_id(2)

            @pl.when(kk == 0)
            def _():
                acc_ref[...] = part

            @pl.when(kk > 0)
            def _():
                acc_ref[...] += part

            @pl.when(kk == nk - 1)
            def _():
                finish(acc_ref[...])

    res = pl.pallas_call(
        body, name=name, grid=(nm, nn, nk), in_specs=[a_spec, b_spec] + [tile] * n_extra + [ANY] * n_after,
        out_specs=[tile] * n_out, out_shape=[_sds((m, n), dt) for dt in out_dtypes],
        scratch_shapes=[pltpu.VMEM((tm, tn), F32)] if nk > 1 else [],
        compiler_params=_params(("parallel", "parallel", "arbitrary"), vmem),
    )(a, b, *extra, *after)
    return res[0] if n_out == 1 else res


def _rstd(x):
    return lax.rsqrt(jnp.mean(x * x, axis=-1, keepdims=True) + RMS_EPS)


def _rms_bwd(x, g, dy):
    rstd = _rstd(x)
    xh = x * rstd
    gy = dy * g
    dx = rstd * (gy - xh * jnp.mean(gy * xh, axis=-1, keepdims=True))
    return dx, dy * xh


def rms_fwd(x, g, *, name):
    s = x.shape[0]
    ts = min(s, 512)

    def body(x_ref, g_ref, h_ref):
        xv = x_ref[...]
        h_ref[...] = (xv * _rstd(xv) * g_ref[...]).astype(h_ref.dtype)

    return pl.pallas_call(
        body, name=name, grid=(s // ts,), in_specs=[_rows(ts, D_MODEL), _fixed((1, D_MODEL))],
        out_specs=_rows(ts, D_MODEL), out_shape=_sds((s, D_MODEL), MXU_DTYPE), compiler_params=_params(("parallel",)),
    )(x, g)


def rms_residual_fwd(x, y, g_post, g_next, *, name):
    s = x.shape[0]
    ts = min(s, 512)
    with_next = g_next is not None

    def body(x_ref, y_ref, gp_ref, *rest):
        yv = y_ref[...].astype(F32)
        xn = x_ref[...] + yv * _rstd(yv) * gp_ref[...]
        if with_next:
            gn_ref, xo_ref, h_ref = rest
            h_ref[...] = (xn * _rstd(xn) * gn_ref[...]).astype(h_ref.dtype)
        else:
            (xo_ref,) = rest
        xo_ref[...] = xn

    gspec = _fixed((1, D_MODEL))
    res = pl.pallas_call(
        body, name=name, grid=(s // ts,),
        in_specs=[_rows(ts, D_MODEL), _rows(ts, D_MODEL), gspec] + [gspec] * with_next,
        out_specs=[_rows(ts, D_MODEL)] * (1 + with_next),
        out_shape=[_sds((s, D_MODEL), F32)] + [_sds((s, D_MODEL), MXU_DTYPE)] * with_next,
        compiler_params=_params(("parallel",)),
    )(x, y, g_post, *([g_next] if with_next else []))
    return (res[0], res[1]) if with_next else (res[0], None)


def rms_bwd(x, g, dy, residual, *, out_dtype, name):
    s = x.shape[0]
    ts = min(s, 256)
    nt = s // ts
    with_res = residual is not None

    def body(x_ref, g_ref, dy_ref, *rest):
        dx_ref, dg_ref, acc_ref = rest[-3:]
        dx, dg_rows = _rms_bwd(x_ref[...].astype(F32), g_ref[...], dy_ref[...].astype(F32))
        if with_res:
            dx = dx + rest[0][...]
        dx_ref[...] = dx.astype(dx_ref.dtype)
        i = pl.program_id(0)

        @pl.when(i == 0)
        def _():
            acc_ref[...] = jnp.zeros_like(acc_ref)

        acc_ref[...] += _fold8(dg_rows)

        @pl.when(i == nt - 1)
        def _():
            dg_ref[...] = acc_ref[...].sum(axis=0, keepdims=True)

    return pl.pallas_call(
        body, name=name, grid=(nt,),
        in_specs=[_rows(ts, D_MODEL), _fixed((1, D_MODEL)), _rows(ts, D_MODEL)] + [_rows(ts, D_MODEL)] * with_res,
        out_specs=[_rows(ts, D_MODEL), _fixed((1, D_MODEL))],
        out_shape=[_sds((s, D_MODEL), out_dtype), _sds((1, D_MODEL), F32)],
        scratch_shapes=[pltpu.VMEM((SUBLANE, D_MODEL), F32)], compiler_params=_params(("arbitrary",)),
    )(x, g, dy, *([residual] if with_res else []))


def rms_bwd_pair(xa, ga, dya, residual, xb, gb, *, name):
    s = xa.shape[0]
    ts = min(s, 256)
    nt = s // ts

    def body(xa_ref, ga_ref, dya_ref, r_ref, xb_ref, gb_ref, d1_ref, d2_ref, dga_ref, dgb_ref, acc_a, acc_b):
        d1, rows_a = _rms_bwd(xa_ref[...].astype(F32), ga_ref[...], dya_ref[...].astype(F32))
        d1 = d1 + r_ref[...]
        d1_ref[...] = d1
        d2, rows_b = _rms_bwd(xb_ref[...].astype(F32), gb_ref[...], d1)
        d2_ref[...] = d2.astype(d2_ref.dtype)
        i = pl.program_id(0)

        @pl.when(i == 0)
        def _():
            acc_a[...] = jnp.zeros_like(acc_a)
            acc_b[...] = jnp.zeros_like(acc_b)

        acc_a[...] += _fold8(rows_a)
        acc_b[...] += _fold8(rows_b)

        @pl.when(i == nt - 1)
        def _():
            dga_ref[...] = acc_a[...].sum(axis=0, keepdims=True)
            dgb_ref[...] = acc_b[...].sum(axis=0, keepdims=True)

    row, gain = _rows(ts, D_MODEL), _fixed((1, D_MODEL))
    return pl.pallas_call(
        body, name=name, grid=(nt,), in_specs=[row, gain, row, row, row, gain], out_specs=[row, row, gain, gain],
        out_shape=[_sds((s, D_MODEL), F32), _sds((s, D_MODEL), MXU_DTYPE), _sds((1, D_MODEL), F32), _sds((1, D_MODEL), F32)],
        scratch_shapes=[pltpu.VMEM((SUBLANE, D_MODEL), F32)] * 2, compiler_params=_params(("arbitrary",)),
    )(xa, ga, dya, residual, xb, gb)


def loss_and_grad(y, target, *, name):
    s = y.shape[0]
    ts = min(s, 512)
    nt = s // ts

    def body(y_ref, t_ref, dy_ref, part_ref):
        e = y_ref[...] - t_ref[...]
        dy_ref[...] = e * (1.0 / D_MODEL)
        sq = _fold8(e * e)
        lanes = sq[:, 0:LANE]
        for j in range(1, D_MODEL // LANE):
            lanes = lanes + sq[:, j * LANE:(j + 1) * LANE]
        i = pl.program_id(0)

        @pl.when(i == 0)
        def _():
            part_ref[...] = jnp.zeros_like(part_ref)

        part_ref[...] += lanes * (0.5 / D_MODEL)

    return pl.pallas_call(
        body, name=name, grid=(nt,), in_specs=[_rows(ts, D_MODEL), _rows(ts, D_MODEL)],
        out_specs=[_rows(ts, D_MODEL), _fixed((SUBLANE, LANE))],
        out_shape=[_sds((s, D_MODEL), F32), _sds((SUBLANE, LANE), F32)], compiler_params=_params(("arbitrary",)),
    )(y, target)


CONV_TC = 2 * LANE


def _conv_taps(u, s):
    row = lax.broadcasted_iota(jnp.int32, u.shape, 0)
    prev = jnp.where(row == 0, 0.0, pltpu.roll(u, 1, 0))
    nxt = jnp.where(row == s - 1, 0.0, pltpu.roll(u, s - 1, 0))
    return prev, nxt


def _zcol(s, off):
    return pl.BlockSpec((s, CONV_TC), lambda j: (0, off // CONV_TC + j))


def conv_fwd(z, w3, *, name):
    s = z.shape[0]

    def body(cb_ref, cc_ref, ci_ref, w_ref, t_ref):
        u = cc_ref[...].astype(F32) * ci_ref[...].astype(F32)
        prev, nxt = _conv_taps(u, s)
        w = w_ref[...]
        conv = w[0:1] * prev + w[1:2] * u + w[2:3] * nxt
        t_ref[...] = (cb_ref[...].astype(F32) * conv).astype(t_ref.dtype)

    return pl.pallas_call(
        body, name=name, grid=(CONV_WIDTH // CONV_TC,),
        in_specs=[_zcol(s, OFF_CB), _zcol(s, OFF_CC), _zcol(s, OFF_CI), pl.BlockSpec((3, CONV_TC), lambda j: (0, j))],
        out_specs=pl.BlockSpec((s, CONV_TC), lambda j: (0, j)), out_shape=_sds((s, CONV_WIDTH), MXU_DTYPE),
        compiler_params=_params(("parallel",)),
    )(z, z, z, w3)


def conv_bwd(dt, z, w3, *, name, after=()):
    s = z.shape[0]

    def body(dt_ref, cb_ref, cc_ref, ci_ref, w_ref, *rest):
        dcb_ref, dcc_ref, dci_ref, dw_ref = rest[-4:]
        cc, ci = cc_ref[...].astype(F32), ci_ref[...].astype(F32)
        u = cc * ci
        prev, nxt = _conv_taps(u, s)
        w = w_ref[...]
        dtv = dt_ref[...].astype(F32)
        dcb_ref[...] = (dtv * (w[0:1] * prev + w[1:2] * u + w[2:3] * nxt)).astype(dcb_ref.dtype)
        dconv = dtv * cb_ref[...].astype(F32)
        dprev, dnxt = _conv_taps(dconv, s)
        du = w[0:1] * dnxt + w[1:2] * dconv + w[2:3] * dprev
        dcc_ref[...] = (du * ci).astype(dcc_ref.dtype)
        dci_ref[...] = (du * cc).astype(dci_ref.dtype)
        dw_ref[0:1, :] = jnp.sum(dconv * prev, axis=0, keepdims=True)
        dw_ref[1:2, :] = jnp.sum(dconv * u, axis=0, keepdims=True)
        dw_ref[2:3, :] = jnp.sum(dconv * nxt, axis=0, keepdims=True)

    col = pl.BlockSpec((s, CONV_TC), lambda j: (0, j))
    wspec = pl.BlockSpec((3, CONV_TC), lambda j: (0, j))
    return pl.pallas_call(
        body, name=name, grid=(CONV_WIDTH // CONV_TC,),
        in_specs=[col, _zcol(s, OFF_CB), _zcol(s, OFF_CC), _zcol(s, OFF_CI), wspec] + [ANY] * len(after),
        out_specs=[col, col, col, wspec],
        out_shape=[_sds((s, CONV_WIDTH), MXU_DTYPE)] * 3 + [_sds((3, CONV_WIDTH), F32)],
        compiler_params=_params(("parallel",)),
    )(dt, z, z, z, w3, *after)


def rope_tables(s):
    n_freq = HEAD_DIM // 4
    t = jnp.arange(s, dtype=jnp.int32)
    inv_freq = ROPE_THETA ** (-jnp.arange(0, HEAD_DIM // 2, 2, dtype=F32) / (HEAD_DIM // 2))
    ang_r = (t // GRID_W).astype(F32)[:, None] * inv_freq
    ang_c = (t % GRID_W).astype(F32)[:, None] * inv_freq
    cos_t = jnp.concatenate([jnp.cos(ang_r)] * 2 + [jnp.cos(ang_c)] * 2, axis=1)
    sin_t = jnp.concatenate([-jnp.sin(ang_r), jnp.sin(ang_r), -jnp.sin(ang_c), jnp.sin(ang_c)], axis=1)
    assert cos_t.shape == (s, 4 * n_freq)
    return cos_t, sin_t


def _swap_halves(v):
    lane = lax.broadcasted_iota(jnp.int32, v.shape, 1)
    return jnp.where(lane % 64 < 32, pltpu.roll(v, HEAD_DIM - 32, 1), pltpu.roll(v, 32, 1))


def qk_prep_fwd(z, qn, kn, cos_t, sin_t, *, name):
    s = z.shape[0]
    ts = min(s, 512)

    def body(q_ref, k_ref, v_ref, qn_ref, kn_ref, c_ref, s_ref, qo_ref, ko_ref, vo_ref):
        cs, sn = c_ref[...], s_ref[...]

        def head(x, g, scale):
            n = x * _rstd(x) * g
            return (n * cs + _swap_halves(n) * sn) * scale

        for h in range(N_Q_HEADS):
            sl = slice(h * HEAD_DIM, (h + 1) * HEAD_DIM)
            qo_ref[:, sl] = head(q_ref[:, sl].astype(F32), qn_ref[...], ATTN_SCALE).astype(qo_ref.dtype)
        for h in range(N_KV_HEADS):
            sl = slice(h * HEAD_DIM, (h + 1) * HEAD_DIM)
            ko_ref[:, sl] = head(k_ref[:, sl].astype(F32), kn_ref[...], 1.0).astype(ko_ref.dtype)
        vo_ref[...] = v_ref[...].astype(vo_ref.dtype)

    tab = _rows(ts, HEAD_DIM)
    gsp = _fixed((1, HEAD_DIM))
    return pl.pallas_call(
        body, name=name, grid=(s // ts,),
        in_specs=[_rows(ts, ATTN_WIDTH, OFF_Q // ATTN_WIDTH), _rows(ts, KV_WIDTH, OFF_K // KV_WIDTH),
                  _rows(ts, KV_WIDTH, OFF_V // KV_WIDTH), gsp, gsp, tab, tab],
        out_specs=[_rows(ts, ATTN_WIDTH), _rows(ts, KV_WIDTH), _rows(ts, KV_WIDTH)],
        out_shape=[_sds((s, ATTN_WIDTH), MXU_DTYPE), _sds((s, KV_WIDTH), MXU_DTYPE), _sds((s, KV_WIDTH), MXU_DTYPE)],
        compiler_params=_params(("parallel",)),
    )(z, z, z, qn, kn, cos_t, sin_t)


def qk_prep_bwd(dqp, dkp, z, qn, kn, cos_t, sin_t, *, name):
    s = z.shape[0]
    ts = min(s, 512)
    nt = s // ts

    def body(dq_ref, dk_ref, q_ref, k_ref, qn_ref, kn_ref, c_ref, s_ref, dqo_ref, dko_ref, dqn_ref, dkn_ref,
             qacc_ref, kacc_ref):
        cs, sn = c_ref[...], s_ref[...]
        i = pl.program_id(0)

        @pl.when(i == 0)
        def _():
            qacc_ref[...] = jnp.zeros_like(qacc_ref)
            kacc_ref[...] = jnp.zeros_like(kacc_ref)

        def head(x, g, dout, scale):
            d = dout.astype(F32) * scale
            dn = d * cs + _swap_halves(d * sn)
            return _rms_bwd(x, g, dn)

        qacc = jnp.zeros((SUBLANE, HEAD_DIM), F32)
        for h in range(N_Q_HEADS):
            sl = slice(h * HEAD_DIM, (h + 1) * HEAD_DIM)
            dx, dg_rows = head(q_ref[:, sl].astype(F32), qn_ref[...], dq_ref[:, sl], ATTN_SCALE)
            dqo_ref[:, sl] = dx.astype(dqo_ref.dtype)
            qacc = qacc + _fold8(dg_rows)
        kacc = jnp.zeros((SUBLANE, HEAD_DIM), F32)
        for h in range(N_KV_HEADS):
            sl = slice(h * HEAD_DIM, (h + 1) * HEAD_DIM)
            dx, dg_rows = head(k_ref[:, sl].astype(F32), kn_ref[...], dk_ref[:, sl], 1.0)
            dko_ref[:, sl] = dx.astype(dko_ref.dtype)
            kacc = kacc + _fold8(dg_rows)
        qacc_ref[...] += qacc
        kacc_ref[...] += kacc

        @pl.when(i == nt - 1)
        def _():
            dqn_ref[...] = qacc_ref[...].sum(axis=0, keepdims=True)
            dkn_ref[...] = kacc_ref[...].sum(axis=0, keepdims=True)

    tab = _rows(ts, HEAD_DIM)
    gsp = _fixed((1, HEAD_DIM))
    return pl.pallas_call(
        body, name=name, grid=(nt,),
        in_specs=[_rows(ts, ATTN_WIDTH), _rows(ts, KV_WIDTH), _rows(ts, ATTN_WIDTH, OFF_Q // ATTN_WIDTH),
                  _rows(ts, KV_WIDTH, OFF_K // KV_WIDTH), gsp, gsp, tab, tab],
        out_specs=[_rows(ts, ATTN_WIDTH), _rows(ts, KV_WIDTH), gsp, gsp],
        out_shape=[_sds((s, ATTN_WIDTH), MXU_DTYPE), _sds((s, KV_WIDTH), MXU_DTYPE), _sds((1, HEAD_DIM), F32),
                   _sds((1, HEAD_DIM), F32)],
        scratch_shapes=[pltpu.VMEM((SUBLANE, HEAD_DIM), F32)] * 2, compiler_params=_params(("arbitrary",)),
    )(dqp, dkp, z, z, qn, kn, cos_t, sin_t)


_NT = (((1,), (1,)), ((), ()))
_GW = GROUP * HEAD_DIM


def _dot(a, b, dims=(((1,), (0,)), ((), ()))):
    return lax.dot_general(a, b, dims, preferred_element_type=F32)


def attn_fwd(qp, kp, vb, *, name, after=()):
    s = qp.shape[0]
    tq = min(s, 1024)
    rows = min(tq, 128)

    def body(q_ref, k_ref, v_ref, *rest):
        o_ref, lse_ref, v_ones = rest[-3:]

        @pl.when(pl.program_id(1) == 0)
        def _():
            v_ones[:, :HEAD_DIM] = v_ref[...]
            v_ones[:, HEAD_DIM:] = jnp.ones((s, HEAD_DIM), v_ones.dtype)

        k = k_ref[...]
        for g in range(GROUP):
            sl = slice(g * HEAD_DIM, (g + 1) * HEAD_DIM)
            for r0 in range(0, tq, rows):
                rs = slice(r0, r0 + rows)
                sc = _dot(q_ref[rs, sl], k, _NT)
                mx = jnp.max(sc, axis=-1, keepdims=True)
                p = jnp.exp(sc - mx).astype(v_ones.dtype)
                o_den = _dot(p, v_ones[...])
                den = o_den[:, HEAD_DIM:HEAD_DIM + 1]
                o_ref[rs, sl] = (o_den[:, :HEAD_DIM] / den).astype(o_ref.dtype)
                lse_ref[rs, g:g + 1] = mx + jnp.log(den)

    return pl.pallas_call(
        body, name=name, grid=(N_KV_HEADS, s // tq),
        in_specs=[pl.BlockSpec((tq, _GW), lambda j, i: (i, j)), pl.BlockSpec((s, HEAD_DIM), lambda j, i: (0, j)),
                  pl.BlockSpec((s, HEAD_DIM), lambda j, i: (0, j))] + [ANY] * len(after),
        out_specs=[pl.BlockSpec((tq, _GW), lambda j, i: (i, j)), pl.BlockSpec((None, tq, GROUP), lambda j, i: (j, i, 0))],
        out_shape=[_sds((s, ATTN_WIDTH), MXU_DTYPE), _sds((N_KV_HEADS, s, GROUP), F32)],
        scratch_shapes=[pltpu.VMEM((s, 2 * HEAD_DIM), MXU_DTYPE)], compiler_params=_params(("parallel", "arbitrary")),
    )(qp, kp, vb, *after)


ATTN_BWD_TQ = 256


def attn_bwd(qp, kp, vb, do, o, lse, *, name):
    s = qp.shape[0]
    tq = min(s, ATTN_BWD_TQ)
    nq = s // tq
    over_rows = (((0,), (0,)), ((), ()))

    def body(q_ref, k_ref, v_ref, do_ref, o_ref, lse_ref, dq_ref, dk_ref, dv_ref, p_all, ds_all, q_all, do_all, dk_acc,
             dv_acc):
        i = pl.program_id(1)

        @pl.when(i == 0)
        def _():
            dk_acc[...] = jnp.zeros_like(dk_acc)
            dv_acc[...] = jnp.zeros_like(dv_acc)

        k, v = k_ref[...], v_ref[...]
        for g in range(GROUP):
            sl = slice(g * HEAD_DIM, (g + 1) * HEAD_DIM)
            rows = slice(g * tq, (g + 1) * tq)
            qg, dog = q_ref[:, sl], do_ref[:, sl]
            dd = jnp.sum(dog.astype(F32) * o_ref[:, sl].astype(F32), axis=-1, keepdims=True)
            p = jnp.exp(_dot(qg, k, _NT) - lse_ref[:, g:g + 1])
            ds = (p * (_dot(dog, v, _NT) - dd)).astype(k.dtype)
            dq_ref[:, sl] = _dot(ds, k).astype(dq_ref.dtype)
            p_all[rows, :] = p.astype(p_all.dtype)
            ds_all[rows, :] = ds
            q_all[rows, :] = qg
            do_all[rows, :] = dog
        dv_acc[...] += _dot(p_all[...], do_all[...], over_rows)
        dk_acc[...] += _dot(ds_all[...], q_all[...], over_rows)

        @pl.when(i == nq - 1)
        def _():
            dk_ref[...] = dk_acc[...].astype(dk_ref.dtype)
            dv_ref[...] = dv_acc[...].astype(dv_ref.dtype)

    qspec = pl.BlockSpec((tq, _GW), lambda j, i: (i, j))
    kspec = pl.BlockSpec((s, HEAD_DIM), lambda j, i: (0, j))
    lspec = pl.BlockSpec((None, tq, GROUP), lambda j, i: (j, i, 0))
    return pl.pallas_call(
        body, name=name, grid=(N_KV_HEADS, nq), in_specs=[qspec, kspec, kspec, qspec, qspec, lspec],
        out_specs=[qspec, kspec, kspec],
        out_shape=[_sds((s, ATTN_WIDTH), ACT_DTYPE), _sds((s, KV_WIDTH), ACT_DTYPE), _sds((s, KV_WIDTH), MXU_DTYPE)],
        scratch_shapes=[pltpu.VMEM((GROUP * tq, s), MXU_DTYPE), pltpu.VMEM((GROUP * tq, s), MXU_DTYPE),
                        pltpu.VMEM((GROUP * tq, HEAD_DIM), MXU_DTYPE), pltpu.VMEM((GROUP * tq, HEAD_DIM), MXU_DTYPE),
                        pltpu.VMEM((s, HEAD_DIM), F32), pltpu.VMEM((s, HEAD_DIM), F32)],
        compiler_params=_params(("parallel", "arbitrary")),
    )(qp, kp, vb, do, o, lse)


GATE_TW = 1024


def gate_fwd(ya, yb, z, bias, *, name):
    s = z.shape[0]
    ts = min(s, 512)

    def body(ya_ref, yb_ref, ga_ref, gb_ref, ba_ref, bb_ref, o_ref):
        sa = jax.nn.sigmoid(ga_ref[...].astype(F32) + ba_ref[...])
        sb = jax.nn.sigmoid(gb_ref[...].astype(F32) + bb_ref[...])
        o_ref[...] = (sa * ya_ref[...].astype(F32) + sb * yb_ref[...].astype(F32)).astype(o_ref.dtype)

    tile = pl.BlockSpec((ts, GATE_TW), lambda i, j: (i, j))
    ga = pl.BlockSpec((ts, GATE_TW), lambda i, j: (i, OFF_GA // GATE_TW + j))
    gb = pl.BlockSpec((ts, GATE_TW), lambda i, j: (i, OFF_GB // GATE_TW + j))
    ba = pl.BlockSpec((1, GATE_TW), lambda i, j: (0, j))
    bb = pl.BlockSpec((1, GATE_TW), lambda i, j: (0, D_MODEL // GATE_TW + j))
    return pl.pallas_call(
        body, name=name, grid=(s // ts, D_MODEL // GATE_TW), in_specs=[tile, tile, ga, gb, ba, bb], out_specs=tile,
        out_shape=_sds((s, D_MODEL), MXU_DTYPE), compiler_params=_params(("parallel", "parallel")),
    )(ya, yb, z, z, bias, bias)


def gate_bwd(dmi, ya, yb, z, bias, *, name):
    s = z.shape[0]
    ts = min(s, 512)
    nt = s // ts

    def body(d_ref, ya_ref, yb_ref, ga_ref, gb_ref, ba_ref, bb_ref, dya_ref, dyb_ref, dga_ref, dgb_ref, dba_ref,
             dbb_ref, acc_a, acc_b):
        d = d_ref[...].astype(F32)
        sa = jax.nn.sigmoid(ga_ref[...].astype(F32) + ba_ref[...])
        sb = jax.nn.sigmoid(gb_ref[...].astype(F32) + bb_ref[...])
        dya_ref[...] = (d * sa).astype(dya_ref.dtype)
        dyb_ref[...] = (d * sb).astype(dyb_ref.dtype)
        dga = d * ya_ref[...].astype(F32) * sa * (1.0 - sa)
        dgb = d * yb_ref[...].astype(F32) * sb * (1.0 - sb)
        dga_ref[...] = dga.astype(dga_ref.dtype)
        dgb_ref[...] = dgb.astype(dgb_ref.dtype)
        i = pl.program_id(1)

        @pl.when(i == 0)
        def _():
            acc_a[...] = jnp.zeros_like(acc_a)
            acc_b[...] = jnp.zeros_like(acc_b)

        acc_a[...] += _fold8(dga)
        acc_b[...] += _fold8(dgb)

        @pl.when(i == nt - 1)
        def _():
            dba_ref[...] = acc_a[...].sum(axis=0, keepdims=True)
            dbb_ref[...] = acc_b[...].sum(axis=0, keepdims=True)

    tile = pl.BlockSpec((ts, GATE_TW), lambda j, i: (i, j))
    ga = pl.BlockSpec((ts, GATE_TW), lambda j, i: (i, OFF_GA // GATE_TW + j))
    gb = pl.BlockSpec((ts, GATE_TW), lambda j, i: (i, OFF_GB // GATE_TW + j))
    ba = pl.BlockSpec((1, GATE_TW), lambda j, i: (0, j))
    bb = pl.BlockSpec((1, GATE_TW), lambda j, i: (0, D_MODEL // GATE_TW + j))
    acc = pltpu.VMEM((SUBLANE, GATE_TW), F32)
    return pl.pallas_call(
        body, name=name, grid=(D_MODEL // GATE_TW, nt), in_specs=[tile, tile, tile, ga, gb, ba, bb],
        out_specs=[tile, tile, tile, tile, ba, ba],
        out_shape=[_sds((s, D_MODEL), MXU_DTYPE)] * 4 + [_sds((1, D_MODEL), F32)] * 2,
        scratch_shapes=[acc, acc], compiler_params=_params(("parallel", "arbitrary")),
    )(dmi, ya, yb, z, z, bias, bias)


HBM = pl.BlockSpec(memory_space=pltpu.HBM)
SEM = pl.BlockSpec(memory_space=pltpu.SEMAPHORE)
EFFECT = pltpu.SideEffectType.DATAFLOW_SIDE_EFFECTING
TOKEN = (SUBLANE, LANE)


def _place():
    return lax.axis_index("x"), lax.axis_index("y"), lax.axis_index("c")


def _window(ref, shard_shape, axis, d):
    r, c = shard_shape
    lead = (slice(None),) * (len(ref.shape) - 2)
    if axis == 0:
        return ref.at[lead + (pl.ds(pl.multiple_of(d * r, SUBLANE), r), slice(None))]
    return ref.at[lead + (slice(None), pl.ds(pl.multiple_of(d * c, LANE), c))]


def _hbm(a):
    return pltpu.with_memory_space_constraint(a, pltpu.HBM)


def _remote(src, dst, send_sems, recv_sems, i, to):
    return pltpu.make_async_remote_copy(src_ref=src, dst_ref=dst, send_sem=send_sems.at[i], recv_sem=recv_sems.at[i],
                                        device_id=to, device_id_type=MESH)


def cast_into_full(w, layer, axis, me, dtype, *, name, after=()):
    _, r, c = w.shape
    tr = min(r, 256)
    nr = r // tr
    in_spec = pl.BlockSpec((None, tr, c), lambda i, me_ref: (layer, i, 0))
    if axis == 0:
        out_spec = pl.BlockSpec((tr, c), lambda i, me_ref: (me_ref[0] * nr + i, 0))
        shape = (N_DEV * r, c)
    else:
        out_spec = pl.BlockSpec((tr, c), lambda i, me_ref: (i, me_ref[0]))
        shape = (r, N_DEV * c)

    def body(me_ref, w_ref, *rest):
        o_ref = rest[-1]
        o_ref[...] = w_ref[...].astype(o_ref.dtype)

    return pl.pallas_call(
        body, name=name,
        grid_spec=pltpu.PrefetchScalarGridSpec(num_scalar_prefetch=1, grid=(nr,), in_specs=[in_spec] + [ANY] * len(after),
                                               out_specs=out_spec),
        out_shape=_sds(shape, dtype), compiler_params=_params(("parallel",)),
    )(me, w, *after)


class _GatherPlan:
    def __init__(self, fulls, shard_shapes, axes):
        x, y, c = _place()
        self.n = len(fulls)
        self.me, self.sibling = (x, y, c), (x, y, 1 - c)
        self.chips = [(1 - x, y), (x, 1 - y), (1 - x, 1 - y)]
        self.win = lambda a, p: _window(fulls[a], shard_shapes[a], axes[a], 4 * p[0] + 2 * p[1] + p[2])

    def first(self, send_sems, recv_sems):
        out = []
        for a in range(self.n):
            mine = self.win(a, self.me)
            out.append(_remote(mine, mine, send_sems, recv_sems, 4 * a, self.sibling))
            out += [_remote(mine, mine, send_sems, recv_sems, 4 * a + 1 + j, (*chip, self.me[2]))
                    for j, chip in enumerate(self.chips)]
        return out

    def first_arrivals(self, send_sems, recv_sems):
        c = self.me[2]
        out = []
        for a in range(self.n):
            blocks = [self.sibling] + [(*chip, c) for chip in self.chips]
            out += [_remote(self.win(a, b), self.win(a, b), send_sems, recv_sems, 4 * a + k, self.me)
                    for k, b in enumerate(blocks)]
        return out

    def passed(self, send_sems, recv_sems):
        c = self.me[2]
        return [_remote(self.win(a, (*chip, c)), self.win(a, (*chip, c)), send_sems, recv_sems, 3 * a + j, self.sibling)
                for a in range(self.n) for j, chip in enumerate(self.chips)]

    def passed_arrivals(self, send_sems, recv_sems):
        c = self.me[2]
        return [_remote(self.win(a, (*chip, 1 - c)), self.win(a, (*chip, 1 - c)), send_sems, recv_sems, 3 * a + j, self.me)
                for a in range(self.n) for j, chip in enumerate(self.chips)]


def gather_start(fulls, after, shard_shapes, axes, *, name):
    n = len(fulls)

    def body(*refs):
        ins, send_sems, recv_sems, token = refs[:n], refs[n + 1], refs[n + 2], refs[-1]
        for cp in _GatherPlan(ins, shard_shapes, axes).first(send_sems, recv_sems):
            cp.start()
        token[...] = jnp.zeros_like(token)

    res = pl.pallas_call(
        body, name=name, in_specs=[HBM] * n + [ANY],
        out_shape=(pltpu.SemaphoreType.DMA((4 * n,)), pltpu.SemaphoreType.DMA((4 * n,)),
                   *[pltpu.HBM(f.shape, f.dtype) for f in fulls], _sds(TOKEN, F32)),
        out_specs=(SEM, SEM, *[HBM] * n, pl.BlockSpec(memory_space=pltpu.VMEM)),
        input_output_aliases={a: 2 + a for a in range(n)}, compiler_params=pltpu.CompilerParams(has_side_effects=EFFECT),
    )(*[_hbm(f) for f in fulls], after)
    return res[0], res[1], list(res[2:2 + n]), res[-1]


def gather_pass(send_sems, recv_sems, fulls, after, shard_shapes, axes, *, name):
    n = len(fulls)

    def body(*refs):
        ins, s1, r1 = refs[:n], refs[n], refs[n + 1]
        s2, r2, token = refs[n + 3], refs[n + 4], refs[-1]
        plan = _GatherPlan(ins, shard_shapes, axes)
        for cp in plan.first_arrivals(s1, r1):
            cp.wait_recv()
        for cp in plan.first(s1, r1):
            cp.wait_send()
        for cp in plan.passed(s2, r2):
            cp.start()
        token[...] = jnp.zeros_like(token)

    res = pl.pallas_call(
        body, name=name, in_specs=[HBM] * n + [SEM, SEM, ANY],
        out_shape=(pltpu.SemaphoreType.DMA((3 * n,)), pltpu.SemaphoreType.DMA((3 * n,)),
                   *[pltpu.HBM(f.shape, f.dtype) for f in fulls], _sds(TOKEN, F32)),
        out_specs=(SEM, SEM, *[HBM] * n, pl.BlockSpec(memory_space=pltpu.VMEM)),
        input_output_aliases={a: 2 + a for a in range(n)}, compiler_params=pltpu.CompilerParams(has_side_effects=EFFECT),
    )(*fulls, send_sems, recv_sems, after)
    return res[0], res[1], list(res[2:2 + n]), res[-1]


def gather_wait(send_sems, recv_sems, fulls, after, shard_shapes, axes, *, name):
    n = len(fulls)

    def body(*refs):
        ins, s2, r2 = refs[:n], refs[n], refs[n + 1]
        plan = _GatherPlan(ins, shard_shapes, axes)
        for cp in plan.passed_arrivals(s2, r2):
            cp.wait_recv()
        for cp in plan.passed(s2, r2):
            cp.wait_send()

    return list(pl.pallas_call(
        body, name=name, in_specs=[HBM] * n + [SEM, SEM, ANY], out_shape=tuple(pltpu.HBM(f.shape, f.dtype) for f in fulls),
        out_specs=tuple([HBM] * n), input_output_aliases={a: a for a in range(n)},
        compiler_params=pltpu.CompilerParams(has_side_effects=EFFECT),
    )(*fulls, send_sems, recv_sems, after))


def _pair_copies(grads, lands, shard_shapes, axes, send_sems, recv_sems):
    x, y, c = _place()
    return [_remote(_window(grads[a], shard_shapes[a], axes[a], 2 * q + (1 - c)), lands[a].at[q], send_sems, recv_sems,
                    N_CHIP * a + q, (x, y, 1 - c))
            for a in range(len(grads)) for q in range(N_CHIP)]


def _chip_sends(sums, lands, send_sems, recv_sems):
    x, y, c = _place()
    return [_remote(sums[a].at[2 * px + py], lands[a].at[2 * x + y], send_sems, recv_sems, 3 * a + j, (px, py, c))
            for a in range(len(sums)) for j, (px, py) in enumerate([(1 - x, y), (x, 1 - y), (1 - x, 1 - y)])]


def _chip_arrivals(sums, lands, send_sems, recv_sems):
    x, y, c = _place()
    return [_remote(sums[a].at[2 * x + y], lands[a].at[2 * px + py], send_sems, recv_sems, 3 * a + j, (x, y, c))
            for a in range(len(sums)) for j, (px, py) in enumerate([(1 - x, y), (x, 1 - y), (1 - x, 1 - y)])]


def exchange_start(srcs, after, land_shapes, make_sends, per_array, *, name):
    n = len(srcs)
    lands = [lax.empty(s, a.dtype) for s, a in zip(land_shapes, srcs)]

    def body(*refs):
        ins, zones = refs[:n], refs[n:2 * n]
        send_sems, recv_sems, token = refs[2 * n + 1], refs[2 * n + 2], refs[-1]
        for cp in make_sends(ins, zones, send_sems, recv_sems):
            cp.start()
        token[...] = jnp.zeros_like(token)

    res = pl.pallas_call(
        body, name=name, in_specs=[HBM] * (2 * n) + [ANY],
        out_shape=(pltpu.SemaphoreType.DMA((per_array * n,)), pltpu.SemaphoreType.DMA((per_array * n,)),
                   *[pltpu.HBM(a.shape, a.dtype) for a in srcs], *[pltpu.HBM(a.shape, a.dtype) for a in lands],
                   _sds(TOKEN, F32)),
        out_specs=(SEM, SEM, *[HBM] * (2 * n), pl.BlockSpec(memory_space=pltpu.VMEM)),
        input_output_aliases={a: 2 + a for a in range(2 * n)}, compiler_params=pltpu.CompilerParams(has_side_effects=EFFECT),
    )(*[_hbm(a) for a in srcs], *[_hbm(a) for a in lands], after)
    return res[0], res[1], list(res[2:2 + n]), list(res[2 + n:2 + 2 * n]), res[-1]


def exchange_wait(send_sems, recv_sems, srcs, lands, after, make_waits, *, name):
    n = len(srcs)

    def body(*refs):
        ins, zones, s, r = refs[:n], refs[n:2 * n], refs[2 * n], refs[2 * n + 1]
        sends, arrivals = make_waits(ins, zones, s, r)
        for cp in arrivals:
            cp.wait_recv()
        for cp in sends:
            cp.wait_send()

    res = pl.pallas_call(
        body, name=name, in_specs=[HBM] * (2 * n) + [SEM, SEM, ANY],
        out_shape=tuple(pltpu.HBM(a.shape, a.dtype) for a in (*srcs, *lands)), out_specs=tuple([HBM] * (2 * n)),
        input_output_aliases={a: a for a in range(2 * n)}, compiler_params=pltpu.CompilerParams(has_side_effects=EFFECT),
    )(*srcs, *lands, send_sems, recv_sems, after)
    return list(res[:n]), list(res[n:])


def pair_sum(grad, recv, shard_shape, axis, core, *, name):
    r, c = shard_shape
    tr = min(r, 1024)
    nr = r // tr
    if axis == 0:
        gspec = pl.BlockSpec((tr, c), lambda q, i, cref: ((2 * q + cref[0]) * nr + i, 0))
    else:
        gspec = pl.BlockSpec((tr, c), lambda q, i, cref: (i, 2 * q + cref[0]))
    rspec = pl.BlockSpec((None, tr, c), lambda q, i, cref: (q, i, 0))

    def body(c_ref, g_ref, r_ref, o_ref):
        o_ref[...] = (g_ref[...].astype(F32) + r_ref[...].astype(F32)).astype(o_ref.dtype)

    return pl.pallas_call(
        body, name=name,
        grid_spec=pltpu.PrefetchScalarGridSpec(num_scalar_prefetch=1, grid=(N_CHIP, nr), in_specs=[gspec, rspec],
                                               out_specs=rspec),
        out_shape=_sds((N_CHIP, r, c), recv.dtype), compiler_params=_params(("parallel", "parallel")),
    )(core, grad, recv)


def all_reduce_small(part, *, name):
    r, c = part.shape

    def body(p_ref, o_ref, gath_ref, send_sems, recv_sems):
        x, y, cc = _place()
        me = 4 * x + 2 * y + cc
        gath_ref[me] = p_ref[...]
        peers = [(1 - x if k & 4 else x, 1 - y if k & 2 else y, 1 - cc if k & 1 else cc) for k in range(1, N_DEV)]
        copies = [_remote(p_ref, gath_ref.at[me], send_sems, recv_sems, i, peer) for i, peer in enumerate(peers)]
        for cp in copies:
            cp.start()
        for i, (px, py, pc) in enumerate(peers):
            _remote(p_ref, gath_ref.at[4 * px + 2 * py + pc], send_sems, recv_sems, i, (x, y, cc)).wait_recv()
        for cp in copies:
            cp.wait_send()
        acc = gath_ref[0]
        for d in range(1, N_DEV):
            acc = acc + gath_ref[d]
        o_ref[...] = acc

    vm = pl.BlockSpec(memory_space=pltpu.VMEM)
    return pl.pallas_call(
        body, name=name, in_specs=[vm], out_specs=vm, out_shape=_sds((r, c), F32),
        scratch_shapes=[pltpu.VMEM((N_DEV, r, c), F32), pltpu.SemaphoreType.DMA((N_DEV - 1,)),
                        pltpu.SemaphoreType.DMA((N_DEV - 1,))],
    )(part)


def _adamw(w, g, m, v):
    m = ADAM_B1 * m + (1.0 - ADAM_B1) * g
    v = ADAM_B2 * v + (1.0 - ADAM_B2) * (g * g)
    m_hat = m / (1.0 - ADAM_B1 ** ADAM_STEP)
    v_hat = v / (1.0 - ADAM_B2 ** ADAM_STEP)
    delta = -ADAM_LR * (m_hat / (jnp.sqrt(v_hat) + ADAM_EPS) + ADAM_WD * w)
    return delta, m, v


def reduce_adam(own, landed, chip, w, m, v, layer, outs, *, name):
    _, r, c = w.shape
    tr = min(r, 256)
    first = outs is None

    def body(chip_ref, own_ref, l1_ref, l2_ref, l3_ref, w_ref, m_ref, v_ref, *rest):
        g_out, d_out, m_out, v_out = rest[-4:]
        g = own_ref[...].astype(F32) + l1_ref[...].astype(F32) + l2_ref[...].astype(F32) + l3_ref[...].astype(F32)
        d, mn, vn = _adamw(w_ref[...], g, m_ref[...], v_ref[...])
        g_out[...] = g
        d_out[...] = d
        m_out[...] = mn
        v_out[...] = vn

    spec = pl.BlockSpec((None, tr, c), lambda i, chip_ref: (layer, i, 0))

    def slot(step):
        return pl.BlockSpec((None, tr, c), lambda i, chip_ref: ((chip_ref[0] + step) % N_CHIP, i, 0))

    n_in = 8
    return pl.pallas_call(
        body, name=name,
        grid_spec=pltpu.PrefetchScalarGridSpec(
            num_scalar_prefetch=1, grid=(r // tr,),
            in_specs=[slot(0), slot(1), slot(2), slot(3), spec, spec, spec] + ([] if first else [ANY] * 4),
            out_specs=[spec] * 4),
        out_shape=[_sds(w.shape, F32)] * 4, input_output_aliases={} if first else {n_in + i: i for i in range(4)},
        compiler_params=_params(("parallel",)),
    )(chip, own, landed, landed, landed, w, m, v, *([] if first else outs))


def adam_small(g, w, m, v, *, name):
    def body(g_ref, w_ref, m_ref, v_ref, d_out, m_out, v_out):
        d, mn, vn = _adamw(w_ref[...], g_ref[...], m_ref[...], v_ref[...])
        d_out[...] = d
        m_out[...] = mn
        v_out[...] = vn

    return pl.pallas_call(body, name=name, out_shape=[_sds(w.shape, F32)] * 3)(g, w, m, v)


BIG = ("w_in", "w_out_conv", "w_out_attn", "w_merge", "w_up", "w_down")
GATHERED = BIG + ("conv_w",)
BIG_AXIS = {"w_in": 1, "w_out_conv": 0, "w_out_attn": 0, "w_merge": 0, "w_up": 1, "w_down": 0, "conv_w": 1}
FIRST_GATHER_GROUPS = (("w_in", "conv_w"), ("w_out_conv", "w_out_attn", "w_merge"), ("w_up", "w_down"))
LAST_REDUCE_GROUPS = (("w_down", "w_up"), ("w_merge", "w_out_conv", "w_out_attn"), ("w_in",))


class _Gather:
    def __init__(self, weights, keys, own, after, tag):
        self.keys, self.tag = keys, tag
        self.shapes = [weights[k].shape[1:] for k in keys]
        self.axes = [BIG_AXIS[k] for k in keys]
        self.send, self.recv, self.fulls, self.token = gather_start([own[k] for k in keys], after, self.shapes, self.axes,
                                                                    name="gather_start_" + tag)

    def pass_on(self, after):
        self.send, self.recv, self.fulls, self.token = gather_pass(self.send, self.recv, self.fulls, after, self.shapes,
                                                                   self.axes, name="gather_pass_" + self.tag)
        return self.token

    def wait(self, after):
        fulls = gather_wait(self.send, self.recv, self.fulls, after, self.shapes, self.axes, name="gather_wait_" + self.tag)
        return dict(zip(self.keys, fulls))


class _Reduce:
    def __init__(self, weights, keys, tag):
        self.keys, self.tag = keys, tag
        self.shapes = [weights[k].shape[1:] for k in keys]
        self.axes = [BIG_AXIS[k] for k in keys]
        self.pair_shapes = [(N_CHIP, *shp) for shp in self.shapes]

    def _pair(self, i, z, ss, rs):
        return _pair_copies(i, z, self.shapes, self.axes, ss, rs)

    def begin(self, grads, after):
        self.send, self.recv, self.src, self.land, self.token = exchange_start(
            [grads[k] for k in self.keys], after, self.pair_shapes, self._pair, N_CHIP, name="rs_pair_start_" + self.tag)
        return self.token

    def middle(self, after, core):
        both = lambda i, z, ss, rs: (self._pair(i, z, ss, rs),) * 2
        grads, from_sibling = exchange_wait(self.send, self.recv, self.src, self.land, after, both,
                                            name="rs_pair_wait_" + self.tag)
        sums = [pair_sum(g, rcv, shp, ax, core, name="pair_sum_" + k)
                for k, g, rcv, shp, ax in zip(self.keys, grads, from_sibling, self.shapes, self.axes)]
        self.send, self.recv, self.src, self.land, self.token = exchange_start(
            sums, after, self.pair_shapes, _chip_sends, 3, name="rs_chips_start_" + self.tag)
        return self.token

    def end(self, after):
        both = lambda i, z, ss, rs: (_chip_sends(i, z, ss, rs), _chip_arrivals(i, z, ss, rs))
        sums, landed = exchange_wait(self.send, self.recv, self.src, self.land, after, both,
                                     name="rs_chips_wait_" + self.tag)
        return list(zip(self.keys, sums, landed))
SMALL = (("norm_mix_pre", D_MODEL), ("gate_bias", 2 * D_MODEL), ("norm_mix_post", D_MODEL), ("norm_mlp_pre", D_MODEL),
         ("norm_mlp_post", D_MODEL), ("q_norm", HEAD_DIM), ("k_norm", HEAD_DIM))
SMALL_WIDTH = sum(w for _, w in SMALL)
WEIGHTS = ("norm_mix_pre", "w_in", "gate_bias", "conv_w", "q_norm", "k_norm", "w_out_conv", "w_out_attn", "w_merge",
           "norm_mix_post", "norm_mlp_pre", "w_up", "w_down", "norm_mlp_post")


def kernel(x, norm_mix_pre, w_in, gate_bias, conv_w, q_norm, k_norm, w_out_conv, w_out_attn, w_merge, norm_mix_post, norm_mlp_pre, w_up, w_down, norm_mlp_post, loss_target, m_norm_mix_pre, m_w_in, m_gate_bias, m_conv_w, m_q_norm, m_k_norm, m_w_out_conv, m_w_out_attn, m_w_merge, m_norm_mix_post, m_norm_mlp_pre, m_w_up, m_w_down, m_norm_mlp_post, v_norm_mix_pre, v_w_in, v_gate_bias, v_conv_w, v_q_norm, v_k_norm, v_w_out_conv, v_w_out_attn, v_w_merge, v_norm_mix_post, v_norm_mlp_pre, v_w_up, v_w_down, v_norm_mlp_post):
    w = dict(norm_mix_pre=norm_mix_pre, w_in=w_in, gate_bias=gate_bias, conv_w=conv_w, q_norm=q_norm, k_norm=k_norm,
             w_out_conv=w_out_conv, w_out_attn=w_out_attn, w_merge=w_merge, norm_mix_post=norm_mix_post,
             norm_mlp_pre=norm_mlp_pre, w_up=w_up, w_down=w_down, norm_mlp_post=norm_mlp_post)
    mom = dict(norm_mix_pre=m_norm_mix_pre, w_in=m_w_in, gate_bias=m_gate_bias, conv_w=m_conv_w, q_norm=m_q_norm,
               k_norm=m_k_norm, w_out_conv=m_w_out_conv, w_out_attn=m_w_out_attn, w_merge=m_w_merge,
               norm_mix_post=m_norm_mix_post, norm_mlp_pre=m_norm_mlp_pre, w_up=m_w_up, w_down=m_w_down,
               norm_mlp_post=m_norm_mlp_post)
    var = dict(norm_mix_pre=v_norm_mix_pre, w_in=v_w_in, gate_bias=v_gate_bias, conv_w=v_conv_w, q_norm=v_q_norm,
               k_norm=v_k_norm, w_out_conv=v_w_out_conv, w_out_attn=v_w_out_attn, w_merge=v_w_merge,
               norm_mix_post=v_norm_mix_post, norm_mlp_pre=v_norm_mlp_pre, w_up=v_w_up, w_down=v_w_down,
               norm_mlp_post=v_norm_mlp_post)
    depth = w_in.shape[0]
    s = x.shape[1]
    xs = x.reshape(s, D_MODEL)
    target = loss_target.reshape(s, D_MODEL)
    x_idx, y_idx, c_idx = _place()
    as_operand = lambda i: jnp.reshape(i, (1,)).astype(jnp.int32)
    core, chip, me = as_operand(c_idx), as_operand(2 * x_idx + y_idx), as_operand(4 * x_idx + 2 * y_idx + c_idx)
    cos_t, sin_t = rope_tables(s)

    def vec(name, l):
        return w[name][l].reshape(1, -1)

    saved = []
    h = rms_fwd(xs, vec("norm_mix_pre", 0), name="rms_first")
    def cast_layer(l, after):
        own = {}
        for k in GATHERED:
            own[k] = cast_into_full(w[k], l, BIG_AXIS[k], me, F32 if k == "conv_w" else MXU_DTYPE, name="cast_" + k,
                                    after=[after])
            after = own[k]
        return own, after

    own, _ = cast_layer(0, h)
    first, after = [], h
    for i, keys in enumerate(FIRST_GATHER_GROUPS):
        first.append(_Gather(w, keys, own, after, f"0{'abc'[i]}"))
        after = first[-1].token
    ahead = {}
    for l in range(1, depth):
        ahead[l], after = cast_layer(l, after)
    full = first[0].wait(first[0].pass_on(after))
    layers = []
    for l in range(depth):
        layers.append(full)
        nxt = l + 1 < depth
        if nxt:
            coming = _Gather(w, GATHERED, ahead[l + 1], full["w_in"], str(l + 1))
        z = matmul(h, full["w_in"], mode="nn", m=s, n=IN_WIDTH, k=D_MODEL, out_dtypes=[ACT_DTYPE], name="mm_in",
                   after=[coming.token] if nxt else [])
        t = conv_fwd(z, full["conv_w"], name="conv_fwd")
        qp, kp, vb = qk_prep_fwd(z, vec("q_norm", l), vec("k_norm", l), cos_t, sin_t, name="qk_fwd")
        o, lse = attn_fwd(qp, kp, vb, name="attn_fwd", after=[first[1].pass_on(qp)] if l == 0 else [])
        if l == 0:
            full.update(first[1].wait(o))
        ya = matmul(t, full["w_out_conv"], mode="nn", m=s, n=D_MODEL, k=CONV_WIDTH, out_dtypes=[ACT_DTYPE], name="mm_out_conv",
                    after=[first[2].pass_on(o)] if l == 0 else [])
        yb = matmul(o, full["w_out_attn"], mode="nn", m=s, n=D_MODEL, k=ATTN_WIDTH, out_dtypes=[ACT_DTYPE], name="mm_out_attn")
        mi = gate_fwd(ya, yb, z, vec("gate_bias", l), name="gate_fwd")
        mixed = matmul(mi, full["w_merge"], mode="nn", m=s, n=D_MODEL, k=D_MODEL, out_dtypes=[ACT_DTYPE], name="mm_merge")
        if l == 0:
            full.update(first[2].wait(mixed))
        x_mid, h2 = rms_residual_fwd(xs, mixed, vec("norm_mix_post", l), vec("norm_mlp_pre", l), name="res_mix")
        act, r = matmul(h2, full["w_up"], mode="nn", m=s, n=D_FF, k=D_MODEL, out_dtypes=[MXU_DTYPE, MXU_DTYPE],
                        name="mm_up", epilogue=lambda acc: (acc, jnp.square(jnp.maximum(acc, 0.0))))
        f = matmul(r, full["w_down"], mode="nn", m=s, n=D_MODEL, k=D_FF, tk=D_FF // 2, vmem=VMEM_LIMIT_DEEP_K,
                   out_dtypes=[ACT_DTYPE], name="mm_down",
                   after=[coming.pass_on(r)] if nxt and l > 0 else [])
        if nxt and l == 0:
            coming.pass_on(f)
        g_next = vec("norm_mix_pre", l + 1) if nxt else None
        x_out, h_next = rms_residual_fwd(x_mid, f, vec("norm_mlp_post", l), g_next, name="res_mlp" if nxt else "res_last")
        saved.append(dict(x_in=xs, h=h, z=z, t=t, qp=qp, kp=kp, vb=vb, o=o, lse=lse, ya=ya, yb=yb, mi=mi, mixed=mixed,
                          x_mid=x_mid, h2=h2, act=act, r=r, f=f))
        if nxt:
            full = coming.wait(x_out)
        xs, h = x_out, h_next

    dx, loss_part = loss_and_grad(xs, target, name="loss")
    loss = lax.psum(jnp.sum(loss_part), ("x", "y", "c"))

    small_rows = [None] * depth
    conv_rows = [None] * depth
    out_g, out_d, out_m, out_v = {}, {}, {}, {}
    big_outs = {k: None for k in BIG}
    pending = None
    last = []
    handed_down = None

    def finish(reduction, layer, after):
        for k, own, landed in reduction.end(after):
            big_outs[k] = reduce_adam(own, landed, chip, w[k], mom[k], var[k], layer, big_outs[k], name="adam_" + k)

    for l in reversed(range(depth)):
        sv, full = saved[l], layers[l]
        grads = {}
        groups = l == 0

        def wgrad(key, lhs, rhs, m, n, after=()):
            grads[key] = matmul(lhs, rhs, mode="tn", m=m, n=n, k=s, tk=s, out_dtypes=[COMM_DTYPE], name="wg_" + key,
                                after=after)

        def begin_group(i, after):
            last.append(_Reduce(w, LAST_REDUCE_GROUPS[i], f"{l}{'abc'[i]}"))
            return [last[i].begin(grads, after)]

        if handed_down is None:
            handed_down = rms_bwd(sv["f"], vec("norm_mlp_post", l), dx, None, out_dtype=MXU_DTYPE, name="rmsb_mlp_post")
        df, dg_mlp_post = handed_down
        da = matmul(df, full["w_down"], mode="nt", m=s, n=D_FF, k=D_MODEL, out_dtypes=[MXU_DTYPE], name="mm_d_down",
                    extra=(sv["act"],), epilogue=lambda acc, a: (acc * (2.0 * jnp.maximum(a.astype(F32), 0.0)),),
                    after=[pending.token] if pending else [])
        tokens = [pending.middle(da, core)] if pending else []
        wgrad("w_down", sv["r"], df, D_FF, D_MODEL)
        dh2 = matmul(da, full["w_up"], mode="nt", m=s, n=D_MODEL, k=D_FF, tk=D_FF // 2, vmem=VMEM_LIMIT_DEEP_K,
                     out_dtypes=[ACT_DTYPE], name="mm_d_up", after=tokens)
        wgrad("w_up", sv["h2"], da, D_MODEL, D_FF)
        tokens = begin_group(0, dh2) if groups else []
        dx_mid, dmixed, dg_mlp_pre, dg_mix_post = rms_bwd_pair(
            sv["x_mid"], vec("norm_mlp_pre", l), dh2, dx, sv["mixed"], vec("norm_mix_post", l), name="rmsb_mlp_pre_mix_post")
        dmi = matmul(dmixed, full["w_merge"], mode="nt", m=s, n=D_MODEL, k=D_MODEL, out_dtypes=[ACT_DTYPE], name="mm_d_merge",
                     after=tokens)
        tokens = [last[0].middle(dmi, core)] if groups else []
        wgrad("w_merge", sv["mi"], dmixed, D_MODEL, D_MODEL, after=tokens)
        dya, dyb, dga, dgb, dba, dbb = gate_bwd(dmi, sv["ya"], sv["yb"], sv["z"], vec("gate_bias", l), name="gate_bwd")
        wgrad("w_out_conv", sv["t"], dya, CONV_WIDTH, D_MODEL)
        wgrad("w_out_attn", sv["o"], dyb, ATTN_WIDTH, D_MODEL)
        tokens = begin_group(1, dyb) if groups else []
        dt = matmul(dya, full["w_out_conv"], mode="nt", m=s, n=CONV_WIDTH, k=D_MODEL, out_dtypes=[ACT_DTYPE],
                    name="mm_d_out_conv", after=tokens)
        do = matmul(dyb, full["w_out_attn"], mode="nt", m=s, n=ATTN_WIDTH, k=D_MODEL, out_dtypes=[MXU_DTYPE],
                    name="mm_d_out_attn")
        tokens = [last[1].middle(do, core)] if groups else []
        dcb, dcc, dci, dconv_w = conv_bwd(dt, sv["z"], full["conv_w"], name="conv_bwd", after=tokens)
        dqp, dkp, dv = attn_bwd(sv["qp"], sv["kp"], sv["vb"], do, sv["o"], sv["lse"], name="attn_bwd")
        dq, dk, dqn, dkn = qk_prep_bwd(dqp, dkp, sv["z"], vec("q_norm", l), vec("k_norm", l), cos_t, sin_t, name="qk_bwd")
        dz = jnp.concatenate([dcb, dcc, dci, dq, dk, dv, dga, dgb], axis=1)
        wgrad("w_in", sv["h"], dz, D_MODEL, IN_WIDTH)
        tokens = begin_group(2, dz) if groups else []
        dh = matmul(dz, full["w_in"], mode="nt", m=s, n=D_MODEL, k=IN_WIDTH, tk=IN_WIDTH // 4, out_dtypes=[ACT_DTYPE],
                    name="mm_d_in", after=tokens)
        if l > 0:
            dx, df_below, dg_mix_pre, dg_post_below = rms_bwd_pair(
                sv["x_in"], vec("norm_mix_pre", l), dh, dx_mid, saved[l - 1]["f"], vec("norm_mlp_post", l - 1),
                name="rmsb_mix_pre_mlp_post")
            handed_down = (df_below, dg_post_below)
        else:
            dx, dg_mix_pre = rms_bwd(sv["x_in"], vec("norm_mix_pre", l), dh, dx_mid, out_dtype=F32, name="rmsb_mix_pre")
        small_rows[l] = jnp.concatenate([dg_mix_pre, dba, dbb, dg_mix_post, dg_mlp_pre, dg_mlp_post, dqn, dkn], axis=1)
        conv_rows[l] = dconv_w.reshape(1, 3 * CONV_WIDTH)
        if pending:
            finish(pending, l + 1, dx)
        pending = None
        if not groups:
            pending = _Reduce(w, BIG, str(l))
            pending.begin(grads, dx)

    grad_x = dx.reshape(1, s, D_MODEL)

    small_part = jnp.concatenate([jnp.concatenate(small_rows, axis=0), jnp.concatenate(conv_rows, axis=0)], axis=1)
    small_sum = all_reduce_small(small_part, name="allreduce_small")
    pack = lambda src: jnp.concatenate([src[k].reshape(depth, wd) for k, wd in SMALL], axis=1)
    g_small = small_sum[:, :SMALL_WIDTH]
    d_small, m_small, v_small = adam_small(g_small, pack(w), pack(mom), pack(var), name="adam_small")
    off = 0
    for k, wd in SMALL:
        for dst, src in ((out_g, g_small), (out_d, d_small), (out_m, m_small), (out_v, v_small)):
            dst[k] = src[:, off:off + wd]
        off += wd
    cshard = CONV_WIDTH // N_DEV
    g_conv = lax.dynamic_slice_in_dim(small_sum[:, SMALL_WIDTH:].reshape(depth, 3, CONV_WIDTH), me[0] * cshard, cshard,
                                      axis=2).reshape(depth, 3 * cshard)
    flat = lambda a: a.reshape(depth, 3 * cshard)
    d_conv, m_conv, v_conv = adam_small(g_conv, flat(conv_w), flat(m_conv_w), flat(v_conv_w), name="adam_conv")
    for dst, src in ((out_g, g_conv), (out_d, d_conv), (out_m, m_conv), (out_v, v_conv)):
        dst["conv_w"] = src.reshape(depth, 3, cshard)

    token = last[2].middle(d_small, core)
    for reduction in last:
        finish(reduction, 0, token)
    for k in BIG:
        out_g[k], out_d[k], out_m[k], out_v[k] = big_outs[k]

    return (loss, grad_x, *[out_g[k] for k in WEIGHTS], *[out_d[k] for k in WEIGHTS], *[out_m[k] for k in WEIGHTS],
            *[out_v[k] for k in WEIGHTS])
```

```python
import math

import jax
import jax.numpy as jnp
from jax import lax
from jax.experimental import pallas as pl
from jax.experimental.pallas import tpu as pltpu

F32 = jnp.float32
MXU_DTYPE = jnp.bfloat16
COMM_DTYPE = jnp.bfloat16
ACT_DTYPE = jnp.bfloat16

D_MODEL = 2048
HEAD_DIM = 128
N_Q_HEADS = 16
N_KV_HEADS = 4
GROUP = N_Q_HEADS // N_KV_HEADS
ATTN_WIDTH = N_Q_HEADS * HEAD_DIM
KV_WIDTH = N_KV_HEADS * HEAD_DIM
CONV_WIDTH = D_MODEL
D_FF = 4 * D_MODEL
GRID_W = 64
ROPE_THETA = 10000.0
RMS_EPS = 1e-6
IN_WIDTH = 3 * CONV_WIDTH + ATTN_WIDTH + 2 * KV_WIDTH + 2 * D_MODEL
OFF_CB, OFF_CC, OFF_CI = 0, CONV_WIDTH, 2 * CONV_WIDTH
OFF_Q = 3 * CONV_WIDTH
OFF_K = OFF_Q + ATTN_WIDTH
OFF_V = OFF_K + KV_WIDTH
OFF_GA = OFF_V + KV_WIDTH
OFF_GB = OFF_GA + D_MODEL
ATTN_SCALE = 1.0 / math.sqrt(HEAD_DIM)

ADAM_LR, ADAM_B1, ADAM_B2, ADAM_EPS, ADAM_WD, ADAM_STEP = 0.001, 0.9, 0.999, 1e-08, 0.01, 10

N_DEV = 8
N_CHIP = 4
LANE = 128
SUBLANE = 8
VMEM_LIMIT = 48 * 1024 * 1024
VMEM_LIMIT_DEEP_K = 56 * 1024 * 1024
MESH = pl.DeviceIdType.MESH
ANY = pl.BlockSpec(memory_space=pl.ANY)


def _sds(shape, dtype):
    return jax.ShapeDtypeStruct(tuple(shape), dtype)


def _params(sem, vmem=VMEM_LIMIT):
    return pltpu.CompilerParams(dimension_semantics=sem, vmem_limit_bytes=vmem)


def _rows(ts, w, col=0):
    return pl.BlockSpec((ts, w), lambda i: (i, col))


def _fixed(shape):
    return pl.BlockSpec(shape, lambda *_: (0,) * len(shape))


def _fold8(v):
    ts, w = v.shape
    return v.reshape(ts // SUBLANE, SUBLANE, w).sum(axis=0)


def matmul(a, b, *, mode, m, n, k, name, out_dtypes, tm=1024, tn=1024, tk=2048, epilogue=None, extra=(), after=(),
           vmem=VMEM_LIMIT):
    tm, tn, tk = min(tm, m), min(tn, n), min(tk, k)
    nm, nn, nk = m // tm, n // tn, k // tk
    assert nm * tm == m and nn * tn == n and nk * tk == k, (name, m, n, k, tm, tn, tk)
    if mode == "tn":
        a_spec = pl.BlockSpec((tk, tm), lambda i, j, kk: (kk, i))
        dims = (((0,), (0,)), ((), ()))
    else:
        a_spec = pl.BlockSpec((tm, tk), lambda i, j, kk: (i, kk))
        dims = (((1,), (1 if mode == "nt" else 0,)), ((), ()))
    if mode == "nt":
        b_spec = pl.BlockSpec((tn, tk), lambda i, j, kk: (j, kk))
    else:
        b_spec = pl.BlockSpec((tk, tn), lambda i, j, kk: (kk, j))
    tile = pl.BlockSpec((tm, tn), lambda i, j, kk: (i, j))
    n_out, n_extra, n_after = len(out_dtypes), len(extra), len(after)
    if epilogue is None:
        epilogue = lambda acc: (acc,)

    def body(a_ref, b_ref, *rest):
        extra_refs = rest[:n_extra]
        outs = rest[n_extra + n_after:][:n_out]
        part = lax.dot_general(a_ref[...].astype(MXU_DTYPE), b_ref[...].astype(MXU_DTYPE), dims,
                               preferred_element_type=F32)

        def finish(acc):
            for o_ref, val in zip(outs, epilogue(acc, *[r[...] for r in extra_refs])):
                o_ref[...] = val.astype(o_ref.dtype)

        if nk == 1:
            finish(part)
        else:
            acc_ref = rest[-1]
            kk = pl.program_id(2)

            @pl.when(kk == 0)
            def _():
                acc_ref[...] = part

            @pl.when(kk > 0)
            def _():
                acc_ref[...] += part

            @pl.when(kk == nk - 1)
            def _():
                finish(acc_ref[...])

    res = pl.pallas_call(
        body, name=name, grid=(nm, nn, nk), in_specs=[a_spec, b_spec] + [tile] * n_extra + [ANY] * n_after,
        out_specs=[tile] * n_out, out_shape=[_sds((m, n), dt) for dt in out_dtypes],
        scratch_shapes=[pltpu.VMEM((tm, tn), F32)] if nk > 1 else [],
        compiler_params=_params(("parallel", "parallel", "arbitrary"), vmem),
    )(a, b, *extra, *after)
    return res[0] if n_out == 1 else res


def _rstd(x):
    return lax.rsqrt(jnp.mean(x * x, axis=-1, keepdims=True) + RMS_EPS)


def _rms_bwd(x, g, dy):
    rstd = _rstd(x)
    xh = x * rstd
    gy = dy * g
    dx = rstd * (gy - xh * jnp.mean(gy * xh, axis=-1, keepdims=True))
    return dx, dy * xh


def rms_fwd(x, g, *, name):
    s = x.shape[0]
    ts = min(s, 512)

    def body(x_ref, g_ref, h_ref):
        xv = x_ref[...]
        h_ref[...] = (xv * _rstd(xv) * g_ref[...]).astype(h_ref.dtype)

    return pl.pallas_call(
        body, name=name, grid=(s // ts,), in_specs=[_rows(ts, D_MODEL), _fixed((1, D_MODEL))],
        out_specs=_rows(ts, D_MODEL), out_shape=_sds((s, D_MODEL), MXU_DTYPE), compiler_params=_params(("parallel",)),
    )(x, g)


def rms_residual_fwd(x, y, g_post, g_next, *, name):
    s = x.shape[0]
    ts = min(s, 512)
    with_next = g_next is not None

    def body(x_ref, y_ref, gp_ref, *rest):
        yv = y_ref[...].astype(F32)
        xn = x_ref[...] + yv * _rstd(yv) * gp_ref[...]
        if with_next:
            gn_ref, xo_ref, h_ref = rest
            h_ref[...] = (xn * _rstd(xn) * gn_ref[...]).astype(h_ref.dtype)
        else:
            (xo_ref,) = rest
        xo_ref[...] = xn

    gspec = _fixed((1, D_MODEL))
    res = pl.pallas_call(
        body, name=name, grid=(s // ts,),
        in_specs=[_rows(ts, D_MODEL), _rows(ts, D_MODEL), gspec] + [gspec] * with_next,
        out_specs=[_rows(ts, D_MODEL)] * (1 + with_next),
        out_shape=[_sds((s, D_MODEL), F32)] + [_sds((s, D_MODEL), MXU_DTYPE)] * with_next,
        compiler_params=_params(("parallel",)),
    )(x, y, g_post, *([g_next] if with_next else []))
    return (res[0], res[1]) if with_next else (res[0], None)


def rms_bwd(x, g, dy, residual, *, out_dtype, name):
    s = x.shape[0]
    ts = min(s, 256)
    nt = s // ts
    with_res = residual is not None

    def body(x_ref, g_ref, dy_ref, *rest):
        dx_ref, dg_ref, acc_ref = rest[-3:]
        dx, dg_rows = _rms_bwd(x_ref[...].astype(F32), g_ref[...], dy_ref[...].astype(F32))
        if with_res:
            dx = dx + rest[0][...]
        dx_ref[...] = dx.astype(dx_ref.dtype)
        i = pl.program_id(0)

        @pl.when(i == 0)
        def _():
            acc_ref[...] = jnp.zeros_like(acc_ref)

        acc_ref[...] += _fold8(dg_rows)

        @pl.when(i == nt - 1)
        def _():
            dg_ref[...] = acc_ref[...].sum(axis=0, keepdims=True)

    return pl.pallas_call(
        body, name=name, grid=(nt,),
        in_specs=[_rows(ts, D_MODEL), _fixed((1, D_MODEL)), _rows(ts, D_MODEL)] + [_rows(ts, D_MODEL)] * with_res,
        out_specs=[_rows(ts, D_MODEL), _fixed((1, D_MODEL))],
        out_shape=[_sds((s, D_MODEL), out_dtype), _sds((1, D_MODEL), F32)],
        scratch_shapes=[pltpu.VMEM((SUBLANE, D_MODEL), F32)], compiler_params=_params(("arbitrary",)),
    )(x, g, dy, *([residual] if with_res else []))


def rms_bwd_pair(xa, ga, dya, residual, xb, gb, *, name):
    s = xa.shape[0]
    ts = min(s, 256)
    nt = s // ts

    def body(xa_ref, ga_ref, dya_ref, r_ref, xb_ref, gb_ref, d1_ref, d2_ref, dga_ref, dgb_ref, acc_a, acc_b):
        d1, rows_a = _rms_bwd(xa_ref[...].astype(F32), ga_ref[...], dya_ref[...].astype(F32))
        d1 = d1 + r_ref[...]
        d1_ref[...] = d1
        d2, rows_b = _rms_bwd(xb_ref[...].astype(F32), gb_ref[...], d1)
        d2_ref[...] = d2.astype(d2_ref.dtype)
        i = pl.program_id(0)

        @pl.when(i == 0)
        def _():
            acc_a[...] = jnp.zeros_like(acc_a)
            acc_b[...] = jnp.zeros_like(acc_b)

        acc_a[...] += _fold8(rows_a)
        acc_b[...] += _fold8(rows_b)

        @pl.when(i == nt - 1)
        def _():
            dga_ref[...] = acc_a[...].sum(axis=0, keepdims=True)
            dgb_ref[...] = acc_b[...].sum(axis=0, keepdims=True)

    row, gain = _rows(ts, D_MODEL), _fixed((1, D_MODEL))
    return pl.pallas_call(
        body, name=name, grid=(nt,), in_specs=[row, gain, row, row, row, gain], out_specs=[row, row, gain, gain],
        out_shape=[_sds((s, D_MODEL), F32), _sds((s, D_MODEL), MXU_DTYPE), _sds((1, D_MODEL), F32), _sds((1, D_MODEL), F32)],
        scratch_shapes=[pltpu.VMEM((SUBLANE, D_MODEL), F32)] * 2, compiler_params=_params(("arbitrary",)),
    )(xa, ga, dya, residual, xb, gb)


def loss_and_grad(y, target, *, name):
    s = y.shape[0]
    ts = min(s, 512)
    nt = s // ts

    def body(y_ref, t_ref, dy_ref, part_ref):
        e = y_ref[...] - t_ref[...]
        dy_ref[...] = e * (1.0 / D_MODEL)
        sq = _fold8(e * e)
        lanes = sq[:, 0:LANE]
        for j in range(1, D_MODEL // LANE):
            lanes = lanes + sq[:, j * LANE:(j + 1) * LANE]
        i = pl.program_id(0)

        @pl.when(i == 0)
        def _():
            part_ref[...] = jnp.zeros_like(part_ref)

        part_ref[...] += lanes * (0.5 / D_MODEL)

    return pl.pallas_call(
        body, name=name, grid=(nt,), in_specs=[_rows(ts, D_MODEL), _rows(ts, D_MODEL)],
        out_specs=[_rows(ts, D_MODEL), _fixed((SUBLANE, LANE))],
        out_shape=[_sds((s, D_MODEL), F32), _sds((SUBLANE, LANE), F32)], compiler_params=_params(("arbitrary",)),
    )(y, target)


CONV_TC = 2 * LANE


def _conv_taps(u, s):
    row = lax.broadcasted_iota(jnp.int32, u.shape, 0)
    prev = jnp.where(row == 0, 0.0, pltpu.roll(u, 1, 0))
    nxt = jnp.where(row == s - 1, 0.0, pltpu.roll(u, s - 1, 0))
    return prev, nxt


def _zcol(s, off):
    return pl.BlockSpec((s, CONV_TC), lambda j: (0, off // CONV_TC + j))


def conv_fwd(z, w3, *, name):
    s = z.shape[0]

    def body(cb_ref, cc_ref, ci_ref, w_ref, t_ref):
        u = cc_ref[...].astype(F32) * ci_ref[...].astype(F32)
        prev, nxt = _conv_taps(u, s)
        w = w_ref[...]
        conv = w[0:1] * prev + w[1:2] * u + w[2:3] * nxt
        t_ref[...] = (cb_ref[...].astype(F32) * conv).astype(t_ref.dtype)

    return pl.pallas_call(
        body, name=name, grid=(CONV_WIDTH // CONV_TC,),
        in_specs=[_zcol(s, OFF_CB), _zcol(s, OFF_CC), _zcol(s, OFF_CI), pl.BlockSpec((3, CONV_TC), lambda j: (0, j))],
        out_specs=pl.BlockSpec((s, CONV_TC), lambda j: (0, j)), out_shape=_sds((s, CONV_WIDTH), MXU_DTYPE),
        compiler_params=_params(("parallel",)),
    )(z, z, z, w3)


def conv_bwd(dt, z, w3, *, name, after=()):
    s = z.shape[0]

    def body(dt_ref, cb_ref, cc_ref, ci_ref, w_ref, *rest):
        dcb_ref, dcc_ref, dci_ref, dw_ref = rest[-4:]
        cc, ci = cc_ref[...].astype(F32), ci_ref[...].astype(F32)
        u = cc * ci
        prev, nxt = _conv_taps(u, s)
        w = w_ref[...]
        dtv = dt_ref[...].astype(F32)
        dcb_ref[...] = (dtv * (w[0:1] * prev + w[1:2] * u + w[2:3] * nxt)).astype(dcb_ref.dtype)
        dconv = dtv * cb_ref[...].astype(F32)
        dprev, dnxt = _conv_taps(dconv, s)
        du = w[0:1] * dnxt + w[1:2] * dconv + w[2:3] * dprev
        dcc_ref[...] = (du * ci).astype(dcc_ref.dtype)
        dci_ref[...] = (du * cc).astype(dci_ref.dtype)
        dw_ref[0:1, :] = jnp.sum(dconv * prev, axis=0, keepdims=True)
        dw_ref[1:2, :] = jnp.sum(dconv * u, axis=0, keepdims=True)
        dw_ref[2:3, :] = jnp.sum(dconv * nxt, axis=0, keepdims=True)

    col = pl.BlockSpec((s, CONV_TC), lambda j: (0, j))
    wspec = pl.BlockSpec((3, CONV_TC), lambda j: (0, j))
    return pl.pallas_call(
        body, name=name, grid=(CONV_WIDTH // CONV_TC,),
        in_specs=[col, _zcol(s, OFF_CB), _zcol(s, OFF_CC), _zcol(s, OFF_CI), wspec] + [ANY] * len(after),
        out_specs=[col, col, col, wspec],
        out_shape=[_sds((s, CONV_WIDTH), MXU_DTYPE)] * 3 + [_sds((3, CONV_WIDTH), F32)],
        compiler_params=_params(("parallel",)),
    )(dt, z, z, z, w3, *after)


def rope_tables(s):
    n_freq = HEAD_DIM // 4
    t = jnp.arange(s, dtype=jnp.int32)
    inv_freq = ROPE_THETA ** (-jnp.arange(0, HEAD_DIM // 2, 2, dtype=F32) / (HEAD_DIM // 2))
    ang_r = (t // GRID_W).astype(F32)[:, None] * inv_freq
    ang_c = (t % GRID_W).astype(F32)[:, None] * inv_freq
    cos_t = jnp.concatenate([jnp.cos(ang_r)] * 2 + [jnp.cos(ang_c)] * 2, axis=1)
    sin_t = jnp.concatenate([-jnp.sin(ang_r), jnp.sin(ang_r), -jnp.sin(ang_c), jnp.sin(ang_c)], axis=1)
    assert cos_t.shape == (s, 4 * n_freq)
    return cos_t, sin_t


def _swap_halves(v):
    lane = lax.broadcasted_iota(jnp.int32, v.shape, 1)
    return jnp.where(lane % 64 < 32, pltpu.roll(v, HEAD_DIM - 32, 1), pltpu.roll(v, 32, 1))


def qk_prep_fwd(z, qn, kn, cos_t, sin_t, *, name):
    s = z.shape[0]
    ts = min(s, 512)

    def body(q_ref, k_ref, v_ref, qn_ref, kn_ref, c_ref, s_ref, qo_ref, ko_ref, vo_ref):
        cs, sn = c_ref[...], s_ref[...]

        def head(x, g, scale):
            n = x * _rstd(x) * g
            return (n * cs + _swap_halves(n) * sn) * scale

        for h in range(N_Q_HEADS):
            sl = slice(h * HEAD_DIM, (h + 1) * HEAD_DIM)
            qo_ref[:, sl] = head(q_ref[:, sl].astype(F32), qn_ref[...], ATTN_SCALE).astype(qo_ref.dtype)
        for h in range(N_KV_HEADS):
            sl = slice(h * HEAD_DIM, (h + 1) * HEAD_DIM)
            ko_ref[:, sl] = head(k_ref[:, sl].astype(F32), kn_ref[...], 1.0).astype(ko_ref.dtype)
        vo_ref[...] = v_ref[...].astype(vo_ref.dtype)

    tab = _rows(ts, HEAD_DIM)
    gsp = _fixed((1, HEAD_DIM))
    return pl.pallas_call(
        body, name=name, grid=(s // ts,),
        in_specs=[_rows(ts, ATTN_WIDTH, OFF_Q // ATTN_WIDTH), _rows(ts, KV_WIDTH, OFF_K // KV_WIDTH),
                  _rows(ts, KV_WIDTH, OFF_V // KV_WIDTH), gsp, gsp, tab, tab],
        out_specs=[_rows(ts, ATTN_WIDTH), _rows(ts, KV_WIDTH), _rows(ts, KV_WIDTH)],
        out_shape=[_sds((s, ATTN_WIDTH), MXU_DTYPE), _sds((s, KV_WIDTH), MXU_DTYPE), _sds((s, KV_WIDTH), MXU_DTYPE)],
        compiler_params=_params(("parallel",)),
    )(z, z, z, qn, kn, cos_t, sin_t)


def qk_prep_bwd(dqp, dkp, z, qn, kn, cos_t, sin_t, *, name):
    s = z.shape[0]
    ts = min(s, 512)
    nt = s // ts

    def body(dq_ref, dk_ref, q_ref, k_ref, qn_ref, kn_ref, c_ref, s_ref, dqo_ref, dko_ref, dqn_ref, dkn_ref,
             qacc_ref, kacc_ref):
        cs, sn = c_ref[...], s_ref[...]
        i = pl.program_id(0)

        @pl.when(i == 0)
        def _():
            qacc_ref[...] = jnp.zeros_like(qacc_ref)
            kacc_ref[...] = jnp.zeros_like(kacc_ref)

        def head(x, g, dout, scale):
            d = dout.astype(F32) * scale
            dn = d * cs + _swap_halves(d * sn)
            return _rms_bwd(x, g, dn)

        qacc = jnp.zeros((SUBLANE, HEAD_DIM), F32)
        for h in range(N_Q_HEADS):
            sl = slice(h * HEAD_DIM, (h + 1) * HEAD_DIM)
            dx, dg_rows = head(q_ref[:, sl].astype(F32), qn_ref[...], dq_ref[:, sl], ATTN_SCALE)
            dqo_ref[:, sl] = dx.astype(dqo_ref.dtype)
            qacc = qacc + _fold8(dg_rows)
        kacc = jnp.zeros((SUBLANE, HEAD_DIM), F32)
        for h in range(N_KV_HEADS):
            sl = slice(h * HEAD_DIM, (h + 1) * HEAD_DIM)
            dx, dg_rows = head(k_ref[:, sl].astype(F32), kn_ref[...], dk_ref[:, sl], 1.0)
            dko_ref[:, sl] = dx.astype(dko_ref.dtype)
            kacc = kacc + _fold8(dg_rows)
        qacc_ref[...] += qacc
        kacc_ref[...] += kacc

        @pl.when(i == nt - 1)
        def _():
            dqn_ref[...] = qacc_ref[...].sum(axis=0, keepdims=True)
            dkn_ref[...] = kacc_ref[...].sum(axis=0, keepdims=True)

    tab = _rows(ts, HEAD_DIM)
    gsp = _fixed((1, HEAD_DIM))
    return pl.pallas_call(
        body, name=name, grid=(nt,),
        in_specs=[_rows(ts, ATTN_WIDTH), _rows(ts, KV_WIDTH), _rows(ts, ATTN_WIDTH, OFF_Q // ATTN_WIDTH),
                  _rows(ts, KV_WIDTH, OFF_K // KV_WIDTH), gsp, gsp, tab, tab],
        out_specs=[_rows(ts, ATTN_WIDTH), _rows(ts, KV_WIDTH), gsp, gsp],
        out_shape=[_sds((s, ATTN_WIDTH), MXU_DTYPE), _sds((s, KV_WIDTH), MXU_DTYPE), _sds((1, HEAD_DIM), F32),
                   _sds((1, HEAD_DIM), F32)],
        scratch_shapes=[pltpu.VMEM((SUBLANE, HEAD_DIM), F32)] * 2, compiler_params=_params(("arbitrary",)),
    )(dqp, dkp, z, z, qn, kn, cos_t, sin_t)


_NT = (((1,), (1,)), ((), ()))
_GW = GROUP * HEAD_DIM


def _dot(a, b, dims=(((1,), (0,)), ((), ()))):
    return lax.dot_general(a, b, dims, preferred_element_type=F32)


def attn_fwd(qp, kp, vb, *, name, after=()):
    s = qp.shape[0]
    tq = min(s, 1024)
    rows = min(tq, 128)

    def body(q_ref, k_ref, v_ref, *rest):
        o_ref, lse_ref, v_ones = rest[-3:]

        @pl.when(pl.program_id(1) == 0)
        def _():
            v_ones[:, :HEAD_DIM] = v_ref[...]
            v_ones[:, HEAD_DIM:] = jnp.ones((s, HEAD_DIM), v_ones.dtype)

        k = k_ref[...]
        for g in range(GROUP):
            sl = slice(g * HEAD_DIM, (g + 1) * HEAD_DIM)
            for r0 in range(0, tq, rows):
                rs = slice(r0, r0 + rows)
                sc = _dot(q_ref[rs, sl], k, _NT)
                mx = jnp.max(sc, axis=-1, keepdims=True)
                p = jnp.exp(sc - mx).astype(v_ones.dtype)
                o_den = _dot(p, v_ones[...])
                den = o_den[:, HEAD_DIM:HEAD_DIM + 1]
                o_ref[rs, sl] = (o_den[:, :HEAD_DIM] / den).astype(o_ref.dtype)
                lse_ref[rs, g:g + 1] = mx + jnp.log(den)

    return pl.pallas_call(
        body, name=name, grid=(N_KV_HEADS, s // tq),
        in_specs=[pl.BlockSpec((tq, _GW), lambda j, i: (i, j)), pl.BlockSpec((s, HEAD_DIM), lambda j, i: (0, j)),
                  pl.BlockSpec((s, HEAD_DIM), lambda j, i: (0, j))] + [ANY] * len(after),
        out_specs=[pl.BlockSpec((tq, _GW), lambda j, i: (i, j)), pl.BlockSpec((None, tq, GROUP), lambda j, i: (j, i, 0))],
        out_shape=[_sds((s, ATTN_WIDTH), MXU_DTYPE), _sds((N_KV_HEADS, s, GROUP), F32)],
        scratch_shapes=[pltpu.VMEM((s, 2 * HEAD_DIM), MXU_DTYPE)], compiler_params=_params(("parallel", "arbitrary")),
    )(qp, kp, vb, *after)


ATTN_BWD_TQ = 256


def attn_bwd(qp, kp, vb, do, o, lse, *, name):
    s = qp.shape[0]
    tq = min(s, ATTN_BWD_TQ)
    nq = s // tq
    over_rows = (((0,), (0,)), ((), ()))

    def body(q_ref, k_ref, v_ref, do_ref, o_ref, lse_ref, dq_ref, dk_ref, dv_ref, p_all, ds_all, q_all, do_all, dk_acc,
             dv_acc):
        i = pl.program_id(1)

        @pl.when(i == 0)
        def _():
            dk_acc[...] = jnp.zeros_like(dk_acc)
            dv_acc[...] = jnp.zeros_like(dv_acc)

        k, v = k_ref[...], v_ref[...]
        for g in range(GROUP):
            sl = slice(g * HEAD_DIM, (g + 1) * HEAD_DIM)
            rows = slice(g * tq, (g + 1) * tq)
            qg, dog = q_ref[:, sl], do_ref[:, sl]
            dd = jnp.sum(dog.astype(F32) * o_ref[:, sl].astype(F32), axis=-1, keepdims=True)
            p = jnp.exp(_dot(qg, k, _NT) - lse_ref[:, g:g + 1])
            ds = (p * (_dot(dog, v, _NT) - dd)).astype(k.dtype)
            dq_ref[:, sl] = _dot(ds, k).astype(dq_ref.dtype)
            p_all[rows, :] = p.astype(p_all.dtype)
            ds_all[rows, :] = ds
            q_all[rows, :] = qg
            do_all[rows, :] = dog
        dv_acc[...] += _dot(p_all[...], do_all[...], over_rows)
        dk_acc[...] += _dot(ds_all[...], q_all[...], over_rows)

        @pl.when(i == nq - 1)
        def _():
            dk_ref[...] = dk_acc[...].astype(dk_ref.dtype)
            dv_ref[...] = dv_acc[...].astype(dv_ref.dtype)

    qspec = pl.BlockSpec((tq, _GW), lambda j, i: (i, j))
    kspec = pl.BlockSpec((s, HEAD_DIM), lambda j, i: (0, j))
    lspec = pl.BlockSpec((None, tq, GROUP), lambda j, i: (j, i, 0))
    return pl.pallas_call(
        body, name=name, grid=(N_KV_HEADS, nq), in_specs=[qspec, kspec, kspec, qspec, qspec, lspec],
        out_specs=[qspec, kspec, kspec],
        out_shape=[_sds((s, ATTN_WIDTH), ACT_DTYPE), _sds((s, KV_WIDTH), ACT_DTYPE), _sds((s, KV_WIDTH), MXU_DTYPE)],
        scratch_shapes=[pltpu.VMEM((GROUP * tq, s), MXU_DTYPE), pltpu.VMEM((GROUP * tq, s), MXU_DTYPE),
                        pltpu.VMEM((GROUP * tq, HEAD_DIM), MXU_DTYPE), pltpu.VMEM((GROUP * tq, HEAD_DIM), MXU_DTYPE),
                        pltpu.VMEM((s, HEAD_DIM), F32), pltpu.VMEM((s, HEAD_DIM), F32)],
        compiler_params=_params(("parallel", "arbitrary")),
    )(qp, kp, vb, do, o, lse)


GATE_TW = 1024


def gate_fwd(ya, yb, z, bias, *, name):
    s = z.shape[0]
    ts = min(s, 512)

    def body(ya_ref, yb_ref, ga_ref, gb_ref, ba_ref, bb_ref, o_ref):
        sa = jax.nn.sigmoid(ga_ref[...].astype(F32) + ba_ref[...])
        sb = jax.nn.sigmoid(gb_ref[...].astype(F32) + bb_ref[...])
        o_ref[...] = (sa * ya_ref[...].astype(F32) + sb * yb_ref[...].astype(F32)).astype(o_ref.dtype)

    tile = pl.BlockSpec((ts, GATE_TW), lambda i, j: (i, j))
    ga = pl.BlockSpec((ts, GATE_TW), lambda i, j: (i, OFF_GA // GATE_TW + j))
    gb = pl.BlockSpec((ts, GATE_TW), lambda i, j: (i, OFF_GB // GATE_TW + j))
    ba = pl.BlockSpec((1, GATE_TW), lambda i, j: (0, j))
    bb = pl.BlockSpec((1, GATE_TW), lambda i, j: (0, D_MODEL // GATE_TW + j))
    return pl.pallas_call(
        body, name=name, grid=(s // ts, D_MODEL // GATE_TW), in_specs=[tile, tile, ga, gb, ba, bb], out_specs=tile,
        out_shape=_sds((s, D_MODEL), MXU_DTYPE), compiler_params=_params(("parallel", "parallel")),
    )(ya, yb, z, z, bias, bias)


def gate_bwd(dmi, ya, yb, z, bias, *, name):
    s = z.shape[0]
    ts = min(s, 512)
    nt = s // ts

    def body(d_ref, ya_ref, yb_ref, ga_ref, gb_ref, ba_ref, bb_ref, dya_ref, dyb_ref, dga_ref, dgb_ref, dba_ref,
             dbb_ref, acc_a, acc_b):
        d = d_ref[...].astype(F32)
        sa = jax.nn.sigmoid(ga_ref[...].astype(F32) + ba_ref[...])
        sb = jax.nn.sigmoid(gb_ref[...].astype(F32) + bb_ref[...])
        dya_ref[...] = (d * sa).astype(dya_ref.dtype)
        dyb_ref[...] = (d * sb).astype(dyb_ref.dtype)
        dga = d * ya_ref[...].astype(F32) * sa * (1.0 - sa)
        dgb = d * yb_ref[...].astype(F32) * sb * (1.0 - sb)
        dga_ref[...] = dga.astype(dga_ref.dtype)
        dgb_ref[...] = dgb.astype(dgb_ref.dtype)
        i = pl.program_id(1)

        @pl.when(i == 0)
        def _():
            acc_a[...] = jnp.zeros_like(acc_a)
            acc_b[...] = jnp.zeros_like(acc_b)

        acc_a[...] += _fold8(dga)
        acc_b[...] += _fold8(dgb)

        @pl.when(i == nt - 1)
        def _():
            dba_ref[...] = acc_a[...].sum(axis=0, keepdims=True)
            dbb_ref[...] = acc_b[...].sum(axis=0, keepdims=True)

    tile = pl.BlockSpec((ts, GATE_TW), lambda j, i: (i, j))
    ga = pl.BlockSpec((ts, GATE_TW), lambda j, i: (i, OFF_GA // GATE_TW + j))
    gb = pl.BlockSpec((ts, GATE_TW), lambda j, i: (i, OFF_GB // GATE_TW + j))
    ba = pl.BlockSpec((1, GATE_TW), lambda j, i: (0, j))
    bb = pl.BlockSpec((1, GATE_TW), lambda j, i: (0, D_MODEL // GATE_TW + j))
    acc = pltpu.VMEM((SUBLANE, GATE_TW), F32)
    return pl.pallas_call(
        body, name=name, grid=(D_MODEL // GATE_TW, nt), in_specs=[tile, tile, tile, ga, gb, ba, bb],
        out_specs=[tile, tile, tile, tile, ba, ba],
        out_shape=[_sds((s, D_MODEL), MXU_DTYPE)] * 4 + [_sds((1, D_MODEL), F32)] * 2,
        scratch_shapes=[acc, acc], compiler_params=_params(("parallel", "arbitrary")),
    )(dmi, ya, yb, z, z, bias, bias)


HBM = pl.BlockSpec(memory_space=pltpu.HBM)
SEM = pl.BlockSpec(memory_space=pltpu.SEMAPHORE)
EFFECT = pltpu.SideEffectType.DATAFLOW_SIDE_EFFECTING
TOKEN = (SUBLANE, LANE)


def _place():
    return lax.axis_index("x"), lax.axis_index("y"), lax.axis_index("c")


def _window(ref, shard_shape, axis, d):
    r, c = shard_shape
    lead = (slice(None),) * (len(ref.shape) - 2)
    if axis == 0:
        return ref.at[lead + (pl.ds(pl.multiple_of(d * r, SUBLANE), r), slice(None))]
    return ref.at[lead + (slice(None), pl.ds(pl.multiple_of(d * c, LANE), c))]


def _hbm(a):
    return pltpu.with_memory_space_constraint(a, pltpu.HBM)


def _remote(src, dst, send_sems, recv_sems, i, to):
    return pltpu.make_async_remote_copy(src_ref=src, dst_ref=dst, send_sem=send_sems.at[i], recv_sem=recv_sems.at[i],
                                        device_id=to, device_id_type=MESH)


def cast_into_full(w, layer, axis, me, dtype, *, name, after=()):
    _, r, c = w.shape
    tr = min(r, 256)
    nr = r // tr
    in_spec = pl.BlockSpec((None, tr, c), lambda i, me_ref: (layer, i, 0))
    if axis == 0:
        out_spec = pl.BlockSpec((tr, c), lambda i, me_ref: (me_ref[0] * nr + i, 0))
        shape = (N_DEV * r, c)
    else:
        out_spec = pl.BlockSpec((tr, c), lambda i, me_ref: (i, me_ref[0]))
        shape = (r, N_DEV * c)

    def body(me_ref, w_ref, *rest):
        o_ref = rest[-1]
        o_ref[...] = w_ref[...].astype(o_ref.dtype)

    return pl.pallas_call(
        body, name=name,
        grid_spec=pltpu.PrefetchScalarGridSpec(num_scalar_prefetch=1, grid=(nr,), in_specs=[in_spec] + [ANY] * len(after),
                                               out_specs=out_spec),
        out_shape=_sds(shape, dtype), compiler_params=_params(("parallel",)),
    )(me, w, *after)


class _GatherPlan:
    def __init__(self, fulls, shard_shapes, axes):
        x, y, c = _place()
        self.n = len(fulls)
        self.me, self.sibling = (x, y, c), (x, y, 1 - c)
        self.chips = [(1 - x, y), (x, 1 - y), (1 - x, 1 - y)]
        self.win = lambda a, p: _window(fulls[a], shard_shapes[a], axes[a], 4 * p[0] + 2 * p[1] + p[2])

    def first(self, send_sems, recv_sems):
        out = []
        for a in range(self.n):
            mine = self.win(a, self.me)
            out.append(_remote(mine, mine, send_sems, recv_sems, 4 * a, self.sibling))
            out += [_remote(mine, mine, send_sems, recv_sems, 4 * a + 1 + j, (*chip, self.me[2]))
                    for j, chip in enumerate(self.chips)]
        return out

    def first_arrivals(self, send_sems, recv_sems):
        c = self.me[2]
        out = []
        for a in range(self.n):
            blocks = [self.sibling] + [(*chip, c) for chip in self.chips]
            out += [_remote(self.win(a, b), self.win(a, b), send_sems, recv_sems, 4 * a + k, self.me)
                    for k, b in enumerate(blocks)]
        return out

    def passed(self, send_sems, recv_sems):
        c = self.me[2]
        return [_remote(self.win(a, (*chip, c)), self.win(a, (*chip, c)), send_sems, recv_sems, 3 * a + j, self.sibling)
                for a in range(self.n) for j, chip in enumerate(self.chips)]

    def passed_arrivals(self, send_sems, recv_sems):
        c = self.me[2]
        return [_remote(self.win(a, (*chip, 1 - c)), self.win(a, (*chip, 1 - c)), send_sems, recv_sems, 3 * a + j, self.me)
                for a in range(self.n) for j, chip in enumerate(self.chips)]


def gather_start(fulls, after, shard_shapes, axes, *, name):
    n = len(fulls)

    def body(*refs):
        ins, send_sems, recv_sems, token = refs[:n], refs[n + 1], refs[n + 2], refs[-1]
        for cp in _GatherPlan(ins, shard_shapes, axes).first(send_sems, recv_sems):
            cp.start()
        token[...] = jnp.zeros_like(token)

    res = pl.pallas_call(
        body, name=name, in_specs=[HBM] * n + [ANY],
        out_shape=(pltpu.SemaphoreType.DMA((4 * n,)), pltpu.SemaphoreType.DMA((4 * n,)),
                   *[pltpu.HBM(f.shape, f.dtype) for f in fulls], _sds(TOKEN, F32)),
        out_specs=(SEM, SEM, *[HBM] * n, pl.BlockSpec(memory_space=pltpu.VMEM)),
        input_output_aliases={a: 2 + a for a in range(n)}, compiler_params=pltpu.CompilerParams(has_side_effects=EFFECT),
    )(*[_hbm(f) for f in fulls], after)
    return res[0], res[1], list(res[2:2 + n]), res[-1]


def gather_pass(send_sems, recv_sems, fulls, after, shard_shapes, axes, *, name):
    n = len(fulls)

    def body(*refs):
        ins, s1, r1 = refs[:n], refs[n], refs[n + 1]
        s2, r2, token = refs[n + 3], refs[n + 4], refs[-1]
        plan = _GatherPlan(ins, shard_shapes, axes)
        for cp in plan.first_arrivals(s1, r1):
            cp.wait_recv()
        for cp in plan.first(s1, r1):
            cp.wait_send()
        for cp in plan.passed(s2, r2):
            cp.start()
        token[...] = jnp.zeros_like(token)

    res = pl.pallas_call(
        body, name=name, in_specs=[HBM] * n + [SEM, SEM, ANY],
        out_shape=(pltpu.SemaphoreType.DMA((3 * n,)), pltpu.SemaphoreType.DMA((3 * n,)),
                   *[pltpu.HBM(f.shape, f.dtype) for f in fulls], _sds(TOKEN, F32)),
        out_specs=(SEM, SEM, *[HBM] * n, pl.BlockSpec(memory_space=pltpu.VMEM)),
        input_output_aliases={a: 2 + a for a in range(n)}, compiler_params=pltpu.CompilerParams(has_side_effects=EFFECT),
    )(*fulls, send_sems, recv_sems, after)
    return res[0], res[1], list(res[2:2 + n]), res[-1]


def gather_wait(send_sems, recv_sems, fulls, after, shard_shapes, axes, *, name):
    n = len(fulls)

    def body(*refs):
        ins, s2, r2 = refs[:n], refs[n], refs[n + 1]
        plan = _GatherPlan(ins, shard_shapes, axes)
        for cp in plan.passed_arrivals(s2, r2):
            cp.wait_recv()
        for cp in plan.passed(s2, r2):
            cp.wait_send()

    return list(pl.pallas_call(
        body, name=name, in_specs=[HBM] * n + [SEM, SEM, ANY], out_shape=tuple(pltpu.HBM(f.shape, f.dtype) for f in fulls),
        out_specs=tuple([HBM] * n), input_output_aliases={a: a for a in range(n)},
        compiler_params=pltpu.CompilerParams(has_side_effects=EFFECT),
    )(*fulls, send_sems, recv_sems, after))


def _pair_copies(grads, lands, shard_shapes, axes, send_sems, recv_sems):
    x, y, c = _place()
    return [_remote(_window(grads[a], shard_shapes[a], axes[a], 2 * q + (1 - c)), lands[a].at[q], send_sems, recv_sems,
                    N_CHIP * a + q, (x, y, 1 - c))
            for a in range(len(grads)) for q in range(N_CHIP)]


def _chip_sends(sums, lands, send_sems, recv_sems):
    x, y, c = _place()
    return [_remote(sums[a].at[2 * px + py], lands[a].at[2 * x + y], send_sems, recv_sems, 3 * a + j, (px, py, c))
            for a in range(len(sums)) for j, (px, py) in enumerate([(1 - x, y), (x, 1 - y), (1 - x, 1 - y)])]


def _chip_arrivals(sums, lands, send_sems, recv_sems):
    x, y, c = _place()
    return [_remote(sums[a].at[2 * x + y], lands[a].at[2 * px + py], send_sems, recv_sems, 3 * a + j, (x, y, c))
            for a in range(len(sums)) for j, (px, py) in enumerate([(1 - x, y), (x, 1 - y), (1 - x, 1 - y)])]


def exchange_start(srcs, after, land_shapes, make_sends, per_array, *, name):
    n = len(srcs)
    lands = [lax.empty(s, a.dtype) for s, a in zip(land_shapes, srcs)]

    def body(*refs):
        ins, zones = refs[:n], refs[n:2 * n]
        send_sems, recv_sems, token = refs[2 * n + 1], refs[2 * n + 2], refs[-1]
        for cp in make_sends(ins, zones, send_sems, recv_sems):
            cp.start()
        token[...] = jnp.zeros_like(token)

    res = pl.pallas_call(
        body, name=name, in_specs=[HBM] * (2 * n) + [ANY],
        out_shape=(pltpu.SemaphoreType.DMA((per_array * n,)), pltpu.SemaphoreType.DMA((per_array * n,)),
                   *[pltpu.HBM(a.shape, a.dtype) for a in srcs], *[pltpu.HBM(a.shape, a.dtype) for a in lands],
                   _sds(TOKEN, F32)),
        out_specs=(SEM, SEM, *[HBM] * (2 * n), pl.BlockSpec(memory_space=pltpu.VMEM)),
        input_output_aliases={a: 2 + a for a in range(2 * n)}, compiler_params=pltpu.CompilerParams(has_side_effects=EFFECT),
    )(*[_hbm(a) for a in srcs], *[_hbm(a) for a in lands], after)
    return res[0], res[1], list(res[2:2 + n]), list(res[2 + n:2 + 2 * n]), res[-1]


def exchange_wait(send_sems, recv_sems, srcs, lands, after, make_waits, *, name):
    n = len(srcs)

    def body(*refs):
        ins, zones, s, r = refs[:n], refs[n:2 * n], refs[2 * n], refs[2 * n + 1]
        sends, arrivals = make_waits(ins, zones, s, r)
        for cp in arrivals:
            cp.wait_recv()
        for cp in sends:
            cp.wait_send()

    res = pl.pallas_call(
        body, name=name, in_specs=[HBM] * (2 * n) + [SEM, SEM, ANY],
        out_shape=tuple(pltpu.HBM(a.shape, a.dtype) for a in (*srcs, *lands)), out_specs=tuple([HBM] * (2 * n)),
        input_output_aliases={a: a for a in range(2 * n)}, compiler_params=pltpu.CompilerParams(has_side_effects=EFFECT),
    )(*srcs, *lands, send_sems, recv_sems, after)
    return list(res[:n]), list(res[n:])


def pair_sum(grad, recv, shard_shape, axis, core, *, name):
    r, c = shard_shape
    tr = min(r, 1024)
    nr = r // tr
    if axis == 0:
        gspec = pl.BlockSpec((tr, c), lambda q, i, cref: ((2 * q + cref[0]) * nr + i, 0))
    else:
        gspec = pl.BlockSpec((tr, c), lambda q, i, cref: (i, 2 * q + cref[0]))
    rspec = pl.BlockSpec((None, tr, c), lambda q, i, cref: (q, i, 0))

    def body(c_ref, g_ref, r_ref, o_ref):
        o_ref[...] = (g_ref[...].astype(F32) + r_ref[...].astype(F32)).astype(o_ref.dtype)

    return pl.pallas_call(
        body, name=name,
        grid_spec=pltpu.PrefetchScalarGridSpec(num_scalar_prefetch=1, grid=(N_CHIP, nr), in_specs=[gspec, rspec],
                                               out_specs=rspec),
        out_shape=_sds((N_CHIP, r, c), recv.dtype), compiler_params=_params(("parallel", "parallel")),
    )(core, grad, recv)


def all_reduce_small(part, *, name):
    r, c = part.shape

    def body(p_ref, o_ref, gath_ref, send_sems, recv_sems):
        x, y, cc = _place()
        me = 4 * x + 2 * y + cc
        gath_ref[me] = p_ref[...]
        peers = [(1 - x if k & 4 else x, 1 - y if k & 2 else y, 1 - cc if k & 1 else cc) for k in range(1, N_DEV)]
        copies = [_remote(p_ref, gath_ref.at[me], send_sems, recv_sems, i, peer) for i, peer in enumerate(peers)]
        for cp in copies:
            cp.start()
        for i, (px, py, pc) in enumerate(peers):
            _remote(p_ref, gath_ref.at[4 * px + 2 * py + pc], send_sems, recv_sems, i, (x, y, cc)).wait_recv()
        for cp in copies:
            cp.wait_send()
        acc = gath_ref[0]
        for d in range(1, N_DEV):
            acc = acc + gath_ref[d]
        o_ref[...] = acc

    vm = pl.BlockSpec(memory_space=pltpu.VMEM)
    return pl.pallas_call(
        body, name=name, in_specs=[vm], out_specs=vm, out_shape=_sds((r, c), F32),
        scratch_shapes=[pltpu.VMEM((N_DEV, r, c), F32), pltpu.SemaphoreType.DMA((N_DEV - 1,)),
                        pltpu.SemaphoreType.DMA((N_DEV - 1,))],
    )(part)


def _adamw(w, g, m, v):
    m = ADAM_B1 * m + (1.0 - ADAM_B1) * g
    v = ADAM_B2 * v + (1.0 - ADAM_B2) * (g * g)
    m_hat = m / (1.0 - ADAM_B1 ** ADAM_STEP)
    v_hat = v / (1.0 - ADAM_B2 ** ADAM_STEP)
    delta = -ADAM_LR * (m_hat / (jnp.sqrt(v_hat) + ADAM_EPS) + ADAM_WD * w)
    return delta, m, v


def reduce_adam(own, landed, chip, w, m, v, layer, outs, *, name):
    _, r, c = w.shape
    tr = min(r, 256)
    first = outs is None

    def body(chip_ref, own_ref, l1_ref, l2_ref, l3_ref, w_ref, m_ref, v_ref, *rest):
        g_out, d_out, m_out, v_out = rest[-4:]
        g = own_ref[...].astype(F32) + l1_ref[...].astype(F32) + l2_ref[...].astype(F32) + l3_ref[...].astype(F32)
        d, mn, vn = _adamw(w_ref[...], g, m_ref[...], v_ref[...])
        g_out[...] = g
        d_out[...] = d
        m_out[...] = mn
        v_out[...] = vn

    spec = pl.BlockSpec((None, tr, c), lambda i, chip_ref: (layer, i, 0))

    def slot(step):
        return pl.BlockSpec((None, tr, c), lambda i, chip_ref: ((chip_ref[0] + step) % N_CHIP, i, 0))

    n_in = 8
    return pl.pallas_call(
        body, name=name,
        grid_spec=pltpu.PrefetchScalarGridSpec(
            num_scalar_prefetch=1, grid=(r // tr,),
            in_specs=[slot(0), slot(1), slot(2), slot(3), spec, spec, spec] + ([] if first else [ANY] * 4),
            out_specs=[spec] * 4),
        out_shape=[_sds(w.shape, F32)] * 4, input_output_aliases={} if first else {n_in + i: i for i in range(4)},
        compiler_params=_params(("parallel",)),
    )(chip, own, landed, landed, landed, w, m, v, *([] if first else outs))


def adam_small(g, w, m, v, *, name):
    def body(g_ref, w_ref, m_ref, v_ref, d_out, m_out, v_out):
        d, mn, vn = _adamw(w_ref[...], g_ref[...], m_ref[...], v_ref[...])
        d_out[...] = d
        m_out[...] = mn
        v_out[...] = vn

    return pl.pallas_call(body, name=name, out_shape=[_sds(w.shape, F32)] * 3)(g, w, m, v)


BIG = ("w_in", "w_out_conv", "w_out_attn", "w_merge", "w_up", "w_down")
GATHERED = BIG + ("conv_w",)
BIG_AXIS = {"w_in": 1, "w_out_conv": 0, "w_out_attn": 0, "w_merge": 0, "w_up": 1, "w_down": 0, "conv_w": 1}
FIRST_GATHER_GROUPS = (("w_in", "conv_w"), ("w_out_conv", "w_out_attn", "w_merge"), ("w_up", "w_down"))
LAST_REDUCE_GROUPS = (("w_down", "w_up"), ("w_merge", "w_out_conv", "w_out_attn"), ("w_in",))


class _Gather:
    def __init__(self, weights, keys, own, after, tag):
        self.keys, self.tag = keys, tag
        self.shapes = [weights[k].shape[1:] for k in keys]
        self.axes = [BIG_AXIS[k] for k in keys]
        self.send, self.recv, self.fulls, self.token = gather_start([own[k] for k in keys], after, self.shapes, self.axes,
                                                                    name="gather_start_" + tag)

    def pass_on(self, after):
        self.send, self.recv, self.fulls, self.token = gather_pass(self.send, self.recv, self.fulls, after, self.shapes,
                                                                   self.axes, name="gather_pass_" + self.tag)
        return self.token

    def wait(self, after):
        fulls = gather_wait(self.send, self.recv, self.fulls, after, self.shapes, self.axes, name="gather_wait_" + self.tag)
        return dict(zip(self.keys, fulls))


class _Reduce:
    def __init__(self, weights, keys, tag):
        self.keys, self.tag = keys, tag
        self.shapes = [weights[k].shape[1:] for k in keys]
        self.axes = [BIG_AXIS[k] for k in keys]
        self.pair_shapes = [(N_CHIP, *shp) for shp in self.shapes]

    def _pair(self, i, z, ss, rs):
        return _pair_copies(i, z, self.shapes, self.axes, ss, rs)

    def begin(self, grads, after):
        self.send, self.recv, self.src, self.land, self.token = exchange_start(
            [grads[k] for k in self.keys], after, self.pair_shapes, self._pair, N_CHIP, name="rs_pair_start_" + self.tag)
        return self.token

    def middle(self, after, core):
        both = lambda i, z, ss, rs: (self._pair(i, z, ss, rs),) * 2
        grads, from_sibling = exchange_wait(self.send, self.recv, self.src, self.land, after, both,
                                            name="rs_pair_wait_" + self.tag)
        sums = [pair_sum(g, rcv, shp, ax, core, name="pair_sum_" + k)
                for k, g, rcv, shp, ax in zip(self.keys, grads, from_sibling, self.shapes, self.axes)]
        self.send, self.recv, self.src, self.land, self.token = exchange_start(
            sums, after, self.pair_shapes, _chip_sends, 3, name="rs_chips_start_" + self.tag)
        return self.token

    def end(self, after):
        both = lambda i, z, ss, rs: (_chip_sends(i, z, ss, rs), _chip_arrivals(i, z, ss, rs))
        sums, landed = exchange_wait(self.send, self.recv, self.src, self.land, after, both,
                                     name="rs_chips_wait_" + self.tag)
        return list(zip(self.keys, sums, landed))
SMALL = (("norm_mix_pre", D_MODEL), ("gate_bias", 2 * D_MODEL), ("norm_mix_post", D_MODEL), ("norm_mlp_pre", D_MODEL),
         ("norm_mlp_post", D_MODEL), ("q_norm", HEAD_DIM), ("k_norm", HEAD_DIM))
SMALL_WIDTH = sum(w for _, w in SMALL)
WEIGHTS = ("norm_mix_pre", "w_in", "gate_bias", "conv_w", "q_norm", "k_norm", "w_out_conv", "w_out_attn", "w_merge",
           "norm_mix_post", "norm_mlp_pre", "w_up", "w_down", "norm_mlp_post")


def kernel(x, norm_mix_pre, w_in, gate_bias, conv_w, q_norm, k_norm, w_out_conv, w_out_attn, w_merge, norm_mix_post, norm_mlp_pre, w_up, w_down, norm_mlp_post, loss_target, m_norm_mix_pre, m_w_in, m_gate_bias, m_conv_w, m_q_norm, m_k_norm, m_w_out_conv, m_w_out_attn, m_w_merge, m_norm_mix_post, m_norm_mlp_pre, m_w_up, m_w_down, m_norm_mlp_post, v_norm_mix_pre, v_w_in, v_gate_bias, v_conv_w, v_q_norm, v_k_norm, v_w_out_conv, v_w_out_attn, v_w_merge, v_norm_mix_post, v_norm_mlp_pre, v_w_up, v_w_down, v_norm_mlp_post):
    w = dict(norm_mix_pre=norm_mix_pre, w_in=w_in, gate_bias=gate_bias, conv_w=conv_w, q_norm=q_norm, k_norm=k_norm,
             w_out_conv=w_out_conv, w_out_attn=w_out_attn, w_merge=w_merge, norm_mix_post=norm_mix_post,
             norm_mlp_pre=norm_mlp_pre, w_up=w_up, w_down=w_down, norm_mlp_post=norm_mlp_post)
    mom = dict(norm_mix_pre=m_norm_mix_pre, w_in=m_w_in, gate_bias=m_gate_bias, conv_w=m_conv_w, q_norm=m_q_norm,
               k_norm=m_k_norm, w_out_conv=m_w_out_conv, w_out_attn=m_w_out_attn, w_merge=m_w_merge,
               norm_mix_post=m_norm_mix_post, norm_mlp_pre=m_norm_mlp_pre, w_up=m_w_up, w_down=m_w_down,
               norm_mlp_post=m_norm_mlp_post)
    var = dict(norm_mix_pre=v_norm_mix_pre, w_in=v_w_in, gate_bias=v_gate_bias, conv_w=v_conv_w, q_norm=v_q_norm,
               k_norm=v_k_norm, w_out_conv=v_w_out_conv, w_out_attn=v_w_out_attn, w_merge=v_w_merge,
               norm_mix_post=v_norm_mix_post, norm_mlp_pre=v_norm_mlp_pre, w_up=v_w_up, w_down=v_w_down,
               norm_mlp_post=v_norm_mlp_post)
    depth = w_in.shape[0]
    s = x.shape[1]
    xs = x.reshape(s, D_MODEL)
    target = loss_target.reshape(s, D_MODEL)
    x_idx, y_idx, c_idx = _place()
    as_operand = lambda i: jnp.reshape(i, (1,)).astype(jnp.int32)
    core, chip, me = as_operand(c_idx), as_operand(2 * x_idx + y_idx), as_operand(4 * x_idx + 2 * y_idx + c_idx)
    cos_t, sin_t = rope_tables(s)

    def vec(name, l):
        return w[name][l].reshape(1, -1)

    saved = []

    def cast_shards(l, keys, after):
        own = {}
        for k in keys:
            own[k] = cast_into_full(w[k], l, BIG_AXIS[k], me, F32 if k == "conv_w" else MXU_DTYPE, name="cast_" + k,
                                    after=[after])
            after = own[k]
        return own, after

    first, after = [], xs
    for i, keys in enumerate(FIRST_GATHER_GROUPS):
        own, after = cast_shards(0, keys, after)
        first.append(_Gather(w, keys, own, after, f"0{'abc'[i]}"))
        after = first[-1].token
    h = rms_fwd(xs, vec("norm_mix_pre", 0), name="rms_first")
    ahead = {}
    for l in range(1, depth):
        ahead[l], after = cast_shards(l, GATHERED, after)
    full = first[0].wait(first[0].pass_on(after))
    layers = []
    for l in range(depth):
        layers.append(full)
        nxt = l + 1 < depth
        if nxt:
            coming = _Gather(w, GATHERED, ahead[l + 1], full["w_in"], str(l + 1))
        z = matmul(h, full["w_in"], mode="nn", m=s, n=IN_WIDTH, k=D_MODEL, out_dtypes=[ACT_DTYPE], name="mm_in",
                   after=[coming.token] if nxt else [])
        t = conv_fwd(z, full["conv_w"], name="conv_fwd")
        qp, kp, vb = qk_prep_fwd(z, vec("q_norm", l), vec("k_norm", l), cos_t, sin_t, name="qk_fwd")
        o, lse = attn_fwd(qp, kp, vb, name="attn_fwd", after=[first[1].pass_on(qp)] if l == 0 else [])
        if l == 0:
            full.update(first[1].wait(o))
        ya = matmul(t, full["w_out_conv"], mode="nn", m=s, n=D_MODEL, k=CONV_WIDTH, out_dtypes=[ACT_DTYPE], name="mm_out_conv",
                    after=[first[2].pass_on(o)] if l == 0 else [])
        yb = matmul(o, full["w_out_attn"], mode="nn", m=s, n=D_MODEL, k=ATTN_WIDTH, out_dtypes=[ACT_DTYPE], name="mm_out_attn")
        mi = gate_fwd(ya, yb, z, vec("gate_bias", l), name="gate_fwd")
        mixed = matmul(mi, full["w_merge"], mode="nn", m=s, n=D_MODEL, k=D_MODEL, out_dtypes=[ACT_DTYPE], name="mm_merge")
        if l == 0:
            full.update(first[2].wait(mixed))
        x_mid, h2 = rms_residual_fwd(xs, mixed, vec("norm_mix_post", l), vec("norm_mlp_pre", l), name="res_mix")
        act, r = matmul(h2, full["w_up"], mode="nn", m=s, n=D_FF, k=D_MODEL, out_dtypes=[MXU_DTYPE, MXU_DTYPE],
                        name="mm_up", epilogue=lambda acc: (acc, jnp.square(jnp.maximum(acc, 0.0))))
        f = matmul(r, full["w_down"], mode="nn", m=s, n=D_MODEL, k=D_FF, tk=D_FF // 2, vmem=VMEM_LIMIT_DEEP_K,
                   out_dtypes=[ACT_DTYPE], name="mm_down",
                   after=[coming.pass_on(r)] if nxt and l > 0 else [])
        if nxt and l == 0:
            coming.pass_on(f)
        g_next = vec("norm_mix_pre", l + 1) if nxt else None
        x_out, h_next = rms_residual_fwd(x_mid, f, vec("norm_mlp_post", l), g_next, name="res_mlp" if nxt else "res_last")
        saved.append(dict(x_in=xs, h=h, z=z, t=t, qp=qp, kp=kp, vb=vb, o=o, lse=lse, ya=ya, yb=yb, mi=mi, mixed=mixed,
                          x_mid=x_mid, h2=h2, act=act, r=r, f=f))
        if nxt:
            full = coming.wait(x_out)
        xs, h = x_out, h_next

    dx, loss_part = loss_and_grad(xs, target, name="loss")
    loss = lax.psum(jnp.sum(loss_part), ("x", "y", "c"))

    small_rows = [None] * depth
    conv_rows = [None] * depth
    out_g, out_d, out_m, out_v = {}, {}, {}, {}
    big_outs = {k: None for k in BIG}
    pending = None
    last = []
    handed_down = None

    def finish(reduction, layer, after):
        for k, own, landed in reduction.end(after):
            big_outs[k] = reduce_adam(own, landed, chip, w[k], mom[k], var[k], layer, big_outs[k], name="adam_" + k)

    for l in reversed(range(depth)):
        sv, full = saved[l], layers[l]
        grads = {}
        groups = l == 0

        def wgrad(key, lhs, rhs, m, n, after=()):
            grads[key] = matmul(lhs, rhs, mode="tn", m=m, n=n, k=s, tk=s, out_dtypes=[COMM_DTYPE], name="wg_" + key,
                                after=after)

        def begin_group(i, after):
            last.append(_Reduce(w, LAST_REDUCE_GROUPS[i], f"{l}{'abc'[i]}"))
            return [last[i].begin(grads, after)]

        if handed_down is None:
            handed_down = rms_bwd(sv["f"], vec("norm_mlp_post", l), dx, None, out_dtype=MXU_DTYPE, name="rmsb_mlp_post")
        df, dg_mlp_post = handed_down
        da = matmul(df, full["w_down"], mode="nt", m=s, n=D_FF, k=D_MODEL, out_dtypes=[MXU_DTYPE], name="mm_d_down",
                    extra=(sv["act"],), epilogue=lambda acc, a: (acc * (2.0 * jnp.maximum(a.astype(F32), 0.0)),),
                    after=[pending.token] if pending else [])
        tokens = [pending.middle(da, core)] if pending else []
        wgrad("w_down", sv["r"], df, D_FF, D_MODEL)
        dh2 = matmul(da, full["w_up"], mode="nt", m=s, n=D_MODEL, k=D_FF, tk=D_FF // 2, vmem=VMEM_LIMIT_DEEP_K,
                     out_dtypes=[ACT_DTYPE], name="mm_d_up", after=tokens)
        wgrad("w_up", sv["h2"], da, D_MODEL, D_FF)
        tokens = begin_group(0, dh2) if groups else []
        dx_mid, dmixed, dg_mlp_pre, dg_mix_post = rms_bwd_pair(
            sv["x_mid"], vec("norm_mlp_pre", l), dh2, dx, sv["mixed"], vec("norm_mix_post", l), name="rmsb_mlp_pre_mix_post")
        dmi = matmul(dmixed, full["w_merge"], mode="nt", m=s, n=D_MODEL, k=D_MODEL, out_dtypes=[ACT_DTYPE], name="mm_d_merge",
                     after=tokens)
        tokens = [last[0].middle(dmi, core)] if groups else []
        wgrad("w_merge", sv["mi"], dmixed, D_MODEL, D_MODEL, after=tokens)
        dya, dyb, dga, dgb, dba, dbb = gate_bwd(dmi, sv["ya"], sv["yb"], sv["z"], vec("gate_bias", l), name="gate_bwd")
        wgrad("w_out_conv", sv["t"], dya, CONV_WIDTH, D_MODEL)
        wgrad("w_out_attn", sv["o"], dyb, ATTN_WIDTH, D_MODEL)
        tokens = begin_group(1, dyb) if groups else []
        dt = matmul(dya, full["w_out_conv"], mode="nt", m=s, n=CONV_WIDTH, k=D_MODEL, out_dtypes=[ACT_DTYPE],
                    name="mm_d_out_conv", after=tokens)
        do = matmul(dyb, full["w_out_attn"], mode="nt", m=s, n=ATTN_WIDTH, k=D_MODEL, out_dtypes=[MXU_DTYPE],
                    name="mm_d_out_attn")
        tokens = [last[1].middle(do, core)] if groups else []
        dcb, dcc, dci, dconv_w = conv_bwd(dt, sv["z"], full["conv_w"], name="conv_bwd", after=tokens)
        dqp, dkp, dv = attn_bwd(sv["qp"], sv["kp"], sv["vb"], do, sv["o"], sv["lse"], name="attn_bwd")
        dq, dk, dqn, dkn = qk_prep_bwd(dqp, dkp, sv["z"], vec("q_norm", l), vec("k_norm", l), cos_t, sin_t, name="qk_bwd")
        dz = jnp.concatenate([dcb, dcc, dci, dq, dk, dv, dga, dgb], axis=1)
        wgrad("w_in", sv["h"], dz, D_MODEL, IN_WIDTH)
        tokens = begin_group(2, dz) if groups else []
        dh = matmul(dz, full["w_in"], mode="nt", m=s, n=D_MODEL, k=IN_WIDTH, tk=IN_WIDTH // 4, out_dtypes=[ACT_DTYPE],
                    name="mm_d_in", after=tokens)
        if l > 0:
            dx, df_below, dg_mix_pre, dg_post_below = rms_bwd_pair(
                sv["x_in"], vec("norm_mix_pre", l), dh, dx_mid, saved[l - 1]["f"], vec("norm_mlp_post", l - 1),
                name="rmsb_mix_pre_mlp_post")
            handed_down = (df_below, dg_post_below)
        else:
            dx, dg_mix_pre = rms_bwd(sv["x_in"], vec("norm_mix_pre", l), dh, dx_mid, out_dtype=F32, name="rmsb_mix_pre")
        small_rows[l] = jnp.concatenate([dg_mix_pre, dba, dbb, dg_mix_post, dg_mlp_pre, dg_mlp_post, dqn, dkn], axis=1)
        conv_rows[l] = dconv_w.reshape(1, 3 * CONV_WIDTH)
        if pending:
            finish(pending, l + 1, dx)
        pending = None
        if not groups:
            pending = _Reduce(w, BIG, str(l))
            pending.begin(grads, dx)

    grad_x = dx.reshape(1, s, D_MODEL)

    small_part = jnp.concatenate([jnp.concatenate(small_rows, axis=0), jnp.concatenate(conv_rows, axis=0)], axis=1)
    small_sum = all_reduce_small(small_part, name="allreduce_small")
    pack = lambda src: jnp.concatenate([src[k].reshape(depth, wd) for k, wd in SMALL], axis=1)
    g_small = small_sum[:, :SMALL_WIDTH]
    d_small, m_small, v_small = adam_small(g_small, pack(w), pack(mom), pack(var), name="adam_small")
    off = 0
    for k, wd in SMALL:
        for dst, src in ((out_g, g_small), (out_d, d_small), (out_m, m_small), (out_v, v_small)):
            dst[k] = src[:, off:off + wd]
        off += wd
    cshard = CONV_WIDTH // N_DEV
    g_conv = lax.dynamic_slice_in_dim(small_sum[:, SMALL_WIDTH:].reshape(depth, 3, CONV_WIDTH), me[0] * cshard, cshard,
                                      axis=2).reshape(depth, 3 * cshard)
    flat = lambda a: a.reshape(depth, 3 * cshard)
    d_conv, m_conv, v_conv = adam_small(g_conv, flat(conv_w), flat(m_conv_w), flat(v_conv_w), name="adam_conv")
    for dst, src in ((out_g, g_conv), (out_d, d_conv), (out_m, m_conv), (out_v, v_conv)):
        dst["conv_w"] = src.reshape(depth, 3, cshard)

    token = last[2].middle(d_small, core)
    for reduction in last:
        finish(reduction, 0, token)
    for k in BIG:
        out_g[k], out_d[k], out_m[k], out_v[k] = big_outs[k]

    return (loss, grad_x, *[out_g[k] for k in WEIGHTS], *[out_d[k] for k in WEIGHTS], *[out_m[k] for k in WEIGHTS],
            *[out_v[k] for k in WEIGHTS])
```

```python
import math

import jax
import jax.numpy as jnp
from jax import lax
from jax.experimental import pallas as pl
from jax.experimental.pallas import tpu as pltpu

F32 = jnp.float32
MXU_DTYPE = jnp.bfloat16
COMM_DTYPE = jnp.bfloat16
ACT_DTYPE = jnp.bfloat16

D_MODEL = 2048
HEAD_DIM = 128
N_Q_HEADS = 16
N_KV_HEADS = 4
GROUP = N_Q_HEADS // N_KV_HEADS
ATTN_WIDTH = N_Q_HEADS * HEAD_DIM
KV_WIDTH = N_KV_HEADS * HEAD_DIM
CONV_WIDTH = D_MODEL
D_FF = 4 * D_MODEL
GRID_W = 64
ROPE_THETA = 10000.0
RMS_EPS = 1e-6
IN_WIDTH = 3 * CONV_WIDTH + ATTN_WIDTH + 2 * KV_WIDTH + 2 * D_MODEL
OFF_CB, OFF_CC, OFF_CI = 0, CONV_WIDTH, 2 * CONV_WIDTH
OFF_Q = 3 * CONV_WIDTH
OFF_K = OFF_Q + ATTN_WIDTH
OFF_V = OFF_K + KV_WIDTH
OFF_GA = OFF_V + KV_WIDTH
OFF_GB = OFF_GA + D_MODEL
ATTN_SCALE = 1.0 / math.sqrt(HEAD_DIM)

ADAM_LR, ADAM_B1, ADAM_B2, ADAM_EPS, ADAM_WD, ADAM_STEP = 0.001, 0.9, 0.999, 1e-08, 0.01, 10

N_DEV = 8
N_CHIP = 4
LANE = 128
SUBLANE = 8
VMEM_LIMIT = 48 * 1024 * 1024
VMEM_LIMIT_DEEP_K = 56 * 1024 * 1024
MESH = pl.DeviceIdType.MESH
ANY = pl.BlockSpec(memory_space=pl.ANY)


def _sds(shape, dtype):
    return jax.ShapeDtypeStruct(tuple(shape), dtype)


def _params(sem, vmem=VMEM_LIMIT):
    return pltpu.CompilerParams(dimension_semantics=sem, vmem_limit_bytes=vmem)


def _rows(ts, w, col=0):
    return pl.BlockSpec((ts, w), lambda i: (i, col))


def _fixed(shape):
    return pl.BlockSpec(shape, lambda *_: (0,) * len(shape))


def _fold8(v):
    ts, w = v.shape
    return v.reshape(ts // SUBLANE, SUBLANE, w).sum(axis=0)


def matmul(a, b, *, mode, m, n, k, name, out_dtypes, tm=1024, tn=1024, tk=2048, epilogue=None, extra=(), after=(),
           vmem=VMEM_LIMIT):
    tm, tn, tk = min(tm, m), min(tn, n), min(tk, k)
    nm, nn, nk = m // tm, n // tn, k // tk
    assert nm * tm == m and nn * tn == n and nk * tk == k, (name, m, n, k, tm, tn, tk)
    if mode == "tn":
        a_spec = pl.BlockSpec((tk, tm), lambda i, j, kk: (kk, i))
        dims = (((0,), (0,)), ((), ()))
    else:
        a_spec = pl.BlockSpec((tm, tk), lambda i, j, kk: (i, kk))
        dims = (((1,), (1 if mode == "nt" else 0,)), ((), ()))
    if mode == "nt":
        b_spec = pl.BlockSpec((tn, tk), lambda i, j, kk: (j, kk))
    else:
        b_spec = pl.BlockSpec((tk, tn), lambda i, j, kk: (kk, j))
    tile = pl.BlockSpec((tm, tn), lambda i, j, kk: (i, j))
    n_out, n_extra, n_after = len(out_dtypes), len(extra), len(after)
    if epilogue is None:
        epilogue = lambda acc: (acc,)

    def body(a_ref, b_ref, *rest):
        extra_refs = rest[:n_extra]
        outs = rest[n_extra + n_after:][:n_out]
        part = lax.dot_general(a_ref[...].astype(MXU_DTYPE), b_ref[...].astype(MXU_DTYPE), dims,
                               preferred_element_type=F32)

        def finish(acc):
            for o_ref, val in zip(outs, epilogue(acc, *[r[...] for r in extra_refs])):
                o_ref[...] = val.astype(o_ref.dtype)

        if nk == 1:
            finish(part)
        else:
            acc_ref = rest[-1]
            kk = pl.program_id(2)

            @pl.when(kk == 0)
            def _():
                acc_ref[...] = part

            @pl.when(kk > 0)
            def _():
                acc_ref[...] += part

            @pl.when(kk == nk - 1)
            def _():
                finish(acc_ref[...])

    res = pl.pallas_call(
        body, name=name, grid=(nm, nn, nk), in_specs=[a_spec, b_spec] + [tile] * n_extra + [ANY] * n_after,
        out_specs=[tile] * n_out, out_shape=[_sds((m, n), dt) for dt in out_dtypes],
        scratch_shapes=[pltpu.VMEM((tm, tn), F32)] if nk > 1 else [],
        compiler_params=_params(("parallel", "parallel", "arbitrary"), vmem),
    )(a, b, *extra, *after)
    return res[0] if n_out == 1 else res


def _rstd(x):
    return lax.rsqrt(jnp.mean(x * x, axis=-1, keepdims=True) + RMS_EPS)


def _rms_bwd(x, g, dy):
    rstd = _rstd(x)
    xh = x * rstd
    gy = dy * g
    dx = rstd * (gy - xh * jnp.mean(gy * xh, axis=-1, keepdims=True))
    return dx, dy * xh


def rms_fwd(x, g, *, name):
    s = x.shape[0]
    ts = min(s, 512)

    def body(x_ref, g_ref, h_ref):
        xv = x_ref[...]
        h_ref[...] = (xv * _rstd(xv) * g_ref[...]).astype(h_ref.dtype)

    return pl.pallas_call(
        body, name=name, grid=(s // ts,), in_specs=[_rows(ts, D_MODEL), _fixed((1, D_MODEL))],
        out_specs=_rows(ts, D_MODEL), out_shape=_sds((s, D_MODEL), MXU_DTYPE), compiler_params=_params(("parallel",)),
    )(x, g)


def rms_residual_fwd(x, y, g_post, g_next, *, name):
    s = x.shape[0]
    ts = min(s, 512)
    with_next = g_next is not None

    def body(x_ref, y_ref, gp_ref, *rest):
        yv = y_ref[...].astype(F32)
        xn = x_ref[...] + yv * _rstd(yv) * gp_ref[...]
        if with_next:
            gn_ref, xo_ref, h_ref = rest
            h_ref[...] = (xn * _rstd(xn) * gn_ref[...]).astype(h_ref.dtype)
        else:
            (xo_ref,) = rest
        xo_ref[...] = xn

    gspec = _fixed((1, D_MODEL))
    res = pl.pallas_call(
        body, name=name, grid=(s // ts,),
        in_specs=[_rows(ts, D_MODEL), _rows(ts, D_MODEL), gspec] + [gspec] * with_next,
        out_specs=[_rows(ts, D_MODEL)] * (1 + with_next),
        out_shape=[_sds((s, D_MODEL), F32)] + [_sds((s, D_MODEL), MXU_DTYPE)] * with_next,
        compiler_params=_params(("parallel",)),
    )(x, y, g_post, *([g_next] if with_next else []))
    return (res[0], res[1]) if with_next else (res[0], None)


def rms_bwd(x, g, dy, residual, *, out_dtype, name):
    s = x.shape[0]
    ts = min(s, 256)
    nt = s // ts
    with_res = residual is not None

    def body(x_ref, g_ref, dy_ref, *rest):
        dx_ref, dg_ref, acc_ref = rest[-3:]
        dx, dg_rows = _rms_bwd(x_ref[...].astype(F32), g_ref[...], dy_ref[...].astype(F32))
        if with_res:
            dx = dx + rest[0][...]
        dx_ref[...] = dx.astype(dx_ref.dtype)
        i = pl.program_id(0)

        @pl.when(i == 0)
        def _():
            acc_ref[...] = jnp.zeros_like(acc_ref)

        acc_ref[...] += _fold8(dg_rows)

        @pl.when(i == nt - 1)
        def _():
            dg_ref[...] = acc_ref[...].sum(axis=0, keepdims=True)

    return pl.pallas_call(
        body, name=name, grid=(nt,),
        in_specs=[_rows(ts, D_MODEL), _fixed((1, D_MODEL)), _rows(ts, D_MODEL)] + [_rows(ts, D_MODEL)] * with_res,
        out_specs=[_rows(ts, D_MODEL), _fixed((1, D_MODEL))],
        out_shape=[_sds((s, D_MODEL), out_dtype), _sds((1, D_MODEL), F32)],
        scratch_shapes=[pltpu.VMEM((SUBLANE, D_MODEL), F32)], compiler_params=_params(("arbitrary",)),
    )(x, g, dy, *([residual] if with_res else []))


def rms_bwd_pair(xa, ga, dya, residual, xb, gb, *, name):
    s = xa.shape[0]
    ts = min(s, 256)
    nt = s // ts

    def body(xa_ref, ga_ref, dya_ref, r_ref, xb_ref, gb_ref, d1_ref, d2_ref, dga_ref, dgb_ref, acc_a, acc_b):
        d1, rows_a = _rms_bwd(xa_ref[...].astype(F32), ga_ref[...], dya_ref[...].astype(F32))
        d1 = d1 + r_ref[...]
        d1_ref[...] = d1
        d2, rows_b = _rms_bwd(xb_ref[...].astype(F32), gb_ref[...], d1)
        d2_ref[...] = d2.astype(d2_ref.dtype)
        i = pl.program_id(0)

        @pl.when(i == 0)
        def _():
            acc_a[...] = jnp.zeros_like(acc_a)
            acc_b[...] = jnp.zeros_like(acc_b)

        acc_a[...] += _fold8(rows_a)
        acc_b[...] += _fold8(rows_b)

        @pl.when(i == nt - 1)
        def _():
            dga_ref[...] = acc_a[...].sum(axis=0, keepdims=True)
            dgb_ref[...] = acc_b[...].sum(axis=0, keepdims=True)

    row, gain = _rows(ts, D_MODEL), _fixed((1, D_MODEL))
    return pl.pallas_call(
        body, name=name, grid=(nt,), in_specs=[row, gain, row, row, row, gain], out_specs=[row, row, gain, gain],
        out_shape=[_sds((s, D_MODEL), F32), _sds((s, D_MODEL), MXU_DTYPE), _sds((1, D_MODEL), F32), _sds((1, D_MODEL), F32)],
        scratch_shapes=[pltpu.VMEM((SUBLANE, D_MODEL), F32)] * 2, compiler_params=_params(("arbitrary",)),
    )(xa, ga, dya, residual, xb, gb)


def loss_and_grad(y, target, *, name):
    s = y.shape[0]
    ts = min(s, 512)
    nt = s // ts

    def body(y_ref, t_ref, dy_ref, part_ref):
        e = y_ref[...] - t_ref[...]
        dy_ref[...] = e * (1.0 / D_MODEL)
        sq = _fold8(e * e)
        lanes = sq[:, 0:LANE]
        for j in range(1, D_MODEL // LANE):
            lanes = lanes + sq[:, j * LANE:(j + 1) * LANE]
        i = pl.program_id(0)

        @pl.when(i == 0)
        def _():
            part_ref[...] = jnp.zeros_like(part_ref)

        part_ref[...] += lanes * (0.5 / D_MODEL)

    return pl.pallas_call(
        body, name=name, grid=(nt,), in_specs=[_rows(ts, D_MODEL), _rows(ts, D_MODEL)],
        out_specs=[_rows(ts, D_MODEL), _fixed((SUBLANE, LANE))],
        out_shape=[_sds((s, D_MODEL), F32), _sds((SUBLANE, LANE), F32)], compiler_params=_params(("arbitrary",)),
    )(y, target)


CONV_TC = 2 * LANE


def _conv_taps(u, s):
    row = lax.broadcasted_iota(jnp.int32, u.shape, 0)
    prev = jnp.where(row == 0, 0.0, pltpu.roll(u, 1, 0))
    nxt = jnp.where(row == s - 1, 0.0, pltpu.roll(u, s - 1, 0))
    return prev, nxt


def _zcol(s, off):
    return pl.BlockSpec((s, CONV_TC), lambda j: (0, off // CONV_TC + j))


def conv_fwd(z, w3, *, name):
    s = z.shape[0]

    def body(cb_ref, cc_ref, ci_ref, w_ref, t_ref):
        u = cc_ref[...].astype(F32) * ci_ref[...].astype(F32)
        prev, nxt = _conv_taps(u, s)
        w = w_ref[...]
        conv = w[0:1] * prev + w[1:2] * u + w[2:3] * nxt
        t_ref[...] = (cb_ref[...].astype(F32) * conv).astype(t_ref.dtype)

    return pl.pallas_call(
        body, name=name, grid=(CONV_WIDTH // CONV_TC,),
        in_specs=[_zcol(s, OFF_CB), _zcol(s, OFF_CC), _zcol(s, OFF_CI), pl.BlockSpec((3, CONV_TC), lambda j: (0, j))],
        out_specs=pl.BlockSpec((s, CONV_TC), lambda j: (0, j)), out_shape=_sds((s, CONV_WIDTH), MXU_DTYPE),
        compiler_params=_params(("parallel",)),
    )(z, z, z, w3)


def conv_bwd(dt, z, w3, *, name, after=()):
    s = z.shape[0]

    def body(dt_ref, cb_ref, cc_ref, ci_ref, w_ref, *rest):
        dcb_ref, dcc_ref, dci_ref, dw_ref = rest[-4:]
        cc, ci = cc_ref[...].astype(F32), ci_ref[...].astype(F32)
        u = cc * ci
        prev, nxt = _conv_taps(u, s)
        w = w_ref[...]
        dtv = dt_ref[...].astype(F32)
        dcb_ref[...] = (dtv * (w[0:1] * prev + w[1:2] * u + w[2:3] * nxt)).astype(dcb_ref.dtype)
        dconv = dtv * cb_ref[...].astype(F32)
        dprev, dnxt = _conv_taps(dconv, s)
        du = w[0:1] * dnxt + w[1:2] * dconv + w[2:3] * dprev
        dcc_ref[...] = (du * ci).astype(dcc_ref.dtype)
        dci_ref[...] = (du * cc).astype(dci_ref.dtype)
        dw_ref[0:1, :] = jnp.sum(dconv * prev, axis=0, keepdims=True)
        dw_ref[1:2, :] = jnp.sum(dconv * u, axis=0, keepdims=True)
        dw_ref[2:3, :] = jnp.sum(dconv * nxt, axis=0, keepdims=True)

    col = pl.BlockSpec((s, CONV_TC), lambda j: (0, j))
    wspec = pl.BlockSpec((3, CONV_TC), lambda j: (0, j))
    return pl.pallas_call(
        body, name=name, grid=(CONV_WIDTH // CONV_TC,),
        in_specs=[col, _zcol(s, OFF_CB), _zcol(s, OFF_CC), _zcol(s, OFF_CI), wspec] + [ANY] * len(after),
        out_specs=[col, col, col, wspec],
        out_shape=[_sds((s, CONV_WIDTH), MXU_DTYPE)] * 3 + [_sds((3, CONV_WIDTH), F32)],
        compiler_params=_params(("parallel",)),
    )(dt, z, z, z, w3, *after)


def rope_tables(s):
    n_freq = HEAD_DIM // 4
    t = jnp.arange(s, dtype=jnp.int32)
    inv_freq = ROPE_THETA ** (-jnp.arange(0, HEAD_DIM // 2, 2, dtype=F32) / (HEAD_DIM // 2))
    ang_r = (t // GRID_W).astype(F32)[:, None] * inv_freq
    ang_c = (t % GRID_W).astype(F32)[:, None] * inv_freq
    cos_t = jnp.concatenate([jnp.cos(ang_r)] * 2 + [jnp.cos(ang_c)] * 2, axis=1)
    sin_t = jnp.concatenate([-jnp.sin(ang_r), jnp.sin(ang_r), -jnp.sin(ang_c), jnp.sin(ang_c)], axis=1)
    assert cos_t.shape == (s, 4 * n_freq)
    return cos_t, sin_t


def _swap_halves(v):
    lane = lax.broadcasted_iota(jnp.int32, v.shape, 1)
    return jnp.where(lane % 64 < 32, pltpu.roll(v, HEAD_DIM - 32, 1), pltpu.roll(v, 32, 1))


def qk_prep_fwd(z, qn, kn, cos_t, sin_t, *, name):
    s = z.shape[0]
    ts = min(s, 512)

    def body(q_ref, k_ref, v_ref, qn_ref, kn_ref, c_ref, s_ref, qo_ref, ko_ref, vo_ref):
        cs, sn = c_ref[...], s_ref[...]

        def head(x, g, scale):
            n = x * _rstd(x) * g
            return (n * cs + _swap_halves(n) * sn) * scale

        for h in range(N_Q_HEADS):
            sl = slice(h * HEAD_DIM, (h + 1) * HEAD_DIM)
            qo_ref[:, sl] = head(q_ref[:, sl].astype(F32), qn_ref[...], ATTN_SCALE).astype(qo_ref.dtype)
        for h in range(N_KV_HEADS):
            sl = slice(h * HEAD_DIM, (h + 1) * HEAD_DIM)
            ko_ref[:, sl] = head(k_ref[:, sl].astype(F32), kn_ref[...], 1.0).astype(ko_ref.dtype)
        vo_ref[...] = v_ref[...].astype(vo_ref.dtype)

    tab = _rows(ts, HEAD_DIM)
    gsp = _fixed((1, HEAD_DIM))
    return pl.pallas_call(
        body, name=name, grid=(s // ts,),
        in_specs=[_rows(ts, ATTN_WIDTH, OFF_Q // ATTN_WIDTH), _rows(ts, KV_WIDTH, OFF_K // KV_WIDTH),
                  _rows(ts, KV_WIDTH, OFF_V // KV_WIDTH), gsp, gsp, tab, tab],
        out_specs=[_rows(ts, ATTN_WIDTH), _rows(ts, KV_WIDTH), _rows(ts, KV_WIDTH)],
        out_shape=[_sds((s, ATTN_WIDTH), MXU_DTYPE), _sds((s, KV_WIDTH), MXU_DTYPE), _sds((s, KV_WIDTH), MXU_DTYPE)],
        compiler_params=_params(("parallel",)),
    )(z, z, z, qn, kn, cos_t, sin_t)


def qk_prep_bwd(dqp, dkp, z, qn, kn, cos_t, sin_t, *, name):
    s = z.shape[0]
    ts = min(s, 512)
    nt = s // ts

    def body(dq_ref, dk_ref, q_ref, k_ref, qn_ref, kn_ref, c_ref, s_ref, dqo_ref, dko_ref, dqn_ref, dkn_ref,
             qacc_ref, kacc_ref):
        cs, sn = c_ref[...], s_ref[...]
        i = pl.program_id(0)

        @pl.when(i == 0)
        def _():
            qacc_ref[...] = jnp.zeros_like(qacc_ref)
            kacc_ref[...] = jnp.zeros_like(kacc_ref)

        def head(x, g, dout, scale):
            d = dout.astype(F32) * scale
            dn = d * cs + _swap_halves(d * sn)
            return _rms_bwd(x, g, dn)

        qacc = jnp.zeros((SUBLANE, HEAD_DIM), F32)
        for h in range(N_Q_HEADS):
            sl = slice(h * HEAD_DIM, (h + 1) * HEAD_DIM)
            dx, dg_rows = head(q_ref[:, sl].astype(F32), qn_ref[...], dq_ref[:, sl], ATTN_SCALE)
            dqo_ref[:, sl] = dx.astype(dqo_ref.dtype)
            qacc = qacc + _fold8(dg_rows)
        kacc = jnp.zeros((SUBLANE, HEAD_DIM), F32)
        for h in range(N_KV_HEADS):
            sl = slice(h * HEAD_DIM, (h + 1) * HEAD_DIM)
            dx, dg_rows = head(k_ref[:, sl].astype(F32), kn_ref[...], dk_ref[:, sl], 1.0)
            dko_ref[:, sl] = dx.astype(dko_ref.dtype)
            kacc = kacc + _fold8(dg_rows)
        qacc_ref[...] += qacc
        kacc_ref[...] += kacc

        @pl.when(i == nt - 1)
        def _():
            dqn_ref[...] = qacc_ref[...].sum(axis=0, keepdims=True)
            dkn_ref[...] = kacc_ref[...].sum(axis=0, keepdims=True)

    tab = _rows(ts, HEAD_DIM)
    gsp = _fixed((1, HEAD_DIM))
    return pl.pallas_call(
        body, name=name, grid=(nt,),
        in_specs=[_rows(ts, ATTN_WIDTH), _rows(ts, KV_WIDTH), _rows(ts, ATTN_WIDTH, OFF_Q // ATTN_WIDTH),
                  _rows(ts, KV_WIDTH, OFF_K // KV_WIDTH), gsp, gsp, tab, tab],
        out_specs=[_rows(ts, ATTN_WIDTH), _rows(ts, KV_WIDTH), gsp, gsp],
        out_shape=[_sds((s, ATTN_WIDTH), MXU_DTYPE), _sds((s, KV_WIDTH), MXU_DTYPE), _sds((1, HEAD_DIM), F32),
                   _sds((1, HEAD_DIM), F32)],
        scratch_shapes=[pltpu.VMEM((SUBLANE, HEAD_DIM), F32)] * 2, compiler_params=_params(("arbitrary",)),
    )(dqp, dkp, z, z, qn, kn, cos_t, sin_t)


_NT = (((1,), (1,)), ((), ()))
_GW = GROUP * HEAD_DIM


def _dot(a, b, dims=(((1,), (0,)), ((), ()))):
    return lax.dot_general(a, b, dims, preferred_element_type=F32)


def attn_fwd(qp, kp, vb, *, name, after=()):
    s = qp.shape[0]
    tq = min(s, 2048)
    rows = min(tq, 128)

    def body(q_ref, k_ref, v_ref, *rest):
        o_ref, lse_ref, v_ones = rest[-3:]

        @pl.when(pl.program_id(1) == 0)
        def _():
            v_ones[:, :HEAD_DIM] = v_ref[...]
            v_ones[:, HEAD_DIM:] = jnp.ones((s, HEAD_DIM), v_ones.dtype)

        k = k_ref[...]
        for g in range(GROUP):
            sl = slice(g * HEAD_DIM, (g + 1) * HEAD_DIM)
            for r0 in range(0, tq, rows):
                rs = slice(r0, r0 + rows)
                sc = _dot(q_ref[rs, sl], k, _NT)
                mx = jnp.max(sc, axis=-1, keepdims=True)
                p = jnp.exp(sc - mx).astype(v_ones.dtype)
                o_den = _dot(p, v_ones[...])
                den = o_den[:, HEAD_DIM:HEAD_DIM + 1]
                o_ref[rs, sl] = (o_den[:, :HEAD_DIM] / den).astype(o_ref.dtype)
                lse_ref[rs, g:g + 1] = mx + jnp.log(den)

    return pl.pallas_call(
        body, name=name, grid=(N_KV_HEADS, s // tq),
        in_specs=[pl.BlockSpec((tq, _GW), lambda j, i: (i, j)), pl.BlockSpec((s, HEAD_DIM), lambda j, i: (0, j)),
                  pl.BlockSpec((s, HEAD_DIM), lambda j, i: (0, j))] + [ANY] * len(after),
        out_specs=[pl.BlockSpec((tq, _GW), lambda j, i: (i, j)), pl.BlockSpec((None, tq, GROUP), lambda j, i: (j, i, 0))],
        out_shape=[_sds((s, ATTN_WIDTH), MXU_DTYPE), _sds((N_KV_HEADS, s, GROUP), F32)],
        scratch_shapes=[pltpu.VMEM((s, 2 * HEAD_DIM), MXU_DTYPE)], compiler_params=_params(("parallel", "arbitrary")),
    )(qp, kp, vb, *after)


ATTN_BWD_TQ = 256


def attn_bwd(qp, kp, vb, do, o, lse, *, name):
    s = qp.shape[0]
    tq = min(s, ATTN_BWD_TQ)
    nq = s // tq
    over_rows = (((0,), (0,)), ((), ()))

    def body(q_ref, k_ref, v_ref, do_ref, o_ref, lse_ref, dq_ref, dk_ref, dv_ref, p_all, ds_all, q_all, do_all, dk_acc,
             dv_acc):
        i = pl.program_id(1)

        @pl.when(i == 0)
        def _():
            dk_acc[...] = jnp.zeros_like(dk_acc)
            dv_acc[...] = jnp.zeros_like(dv_acc)

        k, v = k_ref[...], v_ref[...]
        for g in range(GROUP):
            sl = slice(g * HEAD_DIM, (g + 1) * HEAD_DIM)
            rows = slice(g * tq, (g + 1) * tq)
            qg, dog = q_ref[:, sl], do_ref[:, sl]
            dd = jnp.sum(dog.astype(F32) * o_ref[:, sl].astype(F32), axis=-1, keepdims=True)
            p = jnp.exp(_dot(qg, k, _NT) - lse_ref[:, g:g + 1])
            ds = (p * (_dot(dog, v, _NT) - dd)).astype(k.dtype)
            dq_ref[:, sl] = _dot(ds, k).astype(dq_ref.dtype)
            p_all[rows, :] = p.astype(p_all.dtype)
            ds_all[rows, :] = ds
            q_all[rows, :] = qg
            do_all[rows, :] = dog
        dv_acc[...] += _dot(p_all[...], do_all[...], over_rows)
        dk_acc[...] += _dot(ds_all[...], q_all[...], over_rows)

        @pl.when(i == nq - 1)
        def _():
            dk_ref[...] = dk_acc[...].astype(dk_ref.dtype)
            dv_ref[...] = dv_acc[...].astype(dv_ref.dtype)

    qspec = pl.BlockSpec((tq, _GW), lambda j, i: (i, j))
    kspec = pl.BlockSpec((s, HEAD_DIM), lambda j, i: (0, j))
    lspec = pl.BlockSpec((None, tq, GROUP), lambda j, i: (j, i, 0))
    return pl.pallas_call(
        body, name=name, grid=(N_KV_HEADS, nq), in_specs=[qspec, kspec, kspec, qspec, qspec, lspec],
        out_specs=[qspec, kspec, kspec],
        out_shape=[_sds((s, ATTN_WIDTH), ACT_DTYPE), _sds((s, KV_WIDTH), ACT_DTYPE), _sds((s, KV_WIDTH), MXU_DTYPE)],
        scratch_shapes=[pltpu.VMEM((GROUP * tq, s), MXU_DTYPE), pltpu.VMEM((GROUP * tq, s), MXU_DTYPE),
                        pltpu.VMEM((GROUP * tq, HEAD_DIM), MXU_DTYPE), pltpu.VMEM((GROUP * tq, HEAD_DIM), MXU_DTYPE),
                        pltpu.VMEM((s, HEAD_DIM), F32), pltpu.VMEM((s, HEAD_DIM), F32)],
        compiler_params=_params(("parallel", "arbitrary")),
    )(qp, kp, vb, do, o, lse)


GATE_TW = 1024


def gate_fwd(ya, yb, z, bias, *, name):
    s = z.shape[0]
    ts = min(s, 512)

    def body(ya_ref, yb_ref, ga_ref, gb_ref, ba_ref, bb_ref, o_ref):
        sa = jax.nn.sigmoid(ga_ref[...].astype(F32) + ba_ref[...])
        sb = jax.nn.sigmoid(gb_ref[...].astype(F32) + bb_ref[...])
        o_ref[...] = (sa * ya_ref[...].astype(F32) + sb * yb_ref[...].astype(F32)).astype(o_ref.dtype)

    tile = pl.BlockSpec((ts, GATE_TW), lambda i, j: (i, j))
    ga = pl.BlockSpec((ts, GATE_TW), lambda i, j: (i, OFF_GA // GATE_TW + j))
    gb = pl.BlockSpec((ts, GATE_TW), lambda i, j: (i, OFF_GB // GATE_TW + j))
    ba = pl.BlockSpec((1, GATE_TW), lambda i, j: (0, j))
    bb = pl.BlockSpec((1, GATE_TW), lambda i, j: (0, D_MODEL // GATE_TW + j))
    return pl.pallas_call(
        body, name=name, grid=(s // ts, D_MODEL // GATE_TW), in_specs=[tile, tile, ga, gb, ba, bb], out_specs=tile,
        out_shape=_sds((s, D_MODEL), MXU_DTYPE), compiler_params=_params(("parallel", "parallel")),
    )(ya, yb, z, z, bias, bias)


def gate_bwd(dmi, ya, yb, z, bias, *, name):
    s = z.shape[0]
    ts = min(s, 512)
    nt = s // ts

    def body(d_ref, ya_ref, yb_ref, ga_ref, gb_ref, ba_ref, bb_ref, dya_ref, dyb_ref, dga_ref, dgb_ref, dba_ref,
             dbb_ref, acc_a, acc_b):
        d = d_ref[...].astype(F32)
        sa = jax.nn.sigmoid(ga_ref[...].astype(F32) + ba_ref[...])
        sb = jax.nn.sigmoid(gb_ref[...].astype(F32) + bb_ref[...])
        dya_ref[...] = (d * sa).astype(dya_ref.dtype)
        dyb_ref[...] = (d * sb).astype(dyb_ref.dtype)
        dga = d * ya_ref[...].astype(F32) * sa * (1.0 - sa)
        dgb = d * yb_ref[...].astype(F32) * sb * (1.0 - sb)
        dga_ref[...] = dga.astype(dga_ref.dtype)
        dgb_ref[...] = dgb.astype(dgb_ref.dtype)
        i = pl.program_id(1)

        @pl.when(i == 0)
        def _():
            acc_a[...] = jnp.zeros_like(acc_a)
            acc_b[...] = jnp.zeros_like(acc_b)

        acc_a[...] += _fold8(dga)
        acc_b[...] += _fold8(dgb)

        @pl.when(i == nt - 1)
        def _():
            dba_ref[...] = acc_a[...].sum(axis=0, keepdims=True)
            dbb_ref[...] = acc_b[...].sum(axis=0, keepdims=True)

    tile = pl.BlockSpec((ts, GATE_TW), lambda j, i: (i, j))
    ga = pl.BlockSpec((ts, GATE_TW), lambda j, i: (i, OFF_GA // GATE_TW + j))
    gb = pl.BlockSpec((ts, GATE_TW), lambda j, i: (i, OFF_GB // GATE_TW + j))
    ba = pl.BlockSpec((1, GATE_TW), lambda j, i: (0, j))
    bb = pl.BlockSpec((1, GATE_TW), lambda j, i: (0, D_MODEL // GATE_TW + j))
    acc = pltpu.VMEM((SUBLANE, GATE_TW), F32)
    return pl.pallas_call(
        body, name=name, grid=(D_MODEL // GATE_TW, nt), in_specs=[tile, tile, tile, ga, gb, ba, bb],
        out_specs=[tile, tile, tile, tile, ba, ba],
        out_shape=[_sds((s, D_MODEL), MXU_DTYPE)] * 4 + [_sds((1, D_MODEL), F32)] * 2,
        scratch_shapes=[acc, acc], compiler_params=_params(("parallel", "arbitrary")),
    )(dmi, ya, yb, z, z, bias, bias)


HBM = pl.BlockSpec(memory_space=pltpu.HBM)
SEM = pl.BlockSpec(memory_space=pltpu.SEMAPHORE)
EFFECT = pltpu.SideEffectType.DATAFLOW_SIDE_EFFECTING
TOKEN = (SUBLANE, LANE)


def _place():
    return lax.axis_index("x"), lax.axis_index("y"), lax.axis_index("c")


def _window(ref, shard_shape, axis, d):
    r, c = shard_shape
    lead = (slice(None),) * (len(ref.shape) - 2)
    if axis == 0:
        return ref.at[lead + (pl.ds(pl.multiple_of(d * r, SUBLANE), r), slice(None))]
    return ref.at[lead + (slice(None), pl.ds(pl.multiple_of(d * c, LANE), c))]


def _hbm(a):
    return pltpu.with_memory_space_constraint(a, pltpu.HBM)


def _remote(src, dst, send_sems, recv_sems, i, to):
    return pltpu.make_async_remote_copy(src_ref=src, dst_ref=dst, send_sem=send_sems.at[i], recv_sem=recv_sems.at[i],
                                        device_id=to, device_id_type=MESH)


def cast_into_full(w, layer, axis, me, dtype, *, name, after=()):
    _, r, c = w.shape
    tr = min(r, 256)
    nr = r // tr
    in_spec = pl.BlockSpec((None, tr, c), lambda i, me_ref: (layer, i, 0))
    if axis == 0:
        out_spec = pl.BlockSpec((tr, c), lambda i, me_ref: (me_ref[0] * nr + i, 0))
        shape = (N_DEV * r, c)
    else:
        out_spec = pl.BlockSpec((tr, c), lambda i, me_ref: (i, me_ref[0]))
        shape = (r, N_DEV * c)

    def body(me_ref, w_ref, *rest):
        o_ref = rest[-1]
        o_ref[...] = w_ref[...].astype(o_ref.dtype)

    return pl.pallas_call(
        body, name=name,
        grid_spec=pltpu.PrefetchScalarGridSpec(num_scalar_prefetch=1, grid=(nr,), in_specs=[in_spec] + [ANY] * len(after),
                                               out_specs=out_spec),
        out_shape=_sds(shape, dtype), compiler_params=_params(("parallel",)),
    )(me, w, *after)


class _GatherPlan:
    def __init__(self, fulls, shard_shapes, axes):
        x, y, c = _place()
        self.n = len(fulls)
        self.me, self.sibling = (x, y, c), (x, y, 1 - c)
        self.chips = [(1 - x, y), (x, 1 - y), (1 - x, 1 - y)]
        self.win = lambda a, p: _window(fulls[a], shard_shapes[a], axes[a], 4 * p[0] + 2 * p[1] + p[2])

    def first(self, send_sems, recv_sems):
        out = []
        for a in range(self.n):
            mine = self.win(a, self.me)
            out.append(_remote(mine, mine, send_sems, recv_sems, 4 * a, self.sibling))
            out += [_remote(mine, mine, send_sems, recv_sems, 4 * a + 1 + j, (*chip, self.me[2]))
                    for j, chip in enumerate(self.chips)]
        return out

    def first_arrivals(self, send_sems, recv_sems):
        c = self.me[2]
        out = []
        for a in range(self.n):
            blocks = [self.sibling] + [(*chip, c) for chip in self.chips]
            out += [_remote(self.win(a, b), self.win(a, b), send_sems, recv_sems, 4 * a + k, self.me)
                    for k, b in enumerate(blocks)]
        return out

    def passed(self, send_sems, recv_sems):
        c = self.me[2]
        return [_remote(self.win(a, (*chip, c)), self.win(a, (*chip, c)), send_sems, recv_sems, 3 * a + j, self.sibling)
                for a in range(self.n) for j, chip in enumerate(self.chips)]

    def passed_arrivals(self, send_sems, recv_sems):
        c = self.me[2]
        return [_remote(self.win(a, (*chip, 1 - c)), self.win(a, (*chip, 1 - c)), send_sems, recv_sems, 3 * a + j, self.me)
                for a in range(self.n) for j, chip in enumerate(self.chips)]


def gather_start(fulls, after, shard_shapes, axes, *, name):
    n = len(fulls)

    def body(*refs):
        ins, send_sems, recv_sems, token = refs[:n], refs[n + 1], refs[n + 2], refs[-1]
        for cp in _GatherPlan(ins, shard_shapes, axes).first(send_sems, recv_sems):
            cp.start()
        token[...] = jnp.zeros_like(token)

    res = pl.pallas_call(
        body, name=name, in_specs=[HBM] * n + [ANY],
        out_shape=(pltpu.SemaphoreType.DMA((4 * n,)), pltpu.SemaphoreType.DMA((4 * n,)),
                   *[pltpu.HBM(f.shape, f.dtype) for f in fulls], _sds(TOKEN, F32)),
        out_specs=(SEM, SEM, *[HBM] * n, pl.BlockSpec(memory_space=pltpu.VMEM)),
        input_output_aliases={a: 2 + a for a in range(n)}, compiler_params=pltpu.CompilerParams(has_side_effects=EFFECT),
    )(*[_hbm(f) for f in fulls], after)
    return res[0], res[1], list(res[2:2 + n]), res[-1]


def gather_pass(send_sems, recv_sems, fulls, after, shard_shapes, axes, *, name):
    n = len(fulls)

    def body(*refs):
        ins, s1, r1 = refs[:n], refs[n], refs[n + 1]
        s2, r2, token = refs[n + 3], refs[n + 4], refs[-1]
        plan = _GatherPlan(ins, shard_shapes, axes)
        for cp in plan.first_arrivals(s1, r1):
            cp.wait_recv()
        for cp in plan.first(s1, r1):
            cp.wait_send()
        for cp in plan.passed(s2, r2):
            cp.start()
        token[...] = jnp.zeros_like(token)

    res = pl.pallas_call(
        body, name=name, in_specs=[HBM] * n + [SEM, SEM, ANY],
        out_shape=(pltpu.SemaphoreType.DMA((3 * n,)), pltpu.SemaphoreType.DMA((3 * n,)),
                   *[pltpu.HBM(f.shape, f.dtype) for f in fulls], _sds(TOKEN, F32)),
        out_specs=(SEM, SEM, *[HBM] * n, pl.BlockSpec(memory_space=pltpu.VMEM)),
        input_output_aliases={a: 2 + a for a in range(n)}, compiler_params=pltpu.CompilerParams(has_side_effects=EFFECT),
    )(*fulls, send_sems, recv_sems, after)
    return res[0], res[1], list(res[2:2 + n]), res[-1]


def gather_wait(send_sems, recv_sems, fulls, after, shard_shapes, axes, *, name):
    n = len(fulls)

    def body(*refs):
        ins, s2, r2 = refs[:n], refs[n], refs[n + 1]
        plan = _GatherPlan(ins, shard_shapes, axes)
        for cp in plan.passed_arrivals(s2, r2):
            cp.wait_recv()
        for cp in plan.passed(s2, r2):
            cp.wait_send()

    return list(pl.pallas_call(
        body, name=name, in_specs=[HBM] * n + [SEM, SEM, ANY], out_shape=tuple(pltpu.HBM(f.shape, f.dtype) for f in fulls),
        out_specs=tuple([HBM] * n), input_output_aliases={a: a for a in range(n)},
        compiler_params=pltpu.CompilerParams(has_side_effects=EFFECT),
    )(*fulls, send_sems, recv_sems, after))


def _pair_copies(grads, lands, shard_shapes, axes, send_sems, recv_sems):
    x, y, c = _place()
    return [_remote(_window(grads[a], shard_shapes[a], axes[a], 2 * q + (1 - c)), lands[a].at[q], send_sems, recv_sems,
                    N_CHIP * a + q, (x, y, 1 - c))
            for a in range(len(grads)) for q in range(N_CHIP)]


def _chip_sends(sums, lands, send_sems, recv_sems):
    x, y, c = _place()
    return [_remote(sums[a].at[2 * px + py], lands[a].at[2 * x + y], send_sems, recv_sems, 3 * a + j, (px, py, c))
            for a in range(len(sums)) for j, (px, py) in enumerate([(1 - x, y), (x, 1 - y), (1 - x, 1 - y)])]


def _chip_arrivals(sums, lands, send_sems, recv_sems):
    x, y, c = _place()
    return [_remote(sums[a].at[2 * x + y], lands[a].at[2 * px + py], send_sems, recv_sems, 3 * a + j, (x, y, c))
            for a in range(len(sums)) for j, (px, py) in enumerate([(1 - x, y), (x, 1 - y), (1 - x, 1 - y)])]


def exchange_start(srcs, after, land_shapes, make_sends, per_array, *, name):
    n = len(srcs)
    lands = [lax.empty(s, a.dtype) for s, a in zip(land_shapes, srcs)]

    def body(*refs):
        ins, zones = refs[:n], refs[n:2 * n]
        send_sems, recv_sems, token = refs[2 * n + 1], refs[2 * n + 2], refs[-1]
        for cp in make_sends(ins, zones, send_sems, recv_sems):
            cp.start()
        token[...] = jnp.zeros_like(token)

    res = pl.pallas_call(
        body, name=name, in_specs=[HBM] * (2 * n) + [ANY],
        out_shape=(pltpu.SemaphoreType.DMA((per_array * n,)), pltpu.SemaphoreType.DMA((per_array * n,)),
                   *[pltpu.HBM(a.shape, a.dtype) for a in srcs], *[pltpu.HBM(a.shape, a.dtype) for a in lands],
                   _sds(TOKEN, F32)),
        out_specs=(SEM, SEM, *[HBM] * (2 * n), pl.BlockSpec(memory_space=pltpu.VMEM)),
        input_output_aliases={a: 2 + a for a in range(2 * n)}, compiler_params=pltpu.CompilerParams(has_side_effects=EFFECT),
    )(*[_hbm(a) for a in srcs], *[_hbm(a) for a in lands], after)
    return res[0], res[1], list(res[2:2 + n]), list(res[2 + n:2 + 2 * n]), res[-1]


def exchange_wait(send_sems, recv_sems, srcs, lands, after, make_waits, *, name):
    n = len(srcs)

    def body(*refs):
        ins, zones, s, r = refs[:n], refs[n:2 * n], refs[2 * n], refs[2 * n + 1]
        sends, arrivals = make_waits(ins, zones, s, r)
        for cp in arrivals:
            cp.wait_recv()
        for cp in sends:
            cp.wait_send()

    res = pl.pallas_call(
        body, name=name, in_specs=[HBM] * (2 * n) + [SEM, SEM, ANY],
        out_shape=tuple(pltpu.HBM(a.shape, a.dtype) for a in (*srcs, *lands)), out_specs=tuple([HBM] * (2 * n)),
        input_output_aliases={a: a for a in range(2 * n)}, compiler_params=pltpu.CompilerParams(has_side_effects=EFFECT),
    )(*srcs, *lands, send_sems, recv_sems, after)
    return list(res[:n]), list(res[n:])


def pair_sum(grad, recv, shard_shape, axis, core, *, name):
    r, c = shard_shape
    tr = min(r, 1024)
    nr = r // tr
    if axis == 0:
        gspec = pl.BlockSpec((tr, c), lambda q, i, cref: ((2 * q + cref[0]) * nr + i, 0))
    else:
        gspec = pl.BlockSpec((tr, c), lambda q, i, cref: (i, 2 * q + cref[0]))
    rspec = pl.BlockSpec((None, tr, c), lambda q, i, cref: (q, i, 0))

    def body(c_ref, g_ref, r_ref, o_ref):
        o_ref[...] = (g_ref[...].astype(F32) + r_ref[...].astype(F32)).astype(o_ref.dtype)

    return pl.pallas_call(
        body, name=name,
        grid_spec=pltpu.PrefetchScalarGridSpec(num_scalar_prefetch=1, grid=(N_CHIP, nr), in_specs=[gspec, rspec],
                                               out_specs=rspec),
        out_shape=_sds((N_CHIP, r, c), recv.dtype), compiler_params=_params(("parallel", "parallel")),
    )(core, grad, recv)


def all_reduce_small(part, *, name):
    r, c = part.shape

    def body(p_ref, o_ref, gath_ref, send_sems, recv_sems):
        x, y, cc = _place()
        me = 4 * x + 2 * y + cc
        gath_ref[me] = p_ref[...]
        peers = [(1 - x if k & 4 else x, 1 - y if k & 2 else y, 1 - cc if k & 1 else cc) for k in range(1, N_DEV)]
        copies = [_remote(p_ref, gath_ref.at[me], send_sems, recv_sems, i, peer) for i, peer in enumerate(peers)]
        for cp in copies:
            cp.start()
        for i, (px, py, pc) in enumerate(peers):
            _remote(p_ref, gath_ref.at[4 * px + 2 * py + pc], send_sems, recv_sems, i, (x, y, cc)).wait_recv()
        for cp in copies:
            cp.wait_send()
        acc = gath_ref[0]
        for d in range(1, N_DEV):
            acc = acc + gath_ref[d]
        o_ref[...] = acc

    vm = pl.BlockSpec(memory_space=pltpu.VMEM)
    return pl.pallas_call(
        body, name=name, in_specs=[vm], out_specs=vm, out_shape=_sds((r, c), F32),
        scratch_shapes=[pltpu.VMEM((N_DEV, r, c), F32), pltpu.SemaphoreType.DMA((N_DEV - 1,)),
                        pltpu.SemaphoreType.DMA((N_DEV - 1,))],
    )(part)


def _adamw(w, g, m, v):
    m = ADAM_B1 * m + (1.0 - ADAM_B1) * g
    v = ADAM_B2 * v + (1.0 - ADAM_B2) * (g * g)
    m_hat = m / (1.0 - ADAM_B1 ** ADAM_STEP)
    v_hat = v / (1.0 - ADAM_B2 ** ADAM_STEP)
    delta = -ADAM_LR * (m_hat / (jnp.sqrt(v_hat) + ADAM_EPS) + ADAM_WD * w)
    return delta, m, v


def reduce_adam(own, landed, chip, w, m, v, layer, outs, *, name):
    _, r, c = w.shape
    tr = min(r, 256)
    first = outs is None

    def body(chip_ref, own_ref, l1_ref, l2_ref, l3_ref, w_ref, m_ref, v_ref, *rest):
        g_out, d_out, m_out, v_out = rest[-4:]
        g = own_ref[...].astype(F32) + l1_ref[...].astype(F32) + l2_ref[...].astype(F32) + l3_ref[...].astype(F32)
        d, mn, vn = _adamw(w_ref[...], g, m_ref[...], v_ref[...])
        g_out[...] = g
        d_out[...] = d
        m_out[...] = mn
        v_out[...] = vn

    spec = pl.BlockSpec((None, tr, c), lambda i, chip_ref: (layer, i, 0))

    def slot(step):
        return pl.BlockSpec((None, tr, c), lambda i, chip_ref: ((chip_ref[0] + step) % N_CHIP, i, 0))

    n_in = 8
    return pl.pallas_call(
        body, name=name,
        grid_spec=pltpu.PrefetchScalarGridSpec(
            num_scalar_prefetch=1, grid=(r // tr,),
            in_specs=[slot(0), slot(1), slot(2), slot(3), spec, spec, spec] + ([] if first else [ANY] * 4),
            out_specs=[spec] * 4),
        out_shape=[_sds(w.shape, F32)] * 4, input_output_aliases={} if first else {n_in + i: i for i in range(4)},
        compiler_params=_params(("parallel",)),
    )(chip, own, landed, landed, landed, w, m, v, *([] if first else outs))


def adam_small(g, w, m, v, *, name):
    def body(g_ref, w_ref, m_ref, v_ref, d_out, m_out, v_out):
        d, mn, vn = _adamw(w_ref[...], g_ref[...], m_ref[...], v_ref[...])
        d_out[...] = d
        m_out[...] = mn
        v_out[...] = vn

    return pl.pallas_call(body, name=name, out_shape=[_sds(w.shape, F32)] * 3)(g, w, m, v)


BIG = ("w_in", "w_out_conv", "w_out_attn", "w_merge", "w_up", "w_down")
GATHERED = BIG + ("conv_w",)
BIG_AXIS = {"w_in": 1, "w_out_conv": 0, "w_out_attn": 0, "w_merge": 0, "w_up": 1, "w_down": 0, "conv_w": 1}
FIRST_GATHER_GROUPS = (("w_in", "conv_w"), ("w_out_conv", "w_out_attn", "w_merge"), ("w_up", "w_down"))
LAST_REDUCE_GROUPS = (("w_down", "w_up"), ("w_merge", "w_out_conv", "w_out_attn"), ("w_in",))


class _Gather:
    def __init__(self, weights, keys, own, after, tag):
        self.keys, self.tag = keys, tag
        self.shapes = [weights[k].shape[1:] for k in keys]
        self.axes = [BIG_AXIS[k] for k in keys]
        self.send, self.recv, self.fulls, self.token = gather_start([own[k] for k in keys], after, self.shapes, self.axes,
                                                                    name="gather_start_" + tag)

    def pass_on(self, after):
        self.send, self.recv, self.fulls, self.token = gather_pass(self.send, self.recv, self.fulls, after, self.shapes,
                                                                   self.axes, name="gather_pass_" + self.tag)
        return self.token

    def wait(self, after):
        fulls = gather_wait(self.send, self.recv, self.fulls, after, self.shapes, self.axes, name="gather_wait_" + self.tag)
        return dict(zip(self.keys, fulls))


class _Reduce:
    def __init__(self, weights, keys, tag):
        self.keys, self.tag = keys, tag
        self.shapes = [weights[k].shape[1:] for k in keys]
        self.axes = [BIG_AXIS[k] for k in keys]
        self.pair_shapes = [(N_CHIP, *shp) for shp in self.shapes]

    def _pair(self, i, z, ss, rs):
        return _pair_copies(i, z, self.shapes, self.axes, ss, rs)

    def begin(self, grads, after):
        self.send, self.recv, self.src, self.land, self.token = exchange_start(
            [grads[k] for k in self.keys], after, self.pair_shapes, self._pair, N_CHIP, name="rs_pair_start_" + self.tag)
        return self.token

    def middle(self, after, core):
        both = lambda i, z, ss, rs: (self._pair(i, z, ss, rs),) * 2
        grads, from_sibling = exchange_wait(self.send, self.recv, self.src, self.land, after, both,
                                            name="rs_pair_wait_" + self.tag)
        sums = [pair_sum(g, rcv, shp, ax, core, name="pair_sum_" + k)
                for k, g, rcv, shp, ax in zip(self.keys, grads, from_sibling, self.shapes, self.axes)]
        self.send, self.recv, self.src, self.land, self.token = exchange_start(
            sums, after, self.pair_shapes, _chip_sends, 3, name="rs_chips_start_" + self.tag)
        return self.token

    def end(self, after):
        both = lambda i, z, ss, rs: (_chip_sends(i, z, ss, rs), _chip_arrivals(i, z, ss, rs))
        sums, landed = exchange_wait(self.send, self.recv, self.src, self.land, after, both,
                                     name="rs_chips_wait_" + self.tag)
        return list(zip(self.keys, sums, landed))
SMALL = (("norm_mix_pre", D_MODEL), ("gate_bias", 2 * D_MODEL), ("norm_mix_post", D_MODEL), ("norm_mlp_pre", D_MODEL),
         ("norm_mlp_post", D_MODEL), ("q_norm", HEAD_DIM), ("k_norm", HEAD_DIM))
SMALL_WIDTH = sum(w for _, w in SMALL)
WEIGHTS = ("norm_mix_pre", "w_in", "gate_bias", "conv_w", "q_norm", "k_norm", "w_out_conv", "w_out_attn", "w_merge",
           "norm_mix_post", "norm_mlp_pre", "w_up", "w_down", "norm_mlp_post")


def kernel(x, norm_mix_pre, w_in, gate_bias, conv_w, q_norm, k_norm, w_out_conv, w_out_attn, w_merge, norm_mix_post, norm_mlp_pre, w_up, w_down, norm_mlp_post, loss_target, m_norm_mix_pre, m_w_in, m_gate_bias, m_conv_w, m_q_norm, m_k_norm, m_w_out_conv, m_w_out_attn, m_w_merge, m_norm_mix_post, m_norm_mlp_pre, m_w_up, m_w_down, m_norm_mlp_post, v_norm_mix_pre, v_w_in, v_gate_bias, v_conv_w, v_q_norm, v_k_norm, v_w_out_conv, v_w_out_attn, v_w_merge, v_norm_mix_post, v_norm_mlp_pre, v_w_up, v_w_down, v_norm_mlp_post):
    w = dict(norm_mix_pre=norm_mix_pre, w_in=w_in, gate_bias=gate_bias, conv_w=conv_w, q_norm=q_norm, k_norm=k_norm,
             w_out_conv=w_out_conv, w_out_attn=w_out_attn, w_merge=w_merge, norm_mix_post=norm_mix_post,
             norm_mlp_pre=norm_mlp_pre, w_up=w_up, w_down=w_down, norm_mlp_post=norm_mlp_post)
    mom = dict(norm_mix_pre=m_norm_mix_pre, w_in=m_w_in, gate_bias=m_gate_bias, conv_w=m_conv_w, q_norm=m_q_norm,
               k_norm=m_k_norm, w_out_conv=m_w_out_conv, w_out_attn=m_w_out_attn, w_merge=m_w_merge,
               norm_mix_post=m_norm_mix_post, norm_mlp_pre=m_norm_mlp_pre, w_up=m_w_up, w_down=m_w_down,
               norm_mlp_post=m_norm_mlp_post)
    var = dict(norm_mix_pre=v_norm_mix_pre, w_in=v_w_in, gate_bias=v_gate_bias, conv_w=v_conv_w, q_norm=v_q_norm,
               k_norm=v_k_norm, w_out_conv=v_w_out_conv, w_out_attn=v_w_out_attn, w_merge=v_w_merge,
               norm_mix_post=v_norm_mix_post, norm_mlp_pre=v_norm_mlp_pre, w_up=v_w_up, w_down=v_w_down,
               norm_mlp_post=v_norm_mlp_post)
    depth = w_in.shape[0]
    s = x.shape[1]
    xs = x.reshape(s, D_MODEL)
    target = loss_target.reshape(s, D_MODEL)
    x_idx, y_idx, c_idx = _place()
    as_operand = lambda i: jnp.reshape(i, (1,)).astype(jnp.int32)
    core, chip, me = as_operand(c_idx), as_operand(2 * x_idx + y_idx), as_operand(4 * x_idx + 2 * y_idx + c_idx)
    cos_t, sin_t = rope_tables(s)

    def vec(name, l):
        return w[name][l].reshape(1, -1)

    saved = []

    def cast_shards(l, keys, after):
        own = {}
        for k in keys:
            own[k] = cast_into_full(w[k], l, BIG_AXIS[k], me, F32 if k == "conv_w" else MXU_DTYPE, name="cast_" + k,
                                    after=[after])
            after = own[k]
        return own, after

    first, after = [], xs
    for i, keys in enumerate(FIRST_GATHER_GROUPS):
        own, after = cast_shards(0, keys, after)
        first.append(_Gather(w, keys, own, after, f"0{'abc'[i]}"))
        after = first[-1].token
    h = rms_fwd(xs, vec("norm_mix_pre", 0), name="rms_first")
    ahead = {}
    for l in range(1, depth):
        ahead[l], after = cast_shards(l, GATHERED, after)
    full = first[0].wait(first[0].pass_on(after))
    layers = []
    for l in range(depth):
        layers.append(full)
        nxt = l + 1 < depth
        if nxt:
            coming = _Gather(w, GATHERED, ahead[l + 1], full["w_in"], str(l + 1))
        z = matmul(h, full["w_in"], mode="nn", m=s, n=IN_WIDTH, k=D_MODEL, out_dtypes=[ACT_DTYPE], name="mm_in",
                   after=[coming.token] if nxt else [])
        t = conv_fwd(z, full["conv_w"], name="conv_fwd")
        qp, kp, vb = qk_prep_fwd(z, vec("q_norm", l), vec("k_norm", l), cos_t, sin_t, name="qk_fwd")
        o, lse = attn_fwd(qp, kp, vb, name="attn_fwd", after=[first[1].pass_on(qp)] if l == 0 else [])
        if l == 0:
            full.update(first[1].wait(o))
        ya = matmul(t, full["w_out_conv"], mode="nn", m=s, n=D_MODEL, k=CONV_WIDTH, out_dtypes=[ACT_DTYPE], name="mm_out_conv",
                    after=[first[2].pass_on(o)] if l == 0 else [])
        yb = matmul(o, full["w_out_attn"], mode="nn", m=s, n=D_MODEL, k=ATTN_WIDTH, out_dtypes=[ACT_DTYPE], name="mm_out_attn")
        mi = gate_fwd(ya, yb, z, vec("gate_bias", l), name="gate_fwd")
        mixed = matmul(mi, full["w_merge"], mode="nn", m=s, n=D_MODEL, k=D_MODEL, out_dtypes=[ACT_DTYPE], name="mm_merge")
        if l == 0:
            full.update(first[2].wait(mixed))
        x_mid, h2 = rms_residual_fwd(xs, mixed, vec("norm_mix_post", l), vec("norm_mlp_pre", l), name="res_mix")
        act, r = matmul(h2, full["w_up"], mode="nn", m=s, n=D_FF, k=D_MODEL, out_dtypes=[MXU_DTYPE, MXU_DTYPE],
                        name="mm_up", epilogue=lambda acc: (acc, jnp.square(jnp.maximum(acc, 0.0))))
        f = matmul(r, full["w_down"], mode="nn", m=s, n=D_MODEL, k=D_FF, tk=D_FF // 2, vmem=VMEM_LIMIT_DEEP_K,
                   out_dtypes=[ACT_DTYPE], name="mm_down",
                   after=[coming.pass_on(r)] if nxt and l > 0 else [])
        if nxt and l == 0:
            coming.pass_on(f)
        g_next = vec("norm_mix_pre", l + 1) if nxt else None
        x_out, h_next = rms_residual_fwd(x_mid, f, vec("norm_mlp_post", l), g_next, name="res_mlp" if nxt else "res_last")
        saved.append(dict(x_in=xs, h=h, z=z, t=t, qp=qp, kp=kp, vb=vb, o=o, lse=lse, ya=ya, yb=yb, mi=mi, mixed=mixed,
                          x_mid=x_mid, h2=h2, act=act, r=r, f=f))
        if nxt:
            full = coming.wait(x_out)
        xs, h = x_out, h_next

    dx, loss_part = loss_and_grad(xs, target, name="loss")
    loss = lax.psum(jnp.sum(loss_part), ("x", "y", "c"))

    small_rows = [None] * depth
    conv_rows = [None] * depth
    out_g, out_d, out_m, out_v = {}, {}, {}, {}
    big_outs = {k: None for k in BIG}
    pending = None
    last = []
    handed_down = None

    def finish(reduction, layer, after):
        for k, own, landed in reduction.end(after):
            big_outs[k] = reduce_adam(own, landed, chip, w[k], mom[k], var[k], layer, big_outs[k], name="adam_" + k)

    for l in reversed(range(depth)):
        sv, full = saved[l], layers[l]
        grads = {}
        groups = l == 0

        def wgrad(key, lhs, rhs, m, n, after=()):
            grads[key] = matmul(lhs, rhs, mode="tn", m=m, n=n, k=s, tk=s, out_dtypes=[COMM_DTYPE], name="wg_" + key,
                                after=after)

        def begin_group(i, after):
            last.append(_Reduce(w, LAST_REDUCE_GROUPS[i], f"{l}{'abc'[i]}"))
            return [last[i].begin(grads, after)]

        if handed_down is None:
            handed_down = rms_bwd(sv["f"], vec("norm_mlp_post", l), dx, None, out_dtype=MXU_DTYPE, name="rmsb_mlp_post")
        df, dg_mlp_post = handed_down
        da = matmul(df, full["w_down"], mode="nt", m=s, n=D_FF, k=D_MODEL, out_dtypes=[MXU_DTYPE], name="mm_d_down",
                    extra=(sv["act"],), epilogue=lambda acc, a: (acc * (2.0 * jnp.maximum(a.astype(F32), 0.0)),),
                    after=[pending.token] if pending else [])
        tokens = [pending.middle(da, core)] if pending else []
        wgrad("w_down", sv["r"], df, D_FF, D_MODEL)
        dh2 = matmul(da, full["w_up"], mode="nt", m=s, n=D_MODEL, k=D_FF, tk=D_FF // 2, vmem=VMEM_LIMIT_DEEP_K,
                     out_dtypes=[ACT_DTYPE], name="mm_d_up", after=tokens)
        wgrad("w_up", sv["h2"], da, D_MODEL, D_FF)
        tokens = begin_group(0, dh2) if groups else []
        dx_mid, dmixed, dg_mlp_pre, dg_mix_post = rms_bwd_pair(
            sv["x_mid"], vec("norm_mlp_pre", l), dh2, dx, sv["mixed"], vec("norm_mix_post", l), name="rmsb_mlp_pre_mix_post")
        dmi = matmul(dmixed, full["w_merge"], mode="nt", m=s, n=D_MODEL, k=D_MODEL, out_dtypes=[ACT_DTYPE], name="mm_d_merge",
                     after=tokens)
        tokens = [last[0].middle(dmi, core)] if groups else []
        wgrad("w_merge", sv["mi"], dmixed, D_MODEL, D_MODEL, after=tokens)
        dya, dyb, dga, dgb, dba, dbb = gate_bwd(dmi, sv["ya"], sv["yb"], sv["z"], vec("gate_bias", l), name="gate_bwd")
        wgrad("w_out_conv", sv["t"], dya, CONV_WIDTH, D_MODEL)
        wgrad("w_out_attn", sv["o"], dyb, ATTN_WIDTH, D_MODEL)
        tokens = begin_group(1, dyb) if groups else []
        dt = matmul(dya, full["w_out_conv"], mode="nt", m=s, n=CONV_WIDTH, k=D_MODEL, out_dtypes=[ACT_DTYPE],
                    name="mm_d_out_conv", after=tokens)
        do = matmul(dyb, full["w_out_attn"], mode="nt", m=s, n=ATTN_WIDTH, k=D_MODEL, out_dtypes=[MXU_DTYPE],
                    name="mm_d_out_attn")
        tokens = [last[1].middle(do, core)] if groups else []
        dcb, dcc, dci, dconv_w = conv_bwd(dt, sv["z"], full["conv_w"], name="conv_bwd", after=tokens)
        dqp, dkp, dv = attn_bwd(sv["qp"], sv["kp"], sv["vb"], do, sv["o"], sv["lse"], name="attn_bwd")
        dq, dk, dqn, dkn = qk_prep_bwd(dqp, dkp, sv["z"], vec("q_norm", l), vec("k_norm", l), cos_t, sin_t, name="qk_bwd")
        dz = jnp.concatenate([dcb, dcc, dci, dq, dk, dv, dga, dgb], axis=1)
        wgrad("w_in", sv["h"], dz, D_MODEL, IN_WIDTH)
        tokens = begin_group(2, dz) if groups else []
        dh = matmul(dz, full["w_in"], mode="nt", m=s, n=D_MODEL, k=IN_WIDTH, tk=IN_WIDTH // 4, out_dtypes=[ACT_DTYPE],
                    name="mm_d_in", after=tokens)
        if l > 0:
            dx, df_below, dg_mix_pre, dg_post_below = rms_bwd_pair(
                sv["x_in"], vec("norm_mix_pre", l), dh, dx_mid, saved[l - 1]["f"], vec("norm_mlp_post", l - 1),
                name="rmsb_mix_pre_mlp_post")
            handed_down = (df_below, dg_post_below)
        else:
            dx, dg_mix_pre = rms_bwd(sv["x_in"], vec("norm_mix_pre", l), dh, dx_mid, out_dtype=F32, name="rmsb_mix_pre")
        small_rows[l] = jnp.concatenate([dg_mix_pre, dba, dbb, dg_mix_post, dg_mlp_pre, dg_mlp_post, dqn, dkn], axis=1)
        conv_rows[l] = dconv_w.reshape(1, 3 * CONV_WIDTH)
        if pending:
            finish(pending, l + 1, dx)
        pending = None
        if not groups:
            pending = _Reduce(w, BIG, str(l))
            pending.begin(grads, dx)

    grad_x = dx.reshape(1, s, D_MODEL)

    small_part = jnp.concatenate([jnp.concatenate(small_rows, axis=0), jnp.concatenate(conv_rows, axis=0)], axis=1)
    small_sum = all_reduce_small(small_part, name="allreduce_small")
    pack = lambda src: jnp.concatenate([src[k].reshape(depth, wd) for k, wd in SMALL], axis=1)
    g_small = small_sum[:, :SMALL_WIDTH]
    d_small, m_small, v_small = adam_small(g_small, pack(w), pack(mom), pack(var), name="adam_small")
    off = 0
    for k, wd in SMALL:
        for dst, src in ((out_g, g_small), (out_d, d_small), (out_m, m_small), (out_v, v_small)):
            dst[k] = src[:, off:off + wd]
        off += wd
    cshard = CONV_WIDTH // N_DEV
    g_conv = lax.dynamic_slice_in_dim(small_sum[:, SMALL_WIDTH:].reshape(depth, 3, CONV_WIDTH), me[0] * cshard, cshard,
                                      axis=2).reshape(depth, 3 * cshard)
    flat = lambda a: a.reshape(depth, 3 * cshard)
    d_conv, m_conv, v_conv = adam_small(g_conv, flat(conv_w), flat(m_conv_w), flat(v_conv_w), name="adam_conv")
    for dst, src in ((out_g, g_conv), (out_d, d_conv), (out_m, m_conv), (out_v, v_conv)):
        dst["conv_w"] = src.reshape(depth, 3, cshard)

    token = last[2].middle(d_small, core)
    for reduction in last:
        finish(reduction, 0, token)
    for k in BIG:
        out_g[k], out_d[k], out_m[k], out_v[k] = big_outs[k]

    return (loss, grad_x, *[out_g[k] for k in WEIGHTS], *[out_d[k] for k in WEIGHTS], *[out_m[k] for k in WEIGHTS],
            *[out_v[k] for k in WEIGHTS])
```
